```python
import jax, jax.numpy as jnp
from jax import lax
import numpy as np

D_MODEL = 1024
BATCH = 16
SEQ = 4096
DEPTH = 4

CHUNK = 128
A_HEADS = 4
A_HEAD_DIM = 128
A_WIDTH = A_HEADS * A_HEAD_DIM
POOL_WINDOWS = (2, 4, 8, 16)
B_GROUP_DIM = 128
B_WIDTH = len(POOL_WINDOWS) * B_GROUP_DIM
C_WIDTH = 512
CONV_WIDTH = 3
N_BRANCH = 3
IN_SPLIT_SIZES = (A_WIDTH, A_WIDTH, B_WIDTH, C_WIDTH, C_WIDTH, C_WIDTH, N_BRANCH * D_MODEL)
IN_COLS = sum(IN_SPLIT_SIZES)
_FF_RAW = -(-8 * D_MODEL // 3)
D_FF = -(-_FF_RAW // 256) * 256
N_MOD = 6
EPS = 1e-6

kernel_name = "hybrid_gmlp_pool_conv_gated_trunk"


def rmsnorm(x, g):
    xf = x.astype(jnp.float32)
    xf = xf * lax.rsqrt(jnp.mean(xf * xf, axis=-1, keepdims=True) + EPS)
    return xf.astype(x.dtype) * g


def layernorm(x, g, b):
    xf = x.astype(jnp.float32)
    mu = jnp.mean(xf, axis=-1, keepdims=True)
    var = jnp.mean(jnp.square(xf - mu), axis=-1, keepdims=True)
    return ((xf - mu) * lax.rsqrt(var + EPS)).astype(x.dtype) * g + b


def modulate(h, shift, scale):
    return h * (1.0 + scale[:, None, :]) + shift[:, None, :]


def split_cols(z):
    parts, off = [], 0
    for n in IN_SPLIT_SIZES:
        parts.append(z[..., off:off + n])
        off += n
    return parts


def gmlp_branch(u, v, ln_g, ln_b, w_s, b_s):
    bsz, seq, _ = v.shape
    v = layernorm(v, ln_g, ln_b)
    vc = v.reshape(bsz, seq // CHUNK, CHUNK, A_HEADS, A_HEAD_DIM)
    mask = jnp.tril(jnp.ones((CHUNK, CHUNK), dtype=w_s.dtype))
    w = w_s * mask[None]
    s = jnp.einsum('hts,bnshd->bnthd', w, vc) + jnp.transpose(b_s)[None, None, :, :, None]
    return u * s.reshape(bsz, seq, A_WIDTH)


def pool_branch(xb, pool_w, pool_scale):
    bsz, seq, _ = xb.shape
    xf = xb.astype(jnp.float32)
    cs = jnp.pad(jnp.cumsum(xf, axis=1), ((0, 0), (1, 0), (0, 0)))
    pos = jnp.arange(seq)
    outs = []
    for g, win in enumerate(POOL_WINDOWS):
        sl = slice(g * B_GROUP_DIM, (g + 1) * B_GROUP_DIM)
        csg = cs[..., sl]
        lag = jnp.pad(csg, ((0, 0), (win - 1, 0), (0, 0)))[:, :seq]
        cnt = jnp.minimum(pos + 1, win).astype(jnp.float32)[None, :, None]
        outs.append((csg[:, 1:] - lag) / cnt - xf[..., sl])
    p = jnp.stack(outs, axis=2).astype(xb.dtype)
    y = jnp.einsum('bsgc,gcd->bsgd', p, pool_w).reshape(bsz, seq, B_WIDTH)
    return y * pool_scale


def conv_branch(bg, cg, h, conv_w):
    seq = h.shape[1]
    z = cg * h
    zp = jnp.pad(z, ((0, 0), (CONV_WIDTH - 1, 0), (0, 0)))
    y = conv_w[0] * zp[:, 0:seq]
    for k in range(1, CONV_WIDTH):
        y = y + conv_w[k] * zp[:, k:k + seq]
    return bg * y


def _fwd_setup_inputs(seed: int = 0) -> dict:
    key = jax.random.key(seed)
    ks = jax.random.split(key, 24)
    f32 = jnp.float32
    nrm = lambda k, shape, scale: (jax.random.normal(k, shape, f32) * scale).astype(f32)
    L, D = DEPTH, D_MODEL
    return {
        "x": nrm(ks[0], (BATCH, SEQ, D), 1.0),
        "c": nrm(ks[1], (BATCH, D), 1.0),
        "w_mod": nrm(ks[2], (L, D, N_MOD * D), 0.5 * D ** -0.5),
        "b_mod": nrm(ks[3], (L, N_MOD * D), 0.02),
        "g_mix": 1.0 + nrm(ks[4], (L, D), 0.05),
        "w_in": nrm(ks[5], (L, D, IN_COLS), D ** -0.5),
        "gm_ln_g": 1.0 + nrm(ks[6], (L, A_WIDTH), 0.05),
        "gm_ln_b": nrm(ks[7], (L, A_WIDTH), 0.02),
        "gm_w_s": nrm(ks[8], (L, A_HEADS, CHUNK, CHUNK), CHUNK ** -0.5),
        "gm_b_s": 1.0 + nrm(ks[9], (L, A_HEADS, CHUNK), 0.05),
        "w_pa": nrm(ks[10], (L, A_WIDTH, D), A_WIDTH ** -0.5),
        "pool_w": nrm(ks[11], (L, len(POOL_WINDOWS), B_GROUP_DIM, B_GROUP_DIM), B_GROUP_DIM ** -0.5),
        "pool_scale": 1.0 + nrm(ks[12], (L, B_WIDTH), 0.1),
        "w_pb": nrm(ks[13], (L, B_WIDTH, D), B_WIDTH ** -0.5),
        "conv_w": nrm(ks[14], (L, CONV_WIDTH, C_WIDTH), CONV_WIDTH ** -0.5),
        "w_pc": nrm(ks[15], (L, C_WIDTH, D), C_WIDTH ** -0.5),
        "w_o": nrm(ks[16], (L, D, D), D ** -0.5),
        "g_ffn": 1.0 + nrm(ks[17], (L, D), 0.05),
        "w_13": nrm(ks[18], (L, D, 2 * D_FF), D ** -0.5),
        "w_2": nrm(ks[19], (L, D_FF, D), D_FF ** -0.5),
        "g_final": 1.0 + nrm(ks[20], (D,), 0.05),
    }


def _fwd_reference(x, c, w_mod, b_mod, g_mix, w_in, gm_ln_g, gm_ln_b, gm_w_s, gm_b_s, w_pa,
              pool_w, pool_scale, w_pb, conv_w, w_pc, w_o, g_ffn, w_13, w_2, g_final):
    bsz, seq, d = x.shape
    c_act = jax.nn.silu(c)
    for l in range(DEPTH):
        mod = (c_act @ w_mod[l] + b_mod[l]).reshape(bsz, N_MOD, d)
        shift1, scale1, gate1 = mod[:, 0], mod[:, 1], mod[:, 2]
        shift2, scale2, gate2 = mod[:, 3], mod[:, 4], mod[:, 5]

        h = modulate(rmsnorm(x, g_mix[l]), shift1, scale1)
        z = h @ w_in[l]
        u_a, v_a, x_b, bg_c, cg_c, h_c, gate_logits = split_cols(z)
        y_a = gmlp_branch(jax.nn.gelu(u_a), jax.nn.gelu(v_a), gm_ln_g[l], gm_ln_b[l],
                          gm_w_s[l], gm_b_s[l]) @ w_pa[l]
        y_b = pool_branch(x_b, pool_w[l], pool_scale[l]) @ w_pb[l]
        y_c = conv_branch(bg_c, cg_c, h_c, conv_w[l]) @ w_pc[l]
        g = jax.nn.sigmoid(gate_logits).reshape(bsz, seq, N_BRANCH, d)
        merged = g[:, :, 0] * y_a + g[:, :, 1] * y_b + g[:, :, 2] * y_c
        x = x + gate1[:, None, :] * (merged @ w_o[l])

        h = modulate(rmsnorm(x, g_ffn[l]), shift2, scale2)
        ab = h @ w_13[l]
        a, b = ab[..., :D_FF], ab[..., D_FF:]
        x = x + gate2[:, None, :] * ((jax.nn.silu(a) * b) @ w_2[l])
    return rmsnorm(x, g_final)


import jax as _jax
import jax.numpy as _jnp

TWIN_FORMAT = 'train_step'
FWD_PARAMS = ['x', 'c', 'w_mod', 'b_mod', 'g_mix', 'w_in', 'gm_ln_g', 'gm_ln_b', 'gm_w_s', 'gm_b_s', 'w_pa', 'pool_w', 'pool_scale', 'w_pb', 'conv_w', 'w_pc', 'w_o', 'g_ffn', 'w_13', 'w_2', 'g_final']
TWIN_WEIGHTS = ['w_mod', 'b_mod', 'g_mix', 'w_in', 'gm_ln_g', 'gm_ln_b', 'gm_w_s', 'gm_b_s', 'w_pa', 'pool_w', 'pool_scale', 'w_pb', 'conv_w', 'w_pc', 'w_o', 'g_ffn', 'w_13', 'w_2', 'g_final']
TWIN_DIFF_INPUT = 'x'
TWIN_INPUTS = ['x', 'c', 'w_mod', 'b_mod', 'g_mix', 'w_in', 'gm_ln_g', 'gm_ln_b', 'gm_w_s', 'gm_b_s', 'w_pa', 'pool_w', 'pool_scale', 'w_pb', 'conv_w', 'w_pc', 'w_o', 'g_ffn', 'w_13', 'w_2', 'g_final', 'loss_target', 'm_w_mod', 'm_b_mod', 'm_g_mix', 'm_w_in', 'm_gm_ln_g', 'm_gm_ln_b', 'm_gm_w_s', 'm_gm_b_s', 'm_w_pa', 'm_pool_w', 'm_pool_scale', 'm_w_pb', 'm_conv_w', 'm_w_pc', 'm_w_o', 'm_g_ffn', 'm_w_13', 'm_w_2', 'm_g_final', 'v_w_mod', 'v_b_mod', 'v_g_mix', 'v_w_in', 'v_gm_ln_g', 'v_gm_ln_b', 'v_gm_w_s', 'v_gm_b_s', 'v_w_pa', 'v_pool_w', 'v_pool_scale', 'v_w_pb', 'v_conv_w', 'v_w_pc', 'v_w_o', 'v_g_ffn', 'v_w_13', 'v_w_2', 'v_g_final']
TWIN_OUTPUTS = ['loss', 'grad_x', 'grad_w_mod', 'grad_b_mod', 'grad_g_mix', 'grad_w_in', 'grad_gm_ln_g', 'grad_gm_ln_b', 'grad_gm_w_s', 'grad_gm_b_s', 'grad_w_pa', 'grad_pool_w', 'grad_pool_scale', 'grad_w_pb', 'grad_conv_w', 'grad_w_pc', 'grad_w_o', 'grad_g_ffn', 'grad_w_13', 'grad_w_2', 'grad_g_final', 'delta_w_mod', 'delta_b_mod', 'delta_g_mix', 'delta_w_in', 'delta_gm_ln_g', 'delta_gm_ln_b', 'delta_gm_w_s', 'delta_gm_b_s', 'delta_w_pa', 'delta_pool_w', 'delta_pool_scale', 'delta_w_pb', 'delta_conv_w', 'delta_w_pc', 'delta_w_o', 'delta_g_ffn', 'delta_w_13', 'delta_w_2', 'delta_g_final', 'new_m_w_mod', 'new_m_b_mod', 'new_m_g_mix', 'new_m_w_in', 'new_m_gm_ln_g', 'new_m_gm_ln_b', 'new_m_gm_w_s', 'new_m_gm_b_s', 'new_m_w_pa', 'new_m_pool_w', 'new_m_pool_scale', 'new_m_w_pb', 'new_m_conv_w', 'new_m_w_pc', 'new_m_w_o', 'new_m_g_ffn', 'new_m_w_13', 'new_m_w_2', 'new_m_g_final', 'new_v_w_mod', 'new_v_b_mod', 'new_v_g_mix', 'new_v_w_in', 'new_v_gm_ln_g', 'new_v_gm_ln_b', 'new_v_gm_w_s', 'new_v_gm_b_s', 'new_v_w_pa', 'new_v_pool_w', 'new_v_pool_scale', 'new_v_w_pb', 'new_v_conv_w', 'new_v_w_pc', 'new_v_w_o', 'new_v_g_ffn', 'new_v_w_13', 'new_v_w_2', 'new_v_g_final']
TWIN_LEAF_KINDS = {'loss': 'loss', 'grad_x': 'grad_x', 'grad_w_mod': 'grad_w', 'grad_b_mod': 'grad_w', 'grad_g_mix': 'grad_w', 'grad_w_in': 'grad_w', 'grad_gm_ln_g': 'grad_w', 'grad_gm_ln_b': 'grad_w', 'grad_gm_w_s': 'grad_w', 'grad_gm_b_s': 'grad_w', 'grad_w_pa': 'grad_w', 'grad_pool_w': 'grad_w', 'grad_pool_scale': 'grad_w', 'grad_w_pb': 'grad_w', 'grad_conv_w': 'grad_w', 'grad_w_pc': 'grad_w', 'grad_w_o': 'grad_w', 'grad_g_ffn': 'grad_w', 'grad_w_13': 'grad_w', 'grad_w_2': 'grad_w', 'grad_g_final': 'grad_w', 'delta_w_mod': 'delta_w', 'delta_b_mod': 'delta_w', 'delta_g_mix': 'delta_w', 'delta_w_in': 'delta_w', 'delta_gm_ln_g': 'delta_w', 'delta_gm_ln_b': 'delta_w', 'delta_gm_w_s': 'delta_w', 'delta_gm_b_s': 'delta_w', 'delta_w_pa': 'delta_w', 'delta_pool_w': 'delta_w', 'delta_pool_scale': 'delta_w', 'delta_w_pb': 'delta_w', 'delta_conv_w': 'delta_w', 'delta_w_pc': 'delta_w', 'delta_w_o': 'delta_w', 'delta_g_ffn': 'delta_w', 'delta_w_13': 'delta_w', 'delta_w_2': 'delta_w', 'delta_g_final': 'delta_w', 'new_m_w_mod': 'new_m', 'new_m_b_mod': 'new_m', 'new_m_g_mix': 'new_m', 'new_m_w_in': 'new_m', 'new_m_gm_ln_g': 'new_m', 'new_m_gm_ln_b': 'new_m', 'new_m_gm_w_s': 'new_m', 'new_m_gm_b_s': 'new_m', 'new_m_w_pa': 'new_m', 'new_m_pool_w': 'new_m', 'new_m_pool_scale': 'new_m', 'new_m_w_pb': 'new_m', 'new_m_conv_w': 'new_m', 'new_m_w_pc': 'new_m', 'new_m_w_o': 'new_m', 'new_m_g_ffn': 'new_m', 'new_m_w_13': 'new_m', 'new_m_w_2': 'new_m', 'new_m_g_final': 'new_m', 'new_v_w_mod': 'new_v', 'new_v_b_mod': 'new_v', 'new_v_g_mix': 'new_v', 'new_v_w_in': 'new_v', 'new_v_gm_ln_g': 'new_v', 'new_v_gm_ln_b': 'new_v', 'new_v_gm_w_s': 'new_v', 'new_v_gm_b_s': 'new_v', 'new_v_w_pa': 'new_v', 'new_v_pool_w': 'new_v', 'new_v_pool_scale': 'new_v', 'new_v_w_pb': 'new_v', 'new_v_conv_w': 'new_v', 'new_v_w_pc': 'new_v', 'new_v_w_o': 'new_v', 'new_v_g_ffn': 'new_v', 'new_v_w_13': 'new_v', 'new_v_w_2': 'new_v', 'new_v_g_final': 'new_v'}


def _forward(args):
    return _fwd_reference(*[args[k] for k in FWD_PARAMS])


def _output_shape():
    out = _jax.eval_shape(lambda: _forward(_fwd_setup_inputs(0)))
    return out.shape, out.dtype

N_MICROBATCH = 1
ADAM_LR = 0.001
ADAM_B1 = 0.9
ADAM_B2 = 0.999
ADAM_EPS = 1e-08
ADAM_WD = 0.01
ADAM_STEP = 10
PER_EXAMPLE_BATCH_AXIS = {'x': 0, 'c': 0, 'loss_target': 0}
SHARED_INPUTS = []
_WEIGHT_DTYPES = {'w_mod': _jnp.float32, 'b_mod': _jnp.float32, 'g_mix': _jnp.float32, 'w_in': _jnp.float32, 'gm_ln_g': _jnp.float32, 'gm_ln_b': _jnp.float32, 'gm_w_s': _jnp.float32, 'gm_b_s': _jnp.float32, 'w_pa': _jnp.float32, 'pool_w': _jnp.float32, 'pool_scale': _jnp.float32, 'w_pb': _jnp.float32, 'conv_w': _jnp.float32, 'w_pc': _jnp.float32, 'w_o': _jnp.float32, 'g_ffn': _jnp.float32, 'w_13': _jnp.float32, 'w_2': _jnp.float32, 'g_final': _jnp.float32}
MOMENT_SCALE = {'w_mod': 1.144067e-01, 'b_mod': 2.045373e-01, 'g_mix': 1.139989e-01, 'w_in': 4.777565e-02, 'gm_ln_g': 3.044099e-02, 'gm_ln_b': 3.184005e-02, 'gm_w_s': 3.046078e-02, 'gm_b_s': 4.251615e-02, 'w_pa': 3.758478e-02, 'pool_w': 5.552033e-02, 'pool_scale': 5.445200e-02, 'w_pb': 3.908581e-02, 'conv_w': 7.777726e-02, 'w_pc': 5.477939e-02, 'w_o': 7.738807e-02, 'g_ffn': 7.469364e-02, 'w_13': 3.282903e-02, 'w_2': 5.359662e-02, 'g_final': 6.447137e+01}


def _to_microbatches(a, axis):
    t = _jnp.moveaxis(a, axis, 0)
    t = t.reshape((N_MICROBATCH, t.shape[0] // N_MICROBATCH) + t.shape[1:])
    return _jnp.moveaxis(t, 1, axis + 1)


def setup_inputs(seed: int = 0) -> dict:
    inp = _fwd_setup_inputs(seed)
    key = _jax.random.fold_in(_jax.random.key(seed), 7919)
    shape, _ = _output_shape()
    out = dict(inp)
    out["loss_target"] = _jax.random.normal(_jax.random.fold_in(key, 0), shape, _jnp.float32)
    for i, name in enumerate(TWIN_WEIGHTS):
        w = inp[name].astype(_jnp.float32)
        if MOMENT_SCALE is None:
            s = _jnp.sqrt(_jnp.mean(_jnp.square(w)) + 1e-30)
        else:
            s = MOMENT_SCALE[name]
        km, kv = _jax.random.split(_jax.random.fold_in(key, i + 1))
        out[name] = w
        out["m_" + name] = s * _jax.random.normal(km, w.shape, _jnp.float32)
        out["v_" + name] = (s * s) * _jax.random.uniform(kv, w.shape, _jnp.float32, 0.5, 1.5)
    if N_MICROBATCH > 1:
        for name, axis in PER_EXAMPLE_BATCH_AXIS.items():
            out[name] = _to_microbatches(out[name], axis)
    return {'x': out['x'], 'c': out['c'], 'w_mod': out['w_mod'], 'b_mod': out['b_mod'], 'g_mix': out['g_mix'], 'w_in': out['w_in'], 'gm_ln_g': out['gm_ln_g'], 'gm_ln_b': out['gm_ln_b'], 'gm_w_s': out['gm_w_s'], 'gm_b_s': out['gm_b_s'], 'w_pa': out['w_pa'], 'pool_w': out['pool_w'], 'pool_scale': out['pool_scale'], 'w_pb': out['w_pb'], 'conv_w': out['conv_w'], 'w_pc': out['w_pc'], 'w_o': out['w_o'], 'g_ffn': out['g_ffn'], 'w_13': out['w_13'], 'w_2': out['w_2'], 'g_final': out['g_final'], 'loss_target': out['loss_target'], 'm_w_mod': out['m_w_mod'], 'm_b_mod': out['m_b_mod'], 'm_g_mix': out['m_g_mix'], 'm_w_in': out['m_w_in'], 'm_gm_ln_g': out['m_gm_ln_g'], 'm_gm_ln_b': out['m_gm_ln_b'], 'm_gm_w_s': out['m_gm_w_s'], 'm_gm_b_s': out['m_gm_b_s'], 'm_w_pa': out['m_w_pa'], 'm_pool_w': out['m_pool_w'], 'm_pool_scale': out['m_pool_scale'], 'm_w_pb': out['m_w_pb'], 'm_conv_w': out['m_conv_w'], 'm_w_pc': out['m_w_pc'], 'm_w_o': out['m_w_o'], 'm_g_ffn': out['m_g_ffn'], 'm_w_13': out['m_w_13'], 'm_w_2': out['m_w_2'], 'm_g_final': out['m_g_final'], 'v_w_mod': out['v_w_mod'], 'v_b_mod': out['v_b_mod'], 'v_g_mix': out['v_g_mix'], 'v_w_in': out['v_w_in'], 'v_gm_ln_g': out['v_gm_ln_g'], 'v_gm_ln_b': out['v_gm_ln_b'], 'v_gm_w_s': out['v_gm_w_s'], 'v_gm_b_s': out['v_gm_b_s'], 'v_w_pa': out['v_w_pa'], 'v_pool_w': out['v_pool_w'], 'v_pool_scale': out['v_pool_scale'], 'v_w_pb': out['v_w_pb'], 'v_conv_w': out['v_conv_w'], 'v_w_pc': out['v_w_pc'], 'v_w_o': out['v_w_o'], 'v_g_ffn': out['v_g_ffn'], 'v_w_13': out['v_w_13'], 'v_w_2': out['v_w_2'], 'v_g_final': out['v_g_final']}


def _loss(weights, diff, rest, loss_target):
    with _jax.named_scope("forward"):
        args = {**rest, TWIN_DIFF_INPUT: diff, **{k: w.astype(_WEIGHT_DTYPES[k]) for k, w in weights.items()}}
        y = _forward(args)
    with _jax.named_scope("loss_head"):
        err = _jnp.square(y.astype(_jnp.float32) - loss_target)
        return 0.5 * _jnp.sum(_jnp.mean(err, axis=-1)) if err.ndim else 0.5 * err


def _adamw(w, g, m, v):
    m = ADAM_B1 * m + (1.0 - ADAM_B1) * g
    v = ADAM_B2 * v + (1.0 - ADAM_B2) * _jnp.square(g)
    m_hat = m / (1.0 - ADAM_B1 ** ADAM_STEP)
    v_hat = v / (1.0 - ADAM_B2 ** ADAM_STEP)
    delta = -ADAM_LR * (m_hat / (_jnp.sqrt(v_hat) + ADAM_EPS) + ADAM_WD * w)
    return delta, m, v


def reference(x, c, w_mod, b_mod, g_mix, w_in, gm_ln_g, gm_ln_b, gm_w_s, gm_b_s, w_pa, pool_w, pool_scale, w_pb, conv_w, w_pc, w_o, g_ffn, w_13, w_2, g_final, loss_target, m_w_mod, m_b_mod, m_g_mix, m_w_in, m_gm_ln_g, m_gm_ln_b, m_gm_w_s, m_gm_b_s, m_w_pa, m_pool_w, m_pool_scale, m_w_pb, m_conv_w, m_w_pc, m_w_o, m_g_ffn, m_w_13, m_w_2, m_g_final, v_w_mod, v_b_mod, v_g_mix, v_w_in, v_gm_ln_g, v_gm_ln_b, v_gm_w_s, v_gm_b_s, v_w_pa, v_pool_w, v_pool_scale, v_w_pb, v_conv_w, v_w_pc, v_w_o, v_g_ffn, v_w_13, v_w_2, v_g_final):
    given = dict(x=x, c=c, w_mod=w_mod, b_mod=b_mod, g_mix=g_mix, w_in=w_in, gm_ln_g=gm_ln_g, gm_ln_b=gm_ln_b, gm_w_s=gm_w_s, gm_b_s=gm_b_s, w_pa=w_pa, pool_w=pool_w, pool_scale=pool_scale, w_pb=w_pb, conv_w=conv_w, w_pc=w_pc, w_o=w_o, g_ffn=g_ffn, w_13=w_13, w_2=w_2, g_final=g_final, loss_target=loss_target, m_w_mod=m_w_mod, m_b_mod=m_b_mod, m_g_mix=m_g_mix, m_w_in=m_w_in, m_gm_ln_g=m_gm_ln_g, m_gm_ln_b=m_gm_ln_b, m_gm_w_s=m_gm_w_s, m_gm_b_s=m_gm_b_s, m_w_pa=m_w_pa, m_pool_w=m_pool_w, m_pool_scale=m_pool_scale, m_w_pb=m_w_pb, m_conv_w=m_conv_w, m_w_pc=m_w_pc, m_w_o=m_w_o, m_g_ffn=m_g_ffn, m_w_13=m_w_13, m_w_2=m_w_2, m_g_final=m_g_final, v_w_mod=v_w_mod, v_b_mod=v_b_mod, v_g_mix=v_g_mix, v_w_in=v_w_in, v_gm_ln_g=v_gm_ln_g, v_gm_ln_b=v_gm_ln_b, v_gm_w_s=v_gm_w_s, v_gm_b_s=v_gm_b_s, v_w_pa=v_w_pa, v_pool_w=v_pool_w, v_pool_scale=v_pool_scale, v_w_pb=v_w_pb, v_conv_w=v_conv_w, v_w_pc=v_w_pc, v_w_o=v_w_o, v_g_ffn=v_g_ffn, v_w_13=v_w_13, v_w_2=v_w_2, v_g_final=v_g_final)
    weights = {n: given[n] for n in TWIN_WEIGHTS}
    shared = {n: given[n] for n in SHARED_INPUTS}
    per_example = {n: given[n] for n in ['x', 'c']}
    grad_fn = _jax.value_and_grad(_loss, argnums=(0, 1))

    def one_microbatch(ex, loss_target):
        ex = dict(ex)
        diff = ex.pop(TWIN_DIFF_INPUT)
        return grad_fn(weights, diff, {**shared, **ex}, loss_target)

    if N_MICROBATCH == 1:
        loss, (grad_w, grad_x) = one_microbatch(per_example, given["loss_target"])
    else:
        def body(carry, xs):
            loss_sum, grad_sum = carry
            l_k, (gw_k, gx_k) = one_microbatch(xs[0], xs[1])
            with _jax.named_scope("update"):
                return (loss_sum + l_k, _jax.tree.map(_jnp.add, grad_sum, gw_k)), gx_k

        init = (_jnp.zeros((), _jnp.float32), _jax.tree.map(_jnp.zeros_like, weights))
        (loss, grad_w), grad_x = _jax.lax.scan(body, init, (per_example, given["loss_target"]))
    with _jax.named_scope("update"):
        delta_w, new_m, new_v = {}, {}, {}
        for n in TWIN_WEIGHTS:
            delta_w[n], new_m[n], new_v[n] = _adamw(weights[n], grad_w[n], given["m_" + n], given["v_" + n])
    return (loss, grad_x, *[grad_w[n] for n in TWIN_WEIGHTS], *[delta_w[n] for n in TWIN_WEIGHTS],
            *[new_m[n] for n in TWIN_WEIGHTS], *[new_v[n] for n in TWIN_WEIGHTS])
```

```python
import functools

import jax
import jax.numpy as jnp
from jax import lax
from jax.experimental import pallas as pl
from jax.experimental.pallas import tpu as pltpu

F32 = jnp.float32
BF16 = jnp.bfloat16
MESH_ID = pl.DeviceIdType.MESH

N_DEV = 8
EPS = 1e-6
CHUNK = 128
HEADS = 4
HEAD_DIM = 128
BR_W = 512
POOL_WINDOWS = (2, 4, 8, 16)
POOL_HALO = 16
CONV_HALO = 8
N_MOD = 6
ADAM_LR = 0.001
ADAM_B1 = 0.9
ADAM_B2 = 0.999
ADAM_EPS = 1e-08
ADAM_WD = 0.01
ADAM_STEP = 10

TOKEN_TILE = 256
WGRAD_TOKENS = 512
VMEM_LIMIT = 56 * 1024 * 1024
GELU_K = 0.7978845608028654
GELU_C = 0.044715

ANY = pl.BlockSpec(memory_space=pl.ANY)
VMEM_SPEC = pl.BlockSpec(memory_space=pltpu.VMEM)


def _cparams(**kw):
    return pltpu.CompilerParams(vmem_limit_bytes=VMEM_LIMIT, **kw)


def _dot(a, b):
    return jnp.dot(a, b, preferred_element_type=F32)


def _dot_nt(a, b):
    return lax.dot_general(a, b, (((1,), (1,)), ((), ())), preferred_element_type=F32)


def _dot_tn(a, b):
    return lax.dot_general(a, b, (((0,), (0,)), ((), ())), preferred_element_type=F32)


def _colsum(a):
    return jnp.sum(a, axis=0, keepdims=True)


def _sigmoid(x):
    return 1.0 / (1.0 + jnp.exp(-x))


def _gelu(x):
    t = jnp.tanh(GELU_K * (x + GELU_C * x * x * x))
    return 0.5 * x * (1.0 + t), t


def _gelu_grad(x, t):
    return 0.5 * (1.0 + t) + 0.5 * x * (1.0 - t * t) * (GELU_K * (1.0 + 3.0 * GELU_C * x * x))


def _tril_mask():
    r = lax.broadcasted_iota(jnp.int32, (CHUNK, CHUNK), 0)
    c = lax.broadcasted_iota(jnp.int32, (CHUNK, CHUNK), 1)
    return (r >= c).astype(F32)


def _my_index():
    return 4 * lax.axis_index("x") + 2 * lax.axis_index("y") + lax.axis_index("c")


def _peer(k):
    x, y, c = lax.axis_index("x"), lax.axis_index("y"), lax.axis_index("c")
    px = 1 - x if (k >> 2) & 1 else x
    py = 1 - y if (k >> 1) & 1 else y
    pc = 1 - c if k & 1 else c
    return (px, py, pc), 4 * px + 2 * py + pc


def _load_weights(step, pairs, sem):
    @pl.when(step == 0)
    def _():
        copies = [pltpu.make_async_copy(src, dst, sem.at[n]) for n, (src, dst) in enumerate(pairs)]
        for cp in copies:
            cp.start()
        for cp in copies:
            cp.wait()


def _wp_pairs(wp_hbm, wp_v, layer):
    return [(wp_hbm.at[layer, j], wp_v.at[:, pl.ds(HEAD_DIM * j, HEAD_DIM)]) for j in range(N_DEV)]


def _rms_mod(x, g, shift, scale):
    rstd = lax.rsqrt(jnp.mean(x * x, axis=-1, keepdims=True) + EPS)
    xn = x * rstd
    return xn, rstd, (xn * g) * (1.0 + scale) + shift


def _rms_mod_bwd(dh, xn, rstd, g, scale):
    dxn = dh * (1.0 + scale) * g
    dx = rstd * (dxn - xn * jnp.mean(dxn * xn, axis=-1, keepdims=True))
    return dx, _colsum(dh), _colsum(dh * (xn * g)), _colsum(dh * (1.0 + scale) * xn)


def _gmlp_s(vn_b, wmask_b, bexp, s_scr, tm):
    for ch in range(tm // CHUNK):
        rows = slice(ch * CHUNK, (ch + 1) * CHUNK)
        for hh in range(HEADS):
            cols = slice(hh * HEAD_DIM, (hh + 1) * HEAD_DIM)
            s_scr[rows, cols] = _dot(wmask_b[hh], vn_b[rows, cols]) + bexp[:, cols]
    return s_scr[...]


def _pool_p(xb, xbext, pos, tm):
    ps = []
    for g, win in enumerate(POOL_WINDOWS):
        cols = slice(g * HEAD_DIM, (g + 1) * HEAD_DIM)
        acc = xb[:, cols]
        for k in range(1, win):
            acc = acc + xbext[pl.ds(POOL_HALO - k, tm), cols]
        cnt = jnp.minimum(pos + 1, win).astype(F32)
        ps.append(acc / cnt - xb[:, cols])
    return ps


def _mixer_fwd(x, modl, g_mix, ln_g, ln_b, w_s, bexp, pool_w, pool_scale, conv_w, win, wpa, wpb, wpc, wo, layer, seq):
    t_all, d = x.shape
    tm = min(TOKEN_TILE, seq)
    n_seq_tiles = seq // tm
    blk = win.shape[-1]

    def body(x_ref, mod_ref, gmix_ref, lng_ref, lnb_ref, ws_ref, bexp_ref, pw_ref, ps_ref, cw_ref,
             win_hbm, wpa_hbm, wpb_hbm, wpc_hbm, wo_hbm,
             h_ref, z_ref, ycat_ref, ocat_ref, mrg_ref, mo_ref, xmid_ref,
             win_v, wpa_v, wpb_v, wpc_v, wo_v, xbext, zcext, s_scr, sem):
        i = pl.program_id(0)
        pairs = [(win_hbm.at[layer], win_v), (wo_hbm.at[layer], wo_v)]
        pairs += _wp_pairs(wpa_hbm, wpa_v, layer) + _wp_pairs(wpb_hbm, wpb_v, layer) + _wp_pairs(wpc_hbm, wpc_v, layer)
        _load_weights(i, pairs, sem)
        tile_in_seq = i % n_seq_tiles

        @pl.when(tile_in_seq == 0)
        def _():
            xbext[0:POOL_HALO, :] = jnp.zeros((POOL_HALO, BR_W), F32)
            zcext[0:CONV_HALO, :] = jnp.zeros((CONV_HALO, BR_W), F32)

        x_t = x_ref[...]
        shift1, scale1, gate1 = mod_ref[0, 0:1, :], mod_ref[0, 1:2, :], mod_ref[0, 2:3, :]
        _, _, h = _rms_mod(x_t, gmix_ref[...], shift1, scale1)
        hb = h.astype(BF16)
        h_ref[...] = hb
        def project(j):
            zj = _dot(hb, win_v[j])
            z_ref[:, j * blk:(j + 1) * blk] = zj.astype(BF16)
            return zj

        z0, z1 = project(0), project(1)
        u = z0[:, 0:BR_W]
        v = jnp.concatenate([z0[:, BR_W:blk], z1[:, 0:2 * BR_W - blk]], axis=1)
        xb = z1[:, 2 * BR_W - blk:blk]

        gu, _ = _gelu(u)
        gv, _ = _gelu(v)
        mu = jnp.mean(gv, axis=-1, keepdims=True)
        cen = gv - mu
        rs = lax.rsqrt(jnp.mean(cen * cen, axis=-1, keepdims=True) + EPS)
        vn = (cen * rs) * lng_ref[...] + lnb_ref[...]
        mask = _tril_mask()
        wmask_b = [(ws_ref[hh] * mask).astype(BF16) for hh in range(HEADS)]
        s = _gmlp_s(vn.astype(BF16), wmask_b, bexp_ref[...], s_scr, tm)
        oa = (gu * s).astype(BF16)
        ya = _dot(oa, wpa_v[...])

        xbext[POOL_HALO:POOL_HALO + tm, :] = xb
        pos = tile_in_seq * tm + lax.broadcasted_iota(jnp.int32, (tm, 1), 0)
        ps = _pool_p(xb, xbext, pos, tm)
        qs = [_dot(ps[g].astype(BF16), pw_ref[g].astype(BF16)) for g in range(len(POOL_WINDOWS))]
        ob = (jnp.concatenate(qs, axis=1) * ps_ref[...]).astype(BF16)
        yb = _dot(ob, wpb_v[...])
        xbext[0:POOL_HALO, :] = xbext[tm:tm + POOL_HALO, :]

        z2, z3 = project(2), project(3)
        bg = z2[:, 0:BR_W]
        cg = jnp.concatenate([z2[:, BR_W:blk], z3[:, 0:2 * BR_W - blk]], axis=1)
        hc = z3[:, 2 * BR_W - blk:blk]
        zz = cg * hc
        zcext[CONV_HALO:CONV_HALO + tm, :] = zz
        yconv = (cw_ref[0:1, :] * zcext[pl.ds(CONV_HALO - 2, tm), :] + cw_ref[1:2, :] * zcext[pl.ds(CONV_HALO - 1, tm), :]
                 + cw_ref[2:3, :] * zz)
        oc = (bg * yconv).astype(BF16)
        yc = _dot(oc, wpc_v[...])
        zcext[0:CONV_HALO, :] = zcext[tm:tm + CONV_HALO, :]

        ocat_ref[:, 0:BR_W] = oa
        ocat_ref[:, BR_W:2 * BR_W] = ob
        ocat_ref[:, 2 * BR_W:3 * BR_W] = oc
        ycat_ref[:, 0:d] = ya.astype(BF16)
        ycat_ref[:, d:2 * d] = yb.astype(BF16)
        ycat_ref[:, 2 * d:3 * d] = yc.astype(BF16)

        merged = jnp.zeros((tm, d), F32)
        ys = (ya, yb, yc)
        zg = jnp.concatenate([project(j) for j in range(4, N_DEV)], axis=1)
        for n in range(3):
            merged = merged + _sigmoid(zg[:, n * d:(n + 1) * d]) * ys[n]
        mb = merged.astype(BF16)
        mrg_ref[...] = mb
        mo = _dot(mb, wo_v[...])
        mo_ref[...] = mo.astype(BF16)
        xmid_ref[...] = x_t + gate1 * mo

    tok = lambda cols: pl.BlockSpec((tm, cols), lambda i: (i, 0))
    const2 = lambda a: pl.BlockSpec(a.shape, lambda i: (0,) * a.ndim)
    in_specs = [
        tok(d),
        pl.BlockSpec((1, 8, d), lambda i: (i // n_seq_tiles, 0, 0)),
        const2(g_mix), const2(ln_g), const2(ln_b), const2(w_s), const2(bexp), const2(pool_w), const2(pool_scale),
        const2(conv_w), ANY, ANY, ANY, ANY, ANY,
    ]
    out_shape = [
        jax.ShapeDtypeStruct((t_all, d), BF16),
        jax.ShapeDtypeStruct((t_all, N_DEV * blk), BF16),
        jax.ShapeDtypeStruct((t_all, 3 * d), BF16),
        jax.ShapeDtypeStruct((t_all, 3 * BR_W), BF16),
        jax.ShapeDtypeStruct((t_all, d), BF16),
        jax.ShapeDtypeStruct((t_all, d), BF16),
        jax.ShapeDtypeStruct((t_all, d), F32),
    ]
    out_specs = [tok(d), tok(N_DEV * blk), tok(3 * d), tok(3 * BR_W), tok(d), tok(d), tok(d)]
    scratch = [
        pltpu.VMEM((N_DEV, d, blk), BF16), pltpu.VMEM((BR_W, d), BF16), pltpu.VMEM((BR_W, d), BF16),
        pltpu.VMEM((BR_W, d), BF16), pltpu.VMEM((d, d), BF16),
        pltpu.VMEM((tm + POOL_HALO, BR_W), F32), pltpu.VMEM((tm + CONV_HALO, BR_W), F32), pltpu.VMEM((tm, BR_W), F32),
        pltpu.SemaphoreType.DMA((2 + 3 * N_DEV,)),
    ]
    return pl.pallas_call(
        body, name=f"mixer_fwd_{layer}", grid=(t_all // tm,), in_specs=in_specs, out_specs=out_specs, out_shape=out_shape,
        scratch_shapes=scratch, compiler_params=_cparams(dimension_semantics=("arbitrary",)),
    )(x, modl, g_mix, ln_g, ln_b, w_s, bexp, pool_w, pool_scale, conv_w, win, wpa, wpb, wpc, wo)


def _ffn_fwd(x, modl, g_ffn, w13, w2, layer, seq):
    t_all, d = x.shape
    tm = min(TOKEN_TILE, seq)
    n_seq_tiles = seq // tm
    fb = w13.shape[-1]
    n_hid = N_DEV // 2

    def body(x_ref, mod_ref, g_ref, w13_hbm, w2_hbm, h_ref, ab_ref, hid_ref, f_ref, xo_ref, w13_v, w2_v, sem):
        i = pl.program_id(0)
        _load_weights(i, [(w13_hbm.at[layer], w13_v), (w2_hbm.at[layer], w2_v)], sem)
        x_t = x_ref[...]
        shift2, scale2, gate2 = mod_ref[0, 3:4, :], mod_ref[0, 4:5, :], mod_ref[0, 5:6, :]
        _, _, h = _rms_mod(x_t, g_ref[...], shift2, scale2)
        hb = h.astype(BF16)
        h_ref[...] = hb
        f = jnp.zeros((tm, d), F32)
        for k in range(n_hid):
            a = _dot(hb, w13_v[k])
            b = _dot(hb, w13_v[n_hid + k])
            ab_ref[k] = a.astype(BF16)
            ab_ref[n_hid + k] = b.astype(BF16)
            hid = ((a * _sigmoid(a)) * b).astype(BF16)
            hid_ref[k] = hid
            f = f + _dot(hid, w2_v[k * fb:(k + 1) * fb, :])
        f_ref[...] = f.astype(BF16)
        xo_ref[...] = x_t + gate2 * f

    tok = lambda cols: pl.BlockSpec((tm, cols), lambda i: (i, 0))
    blk3 = lambda n: pl.BlockSpec((n, tm, fb), lambda i: (0, i, 0))
    in_specs = [tok(d), pl.BlockSpec((1, 8, d), lambda i: (i // n_seq_tiles, 0, 0)),
                pl.BlockSpec(g_ffn.shape, lambda i: (0, 0)), ANY, ANY]
    out_shape = [
        jax.ShapeDtypeStruct((t_all, d), BF16),
        jax.ShapeDtypeStruct((N_DEV, t_all, fb), BF16),
        jax.ShapeDtypeStruct((n_hid, t_all, fb), BF16),
        jax.ShapeDtypeStruct((t_all, d), BF16),
        jax.ShapeDtypeStruct((t_all, d), F32),
    ]
    out_specs = [tok(d), blk3(N_DEV), blk3(n_hid), tok(d), tok(d)]
    scratch = [pltpu.VMEM((N_DEV, d, fb), BF16), pltpu.VMEM((n_hid * fb, d), BF16), pltpu.SemaphoreType.DMA((2,))]
    return pl.pallas_call(
        body, name=f"ffn_fwd_{layer}", grid=(t_all // tm,), in_specs=in_specs, out_specs=out_specs, out_shape=out_shape,
        scratch_shapes=scratch, compiler_params=_cparams(dimension_semantics=("arbitrary",)),
    )(x, modl, g_ffn, w13, w2)


def _final_loss(x, g_final, target, seq):
    t_all, d = x.shape
    tm = min(TOKEN_TILE, seq)

    def body(x_ref, g_ref, t_ref, dx_ref, loss_ref, dg_ref):
        i = pl.program_id(0)

        @pl.when(i == 0)
        def _():
            loss_ref[...] = jnp.zeros(loss_ref.shape, F32)
            dg_ref[...] = jnp.zeros(dg_ref.shape, F32)

        x_t = x_ref[...]
        g = g_ref[...]
        rstd = lax.rsqrt(jnp.mean(x_t * x_t, axis=-1, keepdims=True) + EPS)
        xn = x_t * rstd
        err = xn * g - t_ref[...]
        loss_ref[0:1, :] += _colsum(err * err) * (0.5 / d)
        dy = err * (1.0 / d)
        dg_ref[0:1, :] += _colsum(dy * xn)
        dxn = dy * g
        dx_ref[...] = rstd * (dxn - xn * jnp.mean(dxn * xn, axis=-1, keepdims=True))

        @pl.when(i == pl.num_programs(0) - 1)
        def _():
            loss_ref[...] = jnp.broadcast_to(jnp.sum(loss_ref[0:1, :], axis=1, keepdims=True), loss_ref.shape)

    tok = pl.BlockSpec((tm, d), lambda i: (i, 0))
    acc = pl.BlockSpec((8, d), lambda i: (0, 0))
    return pl.pallas_call(
        body, name="final_loss", grid=(t_all // tm,),
        in_specs=[tok, pl.BlockSpec((1, d), lambda i: (0, 0)), tok], out_specs=[tok, acc, acc],
        out_shape=[jax.ShapeDtypeStruct((t_all, d), F32), jax.ShapeDtypeStruct((8, d), F32), jax.ShapeDtypeStruct((8, d), F32)],
        compiler_params=_cparams(dimension_semantics=("arbitrary",)),
    )(x, g_final, target)


def _ffn_bwd(dxo, xmid, ab, f, modl, g_ffn, w13, w2, layer, seq):
    t_all, d = xmid.shape
    tm = min(TOKEN_TILE, seq)
    n_seq_tiles = seq // tm
    fb = w13.shape[-1]
    n_hid = N_DEV // 2

    def body(dxo_ref, x_ref, ab_ref, f_ref, mod_ref, g_ref, w13_hbm, w2_hbm,
             dx_ref, df_ref, dab_ref, dmod_ref, dg_ref, w13_v, w2_v, sem):
        i = pl.program_id(0)
        _load_weights(i, [(w13_hbm.at[layer], w13_v), (w2_hbm.at[layer], w2_v)], sem)

        @pl.when(i == 0)
        def _():
            dg_ref[...] = jnp.zeros(dg_ref.shape, F32)

        @pl.when(i % n_seq_tiles == 0)
        def _():
            dmod_ref[...] = jnp.zeros(dmod_ref.shape, F32)

        scale2, gate2 = mod_ref[0, 4:5, :], mod_ref[0, 5:6, :]
        g = g_ref[...]
        x_t = x_ref[...]
        rstd = lax.rsqrt(jnp.mean(x_t * x_t, axis=-1, keepdims=True) + EPS)
        xn = x_t * rstd
        dxo_t = dxo_ref[...]
        dmod_ref[0, 2:3, :] += _colsum(dxo_t * f_ref[...].astype(F32))
        dfb = (dxo_t * gate2).astype(BF16)
        df_ref[...] = dfb
        dh = jnp.zeros((tm, d), F32)
        for k in range(n_hid):
            dhid = _dot_nt(dfb, w2_v[k * fb:(k + 1) * fb, :])
            a = ab_ref[k].astype(F32)
            b = ab_ref[n_hid + k].astype(F32)
            sg = _sigmoid(a)
            da = (dhid * b * (sg * (1.0 + a * (1.0 - sg)))).astype(BF16)
            db = (dhid * (a * sg)).astype(BF16)
            dab_ref[k] = da
            dab_ref[n_hid + k] = db
            dh = dh + _dot_nt(da, w13_v[k]) + _dot_nt(db, w13_v[n_hid + k])
        dx, dshift, dscale, dg = _rms_mod_bwd(dh, xn, rstd, g, scale2)
        dmod_ref[0, 0:1, :] += dshift
        dmod_ref[0, 1:2, :] += dscale
        dg_ref[0:1, :] += dg
        dx_ref[...] = dxo_t + dx

    tok = lambda cols: pl.BlockSpec((tm, cols), lambda i: (i, 0))
    blk3 = lambda n: pl.BlockSpec((n, tm, fb), lambda i: (0, i, 0))
    modspec = pl.BlockSpec((1, 8, d), lambda i: (i // n_seq_tiles, 0, 0))
    in_specs = [tok(d), tok(d), blk3(N_DEV), tok(d), modspec, pl.BlockSpec(g_ffn.shape, lambda i: (0, 0)), ANY, ANY]
    out_shape = [
        jax.ShapeDtypeStruct((t_all, d), F32), jax.ShapeDtypeStruct((t_all, d), BF16),
        jax.ShapeDtypeStruct((N_DEV, t_all, fb), BF16), jax.ShapeDtypeStruct(modl.shape, F32),
        jax.ShapeDtypeStruct((8, d), F32),
    ]
    out_specs = [tok(d), tok(d), blk3(N_DEV), modspec, pl.BlockSpec((8, d), lambda i: (0, 0))]
    scratch = [pltpu.VMEM((N_DEV, d, fb), BF16), pltpu.VMEM((n_hid * fb, d), BF16), pltpu.SemaphoreType.DMA((2,))]
    return pl.pallas_call(
        body, name=f"ffn_bwd_{layer}", grid=(t_all // tm,), in_specs=in_specs, out_specs=out_specs, out_shape=out_shape,
        scratch_shapes=scratch, compiler_params=_cparams(dimension_semantics=("arbitrary",)),
    )(dxo, xmid, ab, f, modl, g_ffn, w13, w2)


def _mixer_bwd(dxm, x, z, ycat, mo, modl, g_mix, ln_g, ln_b, w_s, bexp, pool_w, pool_scale, conv_w,
               win, wpa, wpb, wpc, wo, layer, seq):
    t_all, d = x.shape
    tm = min(TOKEN_TILE, seq)
    n_seq_tiles = seq // tm
    blk = win.shape[-1]
    n_win = len(POOL_WINDOWS)

    def tile_of(i):
        return (i // n_seq_tiles) * n_seq_tiles + (n_seq_tiles - 1 - i % n_seq_tiles)

    def halo_row_block(i):
        return jnp.maximum(tile_of(i) * (tm // POOL_HALO) - 1, 0)

    def body(dxm_ref, x_ref, z_ref, zpb_ref, zpc_ref, ycat_ref, mo_ref, mod_ref, gmix_ref, lng_ref, lnb_ref, ws_ref,
             bexp_ref, pw_ref, ps_ref, cw_ref, win_hbm, wpa_hbm, wpb_hbm, wpc_hbm, wo_hbm,
             dx_ref, dz_ref, dycat_ref, dmo_ref, dmod_ref, dg_ref, sm_ref, dws_ref, dssum_ref, dpw_ref,
             win_v, wpa_v, wpb_v, wpc_v, wo_v, xbext, zzext, rext, dyext, s_scr, dvn_scr, sem):
        i = pl.program_id(0)
        pairs = [(win_hbm.at[layer], win_v), (wo_hbm.at[layer], wo_v)]
        pairs += _wp_pairs(wpa_hbm, wpa_v, layer) + _wp_pairs(wpb_hbm, wpb_v, layer) + _wp_pairs(wpc_hbm, wpc_v, layer)
        _load_weights(i, pairs, sem)
        tile_in_seq = n_seq_tiles - 1 - i % n_seq_tiles
        first_of_seq = tile_in_seq == 0

        @pl.when(i == 0)
        def _():
            for r in (dg_ref, sm_ref, dws_ref, dssum_ref, dpw_ref):
                r[...] = jnp.zeros(r.shape, F32)

        @pl.when(i % n_seq_tiles == 0)
        def _():
            dmod_ref[...] = jnp.zeros(dmod_ref.shape, F32)
            rext[tm:tm + POOL_HALO, :] = jnp.zeros((POOL_HALO, BR_W), F32)
            dyext[tm:tm + CONV_HALO, :] = jnp.zeros((CONV_HALO, BR_W), F32)

        shift1, scale1, gate1 = mod_ref[0, 0:1, :], mod_ref[0, 1:2, :], mod_ref[0, 2:3, :]
        g = gmix_ref[...]
        x_t = x_ref[...]
        rstd = lax.rsqrt(jnp.mean(x_t * x_t, axis=-1, keepdims=True) + EPS)
        xn = x_t * rstd
        dxm_t = dxm_ref[...]
        dmod_ref[0, 2:3, :] += _colsum(dxm_t * mo_ref[...].astype(F32))
        dmo = (dxm_t * gate1).astype(BF16)
        dmo_ref[...] = dmo
        dmerged = _dot_nt(dmo, wo_v[...])

        dys = []
        for n in range(3):
            zg = z_ref[:, 3 * d + n * d:3 * d + (n + 1) * d].astype(F32)
            gt = _sigmoid(zg)
            yn = ycat_ref[:, n * d:(n + 1) * d].astype(F32)
            dz_ref[:, 3 * d + n * d:3 * d + (n + 1) * d] = (dmerged * yn * gt * (1.0 - gt)).astype(BF16)
            dyn = (dmerged * gt).astype(BF16)
            dycat_ref[:, n * d:(n + 1) * d] = dyn
            dys.append(dyn)
        dh = jnp.zeros((tm, d), F32)
        for j in range(4, N_DEV):
            dh = dh + _dot_nt(dz_ref[:, j * blk:(j + 1) * blk], win_v[j])

        doa = _dot_nt(dys[0], wpa_v[...])
        u = z_ref[:, 0:BR_W].astype(F32)
        v = z_ref[:, BR_W:2 * BR_W].astype(F32)
        gu, tu = _gelu(u)
        gv, tv = _gelu(v)
        mu = jnp.mean(gv, axis=-1, keepdims=True)
        cen = gv - mu
        rs = lax.rsqrt(jnp.mean(cen * cen, axis=-1, keepdims=True) + EPS)
        vhat = cen * rs
        lng = lng_ref[...]
        vn_b = (vhat * lng + lnb_ref[...]).astype(BF16)
        mask = _tril_mask()
        wmask = [ws_ref[hh] * mask for hh in range(HEADS)]
        s = _gmlp_s(vn_b, [w.astype(BF16) for w in wmask], bexp_ref[...], s_scr, tm)
        du = (doa * s) * _gelu_grad(u, tu)
        ds = doa * gu
        ds_b = ds.astype(BF16)
        dssum = jnp.zeros((CHUNK, BR_W), F32)
        for ch in range(tm // CHUNK):
            rows = slice(ch * CHUNK, (ch + 1) * CHUNK)
            dssum = dssum + ds[rows, :]
            for hh in range(HEADS):
                cols = slice(hh * HEAD_DIM, (hh + 1) * HEAD_DIM)
                dvn_scr[rows, cols] = _dot_tn(wmask[hh].astype(BF16), ds_b[rows, cols])
                dws_ref[hh] += _dot_nt(ds_b[rows, cols], vn_b[rows, cols]) * mask
        dssum_ref[...] += dssum
        dvn = dvn_scr[...]
        sm_ref[0:1, :] += _colsum(dvn * vhat)
        sm_ref[1:2, :] += _colsum(dvn)
        dvhat = dvn * lng
        dgv = rs * (dvhat - jnp.mean(dvhat, axis=-1, keepdims=True) - vhat * jnp.mean(dvhat * vhat, axis=-1, keepdims=True))
        dv = dgv * _gelu_grad(v, tv)
        dz_ref[:, 0:BR_W] = du.astype(BF16)
        dz_ref[:, BR_W:2 * BR_W] = dv.astype(BF16)

        dob = _dot_nt(dys[1], wpb_v[...])
        xb = z_ref[:, 2 * BR_W:3 * BR_W].astype(F32)
        xbext[0:POOL_HALO, :] = jnp.where(first_of_seq, 0.0, zpb_ref[...].astype(F32))
        xbext[POOL_HALO:POOL_HALO + tm, :] = xb
        pos = tile_in_seq * tm + lax.broadcasted_iota(jnp.int32, (tm, 1), 0)
        ps = _pool_p(xb, xbext, pos, tm)
        scale_b = ps_ref[...]
        dq = dob * scale_b
        qs, dps = [], []
        for gi, win_len in enumerate(POOL_WINDOWS):
            cols = slice(gi * HEAD_DIM, (gi + 1) * HEAD_DIM)
            pw_b = pw_ref[gi].astype(BF16)
            p_b = ps[gi].astype(BF16)
            dq_b = dq[:, cols].astype(BF16)
            qs.append(_dot(p_b, pw_b))
            dpw_ref[gi] += _dot_tn(p_b, dq_b)
            dp = _dot_nt(dq_b, pw_b)
            dps.append(dp)
            cnt = jnp.minimum(pos + 1, win_len).astype(F32)
            rext[0:tm, cols] = dp / cnt
        sm_ref[2:3, :] += _colsum(dob * jnp.concatenate(qs, axis=1))
        dxbs = []
        for gi, win_len in enumerate(POOL_WINDOWS):
            cols = slice(gi * HEAD_DIM, (gi + 1) * HEAD_DIM)
            acc = rext[0:tm, cols]
            for k in range(1, win_len):
                acc = acc + rext[pl.ds(k, tm), cols]
            dxbs.append(acc - dps[gi])
        dz_ref[:, 2 * BR_W:3 * BR_W] = jnp.concatenate(dxbs, axis=1).astype(BF16)
        rext[tm:tm + POOL_HALO, :] = rext[0:POOL_HALO, :]

        doc = _dot_nt(dys[2], wpc_v[...])
        bg = z_ref[:, 3 * BR_W:4 * BR_W].astype(F32)
        cg = z_ref[:, 4 * BR_W:5 * BR_W].astype(F32)
        hc = z_ref[:, 5 * BR_W:6 * BR_W].astype(F32)
        zz = cg * hc
        zprev = zpc_ref[POOL_HALO - CONV_HALO:POOL_HALO, :].astype(F32)
        zzext[0:CONV_HALO, :] = jnp.where(first_of_seq, 0.0, zprev[:, 0:BR_W] * zprev[:, BR_W:2 * BR_W])
        zzext[CONV_HALO:CONV_HALO + tm, :] = zz
        zm2 = zzext[pl.ds(CONV_HALO - 2, tm), :]
        zm1 = zzext[pl.ds(CONV_HALO - 1, tm), :]
        w0, w1, w2c = cw_ref[0:1, :], cw_ref[1:2, :], cw_ref[2:3, :]
        yconv = w0 * zm2 + w1 * zm1 + w2c * zz
        dyc = doc * bg
        sm_ref[3:4, :] += _colsum(dyc * zm2)
        sm_ref[4:5, :] += _colsum(dyc * zm1)
        sm_ref[5:6, :] += _colsum(dyc * zz)
        dyext[0:tm, :] = dyc
        dzz = w2c * dyc + w1 * dyext[pl.ds(1, tm), :] + w0 * dyext[pl.ds(2, tm), :]
        dyext[tm:tm + CONV_HALO, :] = dyext[0:CONV_HALO, :]
        dz_ref[:, 3 * BR_W:4 * BR_W] = (doc * yconv).astype(BF16)
        dz_ref[:, 4 * BR_W:5 * BR_W] = (dzz * hc).astype(BF16)
        dz_ref[:, 5 * BR_W:6 * BR_W] = (dzz * cg).astype(BF16)

        for j in range(4):
            dh = dh + _dot_nt(dz_ref[:, j * blk:(j + 1) * blk], win_v[j])
        dx, dshift, dscale, dg = _rms_mod_bwd(dh, xn, rstd, g, scale1)
        dmod_ref[0, 0:1, :] += dshift
        dmod_ref[0, 1:2, :] += dscale
        dg_ref[0:1, :] += dg
        dx_ref[...] = dxm_t + dx

    tok = lambda cols: pl.BlockSpec((tm, cols), lambda i: (tile_of(i), 0))
    const2 = lambda a: pl.BlockSpec(a.shape, lambda i: (0,) * a.ndim)
    modspec = pl.BlockSpec((1, 8, d), lambda i: (i // n_seq_tiles, 0, 0))
    in_specs = [
        tok(d), tok(d), tok(N_DEV * blk),
        pl.BlockSpec((POOL_HALO, BR_W), lambda i: (halo_row_block(i), 2)),
        pl.BlockSpec((POOL_HALO, 2 * BR_W), lambda i: (halo_row_block(i), 2)),
        tok(3 * d), tok(d), modspec,
        const2(g_mix), const2(ln_g), const2(ln_b), const2(w_s), const2(bexp), const2(pool_w), const2(pool_scale),
        const2(conv_w), ANY, ANY, ANY, ANY, ANY,
    ]
    acc = lambda shape: pl.BlockSpec(shape, lambda i: (0,) * len(shape))
    out_shape = [
        jax.ShapeDtypeStruct((t_all, d), F32), jax.ShapeDtypeStruct((t_all, N_DEV * blk), BF16),
        jax.ShapeDtypeStruct((t_all, 3 * d), BF16), jax.ShapeDtypeStruct((t_all, d), BF16),
        jax.ShapeDtypeStruct(modl.shape, F32), jax.ShapeDtypeStruct((8, d), F32), jax.ShapeDtypeStruct((8, BR_W), F32),
        jax.ShapeDtypeStruct((HEADS, CHUNK, CHUNK), F32), jax.ShapeDtypeStruct((CHUNK, BR_W), F32),
        jax.ShapeDtypeStruct((n_win, HEAD_DIM, HEAD_DIM), F32),
    ]
    out_specs = [tok(d), tok(N_DEV * blk), tok(3 * d), tok(d), modspec, acc((8, d)), acc((8, BR_W)),
                 acc((HEADS, CHUNK, CHUNK)), acc((CHUNK, BR_W)), acc((n_win, HEAD_DIM, HEAD_DIM))]
    scratch = [
        pltpu.VMEM((N_DEV, d, blk), BF16), pltpu.VMEM((BR_W, d), BF16), pltpu.VMEM((BR_W, d), BF16),
        pltpu.VMEM((BR_W, d), BF16), pltpu.VMEM((d, d), BF16),
        pltpu.VMEM((tm + POOL_HALO, BR_W), F32), pltpu.VMEM((tm + CONV_HALO, BR_W), F32),
        pltpu.VMEM((tm + POOL_HALO, BR_W), F32), pltpu.VMEM((tm + CONV_HALO, BR_W), F32),
        pltpu.VMEM((tm, BR_W), F32), pltpu.VMEM((tm, BR_W), F32),
        pltpu.SemaphoreType.DMA((2 + 3 * N_DEV,)),
    ]
    return pl.pallas_call(
        body, name=f"mixer_bwd_{layer}", grid=(t_all // tm,), in_specs=in_specs, out_specs=out_specs, out_shape=out_shape,
        scratch_shapes=scratch, compiler_params=_cparams(dimension_semantics=("arbitrary",)),
    )(dxm, x, z, z, z, ycat, mo, modl, g_mix, ln_g, ln_b, w_s, bexp, pool_w, pool_scale, conv_w, win, wpa, wpb, wpc, wo)


def _wgrad(a, b, a_spec, b_spec, out_struct, out_spec, grid_kn, tk, tn, split, name):
    t_all = a.shape[-2]
    tt = min(WGRAD_TOKENS, t_all)
    n_t = t_all // tt

    def body(a_ref, b_ref, o_ref, acc):
        t = pl.program_id(2)

        @pl.when(t == 0)
        def _():
            acc[...] = jnp.zeros(acc.shape, F32)

        acc[...] += _dot_tn(a_ref[...], b_ref[...])

        @pl.when(t == n_t - 1)
        def _():
            if split:
                for j in range(split):
                    w = tn // split
                    o_ref[j] = acc[:, j * w:(j + 1) * w].astype(o_ref.dtype)
            else:
                o_ref[...] = acc[...].astype(o_ref.dtype)

    return pl.pallas_call(
        body, name=name, grid=(*grid_kn, n_t), in_specs=[a_spec(tt), b_spec(tt)], out_specs=out_spec, out_shape=out_struct,
        scratch_shapes=[pltpu.VMEM((tk, tn), F32)],
        compiler_params=_cparams(dimension_semantics=("arbitrary", "arbitrary", "arbitrary")),
    )(a, b)


def _layer_wgrads(h, dz, mrg, dmo, ocat, dycat, h2, dab, hid, df, layer):
    d = h.shape[1]
    blk = dz.shape[1] // N_DEV
    fb = dab.shape[-1]
    n_hid = N_DEV // 2
    g_win = _wgrad(
        h, dz, lambda tt: pl.BlockSpec((tt, d), lambda k, n, t: (t, 0)), lambda tt: pl.BlockSpec((tt, blk), lambda k, n, t: (t, n)),
        jax.ShapeDtypeStruct((N_DEV, d, blk), BF16), pl.BlockSpec((None, d, blk), lambda k, n, t: (n, 0, 0)),
        (1, N_DEV), d, blk, 0, f"wgrad_in_{layer}")
    g_wo = _wgrad(
        mrg, dmo, lambda tt: pl.BlockSpec((tt, d), lambda k, n, t: (t, 0)), lambda tt: pl.BlockSpec((tt, d), lambda k, n, t: (t, 0)),
        jax.ShapeDtypeStruct((d, d), BF16), pl.BlockSpec((d, d), lambda k, n, t: (0, 0)), (1, 1), d, d, 0, f"wgrad_o_{layer}")
    g_wp = []
    for n, nm in enumerate("abc"):
        g_wp.append(_wgrad(
            ocat, dycat, lambda tt, n=n: pl.BlockSpec((tt, BR_W), lambda k, nn, t: (t, n)),
            lambda tt, n=n: pl.BlockSpec((tt, d), lambda k, nn, t: (t, n)),
            jax.ShapeDtypeStruct((N_DEV, BR_W, d // N_DEV), BF16),
            pl.BlockSpec((N_DEV, BR_W, d // N_DEV), lambda k, nn, t: (0, 0, 0)), (1, 1), BR_W, d, N_DEV, f"wgrad_p{nm}_{layer}"))
    g_w13 = _wgrad(
        h2, dab, lambda tt: pl.BlockSpec((tt, d), lambda k, n, t: (t, 0)),
        lambda tt: pl.BlockSpec((None, tt, fb), lambda k, n, t: (n, t, 0)),
        jax.ShapeDtypeStruct((N_DEV, d, fb), BF16), pl.BlockSpec((None, d, fb), lambda k, n, t: (n, 0, 0)),
        (1, N_DEV), d, fb, 0, f"wgrad_13_{layer}")
    g_w2 = _wgrad(
        hid, df, lambda tt: pl.BlockSpec((None, tt, fb), lambda k, n, t: (k, t, 0)),
        lambda tt: pl.BlockSpec((tt, d), lambda k, n, t: (t, 0)),
        jax.ShapeDtypeStruct((n_hid * fb, d), BF16), pl.BlockSpec((fb, d), lambda k, n, t: (k, 0)),
        (n_hid, 1), fb, d, 0, f"wgrad_2_{layer}")
    return [g_win, g_w13, g_w2.reshape(N_DEV, fb // 2, d), g_wo.reshape(N_DEV, d // N_DEV, d), *g_wp]


def _remote(src, dst, send_sem, recv_sem, dev):
    return pltpu.make_async_remote_copy(src_ref=src, dst_ref=dst, send_sem=send_sem, recv_sem=recv_sem, device_id=dev,
                                        device_id_type=MESH_ID)


def _gather_weights(shards):
    n = len(shards)

    def body(*refs):
        ins, outs = refs[:n], refs[n:2 * n]
        send_sems, recv_sems, local_sems = refs[2 * n:]
        me = _my_index()
        local = [pltpu.make_async_copy(ins[a], outs[a].at[:, me], local_sems.at[a]) for a in range(n)]
        for cp in local:
            cp.start()
        sends = []
        for k in range(1, N_DEV):
            dev, _ = _peer(k)
            for a in range(n):
                sends.append(_remote(ins[a], outs[a].at[:, me], send_sems.at[a, k - 1], recv_sems.at[a, k - 1], dev))
        for cp in sends:
            cp.start()
        for k in range(1, N_DEV):
            dev, idx = _peer(k)
            for a in range(n):
                _remote(ins[a], outs[a].at[:, idx], send_sems.at[a, k - 1], recv_sems.at[a, k - 1], dev).wait_recv()
        for cp in sends:
            cp.wait_send()
        for cp in local:
            cp.wait()

    out_shape = [jax.ShapeDtypeStruct((s.shape[0], N_DEV, *s.shape[1:]), s.dtype) for s in shards]
    return pl.pallas_call(
        body, name="gather_weights", in_specs=[ANY] * n, out_specs=[ANY] * n, out_shape=out_shape,
        scratch_shapes=[pltpu.SemaphoreType.DMA((n, N_DEV - 1)), pltpu.SemaphoreType.DMA((n, N_DEV - 1)),
                        pltpu.SemaphoreType.DMA((n,))],
        compiler_params=pltpu.CompilerParams(has_side_effects=True),
    )(*shards)


def _exchange_grads(partials):
    n_layers, n = len(partials), len(partials[0])
    flat = [p for layer in partials for p in layer]

    def body(*refs):
        ins, outs = refs[:n_layers * n], refs[n_layers * n:n_layers * n + n]
        send_sems, recv_sems, local_sems = refs[n_layers * n + n:]
        me = _my_index()
        local, sends = [], []
        for l in range(n_layers):
            for a in range(n):
                q = l * n + a
                local.append(pltpu.make_async_copy(ins[q].at[me], outs[a].at[me, l], local_sems.at[q]))
        for cp in local:
            cp.start()
        for k in range(1, N_DEV):
            dev, idx = _peer(k)
            for l in range(n_layers):
                for a in range(n):
                    q = l * n + a
                    sends.append(_remote(ins[q].at[idx], outs[a].at[me, l], send_sems.at[q, k - 1], recv_sems.at[q, k - 1], dev))
        for cp in sends:
            cp.start()
        for k in range(1, N_DEV):
            dev, idx = _peer(k)
            for l in range(n_layers):
                for a in range(n):
                    q = l * n + a
                    _remote(ins[q].at[idx], outs[a].at[idx, l], send_sems.at[q, k - 1], recv_sems.at[q, k - 1], dev).wait_recv()
        for cp in sends:
            cp.wait_send()
        for cp in local:
            cp.wait()

    out_shape = [jax.ShapeDtypeStruct((N_DEV, n_layers, *p.shape[1:]), p.dtype) for p in partials[0]]
    nq = n_layers * n
    return pl.pallas_call(
        body, name="exchange_grads", in_specs=[ANY] * nq, out_specs=[ANY] * n, out_shape=out_shape,
        scratch_shapes=[pltpu.SemaphoreType.DMA((nq, N_DEV - 1)), pltpu.SemaphoreType.DMA((nq, N_DEV - 1)),
                        pltpu.SemaphoreType.DMA((nq,))],
        compiler_params=pltpu.CompilerParams(has_side_effects=True),
    )(*flat)


def _pre(c_pad, conv_pad, w_mod, b_mod_mine):
    n_layers, d, blk = w_mod.shape

    def body(c_ref, conv_ref, wmod_ref, bmod_ref, cact_ref, mod_ref, convall_ref, cact_mine, msh, send_sems, recv_sems):
        me = _my_index()
        c = c_ref[...]
        cact_mine[...] = c * _sigmoid(c)
        cact_ref[me] = cact_mine[...]
        convall_ref[me] = conv_ref[...]
        sends = []
        for k in range(1, N_DEV):
            dev, _ = _peer(k)
            sends.append(_remote(cact_mine, cact_ref.at[me], send_sems.at[0, k - 1], recv_sems.at[0, k - 1], dev))
            sends.append(_remote(conv_ref, convall_ref.at[me], send_sems.at[1, k - 1], recv_sems.at[1, k - 1], dev))
        for cp in sends:
            cp.start()
        for k in range(1, N_DEV):
            dev, idx = _peer(k)
            _remote(cact_mine, cact_ref.at[idx], send_sems.at[0, k - 1], recv_sems.at[0, k - 1], dev).wait_recv()
            _remote(conv_ref, convall_ref.at[idx], send_sems.at[1, k - 1], recv_sems.at[1, k - 1], dev).wait_recv()
        for cp in sends:
            cp.wait_send()
        cact_b = cact_ref[...].reshape(N_DEV * 8, d).astype(BF16)
        for l in range(n_layers):
            m = _dot(cact_b, wmod_ref[l].astype(BF16)) + bmod_ref[l]
            msh[l] = m.reshape(N_DEV, 8, blk)
        mod_ref[me] = msh[:, me]
        sends = []
        for k in range(1, N_DEV):
            dev, idx = _peer(k)
            sends.append(_remote(msh.at[:, idx], mod_ref.at[me], send_sems.at[2, k - 1], recv_sems.at[2, k - 1], dev))
        for cp in sends:
            cp.start()
        for k in range(1, N_DEV):
            dev, idx = _peer(k)
            _remote(msh.at[:, idx], mod_ref.at[idx], send_sems.at[2, k - 1], recv_sems.at[2, k - 1], dev).wait_recv()
        for cp in sends:
            cp.wait_send()

    out_shape = [jax.ShapeDtypeStruct((N_DEV, 8, d), F32), jax.ShapeDtypeStruct((N_DEV, n_layers, 8, blk), F32),
                 jax.ShapeDtypeStruct((N_DEV, *conv_pad.shape), F32)]
    return pl.pallas_call(
        body, name="pre", in_specs=[VMEM_SPEC] * 4, out_specs=[VMEM_SPEC] * 3, out_shape=out_shape,
        scratch_shapes=[pltpu.VMEM((8, d), F32), pltpu.VMEM((n_layers, N_DEV, 8, blk), F32),
                        pltpu.SemaphoreType.DMA((3, N_DEV - 1)), pltpu.SemaphoreType.DMA((3, N_DEV - 1))],
        compiler_params=_cparams(has_side_effects=True),
    )(c_pad, conv_pad, w_mod, b_mod_mine)


def _small(buf, dmod_blocks, cact_all):
    rows = buf.shape[0]
    seg = rows // N_DEV
    _, n_layers, _, blk = dmod_blocks.shape
    d = cact_all.shape[-1]

    def body(buf_ref, dmod_ref, cact_ref, out_ref, gw_ref, rs_recv, red, drecv, send_sems, recv_sems):
        me = _my_index()
        mine = pl.ds(pl.multiple_of(me * seg, 8), seg)
        sends = []
        for k in range(1, N_DEV):
            dev, idx = _peer(k)
            theirs = pl.ds(pl.multiple_of(idx * seg, 8), seg)
            sends.append(_remote(buf_ref.at[theirs], rs_recv.at[k - 1], send_sems.at[0, k - 1], recv_sems.at[0, k - 1], dev))
            sends.append(_remote(dmod_ref.at[idx], drecv.at[me], send_sems.at[1, k - 1], recv_sems.at[1, k - 1], dev))
        for cp in sends:
            cp.start()
        drecv[me] = dmod_ref[me]
        for k in range(1, N_DEV):
            dev, idx = _peer(k)
            _remote(buf_ref.at[mine], rs_recv.at[k - 1], send_sems.at[0, k - 1], recv_sems.at[0, k - 1], dev).wait_recv()
            _remote(dmod_ref.at[idx], drecv.at[idx], send_sems.at[1, k - 1], recv_sems.at[1, k - 1], dev).wait_recv()
        for cp in sends:
            cp.wait_send()
        total = buf_ref[mine, :]
        for k in range(1, N_DEV):
            total = total + rs_recv[k - 1]
        red[...] = total
        out_ref[mine, :] = total
        sends = []
        for k in range(1, N_DEV):
            dev, _ = _peer(k)
            sends.append(_remote(red, out_ref.at[mine], send_sems.at[2, k - 1], recv_sems.at[2, k - 1], dev))
        for cp in sends:
            cp.start()
        cact_b = cact_ref[...].reshape(N_DEV * 8, d).astype(BF16)
        for l in range(n_layers):
            gw_ref[l] = _dot_tn(cact_b, drecv[:, l].reshape(N_DEV * 8, blk).astype(BF16))
        for k in range(1, N_DEV):
            dev, idx = _peer(k)
            theirs = pl.ds(pl.multiple_of(idx * seg, 8), seg)
            _remote(red, out_ref.at[theirs], send_sems.at[2, k - 1], recv_sems.at[2, k - 1], dev).wait_recv()
        for cp in sends:
            cp.wait_send()

    out_shape = [jax.ShapeDtypeStruct(buf.shape, F32), jax.ShapeDtypeStruct((n_layers, d, blk), F32)]
    return pl.pallas_call(
        body, name="small_allreduce", in_specs=[VMEM_SPEC] * 3, out_specs=[VMEM_SPEC] * 2, out_shape=out_shape,
        scratch_shapes=[pltpu.VMEM((N_DEV - 1, seg, 128), F32), pltpu.VMEM((seg, 128), F32),
                        pltpu.VMEM(dmod_blocks.shape, F32),
                        pltpu.SemaphoreType.DMA((3, N_DEV - 1)), pltpu.SemaphoreType.DMA((3, N_DEV - 1))],
        compiler_params=_cparams(has_side_effects=True),
    )(buf, dmod_blocks, cact_all)


def _adamw_math(w, g, m, v):
    m = ADAM_B1 * m + (1.0 - ADAM_B1) * g
    v = ADAM_B2 * v + (1.0 - ADAM_B2) * (g * g)
    m_hat = m / (1.0 - ADAM_B1 ** ADAM_STEP)
    v_hat = v / (1.0 - ADAM_B2 ** ADAM_STEP)
    delta = -ADAM_LR * (m_hat / (jnp.sqrt(v_hat) + ADAM_EPS) + ADAM_WD * w)
    return delta, m, v


def _adamw(parts, w, m, v, name):
    n_parts, n_layers, rows, cols = parts.shape
    tr = rows
    while tr * cols * 4 > (1 << 20) and tr % 32 == 0:
        tr //= 2

    def body(p_ref, w_ref, m_ref, v_ref, g_out, d_out, m_out, v_out):
        g = p_ref[0].astype(F32)
        for p in range(1, n_parts):
            g = g + p_ref[p].astype(F32)
        delta, m_new, v_new = _adamw_math(w_ref[...], g, m_ref[...], v_ref[...])
        g_out[...] = g
        d_out[...] = delta
        m_out[...] = m_new
        v_out[...] = v_new

    spec = pl.BlockSpec((None, tr, cols), lambda l, r: (l, r, 0))
    pspec = pl.BlockSpec((n_parts, None, tr, cols), lambda l, r: (0, l, r, 0))
    out = jax.ShapeDtypeStruct((n_layers, rows, cols), F32)
    return pl.pallas_call(
        body, name=name, grid=(n_layers, rows // tr), in_specs=[pspec, spec, spec, spec], out_specs=[spec] * 4,
        out_shape=[out] * 4, compiler_params=_cparams(dimension_semantics=("arbitrary", "arbitrary")),
    )(parts, w, m, v)


def _adamw_flat(g, w, m, v):
    def body(g_ref, w_ref, m_ref, v_ref, d_out, m_out, v_out):
        delta, m_new, v_new = _adamw_math(w_ref[...], g_ref[...], m_ref[...], v_ref[...])
        d_out[...] = delta
        m_out[...] = m_new
        v_out[...] = v_new

    out = jax.ShapeDtypeStruct(g.shape, F32)
    return pl.pallas_call(body, name="adamw_small", in_specs=[VMEM_SPEC] * 4, out_specs=[VMEM_SPEC] * 3, out_shape=[out] * 3,
                          compiler_params=_cparams())(g, w, m, v)


def _pack(arrays, rows_multiple):
    flat = jnp.concatenate([a.reshape(-1) for a in arrays])
    per = 128 * rows_multiple
    total = -(-flat.shape[0] // per) * per
    return jnp.pad(flat, (0, total - flat.shape[0])).reshape(total // 128, 128)


def _unpack(buf, like):
    flat = buf.reshape(-1)
    out, off = [], 0
    for a in like:
        out.append(flat[off:off + a.size].reshape(a.shape))
        off += a.size
    return out


def kernel(x, c, w_mod, b_mod, g_mix, w_in, gm_ln_g, gm_ln_b, gm_w_s, gm_b_s, w_pa, pool_w, pool_scale, w_pb, conv_w, w_pc, w_o, g_ffn, w_13, w_2, g_final, loss_target, m_w_mod, m_b_mod, m_g_mix, m_w_in, m_gm_ln_g, m_gm_ln_b, m_gm_w_s, m_gm_b_s, m_w_pa, m_pool_w, m_pool_scale, m_w_pb, m_conv_w, m_w_pc, m_w_o, m_g_ffn, m_w_13, m_w_2, m_g_final, v_w_mod, v_b_mod, v_g_mix, v_w_in, v_gm_ln_g, v_gm_ln_b, v_gm_w_s, v_gm_b_s, v_w_pa, v_pool_w, v_pool_scale, v_w_pb, v_conv_w, v_w_pc, v_w_o, v_g_ffn, v_w_13, v_w_2, v_g_final):
    nb, seq, d = x.shape
    n_layers = w_in.shape[0]
    t_all = nb * seq
    blk = w_in.shape[-1]
    me = _my_index()
    conv_shard = conv_w.shape[-1]

    big = [w_in, w_13, w_2, w_o, w_pa, w_pb, w_pc]
    win_g, w13_g, w2_g, wo_g, wpa_g, wpb_g, wpc_g = _gather_weights([w.astype(BF16) for w in big])
    w2_g = w2_g.reshape(n_layers, N_DEV * w_2.shape[1], d)
    wo_g = wo_g.reshape(n_layers, N_DEV * w_o.shape[1], d)

    c_pad = jnp.pad(c, ((0, 8 - nb), (0, 0)))
    conv_pad = jnp.pad(conv_w.reshape(n_layers * 3, conv_shard), ((0, 16 - n_layers * 3), (0, 128 - conv_shard)))
    b_mod_mine = lax.dynamic_slice_in_dim(b_mod, me * blk, blk, axis=1).reshape(n_layers, 1, blk)
    cact_all, mod_blocks, conv_all = _pre(c_pad, conv_pad, w_mod, b_mod_mine)
    mod = jnp.transpose(mod_blocks, (1, 2, 0, 3)).reshape(n_layers, 8, N_MOD, d)[:, :nb]
    mod = jnp.pad(mod, ((0, 0), (0, 0), (0, 8 - N_MOD), (0, 0)))
    conv_full = jnp.transpose(conv_all[:, :n_layers * 3, :conv_shard].reshape(N_DEV, n_layers, 3, conv_shard), (1, 2, 0, 3))
    conv_full = jnp.pad(conv_full.reshape(n_layers, 3, N_DEV * conv_shard), ((0, 0), (0, 5), (0, 0)))
    bexp = jnp.repeat(jnp.transpose(gm_b_s, (0, 2, 1)), HEAD_DIM, axis=2)

    row = lambda a, l: a[l].reshape(1, -1)
    xs = x.reshape(t_all, d)
    saved = []
    for l in range(n_layers):
        h, z, ycat, ocat, mrg, mo, xmid = _mixer_fwd(
            xs, mod[l], row(g_mix, l), row(gm_ln_g, l), row(gm_ln_b, l), gm_w_s[l], bexp[l], pool_w[l], row(pool_scale, l),
            conv_full[l], win_g, wpa_g, wpb_g, wpc_g, wo_g, l, seq)
        h2, ab, hid, f, xo = _ffn_fwd(xmid, mod[l], row(g_ffn, l), w13_g, w2_g, l, seq)
        saved.append((xs, h, z, ycat, ocat, mrg, mo, xmid, h2, ab, hid, f))
        xs = xo

    dx, loss_blk, dgf_blk = _final_loss(xs, g_final.reshape(1, d), loss_target.reshape(t_all, d), seq)
    loss = lax.psum(loss_blk[0, 0], ("x", "y", "c"))

    partials = [None] * n_layers
    small_grads = [None] * n_layers
    dmods = [None] * n_layers
    for l in reversed(range(n_layers)):
        x_in, h, z, ycat, ocat, mrg, mo, xmid, h2, ab, hid, f = saved[l]
        dxm, df, dab, dmod2, dg_ffn = _ffn_bwd(dx, xmid, ab, f, mod[l], row(g_ffn, l), w13_g, w2_g, l, seq)
        dx, dz, dycat, dmo, dmod1, dg_mix, sm, dws, dssum, dpw = _mixer_bwd(
            dxm, x_in, z, ycat, mo, mod[l], row(g_mix, l), row(gm_ln_g, l), row(gm_ln_b, l), gm_w_s[l], bexp[l], pool_w[l],
            row(pool_scale, l), conv_full[l], win_g, wpa_g, wpb_g, wpc_g, wo_g, l, seq)
        partials[l] = _layer_wgrads(h, dz, mrg, dmo, ocat, dycat, h2, dab, hid, df, l)
        dmod = jnp.concatenate([dmod1[:, 0:3], dmod2[:, 0:3]], axis=1).reshape(nb, N_MOD * d)
        dmods[l] = dmod
        db_s = jnp.transpose(jnp.sum(dssum.reshape(CHUNK, HEADS, HEAD_DIM), axis=2))
        small_grads[l] = [jnp.sum(dmod, axis=0), dg_mix[0], sm[0], sm[1], dws, db_s, dpw, sm[2], sm[3:6], dg_ffn[0]]
    grad_x = dx.reshape(nb, seq, d)

    names = ["b_mod", "g_mix", "ln_g", "ln_b", "w_s", "b_s", "pool_w", "pool_scale", "conv_w", "g_ffn"]
    per_name = [jnp.stack([small_grads[l][n] for l in range(n_layers)]) for n in range(len(names))] + [dgf_blk[0]]
    buf = _pack(per_name, 8 * N_DEV)
    dmod_all = jnp.pad(jnp.stack(dmods), ((0, 0), (0, 8 - nb), (0, 0)))
    dmod_blocks = jnp.transpose(dmod_all.reshape(n_layers, 8, N_DEV, blk), (2, 0, 1, 3))
    red, grad_w_mod = _small(buf, dmod_blocks, cact_all)
    (g_b_mod, g_g_mix, g_ln_g, g_ln_b, g_w_s, g_b_s, g_pool_w, g_pool_scale, g_conv_full, g_g_ffn, g_g_final) = _unpack(red, per_name)
    g_conv = lax.dynamic_slice_in_dim(g_conv_full, me * conv_shard, conv_shard, axis=2)

    recv = _exchange_grads(partials)
    big_m = [m_w_in, m_w_13, m_w_2, m_w_o, m_w_pa, m_w_pb, m_w_pc]
    big_v = [v_w_in, v_w_13, v_w_2, v_w_o, v_w_pa, v_w_pb, v_w_pc]
    big_names = ["w_in", "w_13", "w_2", "w_o", "w_pa", "w_pb", "w_pc"]
    results = {}
    for parts, w, m, v, nm in zip(recv, big, big_m, big_v, big_names):
        results[nm] = _adamw(parts, w, m, v, f"adamw_{nm}")
    results["w_mod"] = _adamw(grad_w_mod[None], w_mod, m_w_mod, v_w_mod, "adamw_w_mod")

    small_w = [b_mod, g_mix, gm_ln_g, gm_ln_b, gm_w_s, gm_b_s, pool_w, pool_scale, conv_w, g_ffn, g_final]
    small_m = [m_b_mod, m_g_mix, m_gm_ln_g, m_gm_ln_b, m_gm_w_s, m_gm_b_s, m_pool_w, m_pool_scale, m_conv_w, m_g_ffn, m_g_final]
    small_v = [v_b_mod, v_g_mix, v_gm_ln_g, v_gm_ln_b, v_gm_w_s, v_gm_b_s, v_pool_w, v_pool_scale, v_conv_w, v_g_ffn, v_g_final]
    small_g = [g_b_mod, g_g_mix, g_ln_g, g_ln_b, g_w_s, g_b_s, g_pool_w, g_pool_scale, g_conv, g_g_ffn, g_g_final]
    small_names = ["b_mod", "g_mix", "gm_ln_g", "gm_ln_b", "gm_w_s", "gm_b_s", "pool_w", "pool_scale", "conv_w", "g_ffn", "g_final"]
    sd, sm_new, sv_new = _adamw_flat(_pack(small_g, 8), _pack(small_w, 8), _pack(small_m, 8), _pack(small_v, 8))
    sd, sm_new, sv_new = _unpack(sd, small_w), _unpack(sm_new, small_w), _unpack(sv_new, small_w)
    for n, nm in enumerate(small_names):
        results[nm] = (small_g[n], sd[n], sm_new[n], sv_new[n])

    order = ["w_mod", "b_mod", "g_mix", "w_in", "gm_ln_g", "gm_ln_b", "gm_w_s", "gm_b_s", "w_pa", "pool_w", "pool_scale",
             "w_pb", "conv_w", "w_pc", "w_o", "g_ffn", "w_13", "w_2", "g_final"]
    return (loss, grad_x, *[results[nm][0] for nm in order], *[results[nm][1] for nm in order],
            *[results[nm][2] for nm in order], *[results[nm][3] for nm in order])
```

```python
import functools

import jax
import jax.numpy as jnp
from jax import lax
from jax.experimental import pallas as pl
from jax.experimental.pallas import tpu as pltpu

F32 = jnp.float32
BF16 = jnp.bfloat16
MESH_ID = pl.DeviceIdType.MESH

N_DEV = 8
EPS = 1e-6
CHUNK = 128
HEADS = 4
HEAD_DIM = 128
BR_W = 512
POOL_WINDOWS = (2, 4, 8, 16)
POOL_HALO = 16
CONV_HALO = 8
N_MOD = 6
ADAM_LR = 0.001
ADAM_B1 = 0.9
ADAM_B2 = 0.999
ADAM_EPS = 1e-08
ADAM_WD = 0.01
ADAM_STEP = 10

TOKEN_TILE = 256
WGRAD_TOKENS = 2048
VMEM_LIMIT = 56 * 1024 * 1024
GELU_K = 0.7978845608028654
GELU_C = 0.044715

ANY = pl.BlockSpec(memory_space=pl.ANY)
VMEM_SPEC = pl.BlockSpec(memory_space=pltpu.VMEM)


def _cparams(**kw):
    return pltpu.CompilerParams(vmem_limit_bytes=VMEM_LIMIT, **kw)


def _dot(a, b):
    return jnp.dot(a, b, preferred_element_type=F32)


def _dot_nt(a, b):
    return lax.dot_general(a, b, (((1,), (1,)), ((), ())), preferred_element_type=F32)


def _dot_tn(a, b):
    return lax.dot_general(a, b, (((0,), (0,)), ((), ())), preferred_element_type=F32)


def _colsum(a):
    return jnp.sum(a, axis=0, keepdims=True)


def _sigmoid(x):
    return 1.0 / (1.0 + jnp.exp(-x))


def _gelu(x):
    t = jnp.tanh(GELU_K * (x + GELU_C * x * x * x))
    return 0.5 * x * (1.0 + t), t


def _gelu_grad(x, t):
    return 0.5 * (1.0 + t) + 0.5 * x * (1.0 - t * t) * (GELU_K * (1.0 + 3.0 * GELU_C * x * x))


def _tril_mask():
    r = lax.broadcasted_iota(jnp.int32, (CHUNK, CHUNK), 0)
    c = lax.broadcasted_iota(jnp.int32, (CHUNK, CHUNK), 1)
    return (r >= c).astype(F32)


def _my_index():
    return 4 * lax.axis_index("x") + 2 * lax.axis_index("y") + lax.axis_index("c")


def _peer(k):
    x, y, c = lax.axis_index("x"), lax.axis_index("y"), lax.axis_index("c")
    px = 1 - x if (k >> 2) & 1 else x
    py = 1 - y if (k >> 1) & 1 else y
    pc = 1 - c if k & 1 else c
    return (px, py, pc), 4 * px + 2 * py + pc


def _load_weights(step, pairs, sem):
    @pl.when(step == 0)
    def _():
        copies = [pltpu.make_async_copy(src, dst, sem.at[n]) for n, (src, dst) in enumerate(pairs)]
        for cp in copies:
            cp.start()
        for cp in copies:
            cp.wait()


def _wp_pairs(wp_hbm, wp_v):
    return [(wp_hbm.at[j], wp_v.at[:, pl.ds(HEAD_DIM * j, HEAD_DIM)]) for j in range(N_DEV)]


def _rms_mod(x, g, shift, scale):
    rstd = lax.rsqrt(jnp.mean(x * x, axis=-1, keepdims=True) + EPS)
    xn = x * rstd
    return xn, rstd, (xn * g) * (1.0 + scale) + shift


def _rms_mod_bwd(dh, xn, rstd, g, scale):
    dxn = dh * (1.0 + scale) * g
    dx = rstd * (dxn - xn * jnp.mean(dxn * xn, axis=-1, keepdims=True))
    return dx, _colsum(dh), _colsum(dh * (xn * g)), _colsum(dh * (1.0 + scale) * xn)


def _gmlp_s(vn_b, wmask_b, bexp, s_scr, tm):
    for ch in range(tm // CHUNK):
        rows = slice(ch * CHUNK, (ch + 1) * CHUNK)
        for hh in range(HEADS):
            cols = slice(hh * HEAD_DIM, (hh + 1) * HEAD_DIM)
            s_scr[rows, cols] = _dot(wmask_b[hh], vn_b[rows, cols]) + bexp[:, cols]
    return s_scr[...]


def _pool_p(xb, xbext, pos, tm):
    ps = []
    for g, win in enumerate(POOL_WINDOWS):
        cols = slice(g * HEAD_DIM, (g + 1) * HEAD_DIM)
        acc = xb[:, cols]
        for k in range(1, win):
            acc = acc + xbext[pl.ds(POOL_HALO - k, tm), cols]
        cnt = jnp.minimum(pos + 1, win).astype(F32)
        ps.append(acc / cnt - xb[:, cols])
    return ps


def _mixer_fwd(x, modl, g_mix, ln_g, ln_b, w_s, bexp, pool_w, pool_scale, conv_w, win, wpa, wpb, wpc, wo, layer, seq):
    t_all, d = x.shape
    tm = min(TOKEN_TILE, seq)
    n_seq_tiles = seq // tm
    blk = win.shape[-1]

    def body(x_ref, mod_ref, gmix_ref, lng_ref, lnb_ref, ws_ref, bexp_ref, pw_ref, ps_ref, cw_ref,
             win_hbm, wpa_hbm, wpb_hbm, wpc_hbm, wo_hbm,
             h_ref, z_ref, ycat_ref, ocat_ref, mrg_ref, mo_ref, xmid_ref,
             win_v, wpa_v, wpb_v, wpc_v, wo_v, xbext, zcext, s_scr, sem):
        i = pl.program_id(0)
        pairs = [(win_hbm, win_v), (wo_hbm, wo_v)]
        pairs += _wp_pairs(wpa_hbm, wpa_v) + _wp_pairs(wpb_hbm, wpb_v) + _wp_pairs(wpc_hbm, wpc_v)
        _load_weights(i, pairs, sem)
        tile_in_seq = i % n_seq_tiles

        @pl.when(tile_in_seq == 0)
        def _():
            xbext[0:POOL_HALO, :] = jnp.zeros((POOL_HALO, BR_W), F32)
            zcext[0:CONV_HALO, :] = jnp.zeros((CONV_HALO, BR_W), F32)

        x_t = x_ref[...]
        shift1, scale1, gate1 = mod_ref[0, 0:1, :], mod_ref[0, 1:2, :], mod_ref[0, 2:3, :]
        _, _, h = _rms_mod(x_t, gmix_ref[...], shift1, scale1)
        hb = h.astype(BF16)
        h_ref[...] = hb
        def project(j):
            zj = _dot(hb, win_v[j])
            z_ref[:, j * blk:(j + 1) * blk] = zj.astype(BF16)
            return zj

        z0, z1 = project(0), project(1)
        u = z0[:, 0:BR_W]
        v = jnp.concatenate([z0[:, BR_W:blk], z1[:, 0:2 * BR_W - blk]], axis=1)
        xb = z1[:, 2 * BR_W - blk:blk]

        gu, _ = _gelu(u)
        gv, _ = _gelu(v)
        mu = jnp.mean(gv, axis=-1, keepdims=True)
        cen = gv - mu
        rs = lax.rsqrt(jnp.mean(cen * cen, axis=-1, keepdims=True) + EPS)
        vn = (cen * rs) * lng_ref[...] + lnb_ref[...]
        mask = _tril_mask()
        wmask_b = [(ws_ref[hh] * mask).astype(BF16) for hh in range(HEADS)]
        s = _gmlp_s(vn.astype(BF16), wmask_b, bexp_ref[...], s_scr, tm)
        oa = (gu * s).astype(BF16)
        ya = _dot(oa, wpa_v[...])

        xbext[POOL_HALO:POOL_HALO + tm, :] = xb
        pos = tile_in_seq * tm + lax.broadcasted_iota(jnp.int32, (tm, 1), 0)
        ps = _pool_p(xb, xbext, pos, tm)
        qs = [_dot(ps[g].astype(BF16), pw_ref[g].astype(BF16)) for g in range(len(POOL_WINDOWS))]
        ob = (jnp.concatenate(qs, axis=1) * ps_ref[...]).astype(BF16)
        yb = _dot(ob, wpb_v[...])
        xbext[0:POOL_HALO, :] = xbext[tm:tm + POOL_HALO, :]

        z2, z3 = project(2), project(3)
        bg = z2[:, 0:BR_W]
        cg = jnp.concatenate([z2[:, BR_W:blk], z3[:, 0:2 * BR_W - blk]], axis=1)
        hc = z3[:, 2 * BR_W - blk:blk]
        zz = cg * hc
        zcext[CONV_HALO:CONV_HALO + tm, :] = zz
        yconv = (cw_ref[0:1, :] * zcext[pl.ds(CONV_HALO - 2, tm), :] + cw_ref[1:2, :] * zcext[pl.ds(CONV_HALO - 1, tm), :]
                 + cw_ref[2:3, :] * zz)
        oc = (bg * yconv).astype(BF16)
        yc = _dot(oc, wpc_v[...])
        zcext[0:CONV_HALO, :] = zcext[tm:tm + CONV_HALO, :]

        ocat_ref[:, 0:BR_W] = oa
        ocat_ref[:, BR_W:2 * BR_W] = ob
        ocat_ref[:, 2 * BR_W:3 * BR_W] = oc
        ycat_ref[:, 0:d] = ya.astype(BF16)
        ycat_ref[:, d:2 * d] = yb.astype(BF16)
        ycat_ref[:, 2 * d:3 * d] = yc.astype(BF16)

        merged = jnp.zeros((tm, d), F32)
        ys = (ya, yb, yc)
        zg = jnp.concatenate([project(j) for j in range(4, N_DEV)], axis=1)
        for n in range(3):
            merged = merged + _sigmoid(zg[:, n * d:(n + 1) * d]) * ys[n]
        mb = merged.astype(BF16)
        mrg_ref[...] = mb
        mo = _dot(mb, wo_v[...])
        mo_ref[...] = mo.astype(BF16)
        xmid_ref[...] = x_t + gate1 * mo

    tok = lambda cols: pl.BlockSpec((tm, cols), lambda i: (i, 0))
    const2 = lambda a: pl.BlockSpec(a.shape, lambda i: (0,) * a.ndim)
    in_specs = [
        tok(d),
        pl.BlockSpec((1, 8, d), lambda i: (i // n_seq_tiles, 0, 0)),
        const2(g_mix), const2(ln_g), const2(ln_b), const2(w_s), const2(bexp), const2(pool_w), const2(pool_scale),
        const2(conv_w), ANY, ANY, ANY, ANY, ANY,
    ]
    out_shape = [
        jax.ShapeDtypeStruct((t_all, d), BF16),
        jax.ShapeDtypeStruct((t_all, N_DEV * blk), BF16),
        jax.ShapeDtypeStruct((t_all, 3 * d), BF16),
        jax.ShapeDtypeStruct((t_all, 3 * BR_W), BF16),
        jax.ShapeDtypeStruct((t_all, d), BF16),
        jax.ShapeDtypeStruct((t_all, d), BF16),
        jax.ShapeDtypeStruct((t_all, d), F32),
    ]
    out_specs = [tok(d), tok(N_DEV * blk), tok(3 * d), tok(3 * BR_W), tok(d), tok(d), tok(d)]
    scratch = [
        pltpu.VMEM((N_DEV, d, blk), BF16), pltpu.VMEM((BR_W, d), BF16), pltpu.VMEM((BR_W, d), BF16),
        pltpu.VMEM((BR_W, d), BF16), pltpu.VMEM((d, d), BF16),
        pltpu.VMEM((tm + POOL_HALO, BR_W), F32), pltpu.VMEM((tm + CONV_HALO, BR_W), F32), pltpu.VMEM((tm, BR_W), F32),
        pltpu.SemaphoreType.DMA((2 + 3 * N_DEV,)),
    ]
    return pl.pallas_call(
        body, name=f"mixer_fwd_{layer}", grid=(t_all // tm,), in_specs=in_specs, out_specs=out_specs, out_shape=out_shape,
        scratch_shapes=scratch, compiler_params=_cparams(dimension_semantics=("arbitrary",)),
    )(x, modl, g_mix, ln_g, ln_b, w_s, bexp, pool_w, pool_scale, conv_w, win, wpa, wpb, wpc, wo)


def _ffn_fwd(x, modl, g_ffn, w13, w2, layer, seq):
    t_all, d = x.shape
    tm = min(TOKEN_TILE, seq)
    n_seq_tiles = seq // tm
    fb = w13.shape[-1]
    n_hid = N_DEV // 2

    def body(x_ref, mod_ref, g_ref, w13_hbm, w2_hbm, h_ref, ab_ref, hid_ref, f_ref, xo_ref, w13_v, w2_v, sem):
        i = pl.program_id(0)
        _load_weights(i, [(w13_hbm, w13_v), (w2_hbm, w2_v)], sem)
        x_t = x_ref[...]
        shift2, scale2, gate2 = mod_ref[0, 3:4, :], mod_ref[0, 4:5, :], mod_ref[0, 5:6, :]
        _, _, h = _rms_mod(x_t, g_ref[...], shift2, scale2)
        hb = h.astype(BF16)
        h_ref[...] = hb
        f = jnp.zeros((tm, d), F32)
        for k in range(n_hid):
            a = _dot(hb, w13_v[k])
            b = _dot(hb, w13_v[n_hid + k])
            ab_ref[k] = a.astype(BF16)
            ab_ref[n_hid + k] = b.astype(BF16)
            hid = ((a * _sigmoid(a)) * b).astype(BF16)
            hid_ref[k] = hid
            f = f + _dot(hid, w2_v[k * fb:(k + 1) * fb, :])
        f_ref[...] = f.astype(BF16)
        xo_ref[...] = x_t + gate2 * f

    tok = lambda cols: pl.BlockSpec((tm, cols), lambda i: (i, 0))
    blk3 = lambda n: pl.BlockSpec((n, tm, fb), lambda i: (0, i, 0))
    in_specs = [tok(d), pl.BlockSpec((1, 8, d), lambda i: (i // n_seq_tiles, 0, 0)),
                pl.BlockSpec(g_ffn.shape, lambda i: (0, 0)), ANY, ANY]
    out_shape = [
        jax.ShapeDtypeStruct((t_all, d), BF16),
        jax.ShapeDtypeStruct((N_DEV, t_all, fb), BF16),
        jax.ShapeDtypeStruct((n_hid, t_all, fb), BF16),
        jax.ShapeDtypeStruct((t_all, d), BF16),
        jax.ShapeDtypeStruct((t_all, d), F32),
    ]
    out_specs = [tok(d), blk3(N_DEV), blk3(n_hid), tok(d), tok(d)]
    scratch = [pltpu.VMEM((N_DEV, d, fb), BF16), pltpu.VMEM((n_hid * fb, d), BF16), pltpu.SemaphoreType.DMA((2,))]
    return pl.pallas_call(
        body, name=f"ffn_fwd_{layer}", grid=(t_all // tm,), in_specs=in_specs, out_specs=out_specs, out_shape=out_shape,
        scratch_shapes=scratch, compiler_params=_cparams(dimension_semantics=("arbitrary",)),
    )(x, modl, g_ffn, w13, w2)


def _final_loss(x, g_final, target, seq):
    t_all, d = x.shape
    tm = min(TOKEN_TILE, seq)

    def body(x_ref, g_ref, t_ref, dx_ref, loss_ref, dg_ref):
        i = pl.program_id(0)

        @pl.when(i == 0)
        def _():
            loss_ref[...] = jnp.zeros(loss_ref.shape, F32)
            dg_ref[...] = jnp.zeros(dg_ref.shape, F32)

        x_t = x_ref[...]
        g = g_ref[...]
        rstd = lax.rsqrt(jnp.mean(x_t * x_t, axis=-1, keepdims=True) + EPS)
        xn = x_t * rstd
        err = xn * g - t_ref[...]
        loss_ref[0:1, :] += _colsum(err * err) * (0.5 / d)
        dy = err * (1.0 / d)
        dg_ref[0:1, :] += _colsum(dy * xn)
        dxn = dy * g
        dx_ref[...] = rstd * (dxn - xn * jnp.mean(dxn * xn, axis=-1, keepdims=True))

        @pl.when(i == pl.num_programs(0) - 1)
        def _():
            loss_ref[...] = jnp.broadcast_to(jnp.sum(loss_ref[0:1, :], axis=1, keepdims=True), loss_ref.shape)

    tok = pl.BlockSpec((tm, d), lambda i: (i, 0))
    acc = pl.BlockSpec((8, d), lambda i: (0, 0))
    return pl.pallas_call(
        body, name="final_loss", grid=(t_all // tm,),
        in_specs=[tok, pl.BlockSpec((1, d), lambda i: (0, 0)), tok], out_specs=[tok, acc, acc],
        out_shape=[jax.ShapeDtypeStruct((t_all, d), F32), jax.ShapeDtypeStruct((8, d), F32), jax.ShapeDtypeStruct((8, d), F32)],
        compiler_params=_cparams(dimension_semantics=("arbitrary",)),
    )(x, g_final, target)


def _ffn_bwd(dxo, xmid, ab, f, modl, g_ffn, w13, w2, layer, seq):
    t_all, d = xmid.shape
    tm = min(TOKEN_TILE, seq)
    n_seq_tiles = seq // tm
    fb = w13.shape[-1]
    n_hid = N_DEV // 2

    def body(dxo_ref, x_ref, ab_ref, f_ref, mod_ref, g_ref, w13_hbm, w2_hbm,
             dx_ref, df_ref, dab_ref, dmod_ref, dg_ref, w13_v, w2_v, sem):
        i = pl.program_id(0)
        _load_weights(i, [(w13_hbm, w13_v), (w2_hbm, w2_v)], sem)

        @pl.when(i == 0)
        def _():
            dg_ref[...] = jnp.zeros(dg_ref.shape, F32)

        @pl.when(i % n_seq_tiles == 0)
        def _():
            dmod_ref[...] = jnp.zeros(dmod_ref.shape, F32)

        scale2, gate2 = mod_ref[0, 4:5, :], mod_ref[0, 5:6, :]
        g = g_ref[...]
        x_t = x_ref[...]
        rstd = lax.rsqrt(jnp.mean(x_t * x_t, axis=-1, keepdims=True) + EPS)
        xn = x_t * rstd
        dxo_t = dxo_ref[...]
        dmod_ref[0, 2:3, :] += _colsum(dxo_t * f_ref[...].astype(F32))
        dfb = (dxo_t * gate2).astype(BF16)
        df_ref[...] = dfb
        dh = jnp.zeros((tm, d), F32)
        for k in range(n_hid):
            dhid = _dot_nt(dfb, w2_v[k * fb:(k + 1) * fb, :])
            a = ab_ref[k].astype(F32)
            b = ab_ref[n_hid + k].astype(F32)
            sg = _sigmoid(a)
            da = (dhid * b * (sg * (1.0 + a * (1.0 - sg)))).astype(BF16)
            db = (dhid * (a * sg)).astype(BF16)
            dab_ref[k] = da
            dab_ref[n_hid + k] = db
            dh = dh + _dot_nt(da, w13_v[k]) + _dot_nt(db, w13_v[n_hid + k])
        dx, dshift, dscale, dg = _rms_mod_bwd(dh, xn, rstd, g, scale2)
        dmod_ref[0, 0:1, :] += dshift
        dmod_ref[0, 1:2, :] += dscale
        dg_ref[0:1, :] += dg
        dx_ref[...] = dxo_t + dx

    tok = lambda cols: pl.BlockSpec((tm, cols), lambda i: (i, 0))
    blk3 = lambda n: pl.BlockSpec((n, tm, fb), lambda i: (0, i, 0))
    modspec = pl.BlockSpec((1, 8, d), lambda i: (i // n_seq_tiles, 0, 0))
    in_specs = [tok(d), tok(d), blk3(N_DEV), tok(d), modspec, pl.BlockSpec(g_ffn.shape, lambda i: (0, 0)), ANY, ANY]
    out_shape = [
        jax.ShapeDtypeStruct((t_all, d), F32), jax.ShapeDtypeStruct((t_all, d), BF16),
        jax.ShapeDtypeStruct((N_DEV, t_all, fb), BF16), jax.ShapeDtypeStruct(modl.shape, F32),
        jax.ShapeDtypeStruct((8, d), F32),
    ]
    out_specs = [tok(d), tok(d), blk3(N_DEV), modspec, pl.BlockSpec((8, d), lambda i: (0, 0))]
    scratch = [pltpu.VMEM((N_DEV, d, fb), BF16), pltpu.VMEM((n_hid * fb, d), BF16), pltpu.SemaphoreType.DMA((2,))]
    return pl.pallas_call(
        body, name=f"ffn_bwd_{layer}", grid=(t_all // tm,), in_specs=in_specs, out_specs=out_specs, out_shape=out_shape,
        scratch_shapes=scratch, compiler_params=_cparams(dimension_semantics=("arbitrary",)),
    )(dxo, xmid, ab, f, modl, g_ffn, w13, w2)


def _mixer_bwd(dxm, x, z, ycat, mo, modl, g_mix, ln_g, ln_b, w_s, bexp, pool_w, pool_scale, conv_w,
               win, wpa, wpb, wpc, wo, layer, seq):
    t_all, d = x.shape
    tm = min(TOKEN_TILE, seq)
    n_seq_tiles = seq // tm
    blk = win.shape[-1]
    n_win = len(POOL_WINDOWS)

    def tile_of(i):
        return (i // n_seq_tiles) * n_seq_tiles + (n_seq_tiles - 1 - i % n_seq_tiles)

    def halo_row_block(i):
        return jnp.maximum(tile_of(i) * (tm // POOL_HALO) - 1, 0)

    def body(dxm_ref, x_ref, z_ref, zpb_ref, zpc_ref, ycat_ref, mo_ref, mod_ref, gmix_ref, lng_ref, lnb_ref, ws_ref,
             bexp_ref, pw_ref, ps_ref, cw_ref, win_hbm, wpa_hbm, wpb_hbm, wpc_hbm, wo_hbm,
             dx_ref, dz_ref, dycat_ref, dmo_ref, dmod_ref, dg_ref, sm_ref, dws_ref, dssum_ref, dpw_ref,
             win_v, wpa_v, wpb_v, wpc_v, wo_v, xbext, zzext, rext, dyext, s_scr, dvn_scr, sem):
        i = pl.program_id(0)
        pairs = [(win_hbm, win_v), (wo_hbm, wo_v)]
        pairs += _wp_pairs(wpa_hbm, wpa_v) + _wp_pairs(wpb_hbm, wpb_v) + _wp_pairs(wpc_hbm, wpc_v)
        _load_weights(i, pairs, sem)
        tile_in_seq = n_seq_tiles - 1 - i % n_seq_tiles
        first_of_seq = tile_in_seq == 0

        @pl.when(i == 0)
        def _():
            for r in (dg_ref, sm_ref, dws_ref, dssum_ref, dpw_ref):
                r[...] = jnp.zeros(r.shape, F32)

        @pl.when(i % n_seq_tiles == 0)
        def _():
            dmod_ref[...] = jnp.zeros(dmod_ref.shape, F32)
            rext[tm:tm + POOL_HALO, :] = jnp.zeros((POOL_HALO, BR_W), F32)
            dyext[tm:tm + CONV_HALO, :] = jnp.zeros((CONV_HALO, BR_W), F32)

        shift1, scale1, gate1 = mod_ref[0, 0:1, :], mod_ref[0, 1:2, :], mod_ref[0, 2:3, :]
        g = gmix_ref[...]
        x_t = x_ref[...]
        rstd = lax.rsqrt(jnp.mean(x_t * x_t, axis=-1, keepdims=True) + EPS)
        xn = x_t * rstd
        dxm_t = dxm_ref[...]
        dmod_ref[0, 2:3, :] += _colsum(dxm_t * mo_ref[...].astype(F32))
        dmo = (dxm_t * gate1).astype(BF16)
        dmo_ref[...] = dmo
        dmerged = _dot_nt(dmo, wo_v[...])

        dys = []
        for n in range(3):
            zg = z_ref[:, 3 * d + n * d:3 * d + (n + 1) * d].astype(F32)
            gt = _sigmoid(zg)
            yn = ycat_ref[:, n * d:(n + 1) * d].astype(F32)
            dz_ref[:, 3 * d + n * d:3 * d + (n + 1) * d] = (dmerged * yn * gt * (1.0 - gt)).astype(BF16)
            dyn = (dmerged * gt).astype(BF16)
            dycat_ref[:, n * d:(n + 1) * d] = dyn
            dys.append(dyn)
        dh = jnp.zeros((tm, d), F32)
        for j in range(4, N_DEV):
            dh = dh + _dot_nt(dz_ref[:, j * blk:(j + 1) * blk], win_v[j])

        doa = _dot_nt(dys[0], wpa_v[...])
        u = z_ref[:, 0:BR_W].astype(F32)
        v = z_ref[:, BR_W:2 * BR_W].astype(F32)
        gu, tu = _gelu(u)
        gv, tv = _gelu(v)
        mu = jnp.mean(gv, axis=-1, keepdims=True)
        cen = gv - mu
        rs = lax.rsqrt(jnp.mean(cen * cen, axis=-1, keepdims=True) + EPS)
        vhat = cen * rs
        lng = lng_ref[...]
        vn_b = (vhat * lng + lnb_ref[...]).astype(BF16)
        mask = _tril_mask()
        wmask = [ws_ref[hh] * mask for hh in range(HEADS)]
        s = _gmlp_s(vn_b, [w.astype(BF16) for w in wmask], bexp_ref[...], s_scr, tm)
        du = (doa * s) * _gelu_grad(u, tu)
        ds = doa * gu
        ds_b = ds.astype(BF16)
        dssum = jnp.zeros((CHUNK, BR_W), F32)
        for ch in range(tm // CHUNK):
            rows = slice(ch * CHUNK, (ch + 1) * CHUNK)
            dssum = dssum + ds[rows, :]
            for hh in range(HEADS):
                cols = slice(hh * HEAD_DIM, (hh + 1) * HEAD_DIM)
                dvn_scr[rows, cols] = _dot_tn(wmask[hh].astype(BF16), ds_b[rows, cols])
                dws_ref[hh] += _dot_nt(ds_b[rows, cols], vn_b[rows, cols]) * mask
        dssum_ref[...] += dssum
        dvn = dvn_scr[...]
        sm_ref[0:1, :] += _colsum(dvn * vhat)
        sm_ref[1:2, :] += _colsum(dvn)
        dvhat = dvn * lng
        dgv = rs * (dvhat - jnp.mean(dvhat, axis=-1, keepdims=True) - vhat * jnp.mean(dvhat * vhat, axis=-1, keepdims=True))
        dv = dgv * _gelu_grad(v, tv)
        dz_ref[:, 0:BR_W] = du.astype(BF16)
        dz_ref[:, BR_W:2 * BR_W] = dv.astype(BF16)

        dob = _dot_nt(dys[1], wpb_v[...])
        xb = z_ref[:, 2 * BR_W:3 * BR_W].astype(F32)
        xbext[0:POOL_HALO, :] = jnp.where(first_of_seq, 0.0, zpb_ref[...].astype(F32))
        xbext[POOL_HALO:POOL_HALO + tm, :] = xb
        pos = tile_in_seq * tm + lax.broadcasted_iota(jnp.int32, (tm, 1), 0)
        ps = _pool_p(xb, xbext, pos, tm)
        scale_b = ps_ref[...]
        dq = dob * scale_b
        qs, dps = [], []
        for gi, win_len in enumerate(POOL_WINDOWS):
            cols = slice(gi * HEAD_DIM, (gi + 1) * HEAD_DIM)
            pw_b = pw_ref[gi].astype(BF16)
            p_b = ps[gi].astype(BF16)
            dq_b = dq[:, cols].astype(BF16)
            qs.append(_dot(p_b, pw_b))
            dpw_ref[gi] += _dot_tn(p_b, dq_b)
            dp = _dot_nt(dq_b, pw_b)
            dps.append(dp)
            cnt = jnp.minimum(pos + 1, win_len).astype(F32)
            rext[0:tm, cols] = dp / cnt
        sm_ref[2:3, :] += _colsum(dob * jnp.concatenate(qs, axis=1))
        dxbs = []
        for gi, win_len in enumerate(POOL_WINDOWS):
            cols = slice(gi * HEAD_DIM, (gi + 1) * HEAD_DIM)
            acc = rext[0:tm, cols]
            for k in range(1, win_len):
                acc = acc + rext[pl.ds(k, tm), cols]
            dxbs.append(acc - dps[gi])
        dz_ref[:, 2 * BR_W:3 * BR_W] = jnp.concatenate(dxbs, axis=1).astype(BF16)
        rext[tm:tm + POOL_HALO, :] = rext[0:POOL_HALO, :]

        doc = _dot_nt(dys[2], wpc_v[...])
        bg = z_ref[:, 3 * BR_W:4 * BR_W].astype(F32)
        cg = z_ref[:, 4 * BR_W:5 * BR_W].astype(F32)
        hc = z_ref[:, 5 * BR_W:6 * BR_W].astype(F32)
        zz = cg * hc
        zprev = zpc_ref[POOL_HALO - CONV_HALO:POOL_HALO, :].astype(F32)
        zzext[0:CONV_HALO, :] = jnp.where(first_of_seq, 0.0, zprev[:, 0:BR_W] * zprev[:, BR_W:2 * BR_W])
        zzext[CONV_HALO:CONV_HALO + tm, :] = zz
        zm2 = zzext[pl.ds(CONV_HALO - 2, tm), :]
        zm1 = zzext[pl.ds(CONV_HALO - 1, tm), :]
        w0, w1, w2c = cw_ref[0:1, :], cw_ref[1:2, :], cw_ref[2:3, :]
        yconv = w0 * zm2 + w1 * zm1 + w2c * zz
        dyc = doc * bg
        sm_ref[3:4, :] += _colsum(dyc * zm2)
        sm_ref[4:5, :] += _colsum(dyc * zm1)
        sm_ref[5:6, :] += _colsum(dyc * zz)
        dyext[0:tm, :] = dyc
        dzz = w2c * dyc + w1 * dyext[pl.ds(1, tm), :] + w0 * dyext[pl.ds(2, tm), :]
        dyext[tm:tm + CONV_HALO, :] = dyext[0:CONV_HALO, :]
        dz_ref[:, 3 * BR_W:4 * BR_W] = (doc * yconv).astype(BF16)
        dz_ref[:, 4 * BR_W:5 * BR_W] = (dzz * hc).astype(BF16)
        dz_ref[:, 5 * BR_W:6 * BR_W] = (dzz * cg).astype(BF16)

        for j in range(4):
            dh = dh + _dot_nt(dz_ref[:, j * blk:(j + 1) * blk], win_v[j])
        dx, dshift, dscale, dg = _rms_mod_bwd(dh, xn, rstd, g, scale1)
        dmod_ref[0, 0:1, :] += dshift
        dmod_ref[0, 1:2, :] += dscale
        dg_ref[0:1, :] += dg
        dx_ref[...] = dxm_t + dx

    tok = lambda cols: pl.BlockSpec((tm, cols), lambda i: (tile_of(i), 0))
    const2 = lambda a: pl.BlockSpec(a.shape, lambda i: (0,) * a.ndim)
    modspec = pl.BlockSpec((1, 8, d), lambda i: (i // n_seq_tiles, 0, 0))
    in_specs = [
        tok(d), tok(d), tok(N_DEV * blk),
        pl.BlockSpec((POOL_HALO, BR_W), lambda i: (halo_row_block(i), 2)),
        pl.BlockSpec((POOL_HALO, 2 * BR_W), lambda i: (halo_row_block(i), 2)),
        tok(3 * d), tok(d), modspec,
        const2(g_mix), const2(ln_g), const2(ln_b), const2(w_s), const2(bexp), const2(pool_w), const2(pool_scale),
        const2(conv_w), ANY, ANY, ANY, ANY, ANY,
    ]
    acc = lambda shape: pl.BlockSpec(shape, lambda i: (0,) * len(shape))
    out_shape = [
        jax.ShapeDtypeStruct((t_all, d), F32), jax.ShapeDtypeStruct((t_all, N_DEV * blk), BF16),
        jax.ShapeDtypeStruct((t_all, 3 * d), BF16), jax.ShapeDtypeStruct((t_all, d), BF16),
        jax.ShapeDtypeStruct(modl.shape, F32), jax.ShapeDtypeStruct((8, d), F32), jax.ShapeDtypeStruct((8, BR_W), F32),
        jax.ShapeDtypeStruct((HEADS, CHUNK, CHUNK), F32), jax.ShapeDtypeStruct((CHUNK, BR_W), F32),
        jax.ShapeDtypeStruct((n_win, HEAD_DIM, HEAD_DIM), F32),
    ]
    out_specs = [tok(d), tok(N_DEV * blk), tok(3 * d), tok(d), modspec, acc((8, d)), acc((8, BR_W)),
                 acc((HEADS, CHUNK, CHUNK)), acc((CHUNK, BR_W)), acc((n_win, HEAD_DIM, HEAD_DIM))]
    scratch = [
        pltpu.VMEM((N_DEV, d, blk), BF16), pltpu.VMEM((BR_W, d), BF16), pltpu.VMEM((BR_W, d), BF16),
        pltpu.VMEM((BR_W, d), BF16), pltpu.VMEM((d, d), BF16),
        pltpu.VMEM((tm + POOL_HALO, BR_W), F32), pltpu.VMEM((tm + CONV_HALO, BR_W), F32),
        pltpu.VMEM((tm + POOL_HALO, BR_W), F32), pltpu.VMEM((tm + CONV_HALO, BR_W), F32),
        pltpu.VMEM((tm, BR_W), F32), pltpu.VMEM((tm, BR_W), F32),
        pltpu.SemaphoreType.DMA((2 + 3 * N_DEV,)),
    ]
    return pl.pallas_call(
        body, name=f"mixer_bwd_{layer}", grid=(t_all // tm,), in_specs=in_specs, out_specs=out_specs, out_shape=out_shape,
        scratch_shapes=scratch, compiler_params=_cparams(dimension_semantics=("arbitrary",)),
    )(dxm, x, z, z, z, ycat, mo, modl, g_mix, ln_g, ln_b, w_s, bexp, pool_w, pool_scale, conv_w, win, wpa, wpb, wpc, wo)


def _wgrad(a, b, a_spec, b_spec, out_struct, out_spec, grid_kn, tk, tn, split, name):
    t_all = a.shape[-2]
    tt = min(WGRAD_TOKENS, t_all)
    n_t = t_all // tt

    def body(a_ref, b_ref, o_ref, acc):
        t = pl.program_id(2)

        @pl.when(t == 0)
        def _():
            acc[...] = jnp.zeros(acc.shape, F32)

        acc[...] += _dot_tn(a_ref[...], b_ref[...])

        @pl.when(t == n_t - 1)
        def _():
            if split:
                for j in range(split):
                    w = tn // split
                    o_ref[j] = acc[:, j * w:(j + 1) * w].astype(o_ref.dtype)
            else:
                o_ref[...] = acc[...].astype(o_ref.dtype)

    return pl.pallas_call(
        body, name=name, grid=(*grid_kn, n_t), in_specs=[a_spec(tt), b_spec(tt)], out_specs=out_spec, out_shape=out_struct,
        scratch_shapes=[pltpu.VMEM((tk, tn), F32)],
        compiler_params=_cparams(dimension_semantics=("arbitrary", "arbitrary", "arbitrary")),
    )(a, b)


def _layer_wgrads(h, dz, mrg, dmo, ocat, dycat, h2, dab, hid, df, layer):
    d = h.shape[1]
    blk = dz.shape[1] // N_DEV
    fb = dab.shape[-1]
    n_hid = N_DEV // 2
    g_win = _wgrad(
        h, dz, lambda tt: pl.BlockSpec((tt, d), lambda k, n, t: (t, 0)), lambda tt: pl.BlockSpec((tt, blk), lambda k, n, t: (t, n)),
        jax.ShapeDtypeStruct((N_DEV, d, blk), BF16), pl.BlockSpec((None, d, blk), lambda k, n, t: (n, 0, 0)),
        (1, N_DEV), d, blk, 0, f"wgrad_in_{layer}")
    g_wo = _wgrad(
        mrg, dmo, lambda tt: pl.BlockSpec((tt, d), lambda k, n, t: (t, 0)), lambda tt: pl.BlockSpec((tt, d), lambda k, n, t: (t, 0)),
        jax.ShapeDtypeStruct((d, d), BF16), pl.BlockSpec((d, d), lambda k, n, t: (0, 0)), (1, 1), d, d, 0, f"wgrad_o_{layer}")
    g_wp = []
    for n, nm in enumerate("abc"):
        g_wp.append(_wgrad(
            ocat, dycat, lambda tt, n=n: pl.BlockSpec((tt, BR_W), lambda k, nn, t: (t, n)),
            lambda tt, n=n: pl.BlockSpec((tt, d), lambda k, nn, t: (t, n)),
            jax.ShapeDtypeStruct((N_DEV, BR_W, d // N_DEV), BF16),
            pl.BlockSpec((N_DEV, BR_W, d // N_DEV), lambda k, nn, t: (0, 0, 0)), (1, 1), BR_W, d, N_DEV, f"wgrad_p{nm}_{layer}"))
    g_w13 = _wgrad(
        h2, dab, lambda tt: pl.BlockSpec((tt, d), lambda k, n, t: (t, 0)),
        lambda tt: pl.BlockSpec((None, tt, fb), lambda k, n, t: (n, t, 0)),
        jax.ShapeDtypeStruct((N_DEV, d, fb), BF16), pl.BlockSpec((None, d, fb), lambda k, n, t: (n, 0, 0)),
        (1, N_DEV), d, fb, 0, f"wgrad_13_{layer}")
    g_w2 = _wgrad(
        hid, df, lambda tt: pl.BlockSpec((None, tt, fb), lambda k, n, t: (k, t, 0)),
        lambda tt: pl.BlockSpec((tt, d), lambda k, n, t: (t, 0)),
        jax.ShapeDtypeStruct((n_hid * fb, d), BF16), pl.BlockSpec((fb, d), lambda k, n, t: (k, 0)),
        (n_hid, 1), fb, d, 0, f"wgrad_2_{layer}")
    return [g_win, g_wo.reshape(N_DEV, d // N_DEV, d), *g_wp, g_w13, g_w2.reshape(N_DEV, fb // 2, d)]


def _remote(src, dst, send_sem, recv_sem, dev):
    return pltpu.make_async_remote_copy(src_ref=src, dst_ref=dst, send_sem=send_sem, recv_sem=recv_sem, device_id=dev,
                                        device_id_type=MESH_ID)


HBM_SPEC = pl.BlockSpec(memory_space=pltpu.HBM)
SEM_SPEC = pl.BlockSpec(memory_space=pltpu.SEMAPHORE)
DATAFLOW = pltpu.SideEffectType.DATAFLOW_SIDE_EFFECTING
GATHER, SCATTER = "gather", "scatter"


def _xfer_src(src, kind, peer_index):
    return src if kind == GATHER else src.at[peer_index]


def _xfer_sem(a, k):
    return a * (N_DEV - 1) + k - 1


def _place_own(srcs, kind, name):
    n = len(srcs)

    def body(*refs):
        ins, outs, sems = refs[:n], refs[n:2 * n], refs[2 * n]
        me = _my_index()
        copies = [pltpu.make_async_copy(_xfer_src(ins[a], kind, me), outs[a].at[me], sems.at[a]) for a in range(n)]
        for cp in copies:
            cp.start()
        for cp in copies:
            cp.wait()

    shard = lambda s: s.shape if kind == GATHER else s.shape[1:]
    out_shape = [jax.ShapeDtypeStruct((N_DEV, *shard(s)), s.dtype) for s in srcs]
    return pl.pallas_call(body, name=name, in_specs=[ANY] * n, out_specs=[ANY] * n, out_shape=out_shape,
                          scratch_shapes=[pltpu.SemaphoreType.DMA((n,))])(*srcs)


def _xfer_start(srcs, lands, kind, name):
    n = len(srcs)

    def body(*refs):
        src, land = refs[:n], refs[n:2 * n]
        send_sems, recv_sems = refs[2 * n], refs[2 * n + 1]
        token = refs[-1]
        me = _my_index()
        for k in range(1, N_DEV):
            dev, idx = _peer(k)
            for a in range(n):
                q = _xfer_sem(a, k)
                _remote(_xfer_src(src[a], kind, idx), land[a].at[me], send_sems.at[q], recv_sems.at[q], dev).start()
        token[...] = jnp.zeros(token.shape, token.dtype)

    both = [*srcs, *lands]
    sems = pltpu.SemaphoreType.DMA((n * (N_DEV - 1),))
    out_shape = (sems, sems, *[pltpu.HBM(a.shape, a.dtype) for a in both], jax.ShapeDtypeStruct((8, 128), F32))
    outs = pl.pallas_call(
        body, name=name, in_specs=[HBM_SPEC] * (2 * n), out_specs=(SEM_SPEC, SEM_SPEC, *[HBM_SPEC] * (2 * n), VMEM_SPEC),
        out_shape=out_shape, input_output_aliases={i: 2 + i for i in range(2 * n)},
        compiler_params=pltpu.CompilerParams(has_side_effects=DATAFLOW),
    )(*[pltpu.with_memory_space_constraint(a, pltpu.HBM) for a in both])
    return outs[0], outs[1], list(outs[2:2 + n]), list(outs[2 + n:2 + 2 * n]), outs[-1]


def _xfer_wait(send_sems, recv_sems, srcs, lands, rows, after, kind, name):
    n = len(srcs)

    def body(*refs):
        src, land = refs[:n], refs[n:2 * n]
        send_sems, recv_sems = refs[2 * n], refs[2 * n + 1]
        for k in range(1, N_DEV):
            dev, idx = _peer(k)
            for a in range(n):
                q = _xfer_sem(rows[a], k)
                cp = _remote(_xfer_src(src[a], kind, idx), land[a].at[idx], send_sems.at[q], recv_sems.at[q], dev)
                cp.wait_send()
                cp.wait_recv()

    both = [*srcs, *lands]
    outs = pl.pallas_call(
        body, name=name, in_specs=[HBM_SPEC] * (2 * n) + [SEM_SPEC, SEM_SPEC, ANY], out_specs=[HBM_SPEC] * (2 * n),
        out_shape=[pltpu.HBM(a.shape, a.dtype) for a in both], input_output_aliases={i: i for i in range(2 * n)},
        compiler_params=pltpu.CompilerParams(has_side_effects=DATAFLOW),
    )(*both, send_sems, recv_sems, after)
    return list(outs[n:])


def _pre(c_pad, conv_pad, w_mod, b_mod_mine):
    n_layers, d, blk = w_mod.shape

    def body(c_ref, conv_ref, wmod_ref, bmod_ref, cact_ref, mod_ref, convall_ref, cact_mine, msh, send_sems, recv_sems):
        me = _my_index()
        c = c_ref[...]
        cact_mine[...] = c * _sigmoid(c)
        cact_ref[me] = cact_mine[...]
        convall_ref[me] = conv_ref[...]
        sends = []
        for k in range(1, N_DEV):
            dev, _ = _peer(k)
            sends.append(_remote(cact_mine, cact_ref.at[me], send_sems.at[0, k - 1], recv_sems.at[0, k - 1], dev))
            sends.append(_remote(conv_ref, convall_ref.at[me], send_sems.at[1, k - 1], recv_sems.at[1, k - 1], dev))
        for cp in sends:
            cp.start()
        for k in range(1, N_DEV):
            dev, idx = _peer(k)
            _remote(cact_mine, cact_ref.at[idx], send_sems.at[0, k - 1], recv_sems.at[0, k - 1], dev).wait_recv()
            _remote(conv_ref, convall_ref.at[idx], send_sems.at[1, k - 1], recv_sems.at[1, k - 1], dev).wait_recv()
        for cp in sends:
            cp.wait_send()
        cact_b = cact_ref[...].reshape(N_DEV * 8, d).astype(BF16)
        for l in range(n_layers):
            m = _dot(cact_b, wmod_ref[l].astype(BF16)) + bmod_ref[l]
            msh[l] = m.reshape(N_DEV, 8, blk)
        mod_ref[me] = msh[:, me]
        sends = []
        for k in range(1, N_DEV):
            dev, idx = _peer(k)
            sends.append(_remote(msh.at[:, idx], mod_ref.at[me], send_sems.at[2, k - 1], recv_sems.at[2, k - 1], dev))
        for cp in sends:
            cp.start()
        for k in range(1, N_DEV):
            dev, idx = _peer(k)
            _remote(msh.at[:, idx], mod_ref.at[idx], send_sems.at[2, k - 1], recv_sems.at[2, k - 1], dev).wait_recv()
        for cp in sends:
            cp.wait_send()

    out_shape = [jax.ShapeDtypeStruct((N_DEV, 8, d), F32), jax.ShapeDtypeStruct((N_DEV, n_layers, 8, blk), F32),
                 jax.ShapeDtypeStruct((N_DEV, *conv_pad.shape), F32)]
    return pl.pallas_call(
        body, name="pre", in_specs=[VMEM_SPEC] * 4, out_specs=[VMEM_SPEC] * 3, out_shape=out_shape,
        scratch_shapes=[pltpu.VMEM((8, d), F32), pltpu.VMEM((n_layers, N_DEV, 8, blk), F32),
                        pltpu.SemaphoreType.DMA((3, N_DEV - 1)), pltpu.SemaphoreType.DMA((3, N_DEV - 1))],
        compiler_params=_cparams(has_side_effects=True),
    )(c_pad, conv_pad, w_mod, b_mod_mine)


def _small(buf, dmod_blocks, cact_all):
    rows = buf.shape[0]
    seg = rows // N_DEV
    _, n_layers, _, blk = dmod_blocks.shape
    d = cact_all.shape[-1]

    def body(buf_ref, dmod_ref, cact_ref, out_ref, gw_ref, rs_recv, red, drecv, send_sems, recv_sems):
        me = _my_index()
        mine = pl.ds(pl.multiple_of(me * seg, 8), seg)
        sends = []
        for k in range(1, N_DEV):
            dev, idx = _peer(k)
            theirs = pl.ds(pl.multiple_of(idx * seg, 8), seg)
            sends.append(_remote(buf_ref.at[theirs], rs_recv.at[k - 1], send_sems.at[0, k - 1], recv_sems.at[0, k - 1], dev))
            sends.append(_remote(dmod_ref.at[idx], drecv.at[me], send_sems.at[1, k - 1], recv_sems.at[1, k - 1], dev))
        for cp in sends:
            cp.start()
        drecv[me] = dmod_ref[me]
        for k in range(1, N_DEV):
            dev, idx = _peer(k)
            _remote(buf_ref.at[mine], rs_recv.at[k - 1], send_sems.at[0, k - 1], recv_sems.at[0, k - 1], dev).wait_recv()
            _remote(dmod_ref.at[idx], drecv.at[idx], send_sems.at[1, k - 1], recv_sems.at[1, k - 1], dev).wait_recv()
        for cp in sends:
            cp.wait_send()
        total = buf_ref[mine, :]
        for k in range(1, N_DEV):
            total = total + rs_recv[k - 1]
        red[...] = total
        out_ref[mine, :] = total
        sends = []
        for k in range(1, N_DEV):
            dev, _ = _peer(k)
            sends.append(_remote(red, out_ref.at[mine], send_sems.at[2, k - 1], recv_sems.at[2, k - 1], dev))
        for cp in sends:
            cp.start()
        cact_b = cact_ref[...].reshape(N_DEV * 8, d).astype(BF16)
        for l in range(n_layers):
            gw_ref[l] = _dot_tn(cact_b, drecv[:, l].reshape(N_DEV * 8, blk).astype(BF16))
        for k in range(1, N_DEV):
            dev, idx = _peer(k)
            theirs = pl.ds(pl.multiple_of(idx * seg, 8), seg)
            _remote(red, out_ref.at[theirs], send_sems.at[2, k - 1], recv_sems.at[2, k - 1], dev).wait_recv()
        for cp in sends:
            cp.wait_send()

    out_shape = [jax.ShapeDtypeStruct(buf.shape, F32), jax.ShapeDtypeStruct((n_layers, d, blk), F32)]
    return pl.pallas_call(
        body, name="small_allreduce", in_specs=[VMEM_SPEC] * 3, out_specs=[VMEM_SPEC] * 2, out_shape=out_shape,
        scratch_shapes=[pltpu.VMEM((N_DEV - 1, seg, 128), F32), pltpu.VMEM((seg, 128), F32),
                        pltpu.VMEM(dmod_blocks.shape, F32),
                        pltpu.SemaphoreType.DMA((3, N_DEV - 1)), pltpu.SemaphoreType.DMA((3, N_DEV - 1))],
        compiler_params=_cparams(has_side_effects=True),
    )(buf, dmod_blocks, cact_all)


def _adamw_math(w, g, m, v):
    m = ADAM_B1 * m + (1.0 - ADAM_B1) * g
    v = ADAM_B2 * v + (1.0 - ADAM_B2) * (g * g)
    m_hat = m / (1.0 - ADAM_B1 ** ADAM_STEP)
    v_hat = v / (1.0 - ADAM_B2 ** ADAM_STEP)
    delta = -ADAM_LR * (m_hat / (jnp.sqrt(v_hat) + ADAM_EPS) + ADAM_WD * w)
    return delta, m, v


def _adamw(parts, w, m, v, name):
    n_layers = len(parts)
    n_parts, rows, cols = parts[0].shape
    tr = rows
    while tr * cols * 4 > (1 << 20) and tr % 32 == 0:
        tr //= 2
    n_r = rows // tr

    def body(*refs):
        p_refs = refs[:n_layers]
        w_ref, m_ref, v_ref, g_out, d_out, m_out, v_out = refs[n_layers:]
        layer = pl.program_id(0)
        for q in range(n_layers):

            @pl.when(layer == q)
            def _(q=q):
                g = p_refs[q][0].astype(F32)
                for p in range(1, n_parts):
                    g = g + p_refs[q][p].astype(F32)
                delta, m_new, v_new = _adamw_math(w_ref[...], g, m_ref[...], v_ref[...])
                g_out[...] = g
                d_out[...] = delta
                m_out[...] = m_new
                v_out[...] = v_new

    def parts_spec(q):
        return pl.BlockSpec((n_parts, tr, cols), lambda l, r: (0, jnp.where(l == q, r, jnp.where(l < q, 0, n_r - 1)), 0))

    spec = pl.BlockSpec((None, tr, cols), lambda l, r: (l, r, 0))
    out = jax.ShapeDtypeStruct((n_layers, rows, cols), F32)
    return pl.pallas_call(
        body, name=name, grid=(n_layers, n_r), in_specs=[parts_spec(q) for q in range(n_layers)] + [spec, spec, spec],
        out_specs=[spec] * 4, out_shape=[out] * 4, compiler_params=_cparams(dimension_semantics=("arbitrary", "arbitrary")),
    )(*parts, w, m, v)


def _adamw_flat(g, w, m, v):
    def body(g_ref, w_ref, m_ref, v_ref, d_out, m_out, v_out):
        delta, m_new, v_new = _adamw_math(w_ref[...], g_ref[...], m_ref[...], v_ref[...])
        d_out[...] = delta
        m_out[...] = m_new
        v_out[...] = v_new

    out = jax.ShapeDtypeStruct(g.shape, F32)
    return pl.pallas_call(body, name="adamw_small", in_specs=[VMEM_SPEC] * 4, out_specs=[VMEM_SPEC] * 3, out_shape=[out] * 3,
                          compiler_params=_cparams())(g, w, m, v)


def _pack(arrays, rows_multiple):
    flat = jnp.concatenate([a.reshape(-1) for a in arrays])
    per = 128 * rows_multiple
    total = -(-flat.shape[0] // per) * per
    return jnp.pad(flat, (0, total - flat.shape[0])).reshape(total // 128, 128)


def _unpack(buf, like):
    flat = buf.reshape(-1)
    out, off = [], 0
    for a in like:
        out.append(flat[off:off + a.size].reshape(a.shape))
        off += a.size
    return out


def kernel(x, c, w_mod, b_mod, g_mix, w_in, gm_ln_g, gm_ln_b, gm_w_s, gm_b_s, w_pa, pool_w, pool_scale, w_pb, conv_w, w_pc, w_o, g_ffn, w_13, w_2, g_final, loss_target, m_w_mod, m_b_mod, m_g_mix, m_w_in, m_gm_ln_g, m_gm_ln_b, m_gm_w_s, m_gm_b_s, m_w_pa, m_pool_w, m_pool_scale, m_w_pb, m_conv_w, m_w_pc, m_w_o, m_g_ffn, m_w_13, m_w_2, m_g_final, v_w_mod, v_b_mod, v_g_mix, v_w_in, v_gm_ln_g, v_gm_ln_b, v_gm_w_s, v_gm_b_s, v_w_pa, v_pool_w, v_pool_scale, v_w_pb, v_conv_w, v_w_pc, v_w_o, v_g_ffn, v_w_13, v_w_2, v_g_final):
    nb, seq, d = x.shape
    n_layers = w_in.shape[0]
    t_all = nb * seq
    blk = w_in.shape[-1]
    me = _my_index()
    conv_shard = conv_w.shape[-1]

    big = [w_in, w_o, w_pa, w_pb, w_pc, w_13, w_2]
    n_big, n_mix = len(big), 5
    shards = [w[l].astype(BF16) for l in range(n_layers) for w in big]
    g_send, g_recv, shards, lands, token = _xfer_start(shards, _place_own(shards, GATHER, "gather_place"), GATHER, "gather_start")

    def gathered(l, lo, hi, after, name):
        rows = list(range(l * n_big + lo, l * n_big + hi))
        return _xfer_wait(g_send, g_recv, [shards[r] for r in rows], [lands[r] for r in rows], rows, after, GATHER, name)

    c_pad = jnp.pad(c, ((0, 8 - nb), (0, 0))) + token[0, 0]
    conv_pad = jnp.pad(conv_w.reshape(n_layers * 3, conv_shard), ((0, 16 - n_layers * 3), (0, 128 - conv_shard)))
    b_mod_mine = lax.dynamic_slice_in_dim(b_mod, me * blk, blk, axis=1).reshape(n_layers, 1, blk)
    cact_all, mod_blocks, conv_all = _pre(c_pad, conv_pad, w_mod, b_mod_mine)
    mod = jnp.transpose(mod_blocks, (1, 2, 0, 3)).reshape(n_layers, 8, N_MOD, d)[:, :nb]
    mod = jnp.pad(mod, ((0, 0), (0, 0), (0, 8 - N_MOD), (0, 0)))
    conv_full = jnp.transpose(conv_all[:, :n_layers * 3, :conv_shard].reshape(N_DEV, n_layers, 3, conv_shard), (1, 2, 0, 3))
    conv_full = jnp.pad(conv_full.reshape(n_layers, 3, N_DEV * conv_shard), ((0, 0), (0, 5), (0, 0)))
    bexp = jnp.repeat(jnp.transpose(gm_b_s, (0, 2, 1)), HEAD_DIM, axis=2)

    row = lambda a, l: a[l].reshape(1, -1)
    xs = x.reshape(t_all, d)
    saved, weights = [], []
    after = mod
    for l in range(n_layers):
        win_g, wo_g, wpa_g, wpb_g, wpc_g = gathered(l, 0, n_mix, after, f"gather_wait_mixer_{l}")
        wo_g = wo_g.reshape(d, d)
        h, z, ycat, ocat, mrg, mo, xmid = _mixer_fwd(
            xs, mod[l], row(g_mix, l), row(gm_ln_g, l), row(gm_ln_b, l), gm_w_s[l], bexp[l], pool_w[l], row(pool_scale, l),
            conv_full[l], win_g, wpa_g, wpb_g, wpc_g, wo_g, l, seq)
        w13_g, w2_g = gathered(l, n_mix, n_big, xmid, f"gather_wait_ffn_{l}")
        w2_g = w2_g.reshape(N_DEV * w_2.shape[1], d)
        h2, ab, hid, f, xo = _ffn_fwd(xmid, mod[l], row(g_ffn, l), w13_g, w2_g, l, seq)
        saved.append((xs, h, z, ycat, ocat, mrg, mo, xmid, h2, ab, hid, f))
        weights.append((win_g, wpa_g, wpb_g, wpc_g, wo_g, w13_g, w2_g))
        xs = after = xo

    dx, loss_blk, dgf_blk = _final_loss(xs, g_final.reshape(1, d), loss_target.reshape(t_all, d), seq)
    loss = lax.psum(loss_blk[0, 0], ("x", "y", "c"))

    in_flight = [None] * n_layers
    small_grads = [None] * n_layers
    dmods = [None] * n_layers
    tok = jnp.zeros((), F32)
    for l in reversed(range(n_layers)):
        x_in, h, z, ycat, ocat, mrg, mo, xmid, h2, ab, hid, f = saved[l]
        win_g, wpa_g, wpb_g, wpc_g, wo_g, w13_g, w2_g = weights[l]
        mod_l = mod[l] + tok
        dxm, df, dab, dmod2, dg_ffn = _ffn_bwd(dx, xmid, ab, f, mod_l, row(g_ffn, l), w13_g, w2_g, l, seq)
        dx, dz, dycat, dmo, dmod1, dg_mix, sm, dws, dssum, dpw = _mixer_bwd(
            dxm, x_in, z, ycat, mo, mod_l, row(g_mix, l), row(gm_ln_g, l), row(gm_ln_b, l), gm_w_s[l], bexp[l], pool_w[l],
            row(pool_scale, l), conv_full[l], win_g, wpa_g, wpb_g, wpc_g, wo_g, l, seq)
        partials = _layer_wgrads(h, dz, mrg, dmo, ocat, dycat, h2, dab, hid, df, l)
        in_flight[l] = _xfer_start(partials, _place_own(partials, SCATTER, f"grads_place_{l}"), SCATTER, f"grads_start_{l}")
        tok = in_flight[l][4][0, 0]
        dmod = jnp.concatenate([dmod1[:, 0:3], dmod2[:, 0:3]], axis=1).reshape(nb, N_MOD * d)
        dmods[l] = dmod
        db_s = jnp.transpose(jnp.sum(dssum.reshape(CHUNK, HEADS, HEAD_DIM), axis=2))
        small_grads[l] = [jnp.sum(dmod, axis=0), dg_mix[0], sm[0], sm[1], dws, db_s, dpw, sm[2], sm[3:6], dg_ffn[0]]
    grad_x = dx.reshape(nb, seq, d)

    names = ["b_mod", "g_mix", "ln_g", "ln_b", "w_s", "b_s", "pool_w", "pool_scale", "conv_w", "g_ffn"]
    per_name = [jnp.stack([small_grads[l][n] for l in range(n_layers)]) for n in range(len(names))] + [dgf_blk[0]]
    buf = _pack(per_name, 8 * N_DEV) + tok
    dmod_all = jnp.pad(jnp.stack(dmods), ((0, 0), (0, 8 - nb), (0, 0)))
    dmod_blocks = jnp.transpose(dmod_all.reshape(n_layers, 8, N_DEV, blk), (2, 0, 1, 3))
    red, grad_w_mod = _small(buf, dmod_blocks, cact_all)
    (g_b_mod, g_g_mix, g_ln_g, g_ln_b, g_w_s, g_b_s, g_pool_w, g_pool_scale, g_conv_full, g_g_ffn, g_g_final) = _unpack(red, per_name)
    g_conv = lax.dynamic_slice_in_dim(g_conv_full, me * conv_shard, conv_shard, axis=2)

    results = {}
    results["w_mod"] = _adamw([grad_w_mod[l][None] for l in range(n_layers)], w_mod, m_w_mod, v_w_mod, "adamw_w_mod")
    after = results["w_mod"][1]
    recv = [None] * n_layers
    for l in reversed(range(n_layers)):
        send_sems, recv_sems, srcs, zones, _ = in_flight[l]
        recv[l] = _xfer_wait(send_sems, recv_sems, srcs, zones, list(range(n_big)), after, SCATTER, f"grads_wait_{l}")
    big_m = [m_w_in, m_w_o, m_w_pa, m_w_pb, m_w_pc, m_w_13, m_w_2]
    big_v = [v_w_in, v_w_o, v_w_pa, v_w_pb, v_w_pc, v_w_13, v_w_2]
    big_names = ["w_in", "w_o", "w_pa", "w_pb", "w_pc", "w_13", "w_2"]
    for a, (w, m, v, nm) in enumerate(zip(big, big_m, big_v, big_names)):
        results[nm] = _adamw([recv[l][a] for l in range(n_layers)], w, m, v, f"adamw_{nm}")

    small_w = [b_mod, g_mix, gm_ln_g, gm_ln_b, gm_w_s, gm_b_s, pool_w, pool_scale, conv_w, g_ffn, g_final]
    small_m = [m_b_mod, m_g_mix, m_gm_ln_g, m_gm_ln_b, m_gm_w_s, m_gm_b_s, m_pool_w, m_pool_scale, m_conv_w, m_g_ffn, m_g_final]
    small_v = [v_b_mod, v_g_mix, v_gm_ln_g, v_gm_ln_b, v_gm_w_s, v_gm_b_s, v_pool_w, v_pool_scale, v_conv_w, v_g_ffn, v_g_final]
    small_g = [g_b_mod, g_g_mix, g_ln_g, g_ln_b, g_w_s, g_b_s, g_pool_w, g_pool_scale, g_conv, g_g_ffn, g_g_final]
    small_names = ["b_mod", "g_mix", "gm_ln_g", "gm_ln_b", "gm_w_s", "gm_b_s", "pool_w", "pool_scale", "conv_w", "g_ffn", "g_final"]
    sd, sm_new, sv_new = _adamw_flat(_pack(small_g, 8), _pack(small_w, 8), _pack(small_m, 8), _pack(small_v, 8))
    sd, sm_new, sv_new = _unpack(sd, small_w), _unpack(sm_new, small_w), _unpack(sv_new, small_w)
    for n, nm in enumerate(small_names):
        results[nm] = (small_g[n], sd[n], sm_new[n], sv_new[n])

    order = ["w_mod", "b_mod", "g_mix", "w_in", "gm_ln_g", "gm_ln_b", "gm_w_s", "gm_b_s", "w_pa", "pool_w", "pool_scale",
             "w_pb", "conv_w", "w_pc", "w_o", "g_ffn", "w_13", "w_2", "g_final"]
    return (loss, grad_x, *[results[nm][0] for nm in order], *[results[nm][1] for nm in order],
            *[results[nm][2] for nm in order], *[results[nm][3] for nm in order])
```

```python
import functools

import jax
import jax.numpy as jnp
from jax import lax
from jax.experimental import pallas as pl
from jax.experimental.pallas import tpu as pltpu

F32 = jnp.float32
BF16 = jnp.bfloat16
MESH_ID = pl.DeviceIdType.MESH

N_DEV = 8
EPS = 1e-6
CHUNK = 128
HEADS = 4
HEAD_DIM = 128
BR_W = 512
POOL_WINDOWS = (2, 4, 8, 16)
POOL_HALO = 16
CONV_HALO = 8
N_MOD = 6
ADAM_LR = 0.001
ADAM_B1 = 0.9
ADAM_B2 = 0.999
ADAM_EPS = 1e-08
ADAM_WD = 0.01
ADAM_STEP = 10

TOKEN_TILE = 256
WGRAD_TOKENS = 2048
VMEM_LIMIT = 56 * 1024 * 1024
GELU_K = 0.7978845608028654
GELU_C = 0.044715

ANY = pl.BlockSpec(memory_space=pl.ANY)
VMEM_SPEC = pl.BlockSpec(memory_space=pltpu.VMEM)


def _cparams(**kw):
    return pltpu.CompilerParams(vmem_limit_bytes=VMEM_LIMIT, **kw)


def _dot(a, b):
    return jnp.dot(a, b, preferred_element_type=F32)


def _dot_nt(a, b):
    return lax.dot_general(a, b, (((1,), (1,)), ((), ())), preferred_element_type=F32)


def _dot_tn(a, b):
    return lax.dot_general(a, b, (((0,), (0,)), ((), ())), preferred_element_type=F32)


def _colsum(a):
    return jnp.sum(a, axis=0, keepdims=True)


def _sigmoid(x):
    return 1.0 / (1.0 + jnp.exp(-x))


def _gelu(x):
    t = jnp.tanh(GELU_K * (x + GELU_C * x * x * x))
    return 0.5 * x * (1.0 + t), t


def _gelu_grad(x, t):
    return 0.5 * (1.0 + t) + 0.5 * x * (1.0 - t * t) * (GELU_K * (1.0 + 3.0 * GELU_C * x * x))


def _tril_mask():
    r = lax.broadcasted_iota(jnp.int32, (CHUNK, CHUNK), 0)
    c = lax.broadcasted_iota(jnp.int32, (CHUNK, CHUNK), 1)
    return (r >= c).astype(F32)


def _my_index():
    return 4 * lax.axis_index("x") + 2 * lax.axis_index("y") + lax.axis_index("c")


def _peer(k):
    x, y, c = lax.axis_index("x"), lax.axis_index("y"), lax.axis_index("c")
    px = 1 - x if (k >> 2) & 1 else x
    py = 1 - y if (k >> 1) & 1 else y
    pc = 1 - c if k & 1 else c
    return (px, py, pc), 4 * px + 2 * py + pc


def _load_weights(step, pairs, sem):
    @pl.when(step == 0)
    def _():
        copies = [pltpu.make_async_copy(src, dst, sem.at[n]) for n, (src, dst) in enumerate(pairs)]
        for cp in copies:
            cp.start()
        for cp in copies:
            cp.wait()


def _wp_pairs(wp_hbm, wp_v):
    return [(wp_hbm.at[j], wp_v.at[:, pl.ds(HEAD_DIM * j, HEAD_DIM)]) for j in range(N_DEV)]


def _rms_mod(x, g, shift, scale):
    rstd = lax.rsqrt(jnp.mean(x * x, axis=-1, keepdims=True) + EPS)
    xn = x * rstd
    return xn, rstd, (xn * g) * (1.0 + scale) + shift


def _rms_mod_bwd(dh, xn, rstd, g, scale):
    dxn = dh * (1.0 + scale) * g
    dx = rstd * (dxn - xn * jnp.mean(dxn * xn, axis=-1, keepdims=True))
    return dx, _colsum(dh), _colsum(dh * (xn * g)), _colsum(dh * (1.0 + scale) * xn)


def _gmlp_s(vn_b, wmask_b, bexp, s_scr, tm):
    for ch in range(tm // CHUNK):
        rows = slice(ch * CHUNK, (ch + 1) * CHUNK)
        for hh in range(HEADS):
            cols = slice(hh * HEAD_DIM, (hh + 1) * HEAD_DIM)
            s_scr[rows, cols] = _dot(wmask_b[hh], vn_b[rows, cols]) + bexp[:, cols]
    return s_scr[...]


def _pool_p(xb, xbext, pos, tm):
    ps = []
    for g, win in enumerate(POOL_WINDOWS):
        cols = slice(g * HEAD_DIM, (g + 1) * HEAD_DIM)
        acc = xb[:, cols]
        for k in range(1, win):
            acc = acc + xbext[pl.ds(POOL_HALO - k, tm), cols]
        cnt = jnp.minimum(pos + 1, win).astype(F32)
        ps.append(acc / cnt - xb[:, cols])
    return ps


def _mixer_fwd(x, modl, g_mix, ln_g, ln_b, w_s, bexp, pool_w, pool_scale, conv_w, win, wpa, wpb, wpc, wo, layer, seq):
    t_all, d = x.shape
    tm = min(TOKEN_TILE, seq)
    n_seq_tiles = seq // tm
    blk = win.shape[-1]

    def body(x_ref, mod_ref, gmix_ref, lng_ref, lnb_ref, ws_ref, bexp_ref, pw_ref, ps_ref, cw_ref,
             win_hbm, wpa_hbm, wpb_hbm, wpc_hbm, wo_hbm,
             h_ref, z_ref, ycat_ref, ocat_ref, mrg_ref, mo_ref, xmid_ref,
             win_v, wpa_v, wpb_v, wpc_v, wo_v, xbext, zcext, s_scr, sem):
        i = pl.program_id(0)
        pairs = [(win_hbm, win_v), (wo_hbm, wo_v)]
        pairs += _wp_pairs(wpa_hbm, wpa_v) + _wp_pairs(wpb_hbm, wpb_v) + _wp_pairs(wpc_hbm, wpc_v)
        _load_weights(i, pairs, sem)
        tile_in_seq = i % n_seq_tiles

        @pl.when(tile_in_seq == 0)
        def _():
            xbext[0:POOL_HALO, :] = jnp.zeros((POOL_HALO, BR_W), F32)
            zcext[0:CONV_HALO, :] = jnp.zeros((CONV_HALO, BR_W), F32)

        x_t = x_ref[...]
        shift1, scale1, gate1 = mod_ref[0, 0:1, :], mod_ref[0, 1:2, :], mod_ref[0, 2:3, :]
        _, _, h = _rms_mod(x_t, gmix_ref[...], shift1, scale1)
        hb = h.astype(BF16)
        h_ref[...] = hb
        def project(j):
            zj = _dot(hb, win_v[j])
            z_ref[:, j * blk:(j + 1) * blk] = zj.astype(BF16)
            return zj

        z0, z1 = project(0), project(1)
        u = z0[:, 0:BR_W]
        v = jnp.concatenate([z0[:, BR_W:blk], z1[:, 0:2 * BR_W - blk]], axis=1)
        xb = z1[:, 2 * BR_W - blk:blk]

        gu, _ = _gelu(u)
        gv, _ = _gelu(v)
        mu = jnp.mean(gv, axis=-1, keepdims=True)
        cen = gv - mu
        rs = lax.rsqrt(jnp.mean(cen * cen, axis=-1, keepdims=True) + EPS)
        vn = (cen * rs) * lng_ref[...] + lnb_ref[...]
        mask = _tril_mask()
        wmask_b = [(ws_ref[hh] * mask).astype(BF16) for hh in range(HEADS)]
        s = _gmlp_s(vn.astype(BF16), wmask_b, bexp_ref[...], s_scr, tm)
        oa = (gu * s).astype(BF16)
        ya = _dot(oa, wpa_v[...])

        xbext[POOL_HALO:POOL_HALO + tm, :] = xb
        pos = tile_in_seq * tm + lax.broadcasted_iota(jnp.int32, (tm, 1), 0)
        ps = _pool_p(xb, xbext, pos, tm)
        qs = [_dot(ps[g].astype(BF16), pw_ref[g].astype(BF16)) for g in range(len(POOL_WINDOWS))]
        ob = (jnp.concatenate(qs, axis=1) * ps_ref[...]).astype(BF16)
        yb = _dot(ob, wpb_v[...])
        xbext[0:POOL_HALO, :] = xbext[tm:tm + POOL_HALO, :]

        z2, z3 = project(2), project(3)
        bg = z2[:, 0:BR_W]
        cg = jnp.concatenate([z2[:, BR_W:blk], z3[:, 0:2 * BR_W - blk]], axis=1)
        hc = z3[:, 2 * BR_W - blk:blk]
        zz = cg * hc
        zcext[CONV_HALO:CONV_HALO + tm, :] = zz
        yconv = (cw_ref[0:1, :] * zcext[pl.ds(CONV_HALO - 2, tm), :] + cw_ref[1:2, :] * zcext[pl.ds(CONV_HALO - 1, tm), :]
                 + cw_ref[2:3, :] * zz)
        oc = (bg * yconv).astype(BF16)
        yc = _dot(oc, wpc_v[...])
        zcext[0:CONV_HALO, :] = zcext[tm:tm + CONV_HALO, :]

        ocat_ref[:, 0:BR_W] = oa
        ocat_ref[:, BR_W:2 * BR_W] = ob
        ocat_ref[:, 2 * BR_W:3 * BR_W] = oc
        ycat_ref[:, 0:d] = ya.astype(BF16)
        ycat_ref[:, d:2 * d] = yb.astype(BF16)
        ycat_ref[:, 2 * d:3 * d] = yc.astype(BF16)

        merged = jnp.zeros((tm, d), F32)
        ys = (ya, yb, yc)
        zg = jnp.concatenate([project(j) for j in range(4, N_DEV)], axis=1)
        for n in range(3):
            merged = merged + _sigmoid(zg[:, n * d:(n + 1) * d]) * ys[n]
        mb = merged.astype(BF16)
        mrg_ref[...] = mb
        mo = _dot(mb, wo_v[...])
        mo_ref[...] = mo.astype(BF16)
        xmid_ref[...] = x_t + gate1 * mo

    tok = lambda cols: pl.BlockSpec((tm, cols), lambda i: (i, 0))
    const2 = lambda a: pl.BlockSpec(a.shape, lambda i: (0,) * a.ndim)
    in_specs = [
        tok(d),
        pl.BlockSpec((1, 8, d), lambda i: (i // n_seq_tiles, 0, 0)),
        const2(g_mix), const2(ln_g), const2(ln_b), const2(w_s), const2(bexp), const2(pool_w), const2(pool_scale),
        const2(conv_w), ANY, ANY, ANY, ANY, ANY,
    ]
    out_shape = [
        jax.ShapeDtypeStruct((t_all, d), BF16),
        jax.ShapeDtypeStruct((t_all, N_DEV * blk), BF16),
        jax.ShapeDtypeStruct((t_all, 3 * d), BF16),
        jax.ShapeDtypeStruct((t_all, 3 * BR_W), BF16),
        jax.ShapeDtypeStruct((t_all, d), BF16),
        jax.ShapeDtypeStruct((t_all, d), BF16),
        jax.ShapeDtypeStruct((t_all, d), F32),
    ]
    out_specs = [tok(d), tok(N_DEV * blk), tok(3 * d), tok(3 * BR_W), tok(d), tok(d), tok(d)]
    scratch = [
        pltpu.VMEM((N_DEV, d, blk), BF16), pltpu.VMEM((BR_W, d), BF16), pltpu.VMEM((BR_W, d), BF16),
        pltpu.VMEM((BR_W, d), BF16), pltpu.VMEM((d, d), BF16),
        pltpu.VMEM((tm + POOL_HALO, BR_W), F32), pltpu.VMEM((tm + CONV_HALO, BR_W), F32), pltpu.VMEM((tm, BR_W), F32),
        pltpu.SemaphoreType.DMA((2 + 3 * N_DEV,)),
    ]
    return pl.pallas_call(
        body, name=f"mixer_fwd_{layer}", grid=(t_all // tm,), in_specs=in_specs, out_specs=out_specs, out_shape=out_shape,
        scratch_shapes=scratch, compiler_params=_cparams(dimension_semantics=("arbitrary",)),
    )(x, modl, g_mix, ln_g, ln_b, w_s, bexp, pool_w, pool_scale, conv_w, win, wpa, wpb, wpc, wo)


def _ffn_fwd(x, modl, g_ffn, w13, w2, layer, seq):
    t_all, d = x.shape
    tm = min(TOKEN_TILE, seq)
    n_seq_tiles = seq // tm
    fb = w13.shape[-1]
    n_hid = N_DEV // 2

    def body(x_ref, mod_ref, g_ref, w13_hbm, w2_hbm, h_ref, ab_ref, hid_ref, f_ref, xo_ref, w13_v, w2_v, sem):
        i = pl.program_id(0)
        _load_weights(i, [(w13_hbm, w13_v), (w2_hbm, w2_v)], sem)
        x_t = x_ref[...]
        shift2, scale2, gate2 = mod_ref[0, 3:4, :], mod_ref[0, 4:5, :], mod_ref[0, 5:6, :]
        _, _, h = _rms_mod(x_t, g_ref[...], shift2, scale2)
        hb = h.astype(BF16)
        h_ref[...] = hb
        f = jnp.zeros((tm, d), F32)
        for k in range(n_hid):
            a = _dot(hb, w13_v[k])
            b = _dot(hb, w13_v[n_hid + k])
            ab_ref[k] = a.astype(BF16)
            ab_ref[n_hid + k] = b.astype(BF16)
            hid = ((a * _sigmoid(a)) * b).astype(BF16)
            hid_ref[k] = hid
            f = f + _dot(hid, w2_v[k * fb:(k + 1) * fb, :])
        f_ref[...] = f.astype(BF16)
        xo_ref[...] = x_t + gate2 * f

    tok = lambda cols: pl.BlockSpec((tm, cols), lambda i: (i, 0))
    blk3 = lambda n: pl.BlockSpec((n, tm, fb), lambda i: (0, i, 0))
    in_specs = [tok(d), pl.BlockSpec((1, 8, d), lambda i: (i // n_seq_tiles, 0, 0)),
                pl.BlockSpec(g_ffn.shape, lambda i: (0, 0)), ANY, ANY]
    out_shape = [
        jax.ShapeDtypeStruct((t_all, d), BF16),
        jax.ShapeDtypeStruct((N_DEV, t_all, fb), BF16),
        jax.ShapeDtypeStruct((n_hid, t_all, fb), BF16),
        jax.ShapeDtypeStruct((t_all, d), BF16),
        jax.ShapeDtypeStruct((t_all, d), F32),
    ]
    out_specs = [tok(d), blk3(N_DEV), blk3(n_hid), tok(d), tok(d)]
    scratch = [pltpu.VMEM((N_DEV, d, fb), BF16), pltpu.VMEM((n_hid * fb, d), BF16), pltpu.SemaphoreType.DMA((2,))]
    return pl.pallas_call(
        body, name=f"ffn_fwd_{layer}", grid=(t_all // tm,), in_specs=in_specs, out_specs=out_specs, out_shape=out_shape,
        scratch_shapes=scratch, compiler_params=_cparams(dimension_semantics=("arbitrary",)),
    )(x, modl, g_ffn, w13, w2)


def _final_loss(x, g_final, target, seq):
    t_all, d = x.shape
    tm = min(TOKEN_TILE, seq)

    def body(x_ref, g_ref, t_ref, dx_ref, loss_ref, dg_ref):
        i = pl.program_id(0)

        @pl.when(i == 0)
        def _():
            loss_ref[...] = jnp.zeros(loss_ref.shape, F32)
            dg_ref[...] = jnp.zeros(dg_ref.shape, F32)

        x_t = x_ref[...]
        g = g_ref[...]
        rstd = lax.rsqrt(jnp.mean(x_t * x_t, axis=-1, keepdims=True) + EPS)
        xn = x_t * rstd
        err = xn * g - t_ref[...]
        loss_ref[0:1, :] += _colsum(err * err) * (0.5 / d)
        dy = err * (1.0 / d)
        dg_ref[0:1, :] += _colsum(dy * xn)
        dxn = dy * g
        dx_ref[...] = rstd * (dxn - xn * jnp.mean(dxn * xn, axis=-1, keepdims=True))

        @pl.when(i == pl.num_programs(0) - 1)
        def _():
            loss_ref[...] = jnp.broadcast_to(jnp.sum(loss_ref[0:1, :], axis=1, keepdims=True), loss_ref.shape)

    tok = pl.BlockSpec((tm, d), lambda i: (i, 0))
    acc = pl.BlockSpec((8, d), lambda i: (0, 0))
    return pl.pallas_call(
        body, name="final_loss", grid=(t_all // tm,),
        in_specs=[tok, pl.BlockSpec((1, d), lambda i: (0, 0)), tok], out_specs=[tok, acc, acc],
        out_shape=[jax.ShapeDtypeStruct((t_all, d), F32), jax.ShapeDtypeStruct((8, d), F32), jax.ShapeDtypeStruct((8, d), F32)],
        compiler_params=_cparams(dimension_semantics=("arbitrary",)),
    )(x, g_final, target)


def _ffn_bwd(dxo, xmid, ab, f, modl, g_ffn, w13, w2, layer, seq):
    t_all, d = xmid.shape
    tm = min(TOKEN_TILE, seq)
    n_seq_tiles = seq // tm
    fb = w13.shape[-1]
    n_hid = N_DEV // 2

    def body(dxo_ref, x_ref, ab_ref, f_ref, mod_ref, g_ref, w13_hbm, w2_hbm,
             dx_ref, df_ref, dab_ref, dmod_ref, dg_ref, w13_v, w2_v, sem):
        i = pl.program_id(0)
        _load_weights(i, [(w13_hbm, w13_v), (w2_hbm, w2_v)], sem)

        @pl.when(i == 0)
        def _():
            dg_ref[...] = jnp.zeros(dg_ref.shape, F32)

        @pl.when(i % n_seq_tiles == 0)
        def _():
            dmod_ref[...] = jnp.zeros(dmod_ref.shape, F32)

        scale2, gate2 = mod_ref[0, 4:5, :], mod_ref[0, 5:6, :]
        g = g_ref[...]
        x_t = x_ref[...]
        rstd = lax.rsqrt(jnp.mean(x_t * x_t, axis=-1, keepdims=True) + EPS)
        xn = x_t * rstd
        dxo_t = dxo_ref[...]
        dmod_ref[0, 2:3, :] += _colsum(dxo_t * f_ref[...].astype(F32))
        dfb = (dxo_t * gate2).astype(BF16)
        df_ref[...] = dfb
        dh = jnp.zeros((tm, d), F32)
        for k in range(n_hid):
            dhid = _dot_nt(dfb, w2_v[k * fb:(k + 1) * fb, :])
            a = ab_ref[k].astype(F32)
            b = ab_ref[n_hid + k].astype(F32)
            sg = _sigmoid(a)
            da = (dhid * b * (sg * (1.0 + a * (1.0 - sg)))).astype(BF16)
            db = (dhid * (a * sg)).astype(BF16)
            dab_ref[k] = da
            dab_ref[n_hid + k] = db
            dh = dh + _dot_nt(da, w13_v[k]) + _dot_nt(db, w13_v[n_hid + k])
        dx, dshift, dscale, dg = _rms_mod_bwd(dh, xn, rstd, g, scale2)
        dmod_ref[0, 0:1, :] += dshift
        dmod_ref[0, 1:2, :] += dscale
        dg_ref[0:1, :] += dg
        dx_ref[...] = dxo_t + dx

    tok = lambda cols: pl.BlockSpec((tm, cols), lambda i: (i, 0))
    blk3 = lambda n: pl.BlockSpec((n, tm, fb), lambda i: (0, i, 0))
    modspec = pl.BlockSpec((1, 8, d), lambda i: (i // n_seq_tiles, 0, 0))
    in_specs = [tok(d), tok(d), blk3(N_DEV), tok(d), modspec, pl.BlockSpec(g_ffn.shape, lambda i: (0, 0)), ANY, ANY]
    out_shape = [
        jax.ShapeDtypeStruct((t_all, d), F32), jax.ShapeDtypeStruct((t_all, d), BF16),
        jax.ShapeDtypeStruct((N_DEV, t_all, fb), BF16), jax.ShapeDtypeStruct(modl.shape, F32),
        jax.ShapeDtypeStruct((8, d), F32),
    ]
    out_specs = [tok(d), tok(d), blk3(N_DEV), modspec, pl.BlockSpec((8, d), lambda i: (0, 0))]
    scratch = [pltpu.VMEM((N_DEV, d, fb), BF16), pltpu.VMEM((n_hid * fb, d), BF16), pltpu.SemaphoreType.DMA((2,))]
    return pl.pallas_call(
        body, name=f"ffn_bwd_{layer}", grid=(t_all // tm,), in_specs=in_specs, out_specs=out_specs, out_shape=out_shape,
        scratch_shapes=scratch, compiler_params=_cparams(dimension_semantics=("arbitrary",)),
    )(dxo, xmid, ab, f, modl, g_ffn, w13, w2)


def _mixer_bwd(dxm, x, z, ycat, mo, modl, g_mix, ln_g, ln_b, w_s, bexp, pool_w, pool_scale, conv_w,
               win, wpa, wpb, wpc, wo, layer, seq):
    t_all, d = x.shape
    tm = min(TOKEN_TILE, seq)
    n_seq_tiles = seq // tm
    blk = win.shape[-1]
    n_win = len(POOL_WINDOWS)

    def tile_of(i):
        return (i // n_seq_tiles) * n_seq_tiles + (n_seq_tiles - 1 - i % n_seq_tiles)

    def halo_row_block(i):
        return jnp.maximum(tile_of(i) * (tm // POOL_HALO) - 1, 0)

    def body(dxm_ref, x_ref, z_ref, zpb_ref, zpc_ref, ycat_ref, mo_ref, mod_ref, gmix_ref, lng_ref, lnb_ref, ws_ref,
             bexp_ref, pw_ref, ps_ref, cw_ref, win_hbm, wpa_hbm, wpb_hbm, wpc_hbm, wo_hbm,
             dx_ref, dz_ref, dycat_ref, dmo_ref, dmod_ref, dg_ref, sm_ref, dws_ref, dssum_ref, dpw_ref,
             win_v, wpa_v, wpb_v, wpc_v, wo_v, xbext, zzext, rext, dyext, s_scr, dvn_scr, sem):
        i = pl.program_id(0)
        pairs = [(win_hbm, win_v), (wo_hbm, wo_v)]
        pairs += _wp_pairs(wpa_hbm, wpa_v) + _wp_pairs(wpb_hbm, wpb_v) + _wp_pairs(wpc_hbm, wpc_v)
        _load_weights(i, pairs, sem)
        tile_in_seq = n_seq_tiles - 1 - i % n_seq_tiles
        first_of_seq = tile_in_seq == 0

        @pl.when(i == 0)
        def _():
            for r in (dg_ref, sm_ref, dws_ref, dssum_ref, dpw_ref):
                r[...] = jnp.zeros(r.shape, F32)

        @pl.when(i % n_seq_tiles == 0)
        def _():
            dmod_ref[...] = jnp.zeros(dmod_ref.shape, F32)
            rext[tm:tm + POOL_HALO, :] = jnp.zeros((POOL_HALO, BR_W), F32)
            dyext[tm:tm + CONV_HALO, :] = jnp.zeros((CONV_HALO, BR_W), F32)

        shift1, scale1, gate1 = mod_ref[0, 0:1, :], mod_ref[0, 1:2, :], mod_ref[0, 2:3, :]
        g = gmix_ref[...]
        x_t = x_ref[...]
        rstd = lax.rsqrt(jnp.mean(x_t * x_t, axis=-1, keepdims=True) + EPS)
        xn = x_t * rstd
        dxm_t = dxm_ref[...]
        dmod_ref[0, 2:3, :] += _colsum(dxm_t * mo_ref[...].astype(F32))
        dmo = (dxm_t * gate1).astype(BF16)
        dmo_ref[...] = dmo
        dmerged = _dot_nt(dmo, wo_v[...])

        dys = []
        for n in range(3):
            zg = z_ref[:, 3 * d + n * d:3 * d + (n + 1) * d].astype(F32)
            gt = _sigmoid(zg)
            yn = ycat_ref[:, n * d:(n + 1) * d].astype(F32)
            dz_ref[:, 3 * d + n * d:3 * d + (n + 1) * d] = (dmerged * yn * gt * (1.0 - gt)).astype(BF16)
            dyn = (dmerged * gt).astype(BF16)
            dycat_ref[:, n * d:(n + 1) * d] = dyn
            dys.append(dyn)
        dh = jnp.zeros((tm, d), F32)
        for j in range(4, N_DEV):
            dh = dh + _dot_nt(dz_ref[:, j * blk:(j + 1) * blk], win_v[j])

        doa = _dot_nt(dys[0], wpa_v[...])
        u = z_ref[:, 0:BR_W].astype(F32)
        v = z_ref[:, BR_W:2 * BR_W].astype(F32)
        gu, tu = _gelu(u)
        gv, tv = _gelu(v)
        mu = jnp.mean(gv, axis=-1, keepdims=True)
        cen = gv - mu
        rs = lax.rsqrt(jnp.mean(cen * cen, axis=-1, keepdims=True) + EPS)
        vhat = cen * rs
        lng = lng_ref[...]
        vn_b = (vhat * lng + lnb_ref[...]).astype(BF16)
        mask = _tril_mask()
        wmask = [ws_ref[hh] * mask for hh in range(HEADS)]
        s = _gmlp_s(vn_b, [w.astype(BF16) for w in wmask], bexp_ref[...], s_scr, tm)
        du = (doa * s) * _gelu_grad(u, tu)
        ds = doa * gu
        ds_b = ds.astype(BF16)
        dssum = jnp.zeros((CHUNK, BR_W), F32)
        for ch in range(tm // CHUNK):
            rows = slice(ch * CHUNK, (ch + 1) * CHUNK)
            dssum = dssum + ds[rows, :]
            for hh in range(HEADS):
                cols = slice(hh * HEAD_DIM, (hh + 1) * HEAD_DIM)
                dvn_scr[rows, cols] = _dot_tn(wmask[hh].astype(BF16), ds_b[rows, cols])
                dws_ref[hh] += _dot_nt(ds_b[rows, cols], vn_b[rows, cols]) * mask
        dssum_ref[...] += dssum
        dvn = dvn_scr[...]
        sm_ref[0:1, :] += _colsum(dvn * vhat)
        sm_ref[1:2, :] += _colsum(dvn)
        dvhat = dvn * lng
        dgv = rs * (dvhat - jnp.mean(dvhat, axis=-1, keepdims=True) - vhat * jnp.mean(dvhat * vhat, axis=-1, keepdims=True))
        dv = dgv * _gelu_grad(v, tv)
        dz_ref[:, 0:BR_W] = du.astype(BF16)
        dz_ref[:, BR_W:2 * BR_W] = dv.astype(BF16)

        dob = _dot_nt(dys[1], wpb_v[...])
        xb = z_ref[:, 2 * BR_W:3 * BR_W].astype(F32)
        xbext[0:POOL_HALO, :] = jnp.where(first_of_seq, 0.0, zpb_ref[...].astype(F32))
        xbext[POOL_HALO:POOL_HALO + tm, :] = xb
        pos = tile_in_seq * tm + lax.broadcasted_iota(jnp.int32, (tm, 1), 0)
        ps = _pool_p(xb, xbext, pos, tm)
        scale_b = ps_ref[...]
        dq = dob * scale_b
        qs, dps = [], []
        for gi, win_len in enumerate(POOL_WINDOWS):
            cols = slice(gi * HEAD_DIM, (gi + 1) * HEAD_DIM)
            pw_b = pw_ref[gi].astype(BF16)
            p_b = ps[gi].astype(BF16)
            dq_b = dq[:, cols].astype(BF16)
            qs.append(_dot(p_b, pw_b))
            dpw_ref[gi] += _dot_tn(p_b, dq_b)
            dp = _dot_nt(dq_b, pw_b)
            dps.append(dp)
            cnt = jnp.minimum(pos + 1, win_len).astype(F32)
            rext[0:tm, cols] = dp / cnt
        sm_ref[2:3, :] += _colsum(dob * jnp.concatenate(qs, axis=1))
        dxbs = []
        for gi, win_len in enumerate(POOL_WINDOWS):
            cols = slice(gi * HEAD_DIM, (gi + 1) * HEAD_DIM)
            acc = rext[0:tm, cols]
            for k in range(1, win_len):
                acc = acc + rext[pl.ds(k, tm), cols]
            dxbs.append(acc - dps[gi])
        dz_ref[:, 2 * BR_W:3 * BR_W] = jnp.concatenate(dxbs, axis=1).astype(BF16)
        rext[tm:tm + POOL_HALO, :] = rext[0:POOL_HALO, :]

        doc = _dot_nt(dys[2], wpc_v[...])
        bg = z_ref[:, 3 * BR_W:4 * BR_W].astype(F32)
        cg = z_ref[:, 4 * BR_W:5 * BR_W].astype(F32)
        hc = z_ref[:, 5 * BR_W:6 * BR_W].astype(F32)
        zz = cg * hc
        zprev = zpc_ref[POOL_HALO - CONV_HALO:POOL_HALO, :].astype(F32)
        zzext[0:CONV_HALO, :] = jnp.where(first_of_seq, 0.0, zprev[:, 0:BR_W] * zprev[:, BR_W:2 * BR_W])
        zzext[CONV_HALO:CONV_HALO + tm, :] = zz
        zm2 = zzext[pl.ds(CONV_HALO - 2, tm), :]
        zm1 = zzext[pl.ds(CONV_HALO - 1, tm), :]
        w0, w1, w2c = cw_ref[0:1, :], cw_ref[1:2, :], cw_ref[2:3, :]
        yconv = w0 * zm2 + w1 * zm1 + w2c * zz
        dyc = doc * bg
        sm_ref[3:4, :] += _colsum(dyc * zm2)
        sm_ref[4:5, :] += _colsum(dyc * zm1)
        sm_ref[5:6, :] += _colsum(dyc * zz)
        dyext[0:tm, :] = dyc
        dzz = w2c * dyc + w1 * dyext[pl.ds(1, tm), :] + w0 * dyext[pl.ds(2, tm), :]
        dyext[tm:tm + CONV_HALO, :] = dyext[0:CONV_HALO, :]
        dz_ref[:, 3 * BR_W:4 * BR_W] = (doc * yconv).astype(BF16)
        dz_ref[:, 4 * BR_W:5 * BR_W] = (dzz * hc).astype(BF16)
        dz_ref[:, 5 * BR_W:6 * BR_W] = (dzz * cg).astype(BF16)

        for j in range(4):
            dh = dh + _dot_nt(dz_ref[:, j * blk:(j + 1) * blk], win_v[j])
        dx, dshift, dscale, dg = _rms_mod_bwd(dh, xn, rstd, g, scale1)
        dmod_ref[0, 0:1, :] += dshift
        dmod_ref[0, 1:2, :] += dscale
        dg_ref[0:1, :] += dg
        dx_ref[...] = dxm_t + dx

    tok = lambda cols: pl.BlockSpec((tm, cols), lambda i: (tile_of(i), 0))
    const2 = lambda a: pl.BlockSpec(a.shape, lambda i: (0,) * a.ndim)
    modspec = pl.BlockSpec((1, 8, d), lambda i: (i // n_seq_tiles, 0, 0))
    in_specs = [
        tok(d), tok(d), tok(N_DEV * blk),
        pl.BlockSpec((POOL_HALO, BR_W), lambda i: (halo_row_block(i), 2)),
        pl.BlockSpec((POOL_HALO, 2 * BR_W), lambda i: (halo_row_block(i), 2)),
        tok(3 * d), tok(d), modspec,
        const2(g_mix), const2(ln_g), const2(ln_b), const2(w_s), const2(bexp), const2(pool_w), const2(pool_scale),
        const2(conv_w), ANY, ANY, ANY, ANY, ANY,
    ]
    acc = lambda shape: pl.BlockSpec(shape, lambda i: (0,) * len(shape))
    out_shape = [
        jax.ShapeDtypeStruct((t_all, d), F32), jax.ShapeDtypeStruct((t_all, N_DEV * blk), BF16),
        jax.ShapeDtypeStruct((t_all, 3 * d), BF16), jax.ShapeDtypeStruct((t_all, d), BF16),
        jax.ShapeDtypeStruct(modl.shape, F32), jax.ShapeDtypeStruct((8, d), F32), jax.ShapeDtypeStruct((8, BR_W), F32),
        jax.ShapeDtypeStruct((HEADS, CHUNK, CHUNK), F32), jax.ShapeDtypeStruct((CHUNK, BR_W), F32),
        jax.ShapeDtypeStruct((n_win, HEAD_DIM, HEAD_DIM), F32),
    ]
    out_specs = [tok(d), tok(N_DEV * blk), tok(3 * d), tok(d), modspec, acc((8, d)), acc((8, BR_W)),
                 acc((HEADS, CHUNK, CHUNK)), acc((CHUNK, BR_W)), acc((n_win, HEAD_DIM, HEAD_DIM))]
    scratch = [
        pltpu.VMEM((N_DEV, d, blk), BF16), pltpu.VMEM((BR_W, d), BF16), pltpu.VMEM((BR_W, d), BF16),
        pltpu.VMEM((BR_W, d), BF16), pltpu.VMEM((d, d), BF16),
        pltpu.VMEM((tm + POOL_HALO, BR_W), F32), pltpu.VMEM((tm + CONV_HALO, BR_W), F32),
        pltpu.VMEM((tm + POOL_HALO, BR_W), F32), pltpu.VMEM((tm + CONV_HALO, BR_W), F32),
        pltpu.VMEM((tm, BR_W), F32), pltpu.VMEM((tm, BR_W), F32),
        pltpu.SemaphoreType.DMA((2 + 3 * N_DEV,)),
    ]
    return pl.pallas_call(
        body, name=f"mixer_bwd_{layer}", grid=(t_all // tm,), in_specs=in_specs, out_specs=out_specs, out_shape=out_shape,
        scratch_shapes=scratch, compiler_params=_cparams(dimension_semantics=("arbitrary",)),
    )(dxm, x, z, z, z, ycat, mo, modl, g_mix, ln_g, ln_b, w_s, bexp, pool_w, pool_scale, conv_w, win, wpa, wpb, wpc, wo)


def _wgrad(a, b, a_spec, b_spec, out_struct, out_spec, grid_kn, tk, tn, split, name):
    t_all = a.shape[-2]
    tt = min(WGRAD_TOKENS, t_all)
    n_t = t_all // tt

    def body(a_ref, b_ref, o_ref, acc):
        t = pl.program_id(2)

        @pl.when(t == 0)
        def _():
            acc[...] = jnp.zeros(acc.shape, F32)

        acc[...] += _dot_tn(a_ref[...], b_ref[...])

        @pl.when(t == n_t - 1)
        def _():
            if split:
                for j in range(split):
                    w = tn // split
                    o_ref[j] = acc[:, j * w:(j + 1) * w].astype(o_ref.dtype)
            else:
                o_ref[...] = acc[...].astype(o_ref.dtype)

    return pl.pallas_call(
        body, name=name, grid=(*grid_kn, n_t), in_specs=[a_spec(tt), b_spec(tt)], out_specs=out_spec, out_shape=out_struct,
        scratch_shapes=[pltpu.VMEM((tk, tn), F32)],
        compiler_params=_cparams(dimension_semantics=("arbitrary", "arbitrary", "arbitrary")),
    )(a, b)


def _mixer_wgrads(h, dz, mrg, dmo, ocat, dycat, layer):
    d = h.shape[1]
    blk = dz.shape[1] // N_DEV
    g_win = _wgrad(
        h, dz, lambda tt: pl.BlockSpec((tt, d), lambda k, n, t: (t, 0)), lambda tt: pl.BlockSpec((tt, blk), lambda k, n, t: (t, n)),
        jax.ShapeDtypeStruct((N_DEV, d, blk), BF16), pl.BlockSpec((None, d, blk), lambda k, n, t: (n, 0, 0)),
        (1, N_DEV), d, blk, 0, f"wgrad_in_{layer}")
    g_wo = _wgrad(
        mrg, dmo, lambda tt: pl.BlockSpec((tt, d), lambda k, n, t: (t, 0)), lambda tt: pl.BlockSpec((tt, d), lambda k, n, t: (t, 0)),
        jax.ShapeDtypeStruct((d, d), BF16), pl.BlockSpec((d, d), lambda k, n, t: (0, 0)), (1, 1), d, d, 0, f"wgrad_o_{layer}")
    g_wp = []
    for n, nm in enumerate("abc"):
        g_wp.append(_wgrad(
            ocat, dycat, lambda tt, n=n: pl.BlockSpec((tt, BR_W), lambda k, nn, t: (t, n)),
            lambda tt, n=n: pl.BlockSpec((tt, d), lambda k, nn, t: (t, n)),
            jax.ShapeDtypeStruct((N_DEV, BR_W, d // N_DEV), BF16),
            pl.BlockSpec((N_DEV, BR_W, d // N_DEV), lambda k, nn, t: (0, 0, 0)), (1, 1), BR_W, d, N_DEV, f"wgrad_p{nm}_{layer}"))
    return [g_win, g_wo.reshape(N_DEV, d // N_DEV, d), *g_wp]


def _ffn_wgrads(h2, dab, hid, df, layer):
    d = h2.shape[1]
    fb = dab.shape[-1]
    n_hid = N_DEV // 2
    g_w13 = _wgrad(
        h2, dab, lambda tt: pl.BlockSpec((tt, d), lambda k, n, t: (t, 0)),
        lambda tt: pl.BlockSpec((None, tt, fb), lambda k, n, t: (n, t, 0)),
        jax.ShapeDtypeStruct((N_DEV, d, fb), BF16), pl.BlockSpec((None, d, fb), lambda k, n, t: (n, 0, 0)),
        (1, N_DEV), d, fb, 0, f"wgrad_13_{layer}")
    g_w2 = _wgrad(
        hid, df, lambda tt: pl.BlockSpec((None, tt, fb), lambda k, n, t: (k, t, 0)),
        lambda tt: pl.BlockSpec((tt, d), lambda k, n, t: (t, 0)),
        jax.ShapeDtypeStruct((n_hid * fb, d), BF16), pl.BlockSpec((fb, d), lambda k, n, t: (k, 0)),
        (n_hid, 1), fb, d, 0, f"wgrad_2_{layer}")
    return [g_w13, g_w2.reshape(N_DEV, fb // 2, d)]


def _remote(src, dst, send_sem, recv_sem, dev):
    return pltpu.make_async_remote_copy(src_ref=src, dst_ref=dst, send_sem=send_sem, recv_sem=recv_sem, device_id=dev,
                                        device_id_type=MESH_ID)


HBM_SPEC = pl.BlockSpec(memory_space=pltpu.HBM)
SEM_SPEC = pl.BlockSpec(memory_space=pltpu.SEMAPHORE)
DATAFLOW = pltpu.SideEffectType.DATAFLOW_SIDE_EFFECTING
GATHER, SCATTER = "gather", "scatter"


def _xfer_src(src, kind, peer_index):
    return src if kind == GATHER else src.at[peer_index]


def _xfer_sem(a, k):
    return a * (N_DEV - 1) + k - 1


def _landing_zones(srcs, kind):
    me = _my_index()
    zones = []
    for s in srcs:
        own = s[None] if kind == GATHER else lax.dynamic_index_in_dim(s, me, 0, keepdims=True)
        zones.append(lax.dynamic_update_index_in_dim(lax.empty((N_DEV, *own.shape[1:]), s.dtype), own, me, 0))
    return zones


def _xfer_start(srcs, kind, name, after):
    n = len(srcs)
    lands = _landing_zones(srcs, kind)

    def body(*refs):
        src, land = refs[:n], refs[n:2 * n]
        send_sems, recv_sems = refs[2 * n + 1], refs[2 * n + 2]
        token = refs[-1]
        me = _my_index()
        for k in range(1, N_DEV):
            dev, idx = _peer(k)
            for a in range(n):
                q = _xfer_sem(a, k)
                _remote(_xfer_src(src[a], kind, idx), land[a].at[me], send_sems.at[q], recv_sems.at[q], dev).start()
        token[...] = jnp.zeros(token.shape, token.dtype)

    both = [*srcs, *lands]
    sems = pltpu.SemaphoreType.DMA((n * (N_DEV - 1),))
    out_shape = (sems, sems, *[pltpu.HBM(a.shape, a.dtype) for a in both], jax.ShapeDtypeStruct((8, 128), F32))
    outs = pl.pallas_call(
        body, name=name, in_specs=[HBM_SPEC] * (2 * n) + [ANY],
        out_specs=(SEM_SPEC, SEM_SPEC, *[HBM_SPEC] * (2 * n), VMEM_SPEC),
        out_shape=out_shape, input_output_aliases={i: 2 + i for i in range(2 * n)},
        compiler_params=pltpu.CompilerParams(has_side_effects=DATAFLOW),
    )(*[pltpu.with_memory_space_constraint(a, pltpu.HBM) for a in both], after)
    return outs[0], outs[1], list(outs[2:2 + n]), list(outs[2 + n:2 + 2 * n]), outs[-1]


def _xfer_wait(send_sems, recv_sems, srcs, lands, rows, after, kind, name):
    n = len(srcs)

    def body(*refs):
        src, land = refs[:n], refs[n:2 * n]
        send_sems, recv_sems = refs[2 * n], refs[2 * n + 1]
        for k in range(1, N_DEV):
            dev, idx = _peer(k)
            for a in range(n):
                q = _xfer_sem(rows[a], k)
                cp = _remote(_xfer_src(src[a], kind, idx), land[a].at[idx], send_sems.at[q], recv_sems.at[q], dev)
                cp.wait_send()
                cp.wait_recv()

    both = [*srcs, *lands]
    outs = pl.pallas_call(
        body, name=name, in_specs=[HBM_SPEC] * (2 * n) + [SEM_SPEC, SEM_SPEC, ANY], out_specs=[HBM_SPEC] * (2 * n),
        out_shape=[pltpu.HBM(a.shape, a.dtype) for a in both], input_output_aliases={i: i for i in range(2 * n)},
        compiler_params=pltpu.CompilerParams(has_side_effects=DATAFLOW),
    )(*both, send_sems, recv_sems, after)
    return list(outs[n:])


def _pre(c_pad, conv_pad, w_mod, b_mod_mine):
    n_layers, d, blk = w_mod.shape

    def body(c_ref, conv_ref, wmod_ref, bmod_ref, cact_ref, mod_ref, convall_ref, cact_mine, msh, send_sems, recv_sems):
        me = _my_index()
        c = c_ref[...]
        cact_mine[...] = c * _sigmoid(c)
        cact_ref[me] = cact_mine[...]
        convall_ref[me] = conv_ref[...]
        sends = []
        for k in range(1, N_DEV):
            dev, _ = _peer(k)
            sends.append(_remote(cact_mine, cact_ref.at[me], send_sems.at[0, k - 1], recv_sems.at[0, k - 1], dev))
            sends.append(_remote(conv_ref, convall_ref.at[me], send_sems.at[1, k - 1], recv_sems.at[1, k - 1], dev))
        for cp in sends:
            cp.start()
        for k in range(1, N_DEV):
            dev, idx = _peer(k)
            _remote(cact_mine, cact_ref.at[idx], send_sems.at[0, k - 1], recv_sems.at[0, k - 1], dev).wait_recv()
            _remote(conv_ref, convall_ref.at[idx], send_sems.at[1, k - 1], recv_sems.at[1, k - 1], dev).wait_recv()
        for cp in sends:
            cp.wait_send()
        cact_b = cact_ref[...].reshape(N_DEV * 8, d).astype(BF16)
        for l in range(n_layers):
            m = _dot(cact_b, wmod_ref[l].astype(BF16)) + bmod_ref[l]
            msh[l] = m.reshape(N_DEV, 8, blk)
        mod_ref[me] = msh[:, me]
        sends = []
        for k in range(1, N_DEV):
            dev, idx = _peer(k)
            sends.append(_remote(msh.at[:, idx], mod_ref.at[me], send_sems.at[2, k - 1], recv_sems.at[2, k - 1], dev))
        for cp in sends:
            cp.start()
        for k in range(1, N_DEV):
            dev, idx = _peer(k)
            _remote(msh.at[:, idx], mod_ref.at[idx], send_sems.at[2, k - 1], recv_sems.at[2, k - 1], dev).wait_recv()
        for cp in sends:
            cp.wait_send()

    out_shape = [jax.ShapeDtypeStruct((N_DEV, 8, d), F32), jax.ShapeDtypeStruct((N_DEV, n_layers, 8, blk), F32),
                 jax.ShapeDtypeStruct((N_DEV, *conv_pad.shape), F32)]
    return pl.pallas_call(
        body, name="pre", in_specs=[VMEM_SPEC] * 4, out_specs=[VMEM_SPEC] * 3, out_shape=out_shape,
        scratch_shapes=[pltpu.VMEM((8, d), F32), pltpu.VMEM((n_layers, N_DEV, 8, blk), F32),
                        pltpu.SemaphoreType.DMA((3, N_DEV - 1)), pltpu.SemaphoreType.DMA((3, N_DEV - 1))],
        compiler_params=_cparams(has_side_effects=True),
    )(c_pad, conv_pad, w_mod, b_mod_mine)


def _small(buf, dmod_blocks, cact_all):
    rows = buf.shape[0]
    seg = rows // N_DEV
    _, n_layers, _, blk = dmod_blocks.shape
    d = cact_all.shape[-1]

    def body(buf_ref, dmod_ref, cact_ref, out_ref, gw_ref, rs_recv, red, drecv, send_sems, recv_sems):
        me = _my_index()
        mine = pl.ds(pl.multiple_of(me * seg, 8), seg)
        sends = []
        for k in range(1, N_DEV):
            dev, idx = _peer(k)
            theirs = pl.ds(pl.multiple_of(idx * seg, 8), seg)
            sends.append(_remote(buf_ref.at[theirs], rs_recv.at[k - 1], send_sems.at[0, k - 1], recv_sems.at[0, k - 1], dev))
            sends.append(_remote(dmod_ref.at[idx], drecv.at[me], send_sems.at[1, k - 1], recv_sems.at[1, k - 1], dev))
        for cp in sends:
            cp.start()
        drecv[me] = dmod_ref[me]
        for k in range(1, N_DEV):
            dev, idx = _peer(k)
            _remote(buf_ref.at[mine], rs_recv.at[k - 1], send_sems.at[0, k - 1], recv_sems.at[0, k - 1], dev).wait_recv()
            _remote(dmod_ref.at[idx], drecv.at[idx], send_sems.at[1, k - 1], recv_sems.at[1, k - 1], dev).wait_recv()
        for cp in sends:
            cp.wait_send()
        total = buf_ref[mine, :]
        for k in range(1, N_DEV):
            total = total + rs_recv[k - 1]
        red[...] = total
        out_ref[mine, :] = total
        sends = []
        for k in range(1, N_DEV):
            dev, _ = _peer(k)
            sends.append(_remote(red, out_ref.at[mine], send_sems.at[2, k - 1], recv_sems.at[2, k - 1], dev))
        for cp in sends:
            cp.start()
        cact_b = cact_ref[...].reshape(N_DEV * 8, d).astype(BF16)
        for l in range(n_layers):
            gw_ref[l] = _dot_tn(cact_b, drecv[:, l].reshape(N_DEV * 8, blk).astype(BF16))
        for k in range(1, N_DEV):
            dev, idx = _peer(k)
            theirs = pl.ds(pl.multiple_of(idx * seg, 8), seg)
            _remote(red, out_ref.at[theirs], send_sems.at[2, k - 1], recv_sems.at[2, k - 1], dev).wait_recv()
        for cp in sends:
            cp.wait_send()

    out_shape = [jax.ShapeDtypeStruct(buf.shape, F32), jax.ShapeDtypeStruct((n_layers, d, blk), F32)]
    return pl.pallas_call(
        body, name="small_allreduce", in_specs=[VMEM_SPEC] * 3, out_specs=[VMEM_SPEC] * 2, out_shape=out_shape,
        scratch_shapes=[pltpu.VMEM((N_DEV - 1, seg, 128), F32), pltpu.VMEM((seg, 128), F32),
                        pltpu.VMEM(dmod_blocks.shape, F32),
                        pltpu.SemaphoreType.DMA((3, N_DEV - 1)), pltpu.SemaphoreType.DMA((3, N_DEV - 1))],
        compiler_params=_cparams(has_side_effects=True),
    )(buf, dmod_blocks, cact_all)


def _adamw_math(w, g, m, v):
    m = ADAM_B1 * m + (1.0 - ADAM_B1) * g
    v = ADAM_B2 * v + (1.0 - ADAM_B2) * (g * g)
    m_hat = m / (1.0 - ADAM_B1 ** ADAM_STEP)
    v_hat = v / (1.0 - ADAM_B2 ** ADAM_STEP)
    delta = -ADAM_LR * (m_hat / (jnp.sqrt(v_hat) + ADAM_EPS) + ADAM_WD * w)
    return delta, m, v


def _adamw(parts, w, m, v, name, first=0, earlier=None):
    n_layers = len(parts)
    n_parts, rows, cols = parts[0].shape
    tr = rows
    while tr * cols * 4 > (1 << 20) and tr % 32 == 0:
        tr //= 2
    n_r = rows // tr
    n_earlier = 0 if earlier is None else 4

    def body(*refs):
        p_refs = refs[:n_layers]
        w_ref, m_ref, v_ref = refs[n_layers:n_layers + 3]
        g_out, d_out, m_out, v_out = refs[n_layers + 3 + n_earlier:]
        layer = pl.program_id(0)
        for q in range(n_layers):

            @pl.when(layer == q)
            def _(q=q):
                g = p_refs[q][0].astype(F32)
                for p in range(1, n_parts):
                    g = g + p_refs[q][p].astype(F32)
                delta, m_new, v_new = _adamw_math(w_ref[...], g, m_ref[...], v_ref[...])
                g_out[...] = g
                d_out[...] = delta
                m_out[...] = m_new
                v_out[...] = v_new

    def parts_spec(q):
        return pl.BlockSpec((n_parts, tr, cols), lambda l, r: (0, jnp.where(l == q, r, jnp.where(l < q, 0, n_r - 1)), 0))

    spec = pl.BlockSpec((None, tr, cols), lambda l, r: (first + l, r, 0))
    out = jax.ShapeDtypeStruct(w.shape, F32)
    n_in = n_layers + 3
    return pl.pallas_call(
        body, name=name, grid=(n_layers, n_r),
        in_specs=[parts_spec(q) for q in range(n_layers)] + [spec, spec, spec] + [ANY] * n_earlier,
        out_specs=[spec] * 4, out_shape=[out] * 4, input_output_aliases={n_in + i: i for i in range(n_earlier)},
        compiler_params=_cparams(dimension_semantics=("arbitrary", "arbitrary")),
    )(*parts, w, m, v, *(earlier or ()))


def _adamw_flat(g, w, m, v):
    def body(g_ref, w_ref, m_ref, v_ref, d_out, m_out, v_out):
        delta, m_new, v_new = _adamw_math(w_ref[...], g_ref[...], m_ref[...], v_ref[...])
        d_out[...] = delta
        m_out[...] = m_new
        v_out[...] = v_new

    out = jax.ShapeDtypeStruct(g.shape, F32)
    return pl.pallas_call(body, name="adamw_small", in_specs=[VMEM_SPEC] * 4, out_specs=[VMEM_SPEC] * 3, out_shape=[out] * 3,
                          compiler_params=_cparams())(g, w, m, v)


def _pack(arrays, rows_multiple):
    flat = jnp.concatenate([a.reshape(-1) for a in arrays])
    per = 128 * rows_multiple
    total = -(-flat.shape[0] // per) * per
    return jnp.pad(flat, (0, total - flat.shape[0])).reshape(total // 128, 128)


def _unpack(buf, like):
    flat = buf.reshape(-1)
    out, off = [], 0
    for a in like:
        out.append(flat[off:off + a.size].reshape(a.shape))
        off += a.size
    return out


def kernel(x, c, w_mod, b_mod, g_mix, w_in, gm_ln_g, gm_ln_b, gm_w_s, gm_b_s, w_pa, pool_w, pool_scale, w_pb, conv_w, w_pc, w_o, g_ffn, w_13, w_2, g_final, loss_target, m_w_mod, m_b_mod, m_g_mix, m_w_in, m_gm_ln_g, m_gm_ln_b, m_gm_w_s, m_gm_b_s, m_w_pa, m_pool_w, m_pool_scale, m_w_pb, m_conv_w, m_w_pc, m_w_o, m_g_ffn, m_w_13, m_w_2, m_g_final, v_w_mod, v_b_mod, v_g_mix, v_w_in, v_gm_ln_g, v_gm_ln_b, v_gm_w_s, v_gm_b_s, v_w_pa, v_pool_w, v_pool_scale, v_w_pb, v_conv_w, v_w_pc, v_w_o, v_g_ffn, v_w_13, v_w_2, v_g_final):
    nb, seq, d = x.shape
    n_layers = w_in.shape[0]
    t_all = nb * seq
    blk = w_in.shape[-1]
    me = _my_index()
    conv_shard = conv_w.shape[-1]

    c_pad = jnp.pad(c, ((0, 8 - nb), (0, 0)))
    conv_pad = jnp.pad(conv_w.reshape(n_layers * 3, conv_shard), ((0, 16 - n_layers * 3), (0, 128 - conv_shard)))
    b_mod_mine = lax.dynamic_slice_in_dim(b_mod, me * blk, blk, axis=1).reshape(n_layers, 1, blk)
    cact_all, mod_blocks, conv_all = _pre(c_pad, conv_pad, w_mod, b_mod_mine)
    mod = jnp.transpose(mod_blocks, (1, 2, 0, 3)).reshape(n_layers, 8, N_MOD, d)[:, :nb]
    mod = jnp.pad(mod, ((0, 0), (0, 0), (0, 8 - N_MOD), (0, 0)))
    conv_full = jnp.transpose(conv_all[:, :n_layers * 3, :conv_shard].reshape(N_DEV, n_layers, 3, conv_shard), (1, 2, 0, 3))
    conv_full = jnp.pad(conv_full.reshape(n_layers, 3, N_DEV * conv_shard), ((0, 0), (0, 5), (0, 0)))
    bexp = jnp.repeat(jnp.transpose(gm_b_s, (0, 2, 1)), HEAD_DIM, axis=2)

    big = [w_in, w_o, w_pa, w_pb, w_pc, w_13, w_2]
    n_big, n_mix = len(big), 5
    shards = [w[l].astype(BF16) for l in range(n_layers) for w in big]
    g_send, g_recv, shards, lands, _ = _xfer_start(shards, GATHER, "gather_start", cact_all)

    def gathered(l, lo, hi, after, name):
        rows = list(range(l * n_big + lo, l * n_big + hi))
        return _xfer_wait(g_send, g_recv, [shards[r] for r in rows], [lands[r] for r in rows], rows, after, GATHER, name)

    row = lambda a, l: a[l].reshape(1, -1)
    xs = x.reshape(t_all, d)
    saved, weights = [], []
    after = mod
    for l in range(n_layers):
        win_g, wo_g, wpa_g, wpb_g, wpc_g = gathered(l, 0, n_mix, after, f"gather_wait_mixer_{l}")
        wo_g = wo_g.reshape(d, d)
        h, z, ycat, ocat, mrg, mo, xmid = _mixer_fwd(
            xs, mod[l], row(g_mix, l), row(gm_ln_g, l), row(gm_ln_b, l), gm_w_s[l], bexp[l], pool_w[l], row(pool_scale, l),
            conv_full[l], win_g, wpa_g, wpb_g, wpc_g, wo_g, l, seq)
        w13_g, w2_g = gathered(l, n_mix, n_big, xmid, f"gather_wait_ffn_{l}")
        w2_g = w2_g.reshape(N_DEV * w_2.shape[1], d)
        h2, ab, hid, f, xo = _ffn_fwd(xmid, mod[l], row(g_ffn, l), w13_g, w2_g, l, seq)
        saved.append((xs, h, z, ycat, ocat, mrg, mo, xmid, h2, ab, hid, f))
        weights.append((win_g, wpa_g, wpb_g, wpc_g, wo_g, w13_g, w2_g))
        xs = after = xo

    dx, loss_blk, dgf_blk = _final_loss(xs, g_final.reshape(1, d), loss_target.reshape(t_all, d), seq)
    loss = lax.psum(loss_blk[0, 0], ("x", "y", "c"))

    ffn_flight = [None] * n_layers
    mix_flight = [None] * n_layers
    small_grads = [None] * n_layers
    dmods = [None] * n_layers
    tok = jnp.zeros((), F32)
    for l in reversed(range(n_layers)):
        x_in, h, z, ycat, ocat, mrg, mo, xmid, h2, ab, hid, f = saved[l]
        win_g, wpa_g, wpb_g, wpc_g, wo_g, w13_g, w2_g = weights[l]
        dxm, df, dab, dmod2, dg_ffn = _ffn_bwd(dx, xmid, ab, f, mod[l] + tok, row(g_ffn, l), w13_g, w2_g, l, seq)
        ffn_flight[l] = _xfer_start(_ffn_wgrads(h2, dab, hid, df, l), SCATTER, f"grads_start_ffn_{l}", dxm)
        dx, dz, dycat, dmo, dmod1, dg_mix, sm, dws, dssum, dpw = _mixer_bwd(
            dxm, x_in, z, ycat, mo, mod[l] + ffn_flight[l][4][0, 0], row(g_mix, l), row(gm_ln_g, l), row(gm_ln_b, l),
            gm_w_s[l], bexp[l], pool_w[l], row(pool_scale, l), conv_full[l], win_g, wpa_g, wpb_g, wpc_g, wo_g, l, seq)
        if l > 0:
            mix_flight[l] = _xfer_start(_mixer_wgrads(h, dz, mrg, dmo, ocat, dycat, l), SCATTER, f"grads_start_mixer_{l}", dx)
            tok = mix_flight[l][4][0, 0]
        dmod = jnp.concatenate([dmod1[:, 0:3], dmod2[:, 0:3]], axis=1).reshape(nb, N_MOD * d)
        dmods[l] = dmod
        db_s = jnp.transpose(jnp.sum(dssum.reshape(CHUNK, HEADS, HEAD_DIM), axis=2))
        small_grads[l] = [jnp.sum(dmod, axis=0), dg_mix[0], sm[0], sm[1], dws, db_s, dpw, sm[2], sm[3:6], dg_ffn[0]]
    grad_x = dx.reshape(nb, seq, d)

    names = ["b_mod", "g_mix", "ln_g", "ln_b", "w_s", "b_s", "pool_w", "pool_scale", "conv_w", "g_ffn"]
    per_name = [jnp.stack([small_grads[l][n] for l in range(n_layers)]) for n in range(len(names))] + [dgf_blk[0]]
    buf = _pack(per_name, 8 * N_DEV)
    dmod_all = jnp.pad(jnp.stack(dmods), ((0, 0), (0, 8 - nb), (0, 0)))
    dmod_blocks = jnp.transpose(dmod_all.reshape(n_layers, 8, N_DEV, blk), (2, 0, 1, 3))
    red, grad_w_mod = _small(buf, dmod_blocks, cact_all)
    (g_b_mod, g_g_mix, g_ln_g, g_ln_b, g_w_s, g_b_s, g_pool_w, g_pool_scale, g_conv_full, g_g_ffn, g_g_final) = _unpack(red, per_name)
    g_conv = lax.dynamic_slice_in_dim(g_conv_full, me * conv_shard, conv_shard, axis=2)

    mix_flight[0] = _xfer_start(_mixer_wgrads(h, dz, mrg, dmo, ocat, dycat, 0), SCATTER, "grads_start_mixer_0", red)
    results = {}
    results["w_mod"] = _adamw([grad_w_mod[l][None] for l in range(n_layers)], w_mod, m_w_mod, v_w_mod, "adamw_w_mod")

    small_w =[b_mod, g_mix, gm_ln_g, gm_ln_b, gm_w_s, gm_b_s, pool_w, pool_scale, conv_w, g_ffn, g_final]
    small_m = [m_b_mod, m_g_mix, m_gm_ln_g, m_gm_ln_b, m_gm_w_s, m_gm_b_s, m_pool_w, m_pool_scale, m_conv_w, m_g_ffn, m_g_final]
    small_v = [v_b_mod, v_g_mix, v_gm_ln_g, v_gm_ln_b, v_gm_w_s, v_gm_b_s, v_pool_w, v_pool_scale, v_conv_w, v_g_ffn, v_g_final]
    small_g = [g_b_mod, g_g_mix, g_ln_g, g_ln_b, g_w_s, g_b_s, g_pool_w, g_pool_scale, g_conv, g_g_ffn, g_g_final]
    small_names = ["b_mod", "g_mix", "gm_ln_g", "gm_ln_b", "gm_w_s", "gm_b_s", "pool_w", "pool_scale", "conv_w", "g_ffn", "g_final"]
    sd_buf, sm_new, sv_new = _adamw_flat(_pack(small_g, 8), _pack(small_w, 8), _pack(small_m, 8), _pack(small_v, 8))
    sd, sm_new, sv_new = _unpack(sd_buf, small_w), _unpack(sm_new, small_w), _unpack(sv_new, small_w)
    for n, nm in enumerate(small_names):
        results[nm] = (small_g[n], sd[n], sm_new[n], sv_new[n])

    def arrived(flight, rows, after, name):
        send_sems, recv_sems, srcs, zones, _ = flight
        return _xfer_wait(send_sems, recv_sems, srcs, zones, rows, after, SCATTER, name)

    layers = list(range(n_layers))
    done = (results["w_mod"][1][0, 0, 0] + sd_buf[0, 0]).reshape(1)
    ffn_recv = [arrived(ffn_flight[l], [0, 1], done, f"grads_wait_ffn_{l}") for l in reversed(layers)][::-1]
    mix_recv = [None] + [arrived(mix_flight[l], list(range(n_mix)), done, f"grads_wait_mixer_{l}") for l in reversed(layers[1:])][::-1]
    results["w_13"] = _adamw([ffn_recv[l][0] for l in layers], w_13, m_w_13, v_w_13, "adamw_w_13")
    results["w_2"] = _adamw([ffn_recv[l][1] for l in layers], w_2, m_w_2, v_w_2, "adamw_w_2")
    mix_m = [m_w_in, m_w_o, m_w_pa, m_w_pb, m_w_pc]
    mix_v = [v_w_in, v_w_o, v_w_pa, v_w_pb, v_w_pc]
    mix_names = ["w_in", "w_o", "w_pa", "w_pb", "w_pc"]
    early = [_adamw([mix_recv[l][a] for l in layers[1:]], big[a], mix_m[a], mix_v[a], f"adamw_{mix_names[a]}_later", first=1)
             for a in range(n_mix)]
    done = (results["w_13"][1][0, 0, 0] + results["w_2"][1][0, 0, 0] + sum(e[1][1, 0, 0] for e in early)).reshape(1)
    mix_recv[0] = arrived(mix_flight[0], list(range(n_mix)), done, "grads_wait_mixer_0")
    for a in range(n_mix):
        results[mix_names[a]] = _adamw([mix_recv[0][a]], big[a], mix_m[a], mix_v[a], f"adamw_{mix_names[a]}_first", earlier=early[a])

    order = ["w_mod", "b_mod", "g_mix", "w_in", "gm_ln_g", "gm_ln_b", "gm_w_s", "gm_b_s", "w_pa", "pool_w", "pool_scale",
             "w_pb", "conv_w", "w_pc", "w_o", "g_ffn", "w_13", "w_2", "g_final"]
    return (loss, grad_x, *[results[nm][0] for nm in order], *[results[nm][1] for nm in order],
            *[results[nm][2] for nm in order], *[results[nm][3] for nm in order])
```

```python
import functools

import jax
import jax.numpy as jnp
from jax import lax
from jax.experimental import pallas as pl
from jax.experimental.pallas import tpu as pltpu

F32 = jnp.float32
BF16 = jnp.bfloat16
MESH_ID = pl.DeviceIdType.MESH

N_DEV = 8
EPS = 1e-6
CHUNK = 128
HEADS = 4
HEAD_DIM = 128
BR_W = 512
POOL_WINDOWS = (2, 4, 8, 16)
POOL_HALO = 16
CONV_HALO = 8
N_MOD = 6
ADAM_LR = 0.001
ADAM_B1 = 0.9
ADAM_B2 = 0.999
ADAM_EPS = 1e-08
ADAM_WD = 0.01
ADAM_STEP = 10

TOKEN_TILE = 256
WGRAD_TOKENS = 2048
VMEM_LIMIT = 56 * 1024 * 1024
GELU_K = 0.7978845608028654
GELU_C = 0.044715

ANY = pl.BlockSpec(memory_space=pl.ANY)
VMEM_SPEC = pl.BlockSpec(memory_space=pltpu.VMEM)


def _cparams(**kw):
    return pltpu.CompilerParams(vmem_limit_bytes=VMEM_LIMIT, **kw)


def _dot(a, b):
    return jnp.dot(a, b, preferred_element_type=F32)


def _dot_nt(a, b):
    return lax.dot_general(a, b, (((1,), (1,)), ((), ())), preferred_element_type=F32)


def _dot_tn(a, b):
    return lax.dot_general(a, b, (((0,), (0,)), ((), ())), preferred_element_type=F32)


def _colsum(a):
    return jnp.sum(a, axis=0, keepdims=True)


def _sigmoid(x):
    return 1.0 / (1.0 + jnp.exp(-x))


def _gelu(x):
    t = jnp.tanh(GELU_K * (x + GELU_C * x * x * x))
    return 0.5 * x * (1.0 + t), t


def _gelu_grad(x, t):
    return 0.5 * (1.0 + t) + 0.5 * x * (1.0 - t * t) * (GELU_K * (1.0 + 3.0 * GELU_C * x * x))


def _tril_mask():
    r = lax.broadcasted_iota(jnp.int32, (CHUNK, CHUNK), 0)
    c = lax.broadcasted_iota(jnp.int32, (CHUNK, CHUNK), 1)
    return (r >= c).astype(F32)


def _my_index():
    return 4 * lax.axis_index("x") + 2 * lax.axis_index("y") + lax.axis_index("c")


def _peer(k):
    x, y, c = lax.axis_index("x"), lax.axis_index("y"), lax.axis_index("c")
    px = 1 - x if (k >> 2) & 1 else x
    py = 1 - y if (k >> 1) & 1 else y
    pc = 1 - c if k & 1 else c
    return (px, py, pc), 4 * px + 2 * py + pc


def _load_weights(step, pairs, sem):
    @pl.when(step == 0)
    def _():
        copies = [pltpu.make_async_copy(src, dst, sem.at[n]) for n, (src, dst) in enumerate(pairs)]
        for cp in copies:
            cp.start()
        for cp in copies:
            cp.wait()


def _wp_pairs(wp_hbm, wp_v):
    return [(wp_hbm.at[j], wp_v.at[:, pl.ds(HEAD_DIM * j, HEAD_DIM)]) for j in range(N_DEV)]


def _rms_mod(x, g, shift, scale):
    rstd = lax.rsqrt(jnp.mean(x * x, axis=-1, keepdims=True) + EPS)
    xn = x * rstd
    return xn, rstd, (xn * g) * (1.0 + scale) + shift


def _rms_mod_bwd(dh, xn, rstd, g, scale):
    dxn = dh * (1.0 + scale) * g
    dx = rstd * (dxn - xn * jnp.mean(dxn * xn, axis=-1, keepdims=True))
    return dx, _colsum(dh), _colsum(dh * (xn * g)), _colsum(dh * (1.0 + scale) * xn)


def _gmlp_s(vn_b, wmask_b, bexp, s_scr, tm):
    for ch in range(tm // CHUNK):
        rows = slice(ch * CHUNK, (ch + 1) * CHUNK)
        for hh in range(HEADS):
            cols = slice(hh * HEAD_DIM, (hh + 1) * HEAD_DIM)
            s_scr[rows, cols] = _dot(wmask_b[hh], vn_b[rows, cols]) + bexp[:, cols]
    return s_scr[...]


def _pool_p(xb, xbext, pos, tm):
    ps = []
    for g, win in enumerate(POOL_WINDOWS):
        cols = slice(g * HEAD_DIM, (g + 1) * HEAD_DIM)
        acc = xb[:, cols]
        for k in range(1, win):
            acc = acc + xbext[pl.ds(POOL_HALO - k, tm), cols]
        cnt = jnp.minimum(pos + 1, win).astype(F32)
        ps.append(acc / cnt - xb[:, cols])
    return ps


def _mixer_fwd(x, modl, g_mix, ln_g, ln_b, w_s, bexp, pool_w, pool_scale, conv_w, win, wpa, wpb, wpc, wo, layer, seq):
    t_all, d = x.shape
    tm = min(TOKEN_TILE, seq)
    n_seq_tiles = seq // tm
    blk = win.shape[-1]

    def body(x_ref, mod_ref, gmix_ref, lng_ref, lnb_ref, ws_ref, bexp_ref, pw_ref, ps_ref, cw_ref,
             win_hbm, wpa_hbm, wpb_hbm, wpc_hbm, wo_hbm,
             h_ref, z_ref, ycat_ref, ocat_ref, mrg_ref, mo_ref, xmid_ref,
             win_v, wpa_v, wpb_v, wpc_v, wo_v, xbext, zcext, s_scr, sem):
        i = pl.program_id(0)
        pairs = [(win_hbm, win_v), (wo_hbm, wo_v)]
        pairs += _wp_pairs(wpa_hbm, wpa_v) + _wp_pairs(wpb_hbm, wpb_v) + _wp_pairs(wpc_hbm, wpc_v)
        _load_weights(i, pairs, sem)
        tile_in_seq = i % n_seq_tiles

        @pl.when(tile_in_seq == 0)
        def _():
            xbext[0:POOL_HALO, :] = jnp.zeros((POOL_HALO, BR_W), F32)
            zcext[0:CONV_HALO, :] = jnp.zeros((CONV_HALO, BR_W), F32)

        x_t = x_ref[...]
        shift1, scale1, gate1 = mod_ref[0, 0:1, :], mod_ref[0, 1:2, :], mod_ref[0, 2:3, :]
        _, _, h = _rms_mod(x_t, gmix_ref[...], shift1, scale1)
        hb = h.astype(BF16)
        h_ref[...] = hb
        def project(j):
            zj = _dot(hb, win_v[j])
            z_ref[:, j * blk:(j + 1) * blk] = zj.astype(BF16)
            return zj

        z0, z1 = project(0), project(1)
        u = z0[:, 0:BR_W]
        v = jnp.concatenate([z0[:, BR_W:blk], z1[:, 0:2 * BR_W - blk]], axis=1)
        xb = z1[:, 2 * BR_W - blk:blk]

        gu, _ = _gelu(u)
        gv, _ = _gelu(v)
        mu = jnp.mean(gv, axis=-1, keepdims=True)
        cen = gv - mu
        rs = lax.rsqrt(jnp.mean(cen * cen, axis=-1, keepdims=True) + EPS)
        vn = (cen * rs) * lng_ref[...] + lnb_ref[...]
        mask = _tril_mask()
        wmask_b = [(ws_ref[hh] * mask).astype(BF16) for hh in range(HEADS)]
        s = _gmlp_s(vn.astype(BF16), wmask_b, bexp_ref[...], s_scr, tm)
        oa = (gu * s).astype(BF16)
        ya = _dot(oa, wpa_v[...])

        xbext[POOL_HALO:POOL_HALO + tm, :] = xb
        pos = tile_in_seq * tm + lax.broadcasted_iota(jnp.int32, (tm, 1), 0)
        ps = _pool_p(xb, xbext, pos, tm)
        qs = [_dot(ps[g].astype(BF16), pw_ref[g].astype(BF16)) for g in range(len(POOL_WINDOWS))]
        ob = (jnp.concatenate(qs, axis=1) * ps_ref[...]).astype(BF16)
        yb = _dot(ob, wpb_v[...])
        xbext[0:POOL_HALO, :] = xbext[tm:tm + POOL_HALO, :]

        z2, z3 = project(2), project(3)
        bg = z2[:, 0:BR_W]
        cg = jnp.concatenate([z2[:, BR_W:blk], z3[:, 0:2 * BR_W - blk]], axis=1)
        hc = z3[:, 2 * BR_W - blk:blk]
        zz = cg * hc
        zcext[CONV_HALO:CONV_HALO + tm, :] = zz
        yconv = (cw_ref[0:1, :] * zcext[pl.ds(CONV_HALO - 2, tm), :] + cw_ref[1:2, :] * zcext[pl.ds(CONV_HALO - 1, tm), :]
                 + cw_ref[2:3, :] * zz)
        oc = (bg * yconv).astype(BF16)
        yc = _dot(oc, wpc_v[...])
        zcext[0:CONV_HALO, :] = zcext[tm:tm + CONV_HALO, :]

        ocat_ref[:, 0:BR_W] = oa
        ocat_ref[:, BR_W:2 * BR_W] = ob
        ocat_ref[:, 2 * BR_W:3 * BR_W] = oc
        ycat_ref[:, 0:d] = ya.astype(BF16)
        ycat_ref[:, d:2 * d] = yb.astype(BF16)
        ycat_ref[:, 2 * d:3 * d] = yc.astype(BF16)

        merged = jnp.zeros((tm, d), F32)
        ys = (ya, yb, yc)
        zg = jnp.concatenate([project(j) for j in range(4, N_DEV)], axis=1)
        for n in range(3):
            merged = merged + _sigmoid(zg[:, n * d:(n + 1) * d]) * ys[n]
        mb = merged.astype(BF16)
        mrg_ref[...] = mb
        mo = _dot(mb, wo_v[...])
        mo_ref[...] = mo.astype(BF16)
        xmid_ref[...] = x_t + gate1 * mo

    tok = lambda cols: pl.BlockSpec((tm, cols), lambda i: (i, 0))
    const2 = lambda a: pl.BlockSpec(a.shape, lambda i: (0,) * a.ndim)
    in_specs = [
        tok(d),
        pl.BlockSpec((1, 8, d), lambda i: (i // n_seq_tiles, 0, 0)),
        const2(g_mix), const2(ln_g), const2(ln_b), const2(w_s), const2(bexp), const2(pool_w), const2(pool_scale),
        const2(conv_w), ANY, ANY, ANY, ANY, ANY,
    ]
    out_shape = [
        jax.ShapeDtypeStruct((t_all, d), BF16),
        jax.ShapeDtypeStruct((t_all, N_DEV * blk), BF16),
        jax.ShapeDtypeStruct((t_all, 3 * d), BF16),
        jax.ShapeDtypeStruct((t_all, 3 * BR_W), BF16),
        jax.ShapeDtypeStruct((t_all, d), BF16),
        jax.ShapeDtypeStruct((t_all, d), BF16),
        jax.ShapeDtypeStruct((t_all, d), F32),
    ]
    out_specs = [tok(d), tok(N_DEV * blk), tok(3 * d), tok(3 * BR_W), tok(d), tok(d), tok(d)]
    scratch = [
        pltpu.VMEM((N_DEV, d, blk), BF16), pltpu.VMEM((BR_W, d), BF16), pltpu.VMEM((BR_W, d), BF16),
        pltpu.VMEM((BR_W, d), BF16), pltpu.VMEM((d, d), BF16),
        pltpu.VMEM((tm + POOL_HALO, BR_W), F32), pltpu.VMEM((tm + CONV_HALO, BR_W), F32), pltpu.VMEM((tm, BR_W), F32),
        pltpu.SemaphoreType.DMA((2 + 3 * N_DEV,)),
    ]
    return pl.pallas_call(
        body, name=f"mixer_fwd_{layer}", grid=(t_all // tm,), in_specs=in_specs, out_specs=out_specs, out_shape=out_shape,
        scratch_shapes=scratch, compiler_params=_cparams(dimension_semantics=("arbitrary",)),
    )(x, modl, g_mix, ln_g, ln_b, w_s, bexp, pool_w, pool_scale, conv_w, win, wpa, wpb, wpc, wo)


def _ffn_fwd(x, modl, g_ffn, w13, w2, layer, seq):
    t_all, d = x.shape
    tm = min(TOKEN_TILE, seq)
    n_seq_tiles = seq // tm
    fb = w13.shape[-1]
    n_hid = N_DEV // 2

    def body(x_ref, mod_ref, g_ref, w13_hbm, w2_hbm, h_ref, ab_ref, hid_ref, f_ref, xo_ref, w13_v, w2_v, sem):
        i = pl.program_id(0)
        _load_weights(i, [(w13_hbm, w13_v), (w2_hbm, w2_v)], sem)
        x_t = x_ref[...]
        shift2, scale2, gate2 = mod_ref[0, 3:4, :], mod_ref[0, 4:5, :], mod_ref[0, 5:6, :]
        _, _, h = _rms_mod(x_t, g_ref[...], shift2, scale2)
        hb = h.astype(BF16)
        h_ref[...] = hb
        f = jnp.zeros((tm, d), F32)
        for k in range(n_hid):
            a = _dot(hb, w13_v[k])
            b = _dot(hb, w13_v[n_hid + k])
            ab_ref[k] = a.astype(BF16)
            ab_ref[n_hid + k] = b.astype(BF16)
            hid = ((a * _sigmoid(a)) * b).astype(BF16)
            hid_ref[k] = hid
            f = f + _dot(hid, w2_v[k * fb:(k + 1) * fb, :])
        f_ref[...] = f.astype(BF16)
        xo_ref[...] = x_t + gate2 * f

    tok = lambda cols: pl.BlockSpec((tm, cols), lambda i: (i, 0))
    blk3 = lambda n: pl.BlockSpec((n, tm, fb), lambda i: (0, i, 0))
    in_specs = [tok(d), pl.BlockSpec((1, 8, d), lambda i: (i // n_seq_tiles, 0, 0)),
                pl.BlockSpec(g_ffn.shape, lambda i: (0, 0)), ANY, ANY]
    out_shape = [
        jax.ShapeDtypeStruct((t_all, d), BF16),
        jax.ShapeDtypeStruct((N_DEV, t_all, fb), BF16),
        jax.ShapeDtypeStruct((n_hid, t_all, fb), BF16),
        jax.ShapeDtypeStruct((t_all, d), BF16),
        jax.ShapeDtypeStruct((t_all, d), F32),
    ]
    out_specs = [tok(d), blk3(N_DEV), blk3(n_hid), tok(d), tok(d)]
    scratch = [pltpu.VMEM((N_DEV, d, fb), BF16), pltpu.VMEM((n_hid * fb, d), BF16), pltpu.SemaphoreType.DMA((2,))]
    return pl.pallas_call(
        body, name=f"ffn_fwd_{layer}", grid=(t_all // tm,), in_specs=in_specs, out_specs=out_specs, out_shape=out_shape,
        scratch_shapes=scratch, compiler_params=_cparams(dimension_semantics=("arbitrary",)),
    )(x, modl, g_ffn, w13, w2)


def _final_loss(x, g_final, target, seq):
    t_all, d = x.shape
    tm = min(TOKEN_TILE, seq)

    def body(x_ref, g_ref, t_ref, dx_ref, loss_ref, dg_ref):
        i = pl.program_id(0)

        @pl.when(i == 0)
        def _():
            loss_ref[...] = jnp.zeros(loss_ref.shape, F32)
            dg_ref[...] = jnp.zeros(dg_ref.shape, F32)

        x_t = x_ref[...]
        g = g_ref[...]
        rstd = lax.rsqrt(jnp.mean(x_t * x_t, axis=-1, keepdims=True) + EPS)
        xn = x_t * rstd
        err = xn * g - t_ref[...]
        loss_ref[0:1, :] += _colsum(err * err) * (0.5 / d)
        dy = err * (1.0 / d)
        dg_ref[0:1, :] += _colsum(dy * xn)
        dxn = dy * g
        dx_ref[...] = rstd * (dxn - xn * jnp.mean(dxn * xn, axis=-1, keepdims=True))

        @pl.when(i == pl.num_programs(0) - 1)
        def _():
            loss_ref[...] = jnp.broadcast_to(jnp.sum(loss_ref[0:1, :], axis=1, keepdims=True), loss_ref.shape)

    tok = pl.BlockSpec((tm, d), lambda i: (i, 0))
    acc = pl.BlockSpec((8, d), lambda i: (0, 0))
    return pl.pallas_call(
        body, name="final_loss", grid=(t_all // tm,),
        in_specs=[tok, pl.BlockSpec((1, d), lambda i: (0, 0)), tok], out_specs=[tok, acc, acc],
        out_shape=[jax.ShapeDtypeStruct((t_all, d), F32), jax.ShapeDtypeStruct((8, d), F32), jax.ShapeDtypeStruct((8, d), F32)],
        compiler_params=_cparams(dimension_semantics=("arbitrary",)),
    )(x, g_final, target)


def _ffn_bwd(dxo, xmid, ab, f, modl, g_ffn, w13, w2, layer, seq):
    t_all, d = xmid.shape
    tm = min(TOKEN_TILE, seq)
    n_seq_tiles = seq // tm
    fb = w13.shape[-1]
    n_hid = N_DEV // 2

    def body(dxo_ref, x_ref, ab_ref, f_ref, mod_ref, g_ref, w13_hbm, w2_hbm,
             dx_ref, df_ref, dab_ref, dmod_ref, dg_ref, w13_v, w2_v, sem):
        i = pl.program_id(0)
        _load_weights(i, [(w13_hbm, w13_v), (w2_hbm, w2_v)], sem)

        @pl.when(i == 0)
        def _():
            dg_ref[...] = jnp.zeros(dg_ref.shape, F32)

        @pl.when(i % n_seq_tiles == 0)
        def _():
            dmod_ref[...] = jnp.zeros(dmod_ref.shape, F32)

        scale2, gate2 = mod_ref[0, 4:5, :], mod_ref[0, 5:6, :]
        g = g_ref[...]
        x_t = x_ref[...]
        rstd = lax.rsqrt(jnp.mean(x_t * x_t, axis=-1, keepdims=True) + EPS)
        xn = x_t * rstd
        dxo_t = dxo_ref[...]
        dmod_ref[0, 2:3, :] += _colsum(dxo_t * f_ref[...].astype(F32))
        dfb = (dxo_t * gate2).astype(BF16)
        df_ref[...] = dfb
        dh = jnp.zeros((tm, d), F32)
        for k in range(n_hid):
            dhid = _dot_nt(dfb, w2_v[k * fb:(k + 1) * fb, :])
            a = ab_ref[k].astype(F32)
            b = ab_ref[n_hid + k].astype(F32)
            sg = _sigmoid(a)
            da = (dhid * b * (sg * (1.0 + a * (1.0 - sg)))).astype(BF16)
            db = (dhid * (a * sg)).astype(BF16)
            dab_ref[k] = da
            dab_ref[n_hid + k] = db
            dh = dh + _dot_nt(da, w13_v[k]) + _dot_nt(db, w13_v[n_hid + k])
        dx, dshift, dscale, dg = _rms_mod_bwd(dh, xn, rstd, g, scale2)
        dmod_ref[0, 0:1, :] += dshift
        dmod_ref[0, 1:2, :] += dscale
        dg_ref[0:1, :] += dg
        dx_ref[...] = dxo_t + dx

    tok = lambda cols: pl.BlockSpec((tm, cols), lambda i: (i, 0))
    blk3 = lambda n: pl.BlockSpec((n, tm, fb), lambda i: (0, i, 0))
    modspec = pl.BlockSpec((1, 8, d), lambda i: (i // n_seq_tiles, 0, 0))
    in_specs = [tok(d), tok(d), blk3(N_DEV), tok(d), modspec, pl.BlockSpec(g_ffn.shape, lambda i: (0, 0)), ANY, ANY]
    out_shape = [
        jax.ShapeDtypeStruct((t_all, d), F32), jax.ShapeDtypeStruct((t_all, d), BF16),
        jax.ShapeDtypeStruct((N_DEV, t_all, fb), BF16), jax.ShapeDtypeStruct(modl.shape, F32),
        jax.ShapeDtypeStruct((8, d), F32),
    ]
    out_specs = [tok(d), tok(d), blk3(N_DEV), modspec, pl.BlockSpec((8, d), lambda i: (0, 0))]
    scratch = [pltpu.VMEM((N_DEV, d, fb), BF16), pltpu.VMEM((n_hid * fb, d), BF16), pltpu.SemaphoreType.DMA((2,))]
    return pl.pallas_call(
        body, name=f"ffn_bwd_{layer}", grid=(t_all // tm,), in_specs=in_specs, out_specs=out_specs, out_shape=out_shape,
        scratch_shapes=scratch, compiler_params=_cparams(dimension_semantics=("arbitrary",)),
    )(dxo, xmid, ab, f, modl, g_ffn, w13, w2)


def _mixer_bwd(dxm, x, z, ycat, mo, modl, g_mix, ln_g, ln_b, w_s, bexp, pool_w, pool_scale, conv_w,
               win, wpa, wpb, wpc, wo, layer, seq):
    t_all, d = x.shape
    tm = min(TOKEN_TILE, seq)
    n_seq_tiles = seq // tm
    blk = win.shape[-1]
    n_win = len(POOL_WINDOWS)

    def tile_of(i):
        return (i // n_seq_tiles) * n_seq_tiles + (n_seq_tiles - 1 - i % n_seq_tiles)

    def halo_row_block(i):
        return jnp.maximum(tile_of(i) * (tm // POOL_HALO) - 1, 0)

    def body(dxm_ref, x_ref, z_ref, zpb_ref, zpc_ref, ycat_ref, mo_ref, mod_ref, gmix_ref, lng_ref, lnb_ref, ws_ref,
             bexp_ref, pw_ref, ps_ref, cw_ref, win_hbm, wpa_hbm, wpb_hbm, wpc_hbm, wo_hbm,
             dx_ref, dz_ref, dycat_ref, dmo_ref, dmod_ref, dg_ref, sm_ref, dws_ref, dssum_ref, dpw_ref,
             win_v, wpa_v, wpb_v, wpc_v, wo_v, xbext, zzext, rext, dyext, s_scr, dvn_scr, sem):
        i = pl.program_id(0)
        pairs = [(win_hbm, win_v), (wo_hbm, wo_v)]
        pairs += _wp_pairs(wpa_hbm, wpa_v) + _wp_pairs(wpb_hbm, wpb_v) + _wp_pairs(wpc_hbm, wpc_v)
        _load_weights(i, pairs, sem)
        tile_in_seq = n_seq_tiles - 1 - i % n_seq_tiles
        first_of_seq = tile_in_seq == 0

        @pl.when(i == 0)
        def _():
            for r in (dg_ref, sm_ref, dws_ref, dssum_ref, dpw_ref):
                r[...] = jnp.zeros(r.shape, F32)

        @pl.when(i % n_seq_tiles == 0)
        def _():
            dmod_ref[...] = jnp.zeros(dmod_ref.shape, F32)
            rext[tm:tm + POOL_HALO, :] = jnp.zeros((POOL_HALO, BR_W), F32)
            dyext[tm:tm + CONV_HALO, :] = jnp.zeros((CONV_HALO, BR_W), F32)

        shift1, scale1, gate1 = mod_ref[0, 0:1, :], mod_ref[0, 1:2, :], mod_ref[0, 2:3, :]
        g = gmix_ref[...]
        x_t = x_ref[...]
        rstd = lax.rsqrt(jnp.mean(x_t * x_t, axis=-1, keepdims=True) + EPS)
        xn = x_t * rstd
        dxm_t = dxm_ref[...]
        dmod_ref[0, 2:3, :] += _colsum(dxm_t * mo_ref[...].astype(F32))
        dmo = (dxm_t * gate1).astype(BF16)
        dmo_ref[...] = dmo
        dmerged = _dot_nt(dmo, wo_v[...])

        dys = []
        for n in range(3):
            zg = z_ref[:, 3 * d + n * d:3 * d + (n + 1) * d].astype(F32)
            gt = _sigmoid(zg)
            yn = ycat_ref[:, n * d:(n + 1) * d].astype(F32)
            dz_ref[:, 3 * d + n * d:3 * d + (n + 1) * d] = (dmerged * yn * gt * (1.0 - gt)).astype(BF16)
            dyn = (dmerged * gt).astype(BF16)
            dycat_ref[:, n * d:(n + 1) * d] = dyn
            dys.append(dyn)
        dh = jnp.zeros((tm, d), F32)
        for j in range(4, N_DEV):
            dh = dh + _dot_nt(dz_ref[:, j * blk:(j + 1) * blk], win_v[j])

        doa = _dot_nt(dys[0], wpa_v[...])
        u = z_ref[:, 0:BR_W].astype(F32)
        v = z_ref[:, BR_W:2 * BR_W].astype(F32)
        gu, tu = _gelu(u)
        gv, tv = _gelu(v)
        mu = jnp.mean(gv, axis=-1, keepdims=True)
        cen = gv - mu
        rs = lax.rsqrt(jnp.mean(cen * cen, axis=-1, keepdims=True) + EPS)
        vhat = cen * rs
        lng = lng_ref[...]
        vn_b = (vhat * lng + lnb_ref[...]).astype(BF16)
        mask = _tril_mask()
        wmask = [ws_ref[hh] * mask for hh in range(HEADS)]
        s = _gmlp_s(vn_b, [w.astype(BF16) for w in wmask], bexp_ref[...], s_scr, tm)
        du = (doa * s) * _gelu_grad(u, tu)
        ds = doa * gu
        ds_b = ds.astype(BF16)
        dssum = jnp.zeros((CHUNK, BR_W), F32)
        for ch in range(tm // CHUNK):
            rows = slice(ch * CHUNK, (ch + 1) * CHUNK)
            dssum = dssum + ds[rows, :]
            for hh in range(HEADS):
                cols = slice(hh * HEAD_DIM, (hh + 1) * HEAD_DIM)
                dvn_scr[rows, cols] = _dot_tn(wmask[hh].astype(BF16), ds_b[rows, cols])
                dws_ref[hh] += _dot_nt(ds_b[rows, cols], vn_b[rows, cols]) * mask
        dssum_ref[...] += dssum
        dvn = dvn_scr[...]
        sm_ref[0:1, :] += _colsum(dvn * vhat)
        sm_ref[1:2, :] += _colsum(dvn)
        dvhat = dvn * lng
        dgv = rs * (dvhat - jnp.mean(dvhat, axis=-1, keepdims=True) - vhat * jnp.mean(dvhat * vhat, axis=-1, keepdims=True))
        dv = dgv * _gelu_grad(v, tv)
        dz_ref[:, 0:BR_W] = du.astype(BF16)
        dz_ref[:, BR_W:2 * BR_W] = dv.astype(BF16)

        dob = _dot_nt(dys[1], wpb_v[...])
        xb = z_ref[:, 2 * BR_W:3 * BR_W].astype(F32)
        xbext[0:POOL_HALO, :] = jnp.where(first_of_seq, 0.0, zpb_ref[...].astype(F32))
        xbext[POOL_HALO:POOL_HALO + tm, :] = xb
        pos = tile_in_seq * tm + lax.broadcasted_iota(jnp.int32, (tm, 1), 0)
        ps = _pool_p(xb, xbext, pos, tm)
        scale_b = ps_ref[...]
        dq = dob * scale_b
        qs, dps = [], []
        for gi, win_len in enumerate(POOL_WINDOWS):
            cols = slice(gi * HEAD_DIM, (gi + 1) * HEAD_DIM)
            pw_b = pw_ref[gi].astype(BF16)
            p_b = ps[gi].astype(BF16)
            dq_b = dq[:, cols].astype(BF16)
            qs.append(_dot(p_b, pw_b))
            dpw_ref[gi] += _dot_tn(p_b, dq_b)
            dp = _dot_nt(dq_b, pw_b)
            dps.append(dp)
            cnt = jnp.minimum(pos + 1, win_len).astype(F32)
            rext[0:tm, cols] = dp / cnt
        sm_ref[2:3, :] += _colsum(dob * jnp.concatenate(qs, axis=1))
        dxbs = []
        for gi, win_len in enumerate(POOL_WINDOWS):
            cols = slice(gi * HEAD_DIM, (gi + 1) * HEAD_DIM)
            acc = rext[0:tm, cols]
            for k in range(1, win_len):
                acc = acc + rext[pl.ds(k, tm), cols]
            dxbs.append(acc - dps[gi])
        dz_ref[:, 2 * BR_W:3 * BR_W] = jnp.concatenate(dxbs, axis=1).astype(BF16)
        rext[tm:tm + POOL_HALO, :] = rext[0:POOL_HALO, :]

        doc = _dot_nt(dys[2], wpc_v[...])
        bg = z_ref[:, 3 * BR_W:4 * BR_W].astype(F32)
        cg = z_ref[:, 4 * BR_W:5 * BR_W].astype(F32)
        hc = z_ref[:, 5 * BR_W:6 * BR_W].astype(F32)
        zz = cg * hc
        zprev = zpc_ref[POOL_HALO - CONV_HALO:POOL_HALO, :].astype(F32)
        zzext[0:CONV_HALO, :] = jnp.where(first_of_seq, 0.0, zprev[:, 0:BR_W] * zprev[:, BR_W:2 * BR_W])
        zzext[CONV_HALO:CONV_HALO + tm, :] = zz
        zm2 = zzext[pl.ds(CONV_HALO - 2, tm), :]
        zm1 = zzext[pl.ds(CONV_HALO - 1, tm), :]
        w0, w1, w2c = cw_ref[0:1, :], cw_ref[1:2, :], cw_ref[2:3, :]
        yconv = w0 * zm2 + w1 * zm1 + w2c * zz
        dyc = doc * bg
        sm_ref[3:4, :] += _colsum(dyc * zm2)
        sm_ref[4:5, :] += _colsum(dyc * zm1)
        sm_ref[5:6, :] += _colsum(dyc * zz)
        dyext[0:tm, :] = dyc
        dzz = w2c * dyc + w1 * dyext[pl.ds(1, tm), :] + w0 * dyext[pl.ds(2, tm), :]
        dyext[tm:tm + CONV_HALO, :] = dyext[0:CONV_HALO, :]
        dz_ref[:, 3 * BR_W:4 * BR_W] = (doc * yconv).astype(BF16)
        dz_ref[:, 4 * BR_W:5 * BR_W] = (dzz * hc).astype(BF16)
        dz_ref[:, 5 * BR_W:6 * BR_W] = (dzz * cg).astype(BF16)

        for j in range(4):
            dh = dh + _dot_nt(dz_ref[:, j * blk:(j + 1) * blk], win_v[j])
        dx, dshift, dscale, dg = _rms_mod_bwd(dh, xn, rstd, g, scale1)
        dmod_ref[0, 0:1, :] += dshift
        dmod_ref[0, 1:2, :] += dscale
        dg_ref[0:1, :] += dg
        dx_ref[...] = dxm_t + dx

    tok = lambda cols: pl.BlockSpec((tm, cols), lambda i: (tile_of(i), 0))
    const2 = lambda a: pl.BlockSpec(a.shape, lambda i: (0,) * a.ndim)
    modspec = pl.BlockSpec((1, 8, d), lambda i: (i // n_seq_tiles, 0, 0))
    in_specs = [
        tok(d), tok(d), tok(N_DEV * blk),
        pl.BlockSpec((POOL_HALO, BR_W), lambda i: (halo_row_block(i), 2)),
        pl.BlockSpec((POOL_HALO, 2 * BR_W), lambda i: (halo_row_block(i), 2)),
        tok(3 * d), tok(d), modspec,
        const2(g_mix), const2(ln_g), const2(ln_b), const2(w_s), const2(bexp), const2(pool_w), const2(pool_scale),
        const2(conv_w), ANY, ANY, ANY, ANY, ANY,
    ]
    acc = lambda shape: pl.BlockSpec(shape, lambda i: (0,) * len(shape))
    out_shape = [
        jax.ShapeDtypeStruct((t_all, d), F32), jax.ShapeDtypeStruct((t_all, N_DEV * blk), BF16),
        jax.ShapeDtypeStruct((t_all, 3 * d), BF16), jax.ShapeDtypeStruct((t_all, d), BF16),
        jax.ShapeDtypeStruct(modl.shape, F32), jax.ShapeDtypeStruct((8, d), F32), jax.ShapeDtypeStruct((8, BR_W), F32),
        jax.ShapeDtypeStruct((HEADS, CHUNK, CHUNK), F32), jax.ShapeDtypeStruct((CHUNK, BR_W), F32),
        jax.ShapeDtypeStruct((n_win, HEAD_DIM, HEAD_DIM), F32),
    ]
    out_specs = [tok(d), tok(N_DEV * blk), tok(3 * d), tok(d), modspec, acc((8, d)), acc((8, BR_W)),
                 acc((HEADS, CHUNK, CHUNK)), acc((CHUNK, BR_W)), acc((n_win, HEAD_DIM, HEAD_DIM))]
    scratch = [
        pltpu.VMEM((N_DEV, d, blk), BF16), pltpu.VMEM((BR_W, d), BF16), pltpu.VMEM((BR_W, d), BF16),
        pltpu.VMEM((BR_W, d), BF16), pltpu.VMEM((d, d), BF16),
        pltpu.VMEM((tm + POOL_HALO, BR_W), F32), pltpu.VMEM((tm + CONV_HALO, BR_W), F32),
        pltpu.VMEM((tm + POOL_HALO, BR_W), F32), pltpu.VMEM((tm + CONV_HALO, BR_W), F32),
        pltpu.VMEM((tm, BR_W), F32), pltpu.VMEM((tm, BR_W), F32),
        pltpu.SemaphoreType.DMA((2 + 3 * N_DEV,)),
    ]
    return pl.pallas_call(
        body, name=f"mixer_bwd_{layer}", grid=(t_all // tm,), in_specs=in_specs, out_specs=out_specs, out_shape=out_shape,
        scratch_shapes=scratch, compiler_params=_cparams(dimension_semantics=("arbitrary",)),
    )(dxm, x, z, z, z, ycat, mo, modl, g_mix, ln_g, ln_b, w_s, bexp, pool_w, pool_scale, conv_w, win, wpa, wpb, wpc, wo)


def _wgrad(a, b, a_spec, b_spec, out_struct, out_spec, grid_kn, tk, tn, split, name):
    t_all = a.shape[-2]
    tt = min(WGRAD_TOKENS, t_all)
    n_t = t_all // tt

    def body(a_ref, b_ref, o_ref, acc):
        t = pl.program_id(2)

        @pl.when(t == 0)
        def _():
            acc[...] = jnp.zeros(acc.shape, F32)

        acc[...] += _dot_tn(a_ref[...], b_ref[...])

        @pl.when(t == n_t - 1)
        def _():
            if split:
                for j in range(split):
                    w = tn // split
                    o_ref[j] = acc[:, j * w:(j + 1) * w].astype(o_ref.dtype)
            else:
                o_ref[...] = acc[...].astype(o_ref.dtype)

    return pl.pallas_call(
        body, name=name, grid=(*grid_kn, n_t), in_specs=[a_spec(tt), b_spec(tt)], out_specs=out_spec, out_shape=out_struct,
        scratch_shapes=[pltpu.VMEM((tk, tn), F32)],
        compiler_params=_cparams(dimension_semantics=("arbitrary", "arbitrary", "arbitrary")),
    )(a, b)


def _mixer_wgrads(h, dz, mrg, dmo, ocat, dycat, layer):
    d = h.shape[1]
    blk = dz.shape[1] // N_DEV
    g_win = _wgrad(
        h, dz, lambda tt: pl.BlockSpec((tt, d), lambda k, n, t: (t, 0)), lambda tt: pl.BlockSpec((tt, blk), lambda k, n, t: (t, n)),
        jax.ShapeDtypeStruct((N_DEV, d, blk), BF16), pl.BlockSpec((None, d, blk), lambda k, n, t: (n, 0, 0)),
        (1, N_DEV), d, blk, 0, f"wgrad_in_{layer}")
    g_wo = _wgrad(
        mrg, dmo, lambda tt: pl.BlockSpec((tt, d), lambda k, n, t: (t, 0)), lambda tt: pl.BlockSpec((tt, d), lambda k, n, t: (t, 0)),
        jax.ShapeDtypeStruct((d, d), BF16), pl.BlockSpec((d, d), lambda k, n, t: (0, 0)), (1, 1), d, d, 0, f"wgrad_o_{layer}")
    g_wp = []
    for n, nm in enumerate("abc"):
        g_wp.append(_wgrad(
            ocat, dycat, lambda tt, n=n: pl.BlockSpec((tt, BR_W), lambda k, nn, t: (t, n)),
            lambda tt, n=n: pl.BlockSpec((tt, d), lambda k, nn, t: (t, n)),
            jax.ShapeDtypeStruct((N_DEV, BR_W, d // N_DEV), BF16),
            pl.BlockSpec((N_DEV, BR_W, d // N_DEV), lambda k, nn, t: (0, 0, 0)), (1, 1), BR_W, d, N_DEV, f"wgrad_p{nm}_{layer}"))
    return [g_win, g_wo.reshape(N_DEV, d // N_DEV, d), *g_wp]


def _ffn_wgrads(h2, dab, hid, df, layer):
    d = h2.shape[1]
    fb = dab.shape[-1]
    n_hid = N_DEV // 2
    g_w13 = _wgrad(
        h2, dab, lambda tt: pl.BlockSpec((tt, d), lambda k, n, t: (t, 0)),
        lambda tt: pl.BlockSpec((None, tt, fb), lambda k, n, t: (n, t, 0)),
        jax.ShapeDtypeStruct((N_DEV, d, fb), BF16), pl.BlockSpec((None, d, fb), lambda k, n, t: (n, 0, 0)),
        (1, N_DEV), d, fb, 0, f"wgrad_13_{layer}")
    g_w2 = _wgrad(
        hid, df, lambda tt: pl.BlockSpec((None, tt, fb), lambda k, n, t: (k, t, 0)),
        lambda tt: pl.BlockSpec((tt, d), lambda k, n, t: (t, 0)),
        jax.ShapeDtypeStruct((n_hid * fb, d), BF16), pl.BlockSpec((fb, d), lambda k, n, t: (k, 0)),
        (n_hid, 1), fb, d, 0, f"wgrad_2_{layer}")
    return [g_w13, g_w2.reshape(N_DEV, fb // 2, d)]


def _remote(src, dst, send_sem, recv_sem, dev):
    return pltpu.make_async_remote_copy(src_ref=src, dst_ref=dst, send_sem=send_sem, recv_sem=recv_sem, device_id=dev,
                                        device_id_type=MESH_ID)


HBM_SPEC = pl.BlockSpec(memory_space=pltpu.HBM)
SEM_SPEC = pl.BlockSpec(memory_space=pltpu.SEMAPHORE)
DATAFLOW = pltpu.SideEffectType.DATAFLOW_SIDE_EFFECTING
GATHER, SCATTER = "gather", "scatter"


def _xfer_src(src, kind, peer_index):
    return src if kind == GATHER else src.at[peer_index]


def _xfer_sem(a, k):
    return a * (N_DEV - 1) + k - 1


def _landing_zones(srcs, kind):
    me = _my_index()
    zones = []
    for s in srcs:
        own = s[None] if kind == GATHER else lax.dynamic_index_in_dim(s, me, 0, keepdims=True)
        zones.append(lax.dynamic_update_index_in_dim(lax.empty((N_DEV, *own.shape[1:]), s.dtype), own, me, 0))
    return zones


def _xfer_start(srcs, kind, name, after):
    n = len(srcs)
    lands = _landing_zones(srcs, kind)

    def body(*refs):
        src, land = refs[:n], refs[n:2 * n]
        send_sems, recv_sems = refs[2 * n + 1], refs[2 * n + 2]
        token = refs[-1]
        me = _my_index()
        for k in range(1, N_DEV):
            dev, idx = _peer(k)
            for a in range(n):
                q = _xfer_sem(a, k)
                _remote(_xfer_src(src[a], kind, idx), land[a].at[me], send_sems.at[q], recv_sems.at[q], dev).start()
        token[...] = jnp.zeros(token.shape, token.dtype)

    both = [*srcs, *lands]
    sems = pltpu.SemaphoreType.DMA((n * (N_DEV - 1),))
    out_shape = (sems, sems, *[pltpu.HBM(a.shape, a.dtype) for a in both], jax.ShapeDtypeStruct((8, 128), F32))
    outs = pl.pallas_call(
        body, name=name, in_specs=[HBM_SPEC] * (2 * n) + [ANY],
        out_specs=(SEM_SPEC, SEM_SPEC, *[HBM_SPEC] * (2 * n), VMEM_SPEC),
        out_shape=out_shape, input_output_aliases={i: 2 + i for i in range(2 * n)},
        compiler_params=pltpu.CompilerParams(has_side_effects=DATAFLOW),
    )(*[pltpu.with_memory_space_constraint(a, pltpu.HBM) for a in both], after)
    return outs[0], outs[1], list(outs[2:2 + n]), list(outs[2 + n:2 + 2 * n]), outs[-1]


def _xfer_wait(send_sems, recv_sems, srcs, lands, rows, after, kind, name):
    n = len(srcs)

    def body(*refs):
        src, land = refs[:n], refs[n:2 * n]
        send_sems, recv_sems = refs[2 * n], refs[2 * n + 1]
        for k in range(1, N_DEV):
            dev, idx = _peer(k)
            for a in range(n):
                q = _xfer_sem(rows[a], k)
                cp = _remote(_xfer_src(src[a], kind, idx), land[a].at[idx], send_sems.at[q], recv_sems.at[q], dev)
                cp.wait_send()
                cp.wait_recv()

    both = [*srcs, *lands]
    outs = pl.pallas_call(
        body, name=name, in_specs=[HBM_SPEC] * (2 * n) + [SEM_SPEC, SEM_SPEC, ANY], out_specs=[HBM_SPEC] * (2 * n),
        out_shape=[pltpu.HBM(a.shape, a.dtype) for a in both], input_output_aliases={i: i for i in range(2 * n)},
        compiler_params=pltpu.CompilerParams(has_side_effects=DATAFLOW),
    )(*both, send_sems, recv_sems, after)
    return list(outs[n:])


def _pre(c_pad, conv_pad, w_mod, b_mod_mine):
    n_layers, d, blk = w_mod.shape

    def body(c_ref, conv_ref, wmod_ref, bmod_ref, cact_ref, mod_ref, convall_ref, cact_mine, msh, send_sems, recv_sems):
        me = _my_index()
        c = c_ref[...]
        cact_mine[...] = c * _sigmoid(c)
        cact_ref[me] = cact_mine[...]
        convall_ref[me] = conv_ref[...]
        sends = []
        for k in range(1, N_DEV):
            dev, _ = _peer(k)
            sends.append(_remote(cact_mine, cact_ref.at[me], send_sems.at[0, k - 1], recv_sems.at[0, k - 1], dev))
            sends.append(_remote(conv_ref, convall_ref.at[me], send_sems.at[1, k - 1], recv_sems.at[1, k - 1], dev))
        for cp in sends:
            cp.start()
        for k in range(1, N_DEV):
            dev, idx = _peer(k)
            _remote(cact_mine, cact_ref.at[idx], send_sems.at[0, k - 1], recv_sems.at[0, k - 1], dev).wait_recv()
            _remote(conv_ref, convall_ref.at[idx], send_sems.at[1, k - 1], recv_sems.at[1, k - 1], dev).wait_recv()
        for cp in sends:
            cp.wait_send()
        cact_b = cact_ref[...].reshape(N_DEV * 8, d).astype(BF16)
        for l in range(n_layers):
            m = _dot(cact_b, wmod_ref[l].astype(BF16)) + bmod_ref[l]
            msh[l] = m.reshape(N_DEV, 8, blk)
        mod_ref[me] = msh[:, me]
        sends = []
        for k in range(1, N_DEV):
            dev, idx = _peer(k)
            sends.append(_remote(msh.at[:, idx], mod_ref.at[me], send_sems.at[2, k - 1], recv_sems.at[2, k - 1], dev))
        for cp in sends:
            cp.start()
        for k in range(1, N_DEV):
            dev, idx = _peer(k)
            _remote(msh.at[:, idx], mod_ref.at[idx], send_sems.at[2, k - 1], recv_sems.at[2, k - 1], dev).wait_recv()
        for cp in sends:
            cp.wait_send()

    out_shape = [jax.ShapeDtypeStruct((N_DEV, 8, d), F32), jax.ShapeDtypeStruct((N_DEV, n_layers, 8, blk), F32),
                 jax.ShapeDtypeStruct((N_DEV, *conv_pad.shape), F32)]
    return pl.pallas_call(
        body, name="pre", in_specs=[VMEM_SPEC] * 4, out_specs=[VMEM_SPEC] * 3, out_shape=out_shape,
        scratch_shapes=[pltpu.VMEM((8, d), F32), pltpu.VMEM((n_layers, N_DEV, 8, blk), F32),
                        pltpu.SemaphoreType.DMA((3, N_DEV - 1)), pltpu.SemaphoreType.DMA((3, N_DEV - 1))],
        compiler_params=_cparams(has_side_effects=True),
    )(c_pad, conv_pad, w_mod, b_mod_mine)


def _small(buf, dmod_blocks, cact_all):
    rows = buf.shape[0]
    seg = rows // N_DEV
    _, n_layers, _, blk = dmod_blocks.shape
    d = cact_all.shape[-1]

    def body(buf_ref, dmod_ref, cact_ref, out_ref, gw_ref, rs_recv, red, drecv, send_sems, recv_sems):
        me = _my_index()
        mine = pl.ds(pl.multiple_of(me * seg, 8), seg)
        sends = []
        for k in range(1, N_DEV):
            dev, idx = _peer(k)
            theirs = pl.ds(pl.multiple_of(idx * seg, 8), seg)
            sends.append(_remote(buf_ref.at[theirs], rs_recv.at[k - 1], send_sems.at[0, k - 1], recv_sems.at[0, k - 1], dev))
            sends.append(_remote(dmod_ref.at[idx], drecv.at[me], send_sems.at[1, k - 1], recv_sems.at[1, k - 1], dev))
        for cp in sends:
            cp.start()
        drecv[me] = dmod_ref[me]
        for k in range(1, N_DEV):
            dev, idx = _peer(k)
            _remote(buf_ref.at[mine], rs_recv.at[k - 1], send_sems.at[0, k - 1], recv_sems.at[0, k - 1], dev).wait_recv()
            _remote(dmod_ref.at[idx], drecv.at[idx], send_sems.at[1, k - 1], recv_sems.at[1, k - 1], dev).wait_recv()
        for cp in sends:
            cp.wait_send()
        total = buf_ref[mine, :]
        for k in range(1, N_DEV):
            total = total + rs_recv[k - 1]
        red[...] = total
        out_ref[mine, :] = total
        sends = []
        for k in range(1, N_DEV):
            dev, _ = _peer(k)
            sends.append(_remote(red, out_ref.at[mine], send_sems.at[2, k - 1], recv_sems.at[2, k - 1], dev))
        for cp in sends:
            cp.start()
        cact_b = cact_ref[...].reshape(N_DEV * 8, d).astype(BF16)
        for l in range(n_layers):
            gw_ref[l] = _dot_tn(cact_b, drecv[:, l].reshape(N_DEV * 8, blk).astype(BF16))
        for k in range(1, N_DEV):
            dev, idx = _peer(k)
            theirs = pl.ds(pl.multiple_of(idx * seg, 8), seg)
            _remote(red, out_ref.at[theirs], send_sems.at[2, k - 1], recv_sems.at[2, k - 1], dev).wait_recv()
        for cp in sends:
            cp.wait_send()

    out_shape = [jax.ShapeDtypeStruct(buf.shape, F32), jax.ShapeDtypeStruct((n_layers, d, blk), F32)]
    return pl.pallas_call(
        body, name="small_allreduce", in_specs=[VMEM_SPEC] * 3, out_specs=[VMEM_SPEC] * 2, out_shape=out_shape,
        scratch_shapes=[pltpu.VMEM((N_DEV - 1, seg, 128), F32), pltpu.VMEM((seg, 128), F32),
                        pltpu.VMEM(dmod_blocks.shape, F32),
                        pltpu.SemaphoreType.DMA((3, N_DEV - 1)), pltpu.SemaphoreType.DMA((3, N_DEV - 1))],
        compiler_params=_cparams(has_side_effects=True),
    )(buf, dmod_blocks, cact_all)


def _adamw_math(w, g, m, v):
    m = ADAM_B1 * m + (1.0 - ADAM_B1) * g
    v = ADAM_B2 * v + (1.0 - ADAM_B2) * (g * g)
    m_hat = m / (1.0 - ADAM_B1 ** ADAM_STEP)
    v_hat = v / (1.0 - ADAM_B2 ** ADAM_STEP)
    delta = -ADAM_LR * (m_hat / (jnp.sqrt(v_hat) + ADAM_EPS) + ADAM_WD * w)
    return delta, m, v


def _adamw(parts, w, m, v, name, first=0, earlier=None):
    n_layers = len(parts)
    n_parts, rows, cols = parts[0].shape
    tr = rows
    while tr * cols * 4 > (1 << 20) and tr % 32 == 0:
        tr //= 2
    n_r = rows // tr
    n_earlier = 0 if earlier is None else 4

    def body(*refs):
        p_refs = refs[:n_layers]
        w_ref, m_ref, v_ref = refs[n_layers:n_layers + 3]
        g_out, d_out, m_out, v_out = refs[n_layers + 3 + n_earlier:]
        layer = pl.program_id(0)
        for q in range(n_layers):

            @pl.when(layer == q)
            def _(q=q):
                g = p_refs[q][0].astype(F32)
                for p in range(1, n_parts):
                    g = g + p_refs[q][p].astype(F32)
                delta, m_new, v_new = _adamw_math(w_ref[...], g, m_ref[...], v_ref[...])
                g_out[...] = g
                d_out[...] = delta
                m_out[...] = m_new
                v_out[...] = v_new

    def parts_spec(q):
        return pl.BlockSpec((n_parts, tr, cols), lambda l, r: (0, jnp.where(l == q, r, jnp.where(l < q, 0, n_r - 1)), 0))

    spec = pl.BlockSpec((None, tr, cols), lambda l, r: (first + l, r, 0))
    out = jax.ShapeDtypeStruct(w.shape, F32)
    n_in = n_layers + 3
    return pl.pallas_call(
        body, name=name, grid=(n_layers, n_r),
        in_specs=[parts_spec(q) for q in range(n_layers)] + [spec, spec, spec] + [ANY] * n_earlier,
        out_specs=[spec] * 4, out_shape=[out] * 4, input_output_aliases={n_in + i: i for i in range(n_earlier)},
        compiler_params=_cparams(dimension_semantics=("arbitrary", "arbitrary")),
    )(*parts, w, m, v, *(earlier or ()))


def _adamw_flat(g, w, m, v):
    def body(g_ref, w_ref, m_ref, v_ref, d_out, m_out, v_out):
        delta, m_new, v_new = _adamw_math(w_ref[...], g_ref[...], m_ref[...], v_ref[...])
        d_out[...] = delta
        m_out[...] = m_new
        v_out[...] = v_new

    out = jax.ShapeDtypeStruct(g.shape, F32)
    return pl.pallas_call(body, name="adamw_small", in_specs=[VMEM_SPEC] * 4, out_specs=[VMEM_SPEC] * 3, out_shape=[out] * 3,
                          compiler_params=_cparams())(g, w, m, v)


def _pack(arrays, rows_multiple):
    flat = jnp.concatenate([a.reshape(-1) for a in arrays])
    per = 128 * rows_multiple
    total = -(-flat.shape[0] // per) * per
    return jnp.pad(flat, (0, total - flat.shape[0])).reshape(total // 128, 128)


def _unpack(buf, like):
    flat = buf.reshape(-1)
    out, off = [], 0
    for a in like:
        out.append(flat[off:off + a.size].reshape(a.shape))
        off += a.size
    return out


def kernel(x, c, w_mod, b_mod, g_mix, w_in, gm_ln_g, gm_ln_b, gm_w_s, gm_b_s, w_pa, pool_w, pool_scale, w_pb, conv_w, w_pc, w_o, g_ffn, w_13, w_2, g_final, loss_target, m_w_mod, m_b_mod, m_g_mix, m_w_in, m_gm_ln_g, m_gm_ln_b, m_gm_w_s, m_gm_b_s, m_w_pa, m_pool_w, m_pool_scale, m_w_pb, m_conv_w, m_w_pc, m_w_o, m_g_ffn, m_w_13, m_w_2, m_g_final, v_w_mod, v_b_mod, v_g_mix, v_w_in, v_gm_ln_g, v_gm_ln_b, v_gm_w_s, v_gm_b_s, v_w_pa, v_pool_w, v_pool_scale, v_w_pb, v_conv_w, v_w_pc, v_w_o, v_g_ffn, v_w_13, v_w_2, v_g_final):
    nb, seq, d = x.shape
    n_layers = w_in.shape[0]
    t_all = nb * seq
    blk = w_in.shape[-1]
    me = _my_index()
    conv_shard = conv_w.shape[-1]

    c_pad = jnp.pad(c, ((0, 8 - nb), (0, 0)))
    conv_pad = jnp.pad(conv_w.reshape(n_layers * 3, conv_shard), ((0, 16 - n_layers * 3), (0, 128 - conv_shard)))
    b_mod_mine = lax.dynamic_slice_in_dim(b_mod, me * blk, blk, axis=1).reshape(n_layers, 1, blk)
    cact_all, mod_blocks, conv_all = _pre(c_pad, conv_pad, w_mod, b_mod_mine)
    mod = jnp.transpose(mod_blocks, (1, 2, 0, 3)).reshape(n_layers, 8, N_MOD, d)[:, :nb]
    mod = jnp.pad(mod, ((0, 0), (0, 0), (0, 8 - N_MOD), (0, 0)))
    conv_full = jnp.transpose(conv_all[:, :n_layers * 3, :conv_shard].reshape(N_DEV, n_layers, 3, conv_shard), (1, 2, 0, 3))
    conv_full = jnp.pad(conv_full.reshape(n_layers, 3, N_DEV * conv_shard), ((0, 0), (0, 5), (0, 0)))
    bexp = jnp.repeat(jnp.transpose(gm_b_s, (0, 2, 1)), HEAD_DIM, axis=2)

    mixer_w, ffn_w = [w_in, w_o, w_pa, w_pb, w_pc], [w_13, w_2]
    n_mix = len(mixer_w)

    def send_weights(l, ws, name, after):
        return _xfer_start([w[l].astype(BF16) for w in ws], GATHER, name, after)

    def arrived(flight, after, kind, name):
        send_sems, recv_sems, srcs, zones, _ = flight
        return _xfer_wait(send_sems, recv_sems, srcs, zones, list(range(len(srcs))), after, kind, name)

    row = lambda a, l: a[l].reshape(1, -1)
    xs = x.reshape(t_all, d)
    saved, weights = [], []
    after = mod
    flight = send_weights(0, mixer_w, "gather_start_mixer_0", cact_all)
    for l in range(n_layers):
        win_g, wo_g, wpa_g, wpb_g, wpc_g = arrived(flight, after, GATHER, f"gather_wait_mixer_{l}")
        flight = send_weights(l, ffn_w, f"gather_start_ffn_{l}", win_g)
        wo_g = wo_g.reshape(d, d)
        h, z, ycat, ocat, mrg, mo, xmid = _mixer_fwd(
            xs, mod[l] + flight[4][0, 0], row(g_mix, l), row(gm_ln_g, l), row(gm_ln_b, l), gm_w_s[l], bexp[l], pool_w[l],
            row(pool_scale, l), conv_full[l], win_g, wpa_g, wpb_g, wpc_g, wo_g, l, seq)
        w13_g, w2_g = arrived(flight, xmid, GATHER, f"gather_wait_ffn_{l}")
        tok = jnp.zeros((), F32)
        if l + 1 < n_layers:
            flight = send_weights(l + 1, mixer_w, f"gather_start_mixer_{l + 1}", w13_g)
            tok = flight[4][0, 0]
        w2_g = w2_g.reshape(N_DEV * w_2.shape[1], d)
        h2, ab, hid, f, xo = _ffn_fwd(xmid, mod[l] + tok, row(g_ffn, l), w13_g, w2_g, l, seq)
        saved.append((xs, h, z, ycat, ocat, mrg, mo, xmid, h2, ab, hid, f))
        weights.append((win_g, wpa_g, wpb_g, wpc_g, wo_g, w13_g, w2_g))
        xs = after = xo

    dx, loss_blk, dgf_blk = _final_loss(xs, g_final.reshape(1, d), loss_target.reshape(t_all, d), seq)
    loss = lax.psum(loss_blk[0, 0], ("x", "y", "c"))

    ffn_flight = [None] * n_layers
    mix_flight = [None] * n_layers
    small_grads = [None] * n_layers
    dmods = [None] * n_layers
    tok = jnp.zeros((), F32)
    for l in reversed(range(n_layers)):
        x_in, h, z, ycat, ocat, mrg, mo, xmid, h2, ab, hid, f = saved[l]
        win_g, wpa_g, wpb_g, wpc_g, wo_g, w13_g, w2_g = weights[l]
        dxm, df, dab, dmod2, dg_ffn = _ffn_bwd(dx, xmid, ab, f, mod[l] + tok, row(g_ffn, l), w13_g, w2_g, l, seq)
        ffn_flight[l] = _xfer_start(_ffn_wgrads(h2, dab, hid, df, l), SCATTER, f"grads_start_ffn_{l}", dxm)
        dx, dz, dycat, dmo, dmod1, dg_mix, sm, dws, dssum, dpw = _mixer_bwd(
            dxm, x_in, z, ycat, mo, mod[l] + ffn_flight[l][4][0, 0], row(g_mix, l), row(gm_ln_g, l), row(gm_ln_b, l),
            gm_w_s[l], bexp[l], pool_w[l], row(pool_scale, l), conv_full[l], win_g, wpa_g, wpb_g, wpc_g, wo_g, l, seq)
        if l > 0:
            mix_flight[l] = _xfer_start(_mixer_wgrads(h, dz, mrg, dmo, ocat, dycat, l), SCATTER, f"grads_start_mixer_{l}", dx)
            tok = mix_flight[l][4][0, 0]
        dmod = jnp.concatenate([dmod1[:, 0:3], dmod2[:, 0:3]], axis=1).reshape(nb, N_MOD * d)
        dmods[l] = dmod
        db_s = jnp.transpose(jnp.sum(dssum.reshape(CHUNK, HEADS, HEAD_DIM), axis=2))
        small_grads[l] = [jnp.sum(dmod, axis=0), dg_mix[0], sm[0], sm[1], dws, db_s, dpw, sm[2], sm[3:6], dg_ffn[0]]
    grad_x = dx.reshape(nb, seq, d)

    names = ["b_mod", "g_mix", "ln_g", "ln_b", "w_s", "b_s", "pool_w", "pool_scale", "conv_w", "g_ffn"]
    per_name = [jnp.stack([small_grads[l][n] for l in range(n_layers)]) for n in range(len(names))] + [dgf_blk[0]]
    buf = _pack(per_name, 8 * N_DEV)
    dmod_all = jnp.pad(jnp.stack(dmods), ((0, 0), (0, 8 - nb), (0, 0)))
    dmod_blocks = jnp.transpose(dmod_all.reshape(n_layers, 8, N_DEV, blk), (2, 0, 1, 3))
    red, grad_w_mod = _small(buf, dmod_blocks, cact_all)
    (g_b_mod, g_g_mix, g_ln_g, g_ln_b, g_w_s, g_b_s, g_pool_w, g_pool_scale, g_conv_full, g_g_ffn, g_g_final) = _unpack(red, per_name)
    g_conv = lax.dynamic_slice_in_dim(g_conv_full, me * conv_shard, conv_shard, axis=2)

    mix_flight[0] = _xfer_start(_mixer_wgrads(h, dz, mrg, dmo, ocat, dycat, 0), SCATTER, "grads_start_mixer_0", red)
    results = {}
    results["w_mod"] = _adamw([grad_w_mod[l][None] for l in range(n_layers)], w_mod, m_w_mod, v_w_mod, "adamw_w_mod")

    small_w =[b_mod, g_mix, gm_ln_g, gm_ln_b, gm_w_s, gm_b_s, pool_w, pool_scale, conv_w, g_ffn, g_final]
    small_m = [m_b_mod, m_g_mix, m_gm_ln_g, m_gm_ln_b, m_gm_w_s, m_gm_b_s, m_pool_w, m_pool_scale, m_conv_w, m_g_ffn, m_g_final]
    small_v = [v_b_mod, v_g_mix, v_gm_ln_g, v_gm_ln_b, v_gm_w_s, v_gm_b_s, v_pool_w, v_pool_scale, v_conv_w, v_g_ffn, v_g_final]
    small_g = [g_b_mod, g_g_mix, g_ln_g, g_ln_b, g_w_s, g_b_s, g_pool_w, g_pool_scale, g_conv, g_g_ffn, g_g_final]
    small_names = ["b_mod", "g_mix", "gm_ln_g", "gm_ln_b", "gm_w_s", "gm_b_s", "pool_w", "pool_scale", "conv_w", "g_ffn", "g_final"]
    sd_buf, sm_new, sv_new = _adamw_flat(_pack(small_g, 8), _pack(small_w, 8), _pack(small_m, 8), _pack(small_v, 8))
    sd, sm_new, sv_new = _unpack(sd_buf, small_w), _unpack(sm_new, small_w), _unpack(sv_new, small_w)
    for n, nm in enumerate(small_names):
        results[nm] = (small_g[n], sd[n], sm_new[n], sv_new[n])

    layers = list(range(n_layers))
    done = (results["w_mod"][1][0, 0, 0] + sd_buf[0, 0]).reshape(1)
    ffn_recv = [arrived(ffn_flight[l], done, SCATTER, f"grads_wait_ffn_{l}") for l in reversed(layers)][::-1]
    mix_recv = [None] + [arrived(mix_flight[l], done, SCATTER, f"grads_wait_mixer_{l}") for l in reversed(layers[1:])][::-1]
    results["w_13"] = _adamw([ffn_recv[l][0] for l in layers], w_13, m_w_13, v_w_13, "adamw_w_13")
    results["w_2"] = _adamw([ffn_recv[l][1] for l in layers], w_2, m_w_2, v_w_2, "adamw_w_2")
    mix_m = [m_w_in, m_w_o, m_w_pa, m_w_pb, m_w_pc]
    mix_v = [v_w_in, v_w_o, v_w_pa, v_w_pb, v_w_pc]
    mix_names = ["w_in", "w_o", "w_pa", "w_pb", "w_pc"]
    early = [_adamw([mix_recv[l][a] for l in layers[1:]], mixer_w[a], mix_m[a], mix_v[a], f"adamw_{mix_names[a]}_later", first=1)
             for a in range(n_mix)]
    done = (results["w_13"][1][0, 0, 0] + results["w_2"][1][0, 0, 0] + sum(e[1][1, 0, 0] for e in early)).reshape(1)
    mix_recv[0] = arrived(mix_flight[0], done, SCATTER, "grads_wait_mixer_0")
    for a in range(n_mix):
        results[mix_names[a]] = _adamw([mix_recv[0][a]], mixer_w[a], mix_m[a], mix_v[a], f"adamw_{mix_names[a]}_first",
                                       earlier=early[a])

    order = ["w_mod", "b_mod", "g_mix", "w_in", "gm_ln_g", "gm_ln_b", "gm_w_s", "gm_b_s", "w_pa", "pool_w", "pool_scale",
             "w_pb", "conv_w", "w_pc", "w_o", "g_ffn", "w_13", "w_2", "g_final"]
    return (loss, grad_x, *[results[nm][0] for nm in order], *[results[nm][1] for nm in order],
            *[results[nm][2] for nm in order], *[results[nm][3] for nm in order])
```

```python
import functools

import jax
import jax.numpy as jnp
from jax import lax
from jax.experimental import pallas as pl
from jax.experimental.pallas import tpu as pltpu

F32 = jnp.float32
BF16 = jnp.bfloat16
MESH_ID = pl.DeviceIdType.MESH

N_DEV = 8
EPS = 1e-6
CHUNK = 128
HEADS = 4
HEAD_DIM = 128
BR_W = 512
POOL_WINDOWS = (2, 4, 8, 16)
POOL_HALO = 16
CONV_HALO = 8
N_MOD = 6
ADAM_LR = 0.001
ADAM_B1 = 0.9
ADAM_B2 = 0.999
ADAM_EPS = 1e-08
ADAM_WD = 0.01
ADAM_STEP = 10

TOKEN_TILE = 256
WGRAD_TOKENS = 2048
VMEM_LIMIT = 56 * 1024 * 1024
GELU_K = 0.7978845608028654
GELU_C = 0.044715

ANY = pl.BlockSpec(memory_space=pl.ANY)
VMEM_SPEC = pl.BlockSpec(memory_space=pltpu.VMEM)


def _cparams(**kw):
    return pltpu.CompilerParams(vmem_limit_bytes=VMEM_LIMIT, **kw)


def _dot(a, b):
    return jnp.dot(a, b, preferred_element_type=F32)


def _dot_nt(a, b):
    return lax.dot_general(a, b, (((1,), (1,)), ((), ())), preferred_element_type=F32)


def _dot_tn(a, b):
    return lax.dot_general(a, b, (((0,), (0,)), ((), ())), preferred_element_type=F32)


def _colsum(a):
    return jnp.sum(a, axis=0, keepdims=True)


def _sigmoid(x):
    return 0.5 * jnp.tanh(0.5 * x) + 0.5


def _gelu(x):
    x2 = x * x
    t = jnp.tanh(x * (GELU_K + (GELU_K * GELU_C) * x2))
    return (0.5 * x) * (1.0 + t), t, x2


def _gelu_grad(x, t, x2):
    one_t = 1.0 + t
    return 0.5 * one_t + (0.5 * x) * (one_t * (1.0 - t)) * (GELU_K + (3.0 * GELU_K * GELU_C) * x2)


def _tril_mask():
    r = lax.broadcasted_iota(jnp.int32, (CHUNK, CHUNK), 0)
    c = lax.broadcasted_iota(jnp.int32, (CHUNK, CHUNK), 1)
    return (r >= c).astype(F32)


def _my_index():
    return 4 * lax.axis_index("x") + 2 * lax.axis_index("y") + lax.axis_index("c")


def _peer(k):
    x, y, c = lax.axis_index("x"), lax.axis_index("y"), lax.axis_index("c")
    px = 1 - x if (k >> 2) & 1 else x
    py = 1 - y if (k >> 1) & 1 else y
    pc = 1 - c if k & 1 else c
    return (px, py, pc), 4 * px + 2 * py + pc


def _load_weights(step, pairs, sem):
    @pl.when(step == 0)
    def _():
        copies = [pltpu.make_async_copy(src, dst, sem.at[n]) for n, (src, dst) in enumerate(pairs)]
        for cp in copies:
            cp.start()
        for cp in copies:
            cp.wait()


def _wp_pairs(wp_hbm, wp_v):
    return [(wp_hbm.at[j], wp_v.at[:, pl.ds(HEAD_DIM * j, HEAD_DIM)]) for j in range(N_DEV)]


def _rms_mod(x, g, shift, scale):
    rstd = lax.rsqrt(jnp.mean(x * x, axis=-1, keepdims=True) + EPS)
    xn = x * rstd
    return xn, rstd, (xn * g) * (1.0 + scale) + shift


def _rms_mod_bwd(dh, xn, rstd, g, scale):
    dxn = dh * (1.0 + scale) * g
    dx = rstd * (dxn - xn * jnp.mean(dxn * xn, axis=-1, keepdims=True))
    return dx, _colsum(dh), _colsum(dh * (xn * g)), _colsum(dh * (1.0 + scale) * xn)


def _gmlp_s(vn_b, wmask_b, bexp, s_scr, tm):
    for ch in range(tm // CHUNK):
        rows = slice(ch * CHUNK, (ch + 1) * CHUNK)
        for hh in range(HEADS):
            cols = slice(hh * HEAD_DIM, (hh + 1) * HEAD_DIM)
            s_scr[rows, cols] = _dot(wmask_b[hh], vn_b[rows, cols]) + bexp[:, cols]
    return s_scr[...]


def _inv_count(pos, win):
    return 1.0 / jnp.minimum(pos + 1, win).astype(F32)


def _window_sums(ext, tm, trailing):
    n = tm + POOL_HALO
    sums = []
    for g, win in enumerate(POOL_WINDOWS):
        s = ext[:, g * HEAD_DIM:(g + 1) * HEAD_DIM]
        span = 1
        while span < win:
            s = s + pltpu.roll(s, span if trailing else n - span, 0)
            span *= 2
        sums.append(s[POOL_HALO:POOL_HALO + tm] if trailing else s[0:tm])
    return sums


def _pool_p(xb, xbext, pos, tm):
    sums = _window_sums(xbext, tm, True)
    return [sums[g] * _inv_count(pos, win) - xb[:, g * HEAD_DIM:(g + 1) * HEAD_DIM] for g, win in enumerate(POOL_WINDOWS)]


def _mixer_fwd(x, modl, g_mix, ln_g, ln_b, w_s, bexp, pool_w, pool_scale, conv_w, win, wpa, wpb, wpc, wo, layer, seq):
    t_all, d = x.shape
    tm = min(TOKEN_TILE, seq)
    n_seq_tiles = seq // tm
    blk = win.shape[-1]

    def body(x_ref, mod_ref, gmix_ref, lng_ref, lnb_ref, ws_ref, bexp_ref, pw_ref, ps_ref, cw_ref,
             win_hbm, wpa_hbm, wpb_hbm, wpc_hbm, wo_hbm,
             h_ref, z_ref, ycat_ref, ocat_ref, mrg_ref, mo_ref, xmid_ref,
             win_v, wpa_v, wpb_v, wpc_v, wo_v, xbext, zcext, s_scr, sem):
        i = pl.program_id(0)
        pairs = [(win_hbm, win_v), (wo_hbm, wo_v)]
        pairs += _wp_pairs(wpa_hbm, wpa_v) + _wp_pairs(wpb_hbm, wpb_v) + _wp_pairs(wpc_hbm, wpc_v)
        _load_weights(i, pairs, sem)
        tile_in_seq = i % n_seq_tiles

        @pl.when(tile_in_seq == 0)
        def _():
            xbext[0:POOL_HALO, :] = jnp.zeros((POOL_HALO, BR_W), F32)
            zcext[0:CONV_HALO, :] = jnp.zeros((CONV_HALO, BR_W), F32)

        x_t = x_ref[...]
        shift1, scale1, gate1 = mod_ref[0, 0:1, :], mod_ref[0, 1:2, :], mod_ref[0, 2:3, :]
        _, _, h = _rms_mod(x_t, gmix_ref[...], shift1, scale1)
        hb = h.astype(BF16)
        h_ref[...] = hb
        def project(j):
            zj = _dot(hb, win_v[j])
            z_ref[:, j * blk:(j + 1) * blk] = zj.astype(BF16)
            return zj

        z0, z1 = project(0), project(1)
        u = z0[:, 0:BR_W]
        v = jnp.concatenate([z0[:, BR_W:blk], z1[:, 0:2 * BR_W - blk]], axis=1)
        xb = z1[:, 2 * BR_W - blk:blk]

        gu = _gelu(u)[0]
        gv = _gelu(v)[0]
        mu = jnp.mean(gv, axis=-1, keepdims=True)
        cen = gv - mu
        rs = lax.rsqrt(jnp.mean(cen * cen, axis=-1, keepdims=True) + EPS)
        vn = (cen * rs) * lng_ref[...] + lnb_ref[...]
        mask = _tril_mask()
        wmask_b = [(ws_ref[hh] * mask).astype(BF16) for hh in range(HEADS)]
        s = _gmlp_s(vn.astype(BF16), wmask_b, bexp_ref[...], s_scr, tm)
        oa = (gu * s).astype(BF16)
        ya = _dot(oa, wpa_v[...])

        xbext[POOL_HALO:POOL_HALO + tm, :] = xb
        pos = tile_in_seq * tm + lax.broadcasted_iota(jnp.int32, (tm, 1), 0)
        ps = _pool_p(xb, xbext, pos, tm)
        qs = [_dot(ps[g].astype(BF16), pw_ref[g].astype(BF16)) for g in range(len(POOL_WINDOWS))]
        ob = (jnp.concatenate(qs, axis=1) * ps_ref[...]).astype(BF16)
        yb = _dot(ob, wpb_v[...])
        xbext[0:POOL_HALO, :] = xbext[tm:tm + POOL_HALO, :]

        z2, z3 = project(2), project(3)
        bg = z2[:, 0:BR_W]
        cg = jnp.concatenate([z2[:, BR_W:blk], z3[:, 0:2 * BR_W - blk]], axis=1)
        hc = z3[:, 2 * BR_W - blk:blk]
        zz = cg * hc
        zcext[CONV_HALO:CONV_HALO + tm, :] = zz
        yconv = (cw_ref[0:1, :] * zcext[pl.ds(CONV_HALO - 2, tm), :] + cw_ref[1:2, :] * zcext[pl.ds(CONV_HALO - 1, tm), :]
                 + cw_ref[2:3, :] * zz)
        oc = (bg * yconv).astype(BF16)
        yc = _dot(oc, wpc_v[...])
        zcext[0:CONV_HALO, :] = zcext[tm:tm + CONV_HALO, :]

        ocat_ref[:, 0:BR_W] = oa
        ocat_ref[:, BR_W:2 * BR_W] = ob
        ocat_ref[:, 2 * BR_W:3 * BR_W] = oc
        ycat_ref[:, 0:d] = ya.astype(BF16)
        ycat_ref[:, d:2 * d] = yb.astype(BF16)
        ycat_ref[:, 2 * d:3 * d] = yc.astype(BF16)

        ys = (ya, yb, yc)
        zg = jnp.concatenate([project(j).astype(BF16) for j in range(4, N_DEV)], axis=1)
        mb = _sigmoid(zg[:, 0:d]) * ys[0].astype(BF16)
        for n in range(1, 3):
            mb = mb + _sigmoid(zg[:, n * d:(n + 1) * d]) * ys[n].astype(BF16)
        mrg_ref[...] = mb
        mo = _dot(mb, wo_v[...])
        mo_ref[...] = mo.astype(BF16)
        xmid_ref[...] = x_t + gate1 * mo

    tok = lambda cols: pl.BlockSpec((tm, cols), lambda i: (i, 0))
    const2 = lambda a: pl.BlockSpec(a.shape, lambda i: (0,) * a.ndim)
    in_specs = [
        tok(d),
        pl.BlockSpec((1, 8, d), lambda i: (i // n_seq_tiles, 0, 0)),
        const2(g_mix), const2(ln_g), const2(ln_b), const2(w_s), const2(bexp), const2(pool_w), const2(pool_scale),
        const2(conv_w), ANY, ANY, ANY, ANY, ANY,
    ]
    out_shape = [
        jax.ShapeDtypeStruct((t_all, d), BF16),
        jax.ShapeDtypeStruct((t_all, N_DEV * blk), BF16),
        jax.ShapeDtypeStruct((t_all, 3 * d), BF16),
        jax.ShapeDtypeStruct((t_all, 3 * BR_W), BF16),
        jax.ShapeDtypeStruct((t_all, d), BF16),
        jax.ShapeDtypeStruct((t_all, d), BF16),
        jax.ShapeDtypeStruct((t_all, d), F32),
    ]
    out_specs = [tok(d), tok(N_DEV * blk), tok(3 * d), tok(3 * BR_W), tok(d), tok(d), tok(d)]
    scratch = [
        pltpu.VMEM((N_DEV, d, blk), BF16), pltpu.VMEM((BR_W, d), BF16), pltpu.VMEM((BR_W, d), BF16),
        pltpu.VMEM((BR_W, d), BF16), pltpu.VMEM((d, d), BF16),
        pltpu.VMEM((tm + POOL_HALO, BR_W), F32), pltpu.VMEM((tm + CONV_HALO, BR_W), F32), pltpu.VMEM((tm, BR_W), F32),
        pltpu.SemaphoreType.DMA((2 + 3 * N_DEV,)),
    ]
    return pl.pallas_call(
        body, name=f"mixer_fwd_{layer}", grid=(t_all // tm,), in_specs=in_specs, out_specs=out_specs, out_shape=out_shape,
        scratch_shapes=scratch, compiler_params=_cparams(dimension_semantics=("arbitrary",)),
    )(x, modl, g_mix, ln_g, ln_b, w_s, bexp, pool_w, pool_scale, conv_w, win, wpa, wpb, wpc, wo)


def _ffn_fwd(x, modl, g_ffn, w13, w2, layer, seq):
    t_all, d = x.shape
    tm = min(TOKEN_TILE, seq)
    n_seq_tiles = seq // tm
    fb = w13.shape[-1]
    n_hid = N_DEV // 2

    def body(x_ref, mod_ref, g_ref, w13_hbm, w2_hbm, h_ref, ab_ref, hid_ref, f_ref, xo_ref, w13_v, w2_v, sem):
        i = pl.program_id(0)
        _load_weights(i, [(w13_hbm, w13_v), (w2_hbm, w2_v)], sem)
        x_t = x_ref[...]
        shift2, scale2, gate2 = mod_ref[0, 3:4, :], mod_ref[0, 4:5, :], mod_ref[0, 5:6, :]
        _, _, h = _rms_mod(x_t, g_ref[...], shift2, scale2)
        hb = h.astype(BF16)
        h_ref[...] = hb
        f = jnp.zeros((tm, d), F32)
        for k in range(n_hid):
            a = _dot(hb, w13_v[k])
            b = _dot(hb, w13_v[n_hid + k])
            a, b = a.astype(BF16), b.astype(BF16)
            ab_ref[k] = a
            ab_ref[n_hid + k] = b
            hid = (a * _sigmoid(a)) * b
            hid_ref[k] = hid
            f = f + _dot(hid, w2_v[k * fb:(k + 1) * fb, :])
        f_ref[...] = f.astype(BF16)
        xo_ref[...] = x_t + gate2 * f

    tok = lambda cols: pl.BlockSpec((tm, cols), lambda i: (i, 0))
    blk3 = lambda n: pl.BlockSpec((n, tm, fb), lambda i: (0, i, 0))
    in_specs = [tok(d), pl.BlockSpec((1, 8, d), lambda i: (i // n_seq_tiles, 0, 0)),
                pl.BlockSpec(g_ffn.shape, lambda i: (0, 0)), ANY, ANY]
    out_shape = [
        jax.ShapeDtypeStruct((t_all, d), BF16),
        jax.ShapeDtypeStruct((N_DEV, t_all, fb), BF16),
        jax.ShapeDtypeStruct((n_hid, t_all, fb), BF16),
        jax.ShapeDtypeStruct((t_all, d), BF16),
        jax.ShapeDtypeStruct((t_all, d), F32),
    ]
    out_specs = [tok(d), blk3(N_DEV), blk3(n_hid), tok(d), tok(d)]
    scratch = [pltpu.VMEM((N_DEV, d, fb), BF16), pltpu.VMEM((n_hid * fb, d), BF16), pltpu.SemaphoreType.DMA((2,))]
    return pl.pallas_call(
        body, name=f"ffn_fwd_{layer}", grid=(t_all // tm,), in_specs=in_specs, out_specs=out_specs, out_shape=out_shape,
        scratch_shapes=scratch, compiler_params=_cparams(dimension_semantics=("arbitrary",)),
    )(x, modl, g_ffn, w13, w2)


def _final_loss(x, g_final, target, seq):
    t_all, d = x.shape
    tm = min(TOKEN_TILE, seq)

    def body(x_ref, g_ref, t_ref, dx_ref, loss_ref, dg_ref):
        i = pl.program_id(0)

        @pl.when(i == 0)
        def _():
            loss_ref[...] = jnp.zeros(loss_ref.shape, F32)
            dg_ref[...] = jnp.zeros(dg_ref.shape, F32)

        x_t = x_ref[...]
        g = g_ref[...]
        rstd = lax.rsqrt(jnp.mean(x_t * x_t, axis=-1, keepdims=True) + EPS)
        xn = x_t * rstd
        err = xn * g - t_ref[...]
        loss_ref[0:1, :] += _colsum(err * err) * (0.5 / d)
        dy = err * (1.0 / d)
        dg_ref[0:1, :] += _colsum(dy * xn)
        dxn = dy * g
        dx_ref[...] = rstd * (dxn - xn * jnp.mean(dxn * xn, axis=-1, keepdims=True))

        @pl.when(i == pl.num_programs(0) - 1)
        def _():
            loss_ref[...] = jnp.broadcast_to(jnp.sum(loss_ref[0:1, :], axis=1, keepdims=True), loss_ref.shape)

    tok = pl.BlockSpec((tm, d), lambda i: (i, 0))
    acc = pl.BlockSpec((8, d), lambda i: (0, 0))
    return pl.pallas_call(
        body, name="final_loss", grid=(t_all // tm,),
        in_specs=[tok, pl.BlockSpec((1, d), lambda i: (0, 0)), tok], out_specs=[tok, acc, acc],
        out_shape=[jax.ShapeDtypeStruct((t_all, d), F32), jax.ShapeDtypeStruct((8, d), F32), jax.ShapeDtypeStruct((8, d), F32)],
        compiler_params=_cparams(dimension_semantics=("arbitrary",)),
    )(x, g_final, target)


def _ffn_bwd(dxo, xmid, ab, f, modl, g_ffn, w13, w2, layer, seq):
    t_all, d = xmid.shape
    tm = min(TOKEN_TILE, seq)
    n_seq_tiles = seq // tm
    fb = w13.shape[-1]
    n_hid = N_DEV // 2

    def body(dxo_ref, x_ref, ab_ref, f_ref, mod_ref, g_ref, w13_hbm, w2_hbm,
             dx_ref, df_ref, dab_ref, dmod_ref, dg_ref, w13_v, w2_v, sem):
        i = pl.program_id(0)
        _load_weights(i, [(w13_hbm, w13_v), (w2_hbm, w2_v)], sem)

        @pl.when(i == 0)
        def _():
            dg_ref[...] = jnp.zeros(dg_ref.shape, F32)

        @pl.when(i % n_seq_tiles == 0)
        def _():
            dmod_ref[...] = jnp.zeros(dmod_ref.shape, F32)

        scale2, gate2 = mod_ref[0, 4:5, :], mod_ref[0, 5:6, :]
        g = g_ref[...]
        x_t = x_ref[...]
        rstd = lax.rsqrt(jnp.mean(x_t * x_t, axis=-1, keepdims=True) + EPS)
        xn = x_t * rstd
        dxo_t = dxo_ref[...]
        dmod_ref[0, 2:3, :] += _colsum(dxo_t * f_ref[...].astype(F32))
        dfb = (dxo_t * gate2).astype(BF16)
        df_ref[...] = dfb
        dh = jnp.zeros((tm, d), F32)
        for k in range(n_hid):
            dhid = _dot_nt(dfb, w2_v[k * fb:(k + 1) * fb, :]).astype(BF16)
            a = ab_ref[k]
            b = ab_ref[n_hid + k]
            sg = _sigmoid(a)
            da = dhid * b * (sg * (1.0 + a * (1.0 - sg)))
            db = dhid * (a * sg)
            dab_ref[k] = da
            dab_ref[n_hid + k] = db
            dh = dh + _dot_nt(da, w13_v[k]) + _dot_nt(db, w13_v[n_hid + k])
        dx, dshift, dscale, dg = _rms_mod_bwd(dh, xn, rstd, g, scale2)
        dmod_ref[0, 0:1, :] += dshift
        dmod_ref[0, 1:2, :] += dscale
        dg_ref[0:1, :] += dg
        dx_ref[...] = dxo_t + dx

    tok = lambda cols: pl.BlockSpec((tm, cols), lambda i: (i, 0))
    blk3 = lambda n: pl.BlockSpec((n, tm, fb), lambda i: (0, i, 0))
    modspec = pl.BlockSpec((1, 8, d), lambda i: (i // n_seq_tiles, 0, 0))
    in_specs = [tok(d), tok(d), blk3(N_DEV), tok(d), modspec, pl.BlockSpec(g_ffn.shape, lambda i: (0, 0)), ANY, ANY]
    out_shape = [
        jax.ShapeDtypeStruct((t_all, d), F32), jax.ShapeDtypeStruct((t_all, d), BF16),
        jax.ShapeDtypeStruct((N_DEV, t_all, fb), BF16), jax.ShapeDtypeStruct(modl.shape, F32),
        jax.ShapeDtypeStruct((8, d), F32),
    ]
    out_specs = [tok(d), tok(d), blk3(N_DEV), modspec, pl.BlockSpec((8, d), lambda i: (0, 0))]
    scratch = [pltpu.VMEM((N_DEV, d, fb), BF16), pltpu.VMEM((n_hid * fb, d), BF16), pltpu.SemaphoreType.DMA((2,))]
    return pl.pallas_call(
        body, name=f"ffn_bwd_{layer}", grid=(t_all // tm,), in_specs=in_specs, out_specs=out_specs, out_shape=out_shape,
        scratch_shapes=scratch, compiler_params=_cparams(dimension_semantics=("arbitrary",)),
    )(dxo, xmid, ab, f, modl, g_ffn, w13, w2)


def _mixer_bwd(dxm, x, z, ycat, mo, modl, g_mix, ln_g, ln_b, w_s, bexp, pool_w, pool_scale, conv_w,
               win, wpa, wpb, wpc, wo, layer, seq):
    t_all, d = x.shape
    tm = min(TOKEN_TILE, seq)
    n_seq_tiles = seq // tm
    blk = win.shape[-1]
    n_win = len(POOL_WINDOWS)

    def tile_of(i):
        return (i // n_seq_tiles) * n_seq_tiles + (n_seq_tiles - 1 - i % n_seq_tiles)

    def halo_row_block(i):
        return jnp.maximum(tile_of(i) * (tm // POOL_HALO) - 1, 0)

    def body(dxm_ref, x_ref, z_ref, zpb_ref, zpc_ref, ycat_ref, mo_ref, mod_ref, gmix_ref, lng_ref, lnb_ref, ws_ref,
             bexp_ref, pw_ref, ps_ref, cw_ref, win_hbm, wpa_hbm, wpb_hbm, wpc_hbm, wo_hbm,
             dx_ref, dz_ref, dycat_ref, dmo_ref, dmod_ref, dg_ref, sm_ref, dws_ref, dssum_ref, dpw_ref,
             win_v, wpa_v, wpb_v, wpc_v, wo_v, xbext, zzext, rext, dyext, s_scr, dvn_scr, sem):
        i = pl.program_id(0)
        pairs = [(win_hbm.at[j], win_v.at[:, pl.ds(blk * j, blk)]) for j in range(N_DEV)] + [(wo_hbm, wo_v)]
        pairs += _wp_pairs(wpa_hbm, wpa_v) + _wp_pairs(wpb_hbm, wpb_v) + _wp_pairs(wpc_hbm, wpc_v)
        _load_weights(i, pairs, sem)
        tile_in_seq = n_seq_tiles - 1 - i % n_seq_tiles
        first_of_seq = tile_in_seq == 0

        @pl.when(i == 0)
        def _():
            for r in (dg_ref, sm_ref, dws_ref, dssum_ref, dpw_ref):
                r[...] = jnp.zeros(r.shape, F32)

        @pl.when(i % n_seq_tiles == 0)
        def _():
            dmod_ref[...] = jnp.zeros(dmod_ref.shape, F32)
            rext[tm:tm + POOL_HALO, :] = jnp.zeros((POOL_HALO, BR_W), F32)
            dyext[tm:tm + CONV_HALO, :] = jnp.zeros((CONV_HALO, BR_W), F32)

        shift1, scale1, gate1 = mod_ref[0, 0:1, :], mod_ref[0, 1:2, :], mod_ref[0, 2:3, :]
        g = gmix_ref[...]
        x_t = x_ref[...]
        rstd = lax.rsqrt(jnp.mean(x_t * x_t, axis=-1, keepdims=True) + EPS)
        xn = x_t * rstd
        dxm_t = dxm_ref[...]
        dmod_ref[0, 2:3, :] += _colsum(dxm_t * mo_ref[...].astype(F32))
        dmo = (dxm_t * gate1).astype(BF16)
        dmo_ref[...] = dmo
        dmerged = _dot_nt(dmo, wo_v[...])

        dh_parts = []

        def emit_dz(lo, hi, value):
            vb = value.astype(BF16)
            dz_ref[:, lo:hi] = vb
            dh_parts.append(_dot_nt(vb, win_v[:, lo:hi]))

        dys = []
        dmerged = dmerged.astype(BF16)
        for n in range(3):
            gt = _sigmoid(z_ref[:, 3 * d + n * d:3 * d + (n + 1) * d])
            dyn = dmerged * gt
            emit_dz(3 * d + n * d, 3 * d + (n + 1) * d, dyn * ycat_ref[:, n * d:(n + 1) * d] * (1.0 - gt))
            dycat_ref[:, n * d:(n + 1) * d] = dyn
            dys.append(dyn)

        doa = _dot_nt(dys[0], wpa_v[...])
        u = z_ref[:, 0:BR_W].astype(F32)
        v = z_ref[:, BR_W:2 * BR_W].astype(F32)
        gu, tu, u2 = _gelu(u)
        gv, tv, v2 = _gelu(v)
        mu = jnp.mean(gv, axis=-1, keepdims=True)
        cen = gv - mu
        rs = lax.rsqrt(jnp.mean(cen * cen, axis=-1, keepdims=True) + EPS)
        vhat = cen * rs
        lng = lng_ref[...]
        vn_b = (vhat * lng + lnb_ref[...]).astype(BF16)
        mask = _tril_mask()
        wmask = [ws_ref[hh] * mask for hh in range(HEADS)]
        s = _gmlp_s(vn_b, [w.astype(BF16) for w in wmask], bexp_ref[...], s_scr, tm)
        du = (doa * s) * _gelu_grad(u, tu, u2)
        ds = doa * gu
        ds_b = ds.astype(BF16)
        dssum = jnp.zeros((CHUNK, BR_W), F32)
        for ch in range(tm // CHUNK):
            rows = slice(ch * CHUNK, (ch + 1) * CHUNK)
            dssum = dssum + ds[rows, :]
            for hh in range(HEADS):
                cols = slice(hh * HEAD_DIM, (hh + 1) * HEAD_DIM)
                dvn_scr[rows, cols] = _dot_tn(wmask[hh].astype(BF16), ds_b[rows, cols])
                dws_ref[hh] += _dot_nt(ds_b[rows, cols], vn_b[rows, cols]) * mask
        dssum_ref[...] += dssum
        dvn = dvn_scr[...]
        sm_ref[0:1, :] += _colsum(dvn * vhat)
        sm_ref[1:2, :] += _colsum(dvn)
        dvhat = dvn * lng
        dgv = rs * (dvhat - jnp.mean(dvhat, axis=-1, keepdims=True) - vhat * jnp.mean(dvhat * vhat, axis=-1, keepdims=True))
        dv = dgv * _gelu_grad(v, tv, v2)
        emit_dz(0, 2 * BR_W, jnp.concatenate([du, dv], axis=1))

        dob = _dot_nt(dys[1], wpb_v[...])
        xb = z_ref[:, 2 * BR_W:3 * BR_W].astype(F32)
        xbext[0:POOL_HALO, :] = jnp.where(first_of_seq, 0.0, zpb_ref[...].astype(F32))
        xbext[POOL_HALO:POOL_HALO + tm, :] = xb
        pos = tile_in_seq * tm + lax.broadcasted_iota(jnp.int32, (tm, 1), 0)
        ps = _pool_p(xb, xbext, pos, tm)
        scale_b = ps_ref[...]
        dq = dob * scale_b
        qs, dps = [], []
        for gi, win_len in enumerate(POOL_WINDOWS):
            cols = slice(gi * HEAD_DIM, (gi + 1) * HEAD_DIM)
            pw_b = pw_ref[gi].astype(BF16)
            p_b = ps[gi].astype(BF16)
            dq_b = dq[:, cols].astype(BF16)
            qs.append(_dot(p_b, pw_b))
            dpw_ref[gi] += _dot_tn(p_b, dq_b)
            dp = _dot_nt(dq_b, pw_b)
            dps.append(dp)
            rext[0:tm, cols] = dp * _inv_count(pos, win_len)
        sm_ref[2:3, :] += _colsum(dob * jnp.concatenate(qs, axis=1))
        dxbs = [acc - dp for acc, dp in zip(_window_sums(rext, tm, False), dps)]
        emit_dz(2 * BR_W, 3 * BR_W, jnp.concatenate(dxbs, axis=1))
        rext[tm:tm + POOL_HALO, :] = rext[0:POOL_HALO, :]

        doc = _dot_nt(dys[2], wpc_v[...])
        bg = z_ref[:, 3 * BR_W:4 * BR_W].astype(F32)
        cg = z_ref[:, 4 * BR_W:5 * BR_W].astype(F32)
        hc = z_ref[:, 5 * BR_W:6 * BR_W].astype(F32)
        zz = cg * hc
        zprev = zpc_ref[POOL_HALO - CONV_HALO:POOL_HALO, :].astype(F32)
        zzext[0:CONV_HALO, :] = jnp.where(first_of_seq, 0.0, zprev[:, 0:BR_W] * zprev[:, BR_W:2 * BR_W])
        zzext[CONV_HALO:CONV_HALO + tm, :] = zz
        zm2 = zzext[pl.ds(CONV_HALO - 2, tm), :]
        zm1 = zzext[pl.ds(CONV_HALO - 1, tm), :]
        w0, w1, w2c = cw_ref[0:1, :], cw_ref[1:2, :], cw_ref[2:3, :]
        yconv = w0 * zm2 + w1 * zm1 + w2c * zz
        dyc = doc * bg
        sm_ref[3:4, :] += _colsum(dyc * zm2)
        sm_ref[4:5, :] += _colsum(dyc * zm1)
        sm_ref[5:6, :] += _colsum(dyc * zz)
        dyext[0:tm, :] = dyc
        dzz = w2c * dyc + w1 * dyext[pl.ds(1, tm), :] + w0 * dyext[pl.ds(2, tm), :]
        dyext[tm:tm + CONV_HALO, :] = dyext[0:CONV_HALO, :]
        emit_dz(3 * BR_W, 6 * BR_W, jnp.concatenate([doc * yconv, dzz * hc, dzz * cg], axis=1))

        dh = dh_parts[0]
        for part in dh_parts[1:]:
            dh = dh + part
        dx, dshift, dscale, dg = _rms_mod_bwd(dh, xn, rstd, g, scale1)
        dmod_ref[0, 0:1, :] += dshift
        dmod_ref[0, 1:2, :] += dscale
        dg_ref[0:1, :] += dg
        dx_ref[...] = dxm_t + dx

    tok = lambda cols: pl.BlockSpec((tm, cols), lambda i: (tile_of(i), 0))
    const2 = lambda a: pl.BlockSpec(a.shape, lambda i: (0,) * a.ndim)
    modspec = pl.BlockSpec((1, 8, d), lambda i: (i // n_seq_tiles, 0, 0))
    in_specs = [
        tok(d), tok(d), tok(N_DEV * blk),
        pl.BlockSpec((POOL_HALO, BR_W), lambda i: (halo_row_block(i), 2)),
        pl.BlockSpec((POOL_HALO, 2 * BR_W), lambda i: (halo_row_block(i), 2)),
        tok(3 * d), tok(d), modspec,
        const2(g_mix), const2(ln_g), const2(ln_b), const2(w_s), const2(bexp), const2(pool_w), const2(pool_scale),
        const2(conv_w), ANY, ANY, ANY, ANY, ANY,
    ]
    acc = lambda shape: pl.BlockSpec(shape, lambda i: (0,) * len(shape))
    out_shape = [
        jax.ShapeDtypeStruct((t_all, d), F32), jax.ShapeDtypeStruct((t_all, N_DEV * blk), BF16),
        jax.ShapeDtypeStruct((t_all, 3 * d), BF16), jax.ShapeDtypeStruct((t_all, d), BF16),
        jax.ShapeDtypeStruct(modl.shape, F32), jax.ShapeDtypeStruct((8, d), F32), jax.ShapeDtypeStruct((8, BR_W), F32),
        jax.ShapeDtypeStruct((HEADS, CHUNK, CHUNK), F32), jax.ShapeDtypeStruct((CHUNK, BR_W), F32),
        jax.ShapeDtypeStruct((n_win, HEAD_DIM, HEAD_DIM), F32),
    ]
    out_specs = [tok(d), tok(N_DEV * blk), tok(3 * d), tok(d), modspec, acc((8, d)), acc((8, BR_W)),
                 acc((HEADS, CHUNK, CHUNK)), acc((CHUNK, BR_W)), acc((n_win, HEAD_DIM, HEAD_DIM))]
    scratch = [
        pltpu.VMEM((d, N_DEV * blk), BF16), pltpu.VMEM((BR_W, d), BF16), pltpu.VMEM((BR_W, d), BF16),
        pltpu.VMEM((BR_W, d), BF16), pltpu.VMEM((d, d), BF16),
        pltpu.VMEM((tm + POOL_HALO, BR_W), F32), pltpu.VMEM((tm + CONV_HALO, BR_W), F32),
        pltpu.VMEM((tm + POOL_HALO, BR_W), F32), pltpu.VMEM((tm + CONV_HALO, BR_W), F32),
        pltpu.VMEM((tm, BR_W), F32), pltpu.VMEM((tm, BR_W), F32),
        pltpu.SemaphoreType.DMA((1 + 4 * N_DEV,)),
    ]
    return pl.pallas_call(
        body, name=f"mixer_bwd_{layer}", grid=(t_all // tm,), in_specs=in_specs, out_specs=out_specs, out_shape=out_shape,
        scratch_shapes=scratch, compiler_params=_cparams(dimension_semantics=("arbitrary",)),
    )(dxm, x, z, z, z, ycat, mo, modl, g_mix, ln_g, ln_b, w_s, bexp, pool_w, pool_scale, conv_w, win, wpa, wpb, wpc, wo)


def _wgrad(a, b, a_spec, b_spec, out_struct, out_spec, grid_kn, tk, tn, split, name):
    t_all = a.shape[-2]
    tt = min(WGRAD_TOKENS, t_all)
    n_t = t_all // tt

    def body(a_ref, b_ref, o_ref, acc):
        t = pl.program_id(2)

        @pl.when(t == 0)
        def _():
            acc[...] = jnp.zeros(acc.shape, F32)

        acc[...] += _dot_tn(a_ref[...], b_ref[...])

        @pl.when(t == n_t - 1)
        def _():
            if split:
                for j in range(split):
                    w = tn // split
                    o_ref[j] = acc[:, j * w:(j + 1) * w].astype(o_ref.dtype)
            else:
                o_ref[...] = acc[...].astype(o_ref.dtype)

    return pl.pallas_call(
        body, name=name, grid=(*grid_kn, n_t), in_specs=[a_spec(tt), b_spec(tt)], out_specs=out_spec, out_shape=out_struct,
        scratch_shapes=[pltpu.VMEM((tk, tn), F32)],
        compiler_params=_cparams(dimension_semantics=("arbitrary", "arbitrary", "arbitrary")),
    )(a, b)


def _mixer_wgrads(h, dz, mrg, dmo, ocat, dycat, layer):
    d = h.shape[1]
    blk = dz.shape[1] // N_DEV
    g_win = _wgrad(
        h, dz, lambda tt: pl.BlockSpec((tt, d), lambda k, n, t: (t, 0)), lambda tt: pl.BlockSpec((tt, blk), lambda k, n, t: (t, n)),
        jax.ShapeDtypeStruct((N_DEV, d, blk), BF16), pl.BlockSpec((None, d, blk), lambda k, n, t: (n, 0, 0)),
        (1, N_DEV), d, blk, 0, f"wgrad_in_{layer}")
    g_wo = _wgrad(
        mrg, dmo, lambda tt: pl.BlockSpec((tt, d), lambda k, n, t: (t, 0)), lambda tt: pl.BlockSpec((tt, d), lambda k, n, t: (t, 0)),
        jax.ShapeDtypeStruct((d, d), BF16), pl.BlockSpec((d, d), lambda k, n, t: (0, 0)), (1, 1), d, d, 0, f"wgrad_o_{layer}")
    g_wp = []
    for n, nm in enumerate("abc"):
        g_wp.append(_wgrad(
            ocat, dycat, lambda tt, n=n: pl.BlockSpec((tt, BR_W), lambda k, nn, t: (t, n)),
            lambda tt, n=n: pl.BlockSpec((tt, d), lambda k, nn, t: (t, n)),
            jax.ShapeDtypeStruct((N_DEV, BR_W, d // N_DEV), BF16),
            pl.BlockSpec((N_DEV, BR_W, d // N_DEV), lambda k, nn, t: (0, 0, 0)), (1, 1), BR_W, d, N_DEV, f"wgrad_p{nm}_{layer}"))
    return [g_win, g_wo.reshape(N_DEV, d // N_DEV, d), *g_wp]


def _ffn_wgrads(h2, dab, hid, df, layer):
    d = h2.shape[1]
    fb = dab.shape[-1]
    n_hid = N_DEV // 2
    g_w13 = _wgrad(
        dab, h2, lambda tt: pl.BlockSpec((None, tt, fb), lambda k, n, t: (k, t, 0)),
        lambda tt: pl.BlockSpec((tt, d), lambda k, n, t: (t, 0)),
        jax.ShapeDtypeStruct((N_DEV, fb, d), BF16), pl.BlockSpec((None, fb, d), lambda k, n, t: (k, 0, 0)),
        (N_DEV, 1), fb, d, 0, f"wgrad_13_{layer}")
    g_w2 = _wgrad(
        hid, df, lambda tt: pl.BlockSpec((None, tt, fb), lambda k, n, t: (k, t, 0)),
        lambda tt: pl.BlockSpec((tt, d), lambda k, n, t: (t, 0)),
        jax.ShapeDtypeStruct((n_hid * fb, d), BF16), pl.BlockSpec((fb, d), lambda k, n, t: (k, 0)),
        (n_hid, 1), fb, d, 0, f"wgrad_2_{layer}")
    return [g_w13, g_w2.reshape(N_DEV, fb // 2, d)]


def _remote(src, dst, send_sem, recv_sem, dev):
    return pltpu.make_async_remote_copy(src_ref=src, dst_ref=dst, send_sem=send_sem, recv_sem=recv_sem, device_id=dev,
                                        device_id_type=MESH_ID)


HBM_SPEC = pl.BlockSpec(memory_space=pltpu.HBM)
SEM_SPEC = pl.BlockSpec(memory_space=pltpu.SEMAPHORE)
DATAFLOW = pltpu.SideEffectType.DATAFLOW_SIDE_EFFECTING
GATHER, SCATTER = "gather", "scatter"


def _xfer_src(src, kind, peer_index):
    return src if kind == GATHER else src.at[peer_index]


def _xfer_sem(a, k):
    return a * (N_DEV - 1) + k - 1


def _landing_zones(srcs, kind):
    me = _my_index()
    zones = []
    for s in srcs:
        own = s[None] if kind == GATHER else lax.dynamic_index_in_dim(s, me, 0, keepdims=True)
        zones.append(lax.dynamic_update_index_in_dim(lax.empty((N_DEV, *own.shape[1:]), s.dtype), own, me, 0))
    return zones


def _xfer_start(srcs, kind, name, after):
    n = len(srcs)
    lands = _landing_zones(srcs, kind)

    def body(*refs):
        src, land = refs[:n], refs[n:2 * n]
        send_sems, recv_sems = refs[2 * n + 1], refs[2 * n + 2]
        token = refs[-1]
        me = _my_index()
        for k in range(1, N_DEV):
            dev, idx = _peer(k)
            for a in range(n):
                q = _xfer_sem(a, k)
                _remote(_xfer_src(src[a], kind, idx), land[a].at[me], send_sems.at[q], recv_sems.at[q], dev).start()
        token[...] = jnp.zeros(token.shape, token.dtype)

    both = [*srcs, *lands]
    sems = pltpu.SemaphoreType.DMA((n * (N_DEV - 1),))
    out_shape = (sems, sems, *[pltpu.HBM(a.shape, a.dtype) for a in both], jax.ShapeDtypeStruct((8, 128), F32))
    outs = pl.pallas_call(
        body, name=name, in_specs=[HBM_SPEC] * (2 * n) + [ANY],
        out_specs=(SEM_SPEC, SEM_SPEC, *[HBM_SPEC] * (2 * n), VMEM_SPEC),
        out_shape=out_shape, input_output_aliases={i: 2 + i for i in range(2 * n)},
        compiler_params=pltpu.CompilerParams(has_side_effects=DATAFLOW),
    )(*[pltpu.with_memory_space_constraint(a, pltpu.HBM) for a in both], after)
    return outs[0], outs[1], list(outs[2:2 + n]), list(outs[2 + n:2 + 2 * n]), outs[-1]


def _xfer_wait(send_sems, recv_sems, srcs, lands, rows, after, kind, name):
    n = len(srcs)

    def body(*refs):
        src, land = refs[:n], refs[n:2 * n]
        send_sems, recv_sems = refs[2 * n], refs[2 * n + 1]
        for k in range(1, N_DEV):
            dev, idx = _peer(k)
            for a in range(n):
                q = _xfer_sem(rows[a], k)
                cp = _remote(_xfer_src(src[a], kind, idx), land[a].at[idx], send_sems.at[q], recv_sems.at[q], dev)
                cp.wait_send()
                cp.wait_recv()

    both = [*srcs, *lands]
    outs = pl.pallas_call(
        body, name=name, in_specs=[HBM_SPEC] * (2 * n) + [SEM_SPEC, SEM_SPEC, ANY], out_specs=[HBM_SPEC] * (2 * n),
        out_shape=[pltpu.HBM(a.shape, a.dtype) for a in both], input_output_aliases={i: i for i in range(2 * n)},
        compiler_params=pltpu.CompilerParams(has_side_effects=DATAFLOW),
    )(*both, send_sems, recv_sems, after)
    return list(outs[n:])


def _pre(c_pad, conv_pad, w_mod, b_mod_mine):
    n_layers, d, blk = w_mod.shape

    def body(c_ref, conv_ref, wmod_ref, bmod_ref, cact_ref, mod_ref, convall_ref, cact_mine, msh, send_sems, recv_sems):
        me = _my_index()
        c = c_ref[...]
        cact_mine[...] = c * _sigmoid(c)
        cact_ref[me] = cact_mine[...]
        convall_ref[me] = conv_ref[...]
        sends = []
        for k in range(1, N_DEV):
            dev, _ = _peer(k)
            sends.append(_remote(cact_mine, cact_ref.at[me], send_sems.at[0, k - 1], recv_sems.at[0, k - 1], dev))
            sends.append(_remote(conv_ref, convall_ref.at[me], send_sems.at[1, k - 1], recv_sems.at[1, k - 1], dev))
        for cp in sends:
            cp.start()
        for k in range(1, N_DEV):
            dev, idx = _peer(k)
            _remote(cact_mine, cact_ref.at[idx], send_sems.at[0, k - 1], recv_sems.at[0, k - 1], dev).wait_recv()
            _remote(conv_ref, convall_ref.at[idx], send_sems.at[1, k - 1], recv_sems.at[1, k - 1], dev).wait_recv()
        for cp in sends:
            cp.wait_send()
        cact_b = cact_ref[...].reshape(N_DEV * 8, d).astype(BF16)
        for l in range(n_layers):
            m = _dot(cact_b, wmod_ref[l].astype(BF16)) + bmod_ref[l]
            msh[l] = m.reshape(N_DEV, 8, blk)
        mod_ref[me] = msh[:, me]
        sends = []
        for k in range(1, N_DEV):
            dev, idx = _peer(k)
            sends.append(_remote(msh.at[:, idx], mod_ref.at[me], send_sems.at[2, k - 1], recv_sems.at[2, k - 1], dev))
        for cp in sends:
            cp.start()
        for k in range(1, N_DEV):
            dev, idx = _peer(k)
            _remote(msh.at[:, idx], mod_ref.at[idx], send_sems.at[2, k - 1], recv_sems.at[2, k - 1], dev).wait_recv()
        for cp in sends:
            cp.wait_send()

    out_shape = [jax.ShapeDtypeStruct((N_DEV, 8, d), F32), jax.ShapeDtypeStruct((N_DEV, n_layers, 8, blk), F32),
                 jax.ShapeDtypeStruct((N_DEV, *conv_pad.shape), F32)]
    return pl.pallas_call(
        body, name="pre", in_specs=[VMEM_SPEC] * 4, out_specs=[VMEM_SPEC] * 3, out_shape=out_shape,
        scratch_shapes=[pltpu.VMEM((8, d), F32), pltpu.VMEM((n_layers, N_DEV, 8, blk), F32),
                        pltpu.SemaphoreType.DMA((3, N_DEV - 1)), pltpu.SemaphoreType.DMA((3, N_DEV - 1))],
        compiler_params=_cparams(has_side_effects=True),
    )(c_pad, conv_pad, w_mod, b_mod_mine)


def _small(buf, dmod_blocks, cact_all):
    rows = buf.shape[0]
    seg = rows // N_DEV
    _, n_layers, _, blk = dmod_blocks.shape
    d = cact_all.shape[-1]

    def body(buf_ref, dmod_ref, cact_ref, out_ref, gw_ref, rs_recv, red, drecv, send_sems, recv_sems):
        me = _my_index()
        mine = pl.ds(pl.multiple_of(me * seg, 8), seg)
        sends = []
        for k in range(1, N_DEV):
            dev, idx = _peer(k)
            theirs = pl.ds(pl.multiple_of(idx * seg, 8), seg)
            sends.append(_remote(buf_ref.at[theirs], rs_recv.at[k - 1], send_sems.at[0, k - 1], recv_sems.at[0, k - 1], dev))
            sends.append(_remote(dmod_ref.at[idx], drecv.at[me], send_sems.at[1, k - 1], recv_sems.at[1, k - 1], dev))
        for cp in sends:
            cp.start()
        drecv[me] = dmod_ref[me]
        for k in range(1, N_DEV):
            dev, idx = _peer(k)
            _remote(buf_ref.at[mine], rs_recv.at[k - 1], send_sems.at[0, k - 1], recv_sems.at[0, k - 1], dev).wait_recv()
            _remote(dmod_ref.at[idx], drecv.at[idx], send_sems.at[1, k - 1], recv_sems.at[1, k - 1], dev).wait_recv()
        for cp in sends:
            cp.wait_send()
        total = buf_ref[mine, :]
        for k in range(1, N_DEV):
            total = total + rs_recv[k - 1]
        red[...] = total
        out_ref[mine, :] = total
        sends = []
        for k in range(1, N_DEV):
            dev, _ = _peer(k)
            sends.append(_remote(red, out_ref.at[mine], send_sems.at[2, k - 1], recv_sems.at[2, k - 1], dev))
        for cp in sends:
            cp.start()
        cact_b = cact_ref[...].reshape(N_DEV * 8, d).astype(BF16)
        for l in range(n_layers):
            gw_ref[l] = _dot_tn(cact_b, drecv[:, l].reshape(N_DEV * 8, blk).astype(BF16))
        for k in range(1, N_DEV):
            dev, idx = _peer(k)
            theirs = pl.ds(pl.multiple_of(idx * seg, 8), seg)
            _remote(red, out_ref.at[theirs], send_sems.at[2, k - 1], recv_sems.at[2, k - 1], dev).wait_recv()
        for cp in sends:
            cp.wait_send()

    out_shape = [jax.ShapeDtypeStruct(buf.shape, F32), jax.ShapeDtypeStruct((n_layers, d, blk), F32)]
    return pl.pallas_call(
        body, name="small_allreduce", in_specs=[VMEM_SPEC] * 3, out_specs=[VMEM_SPEC] * 2, out_shape=out_shape,
        scratch_shapes=[pltpu.VMEM((N_DEV - 1, seg, 128), F32), pltpu.VMEM((seg, 128), F32),
                        pltpu.VMEM(dmod_blocks.shape, F32),
                        pltpu.SemaphoreType.DMA((3, N_DEV - 1)), pltpu.SemaphoreType.DMA((3, N_DEV - 1))],
        compiler_params=_cparams(has_side_effects=True),
    )(buf, dmod_blocks, cact_all)


def _adamw_math(w, g, m, v):
    m = ADAM_B1 * m + (1.0 - ADAM_B1) * g
    v = ADAM_B2 * v + (1.0 - ADAM_B2) * (g * g)
    m_hat = m / (1.0 - ADAM_B1 ** ADAM_STEP)
    v_hat = v / (1.0 - ADAM_B2 ** ADAM_STEP)
    delta = -ADAM_LR * (m_hat / (jnp.sqrt(v_hat) + ADAM_EPS) + ADAM_WD * w)
    return delta, m, v


def _adamw(parts, w, m, v, name, first=0, earlier=None):
    n_layers = len(parts)
    n_parts, rows, cols = parts[0].shape
    tr = rows
    while tr * cols * 4 > (1 << 20) and tr % 32 == 0:
        tr //= 2
    n_r = rows // tr
    n_earlier = 0 if earlier is None else 4

    def body(*refs):
        p_refs = refs[:n_layers]
        w_ref, m_ref, v_ref = refs[n_layers:n_layers + 3]
        g_out, d_out, m_out, v_out = refs[n_layers + 3 + n_earlier:]
        layer = pl.program_id(0)
        for q in range(n_layers):

            @pl.when(layer == q)
            def _(q=q):
                g = p_refs[q][0].astype(F32)
                for p in range(1, n_parts):
                    g = g + p_refs[q][p].astype(F32)
                delta, m_new, v_new = _adamw_math(w_ref[...], g, m_ref[...], v_ref[...])
                g_out[...] = g
                d_out[...] = delta
                m_out[...] = m_new
                v_out[...] = v_new

    def parts_spec(q):
        return pl.BlockSpec((n_parts, tr, cols), lambda l, r: (0, jnp.where(l == q, r, jnp.where(l < q, 0, n_r - 1)), 0))

    spec = pl.BlockSpec((None, tr, cols), lambda l, r: (first + l, r, 0))
    out = jax.ShapeDtypeStruct(w.shape, F32)
    n_in = n_layers + 3
    return pl.pallas_call(
        body, name=name, grid=(n_layers, n_r),
        in_specs=[parts_spec(q) for q in range(n_layers)] + [spec, spec, spec] + [ANY] * n_earlier,
        out_specs=[spec] * 4, out_shape=[out] * 4, input_output_aliases={n_in + i: i for i in range(n_earlier)},
        compiler_params=_cparams(dimension_semantics=("arbitrary", "arbitrary")),
    )(*parts, w, m, v, *(earlier or ()))


def _adamw_flat(g, w, m, v):
    def body(g_ref, w_ref, m_ref, v_ref, d_out, m_out, v_out):
        delta, m_new, v_new = _adamw_math(w_ref[...], g_ref[...], m_ref[...], v_ref[...])
        d_out[...] = delta
        m_out[...] = m_new
        v_out[...] = v_new

    out = jax.ShapeDtypeStruct(g.shape, F32)
    return pl.pallas_call(body, name="adamw_small", in_specs=[VMEM_SPEC] * 4, out_specs=[VMEM_SPEC] * 3, out_shape=[out] * 3,
                          compiler_params=_cparams())(g, w, m, v)


def _pack(arrays, rows_multiple):
    flat = jnp.concatenate([a.reshape(-1) for a in arrays])
    per = 128 * rows_multiple
    total = -(-flat.shape[0] // per) * per
    return jnp.pad(flat, (0, total - flat.shape[0])).reshape(total // 128, 128)


def _unpack(buf, like):
    flat = buf.reshape(-1)
    out, off = [], 0
    for a in like:
        out.append(flat[off:off + a.size].reshape(a.shape))
        off += a.size
    return out


def kernel(x, c, w_mod, b_mod, g_mix, w_in, gm_ln_g, gm_ln_b, gm_w_s, gm_b_s, w_pa, pool_w, pool_scale, w_pb, conv_w, w_pc, w_o, g_ffn, w_13, w_2, g_final, loss_target, m_w_mod, m_b_mod, m_g_mix, m_w_in, m_gm_ln_g, m_gm_ln_b, m_gm_w_s, m_gm_b_s, m_w_pa, m_pool_w, m_pool_scale, m_w_pb, m_conv_w, m_w_pc, m_w_o, m_g_ffn, m_w_13, m_w_2, m_g_final, v_w_mod, v_b_mod, v_g_mix, v_w_in, v_gm_ln_g, v_gm_ln_b, v_gm_w_s, v_gm_b_s, v_w_pa, v_pool_w, v_pool_scale, v_w_pb, v_conv_w, v_w_pc, v_w_o, v_g_ffn, v_w_13, v_w_2, v_g_final):
    nb, seq, d = x.shape
    n_layers = w_in.shape[0]
    t_all = nb * seq
    blk = w_in.shape[-1]
    me = _my_index()
    conv_shard = conv_w.shape[-1]

    c_pad = jnp.pad(c, ((0, 8 - nb), (0, 0)))
    conv_pad = jnp.pad(conv_w.reshape(n_layers * 3, conv_shard), ((0, 16 - n_layers * 3), (0, 128 - conv_shard)))
    b_mod_mine = lax.dynamic_slice_in_dim(b_mod, me * blk, blk, axis=1).reshape(n_layers, 1, blk)
    cact_all, mod_blocks, conv_all = _pre(c_pad, conv_pad, w_mod, b_mod_mine)
    mod = jnp.transpose(mod_blocks, (1, 2, 0, 3)).reshape(n_layers, 8, N_MOD, d)[:, :nb]
    mod = jnp.pad(mod, ((0, 0), (0, 0), (0, 8 - N_MOD), (0, 0)))
    conv_full = jnp.transpose(conv_all[:, :n_layers * 3, :conv_shard].reshape(N_DEV, n_layers, 3, conv_shard), (1, 2, 0, 3))
    conv_full = jnp.pad(conv_full.reshape(n_layers, 3, N_DEV * conv_shard), ((0, 0), (0, 5), (0, 0)))
    bexp = jnp.repeat(jnp.transpose(gm_b_s, (0, 2, 1)), HEAD_DIM, axis=2)

    mixer_w, ffn_w = [w_in, w_o, w_pa, w_pb, w_pc], [w_13, w_2]
    n_mix = len(mixer_w)

    def send_weights(l, ws, name, after):
        return _xfer_start([w[l].astype(BF16) for w in ws], GATHER, name, after)

    def arrived(flight, after, kind, name):
        send_sems, recv_sems, srcs, zones, _ = flight
        return _xfer_wait(send_sems, recv_sems, srcs, zones, list(range(len(srcs))), after, kind, name)

    row = lambda a, l: a[l].reshape(1, -1)
    xs = x.reshape(t_all, d)
    saved, weights = [], []
    after = mod
    flight = send_weights(0, mixer_w, "gather_start_mixer_0", cact_all)
    for l in range(n_layers):
        win_g, wo_g, wpa_g, wpb_g, wpc_g = arrived(flight, after, GATHER, f"gather_wait_mixer_{l}")
        flight = send_weights(l, ffn_w, f"gather_start_ffn_{l}", win_g)
        wo_g = wo_g.reshape(d, d)
        h, z, ycat, ocat, mrg, mo, xmid = _mixer_fwd(
            xs, mod[l] + flight[4][0, 0], row(g_mix, l), row(gm_ln_g, l), row(gm_ln_b, l), gm_w_s[l], bexp[l], pool_w[l],
            row(pool_scale, l), conv_full[l], win_g, wpa_g, wpb_g, wpc_g, wo_g, l, seq)
        w13_g, w2_g = arrived(flight, xmid, GATHER, f"gather_wait_ffn_{l}")
        tok = jnp.zeros((), F32)
        if l + 1 < n_layers:
            flight = send_weights(l + 1, mixer_w, f"gather_start_mixer_{l + 1}", w13_g)
            tok = flight[4][0, 0]
        w2_g = w2_g.reshape(N_DEV * w_2.shape[1], d)
        h2, ab, hid, f, xo = _ffn_fwd(xmid, mod[l] + tok, row(g_ffn, l), w13_g, w2_g, l, seq)
        saved.append((xs, h, z, ycat, ocat, mrg, mo, xmid, h2, ab, hid, f))
        weights.append((win_g, wpa_g, wpb_g, wpc_g, wo_g, w13_g, w2_g))
        xs = after = xo

    dx, loss_blk, dgf_blk = _final_loss(xs, g_final.reshape(1, d), loss_target.reshape(t_all, d), seq)
    loss = lax.psum(loss_blk[0, 0], ("x", "y", "c"))

    ffn_flight = [None] * n_layers
    mix_flight = [None] * n_layers
    small_grads = [None] * n_layers
    dmods = [None] * n_layers
    tok = jnp.zeros((), F32)
    for l in reversed(range(n_layers)):
        x_in, h, z, ycat, ocat, mrg, mo, xmid, h2, ab, hid, f = saved[l]
        win_g, wpa_g, wpb_g, wpc_g, wo_g, w13_g, w2_g = weights[l]
        dxm, df, dab, dmod2, dg_ffn = _ffn_bwd(dx, xmid, ab, f, mod[l] + tok, row(g_ffn, l), w13_g, w2_g, l, seq)
        ffn_flight[l] = _xfer_start(_ffn_wgrads(h2, dab, hid, df, l), SCATTER, f"grads_start_ffn_{l}", dxm)
        dx, dz, dycat, dmo, dmod1, dg_mix, sm, dws, dssum, dpw = _mixer_bwd(
            dxm, x_in, z, ycat, mo, mod[l] + ffn_flight[l][4][0, 0], row(g_mix, l), row(gm_ln_g, l), row(gm_ln_b, l),
            gm_w_s[l], bexp[l], pool_w[l], row(pool_scale, l), conv_full[l], win_g, wpa_g, wpb_g, wpc_g, wo_g, l, seq)
        if l > 0:
            mix_flight[l] = _xfer_start(_mixer_wgrads(h, dz, mrg, dmo, ocat, dycat, l), SCATTER, f"grads_start_mixer_{l}", dx)
            tok = mix_flight[l][4][0, 0]
        dmod = jnp.concatenate([dmod1[:, 0:3], dmod2[:, 0:3]], axis=1).reshape(nb, N_MOD * d)
        dmods[l] = dmod
        db_s = jnp.transpose(jnp.sum(dssum.reshape(CHUNK, HEADS, HEAD_DIM), axis=2))
        small_grads[l] = [jnp.sum(dmod, axis=0), dg_mix[0], sm[0], sm[1], dws, db_s, dpw, sm[2], sm[3:6], dg_ffn[0]]
    grad_x = dx.reshape(nb, seq, d)

    names = ["b_mod", "g_mix", "ln_g", "ln_b", "w_s", "b_s", "pool_w", "pool_scale", "conv_w", "g_ffn"]
    per_name = [jnp.stack([small_grads[l][n] for l in range(n_layers)]) for n in range(len(names))] + [dgf_blk[0]]
    buf = _pack(per_name, 8 * N_DEV)
    dmod_all = jnp.pad(jnp.stack(dmods), ((0, 0), (0, 8 - nb), (0, 0)))
    dmod_blocks = jnp.transpose(dmod_all.reshape(n_layers, 8, N_DEV, blk), (2, 0, 1, 3))
    red, grad_w_mod = _small(buf, dmod_blocks, cact_all)
    (g_b_mod, g_g_mix, g_ln_g, g_ln_b, g_w_s, g_b_s, g_pool_w, g_pool_scale, g_conv_full, g_g_ffn, g_g_final) = _unpack(red, per_name)
    g_conv = lax.dynamic_slice_in_dim(g_conv_full, me * conv_shard, conv_shard, axis=2)

    mix_flight[0] = _xfer_start(_mixer_wgrads(h, dz, mrg, dmo, ocat, dycat, 0), SCATTER, "grads_start_mixer_0", red)
    results = {}
    results["w_mod"] = _adamw([grad_w_mod[l][None] for l in range(n_layers)], w_mod, m_w_mod, v_w_mod, "adamw_w_mod")

    small_w =[b_mod, g_mix, gm_ln_g, gm_ln_b, gm_w_s, gm_b_s, pool_w, pool_scale, conv_w, g_ffn, g_final]
    small_m = [m_b_mod, m_g_mix, m_gm_ln_g, m_gm_ln_b, m_gm_w_s, m_gm_b_s, m_pool_w, m_pool_scale, m_conv_w, m_g_ffn, m_g_final]
    small_v = [v_b_mod, v_g_mix, v_gm_ln_g, v_gm_ln_b, v_gm_w_s, v_gm_b_s, v_pool_w, v_pool_scale, v_conv_w, v_g_ffn, v_g_final]
    small_g = [g_b_mod, g_g_mix, g_ln_g, g_ln_b, g_w_s, g_b_s, g_pool_w, g_pool_scale, g_conv, g_g_ffn, g_g_final]
    small_names = ["b_mod", "g_mix", "gm_ln_g", "gm_ln_b", "gm_w_s", "gm_b_s", "pool_w", "pool_scale", "conv_w", "g_ffn", "g_final"]
    sd_buf, sm_new, sv_new = _adamw_flat(_pack(small_g, 8), _pack(small_w, 8), _pack(small_m, 8), _pack(small_v, 8))
    sd, sm_new, sv_new = _unpack(sd_buf, small_w), _unpack(sm_new, small_w), _unpack(sv_new, small_w)
    for n, nm in enumerate(small_names):
        results[nm] = (small_g[n], sd[n], sm_new[n], sv_new[n])

    layers = list(range(n_layers))
    done = (results["w_mod"][1][0, 0, 0] + sd_buf[0, 0]).reshape(1)
    ffn_recv = [arrived(ffn_flight[l], done, SCATTER, f"grads_wait_ffn_{l}") for l in reversed(layers)][::-1]
    mix_recv = [None] + [arrived(mix_flight[l], done, SCATTER, f"grads_wait_mixer_{l}") for l in reversed(layers[1:])][::-1]
    swap = lambda a: jnp.swapaxes(a, 1, 2)
    results["w_13"] = [swap(r) for r in _adamw([ffn_recv[l][0] for l in layers], swap(w_13), swap(m_w_13), swap(v_w_13), "adamw_w_13")]
    results["w_2"] = _adamw([ffn_recv[l][1] for l in layers], w_2, m_w_2, v_w_2, "adamw_w_2")
    mix_m = [m_w_in, m_w_o, m_w_pa, m_w_pb, m_w_pc]
    mix_v = [v_w_in, v_w_o, v_w_pa, v_w_pb, v_w_pc]
    mix_names = ["w_in", "w_o", "w_pa", "w_pb", "w_pc"]
    early = [_adamw([mix_recv[l][a] for l in layers[1:]], mixer_w[a], mix_m[a], mix_v[a], f"adamw_{mix_names[a]}_later", first=1)
             for a in range(n_mix)]
    done = (results["w_13"][1][0, 0, 0] + results["w_2"][1][0, 0, 0] + sum(e[1][1, 0, 0] for e in early)).reshape(1)
    mix_recv[0] = arrived(mix_flight[0], done, SCATTER, "grads_wait_mixer_0")
    for a in range(n_mix):
        results[mix_names[a]] = _adamw([mix_recv[0][a]], mixer_w[a], mix_m[a], mix_v[a], f"adamw_{mix_names[a]}_first",
                                       earlier=early[a])

    order = ["w_mod", "b_mod", "g_mix", "w_in", "gm_ln_g", "gm_ln_b", "gm_w_s", "gm_b_s", "w_pa", "pool_w", "pool_scale",
             "w_pb", "conv_w", "w_pc", "w_o", "g_ffn", "w_13", "w_2", "g_final"]
    return (loss, grad_x, *[results[nm][0] for nm in order], *[results[nm][1] for nm in order],
            *[results[nm][2] for nm in order], *[results[nm][3] for nm in order])
```

```python
import functools

import jax
import jax.numpy as jnp
from jax import lax
from jax.experimental import pallas as pl
from jax.experimental.pallas import tpu as pltpu

F32 = jnp.float32
BF16 = jnp.bfloat16
MESH_ID = pl.DeviceIdType.MESH

N_DEV = 8
EPS = 1e-6
CHUNK = 128
HEADS = 4
HEAD_DIM = 128
BR_W = 512
POOL_WINDOWS = (2, 4, 8, 16)
POOL_HALO = 16
CONV_HALO = 8
N_MOD = 6
ADAM_LR = 0.001
ADAM_B1 = 0.9
ADAM_B2 = 0.999
ADAM_EPS = 1e-08
ADAM_WD = 0.01
ADAM_STEP = 10

TOKEN_TILE = 256
WGRAD_TOKENS = 2048
VMEM_LIMIT = 56 * 1024 * 1024
GELU_K = 0.7978845608028654
GELU_C = 0.044715

ANY = pl.BlockSpec(memory_space=pl.ANY)
VMEM_SPEC = pl.BlockSpec(memory_space=pltpu.VMEM)


def _cparams(**kw):
    return pltpu.CompilerParams(vmem_limit_bytes=VMEM_LIMIT, **kw)


def _dot(a, b):
    return jnp.dot(a, b, preferred_element_type=F32)


def _dot_nt(a, b):
    return lax.dot_general(a, b, (((1,), (1,)), ((), ())), preferred_element_type=F32)


def _dot_tn(a, b):
    return lax.dot_general(a, b, (((0,), (0,)), ((), ())), preferred_element_type=F32)


def _colsum(a):
    return jnp.sum(a, axis=0, keepdims=True)


def _sigmoid(x):
    return 0.5 * jnp.tanh(0.5 * x) + 0.5


def _gelu(x):
    x2 = x * x
    t = jnp.tanh(x * (GELU_K + (GELU_K * GELU_C) * x2))
    return (0.5 * x) * (1.0 + t), t, x2


def _gelu_grad(x, t, x2):
    one_t = 1.0 + t
    return 0.5 * one_t + (0.5 * x) * (one_t * (1.0 - t)) * (GELU_K + (3.0 * GELU_K * GELU_C) * x2)


def _tril_mask():
    r = lax.broadcasted_iota(jnp.int32, (CHUNK, CHUNK), 0)
    c = lax.broadcasted_iota(jnp.int32, (CHUNK, CHUNK), 1)
    return (r >= c).astype(F32)


def _my_index():
    return 4 * lax.axis_index("x") + 2 * lax.axis_index("y") + lax.axis_index("c")


def _peer(k):
    x, y, c = lax.axis_index("x"), lax.axis_index("y"), lax.axis_index("c")
    px = 1 - x if (k >> 2) & 1 else x
    py = 1 - y if (k >> 1) & 1 else y
    pc = 1 - c if k & 1 else c
    return (px, py, pc), 4 * px + 2 * py + pc


def _load_weights(step, pairs, sem):
    @pl.when(step == 0)
    def _():
        copies = [pltpu.make_async_copy(src, dst, sem.at[n]) for n, (src, dst) in enumerate(pairs)]
        for cp in copies:
            cp.start()
        for cp in copies:
            cp.wait()


def _wp_pairs(wp_hbm, wp_v):
    return [(wp_hbm.at[j], wp_v.at[:, pl.ds(HEAD_DIM * j, HEAD_DIM)]) for j in range(N_DEV)]


def _rms_mod(x, g, shift, scale):
    rstd = lax.rsqrt(jnp.mean(x * x, axis=-1, keepdims=True) + EPS)
    xn = x * rstd
    return xn, rstd, (xn * g) * (1.0 + scale) + shift


def _rms_mod_bwd(dh, xn, rstd, g, scale):
    dxn = dh * (1.0 + scale) * g
    dx = rstd * (dxn - xn * jnp.mean(dxn * xn, axis=-1, keepdims=True))
    return dx, _colsum(dh), _colsum(dh * (xn * g)), _colsum(dh * (1.0 + scale) * xn)


def _gmlp_s(vn_b, wmask_b, bexp, s_scr, tm):
    for ch in range(tm // CHUNK):
        rows = slice(ch * CHUNK, (ch + 1) * CHUNK)
        for hh in range(HEADS):
            cols = slice(hh * HEAD_DIM, (hh + 1) * HEAD_DIM)
            s_scr[rows, cols] = _dot(wmask_b[hh], vn_b[rows, cols]) + bexp[:, cols]
    return s_scr[...]


def _inv_count(pos, win):
    return 1.0 / jnp.minimum(pos + 1, win).astype(F32)


def _window_sums(ext, tm, trailing):
    n = tm + POOL_HALO
    sums = []
    for g, win in enumerate(POOL_WINDOWS):
        s = ext[:, g * HEAD_DIM:(g + 1) * HEAD_DIM]
        span = 1
        while span < win:
            s = s + pltpu.roll(s, span if trailing else n - span, 0)
            span *= 2
        sums.append(s[POOL_HALO:POOL_HALO + tm] if trailing else s[0:tm])
    return sums


def _pool_p(xb, xbext, pos, tm):
    sums = _window_sums(xbext, tm, True)
    return [sums[g] * _inv_count(pos, win) - xb[:, g * HEAD_DIM:(g + 1) * HEAD_DIM] for g, win in enumerate(POOL_WINDOWS)]


def _mixer_fwd(x, modl, g_mix, ln_g, ln_b, w_s, bexp, pool_w, pool_scale, conv_w, win, wpa, wpb, wpc, wo, layer, seq):
    t_all, d = x.shape
    tm = min(TOKEN_TILE, seq)
    n_seq_tiles = seq // tm
    blk = win.shape[-1]

    def body(x_ref, mod_ref, gmix_ref, lng_ref, lnb_ref, ws_ref, bexp_ref, pw_ref, ps_ref, cw_ref,
             win_hbm, wpa_hbm, wpb_hbm, wpc_hbm, wo_hbm,
             h_ref, z_ref, ycat_ref, ocat_ref, mrg_ref, mo_ref, xmid_ref,
             win_v, wpa_v, wpb_v, wpc_v, wo_v, xbext, zcext, s_scr, sem):
        i = pl.program_id(0)
        pairs = [(win_hbm, win_v), (wo_hbm, wo_v)]
        pairs += _wp_pairs(wpa_hbm, wpa_v) + _wp_pairs(wpb_hbm, wpb_v) + _wp_pairs(wpc_hbm, wpc_v)
        _load_weights(i, pairs, sem)
        tile_in_seq = i % n_seq_tiles

        @pl.when(tile_in_seq == 0)
        def _():
            xbext[0:POOL_HALO, :] = jnp.zeros((POOL_HALO, BR_W), F32)
            zcext[0:CONV_HALO, :] = jnp.zeros((CONV_HALO, BR_W), F32)

        x_t = x_ref[...]
        shift1, scale1, gate1 = mod_ref[0, 0:1, :], mod_ref[0, 1:2, :], mod_ref[0, 2:3, :]
        _, _, h = _rms_mod(x_t, gmix_ref[...], shift1, scale1)
        hb = h.astype(BF16)
        h_ref[...] = hb
        def project(j):
            zj = _dot(hb, win_v[j])
            z_ref[:, j * blk:(j + 1) * blk] = zj.astype(BF16)
            return zj

        z0, z1 = project(0), project(1)
        u = z0[:, 0:BR_W]
        v = jnp.concatenate([z0[:, BR_W:blk], z1[:, 0:2 * BR_W - blk]], axis=1)
        xb = z1[:, 2 * BR_W - blk:blk]

        gu = _gelu(u)[0]
        gv = _gelu(v)[0]
        mu = jnp.mean(gv, axis=-1, keepdims=True)
        cen = gv - mu
        rs = lax.rsqrt(jnp.mean(cen * cen, axis=-1, keepdims=True) + EPS)
        vn = (cen * rs) * lng_ref[...] + lnb_ref[...]
        mask = _tril_mask()
        wmask_b = [(ws_ref[hh] * mask).astype(BF16) for hh in range(HEADS)]
        s = _gmlp_s(vn.astype(BF16), wmask_b, bexp_ref[...], s_scr, tm)
        oa = (gu * s).astype(BF16)
        ya = _dot(oa, wpa_v[...])

        xbext[POOL_HALO:POOL_HALO + tm, :] = xb
        pos = tile_in_seq * tm + lax.broadcasted_iota(jnp.int32, (tm, 1), 0)
        ps = _pool_p(xb, xbext, pos, tm)
        qs = [_dot(ps[g].astype(BF16), pw_ref[g].astype(BF16)) for g in range(len(POOL_WINDOWS))]
        ob = (jnp.concatenate(qs, axis=1) * ps_ref[...]).astype(BF16)
        yb = _dot(ob, wpb_v[...])
        xbext[0:POOL_HALO, :] = xbext[tm:tm + POOL_HALO, :]

        z2, z3 = project(2), project(3)
        bg = z2[:, 0:BR_W]
        cg = jnp.concatenate([z2[:, BR_W:blk], z3[:, 0:2 * BR_W - blk]], axis=1)
        hc = z3[:, 2 * BR_W - blk:blk]
        zz = cg * hc
        zcext[CONV_HALO:CONV_HALO + tm, :] = zz
        yconv = (cw_ref[0:1, :] * zcext[pl.ds(CONV_HALO - 2, tm), :] + cw_ref[1:2, :] * zcext[pl.ds(CONV_HALO - 1, tm), :]
                 + cw_ref[2:3, :] * zz)
        oc = (bg * yconv).astype(BF16)
        yc = _dot(oc, wpc_v[...])
        zcext[0:CONV_HALO, :] = zcext[tm:tm + CONV_HALO, :]

        ocat_ref[:, 0:BR_W] = oa
        ocat_ref[:, BR_W:2 * BR_W] = ob
        ocat_ref[:, 2 * BR_W:3 * BR_W] = oc
        ycat_ref[:, 0:d] = ya.astype(BF16)
        ycat_ref[:, d:2 * d] = yb.astype(BF16)
        ycat_ref[:, 2 * d:3 * d] = yc.astype(BF16)

        ys = (ya, yb, yc)
        zg = jnp.concatenate([project(j).astype(BF16) for j in range(4, N_DEV)], axis=1)
        mb = _sigmoid(zg[:, 0:d]) * ys[0].astype(BF16)
        for n in range(1, 3):
            mb = mb + _sigmoid(zg[:, n * d:(n + 1) * d]) * ys[n].astype(BF16)
        mrg_ref[...] = mb
        mo = _dot(mb, wo_v[...])
        mo_ref[...] = mo.astype(BF16)
        xmid_ref[...] = x_t + gate1 * mo

    tok = lambda cols: pl.BlockSpec((tm, cols), lambda i: (i, 0))
    const2 = lambda a: pl.BlockSpec(a.shape, lambda i: (0,) * a.ndim)
    in_specs = [
        tok(d),
        pl.BlockSpec((1, 8, d), lambda i: (i // n_seq_tiles, 0, 0)),
        const2(g_mix), const2(ln_g), const2(ln_b), const2(w_s), const2(bexp), const2(pool_w), const2(pool_scale),
        const2(conv_w), ANY, ANY, ANY, ANY, ANY,
    ]
    out_shape = [
        jax.ShapeDtypeStruct((t_all, d), BF16),
        jax.ShapeDtypeStruct((t_all, N_DEV * blk), BF16),
        jax.ShapeDtypeStruct((t_all, 3 * d), BF16),
        jax.ShapeDtypeStruct((t_all, 3 * BR_W), BF16),
        jax.ShapeDtypeStruct((t_all, d), BF16),
        jax.ShapeDtypeStruct((t_all, d), BF16),
        jax.ShapeDtypeStruct((t_all, d), F32),
    ]
    out_specs = [tok(d), tok(N_DEV * blk), tok(3 * d), tok(3 * BR_W), tok(d), tok(d), tok(d)]
    scratch = [
        pltpu.VMEM((N_DEV, d, blk), BF16), pltpu.VMEM((BR_W, d), BF16), pltpu.VMEM((BR_W, d), BF16),
        pltpu.VMEM((BR_W, d), BF16), pltpu.VMEM((d, d), BF16),
        pltpu.VMEM((tm + POOL_HALO, BR_W), F32), pltpu.VMEM((tm + CONV_HALO, BR_W), F32), pltpu.VMEM((tm, BR_W), F32),
        pltpu.SemaphoreType.DMA((2 + 3 * N_DEV,)),
    ]
    return pl.pallas_call(
        body, name=f"mixer_fwd_{layer}", grid=(t_all // tm,), in_specs=in_specs, out_specs=out_specs, out_shape=out_shape,
        scratch_shapes=scratch, compiler_params=_cparams(dimension_semantics=("arbitrary",)),
    )(x, modl, g_mix, ln_g, ln_b, w_s, bexp, pool_w, pool_scale, conv_w, win, wpa, wpb, wpc, wo)


def _ffn_fwd(x, modl, g_ffn, w13, w2, layer, seq):
    t_all, d = x.shape
    tm = min(TOKEN_TILE, seq)
    n_seq_tiles = seq // tm
    fb = w13.shape[-1]
    n_hid = N_DEV // 2

    def body(x_ref, mod_ref, g_ref, w13_hbm, w2_hbm, h_ref, ab_ref, hid_ref, f_ref, xo_ref, w13_v, w2_v, sem):
        i = pl.program_id(0)
        _load_weights(i, [(w13_hbm, w13_v), (w2_hbm, w2_v)], sem)
        x_t = x_ref[...]
        shift2, scale2, gate2 = mod_ref[0, 3:4, :], mod_ref[0, 4:5, :], mod_ref[0, 5:6, :]
        _, _, h = _rms_mod(x_t, g_ref[...], shift2, scale2)
        hb = h.astype(BF16)
        h_ref[...] = hb
        f = jnp.zeros((tm, d), F32)
        for k in range(n_hid):
            a = _dot(hb, w13_v[k])
            b = _dot(hb, w13_v[n_hid + k])
            a, b = a.astype(BF16), b.astype(BF16)
            ab_ref[k] = a
            ab_ref[n_hid + k] = b
            hid = (a * _sigmoid(a)) * b
            hid_ref[k] = hid
            f = f + _dot(hid, w2_v[k * fb:(k + 1) * fb, :])
        f_ref[...] = f.astype(BF16)
        xo_ref[...] = x_t + gate2 * f

    tok = lambda cols: pl.BlockSpec((tm, cols), lambda i: (i, 0))
    blk3 = lambda n: pl.BlockSpec((n, tm, fb), lambda i: (0, i, 0))
    in_specs = [tok(d), pl.BlockSpec((1, 8, d), lambda i: (i // n_seq_tiles, 0, 0)),
                pl.BlockSpec(g_ffn.shape, lambda i: (0, 0)), ANY, ANY]
    out_shape = [
        jax.ShapeDtypeStruct((t_all, d), BF16),
        jax.ShapeDtypeStruct((N_DEV, t_all, fb), BF16),
        jax.ShapeDtypeStruct((n_hid, t_all, fb), BF16),
        jax.ShapeDtypeStruct((t_all, d), BF16),
        jax.ShapeDtypeStruct((t_all, d), F32),
    ]
    out_specs = [tok(d), blk3(N_DEV), blk3(n_hid), tok(d), tok(d)]
    scratch = [pltpu.VMEM((N_DEV, d, fb), BF16), pltpu.VMEM((n_hid * fb, d), BF16), pltpu.SemaphoreType.DMA((2,))]
    return pl.pallas_call(
        body, name=f"ffn_fwd_{layer}", grid=(t_all // tm,), in_specs=in_specs, out_specs=out_specs, out_shape=out_shape,
        scratch_shapes=scratch, compiler_params=_cparams(dimension_semantics=("arbitrary",)),
    )(x, modl, g_ffn, w13, w2)


def _final_loss(x, g_final, target, seq):
    t_all, d = x.shape
    tm = min(TOKEN_TILE, seq)

    def body(x_ref, g_ref, t_ref, dx_ref, loss_ref, dg_ref):
        i = pl.program_id(0)

        @pl.when(i == 0)
        def _():
            loss_ref[...] = jnp.zeros(loss_ref.shape, F32)
            dg_ref[...] = jnp.zeros(dg_ref.shape, F32)

        x_t = x_ref[...]
        g = g_ref[...]
        rstd = lax.rsqrt(jnp.mean(x_t * x_t, axis=-1, keepdims=True) + EPS)
        xn = x_t * rstd
        err = xn * g - t_ref[...]
        loss_ref[0:1, :] += _colsum(err * err) * (0.5 / d)
        dy = err * (1.0 / d)
        dg_ref[0:1, :] += _colsum(dy * xn)
        dxn = dy * g
        dx_ref[...] = rstd * (dxn - xn * jnp.mean(dxn * xn, axis=-1, keepdims=True))

        @pl.when(i == pl.num_programs(0) - 1)
        def _():
            loss_ref[...] = jnp.broadcast_to(jnp.sum(loss_ref[0:1, :], axis=1, keepdims=True), loss_ref.shape)

    tok = pl.BlockSpec((tm, d), lambda i: (i, 0))
    acc = pl.BlockSpec((8, d), lambda i: (0, 0))
    return pl.pallas_call(
        body, name="final_loss", grid=(t_all // tm,),
        in_specs=[tok, pl.BlockSpec((1, d), lambda i: (0, 0)), tok], out_specs=[tok, acc, acc],
        out_shape=[jax.ShapeDtypeStruct((t_all, d), F32), jax.ShapeDtypeStruct((8, d), F32), jax.ShapeDtypeStruct((8, d), F32)],
        compiler_params=_cparams(dimension_semantics=("arbitrary",)),
    )(x, g_final, target)


def _ffn_bwd(dxo, xmid, ab, f, modl, g_ffn, w13, w2, layer, seq):
    t_all, d = xmid.shape
    tm = min(TOKEN_TILE, seq)
    n_seq_tiles = seq // tm
    fb = w13.shape[-1]
    n_hid = N_DEV // 2

    def body(dxo_ref, x_ref, ab_ref, f_ref, mod_ref, g_ref, w13_hbm, w2_hbm,
             dx_ref, df_ref, dab_ref, dmod_ref, dg_ref, w13_v, w2_v, sem):
        i = pl.program_id(0)
        _load_weights(i, [(w13_hbm, w13_v), (w2_hbm, w2_v)], sem)

        @pl.when(i == 0)
        def _():
            dg_ref[...] = jnp.zeros(dg_ref.shape, F32)

        @pl.when(i % n_seq_tiles == 0)
        def _():
            dmod_ref[...] = jnp.zeros(dmod_ref.shape, F32)

        scale2, gate2 = mod_ref[0, 4:5, :], mod_ref[0, 5:6, :]
        g = g_ref[...]
        x_t = x_ref[...]
        rstd = lax.rsqrt(jnp.mean(x_t * x_t, axis=-1, keepdims=True) + EPS)
        xn = x_t * rstd
        dxo_t = dxo_ref[...]
        dmod_ref[0, 2:3, :] += _colsum(dxo_t * f_ref[...].astype(F32))
        dfb = (dxo_t * gate2).astype(BF16)
        df_ref[...] = dfb
        dh = jnp.zeros((tm, d), F32)
        for k in range(n_hid):
            dhid = _dot_nt(dfb, w2_v[k * fb:(k + 1) * fb, :]).astype(BF16)
            a = ab_ref[k]
            b = ab_ref[n_hid + k]
            sg = _sigmoid(a)
            da = dhid * b * (sg * (1.0 + a * (1.0 - sg)))
            db = dhid * (a * sg)
            dab_ref[k] = da
            dab_ref[n_hid + k] = db
            dh = dh + _dot_nt(da, w13_v[k]) + _dot_nt(db, w13_v[n_hid + k])
        dx, dshift, dscale, dg = _rms_mod_bwd(dh, xn, rstd, g, scale2)
        dmod_ref[0, 0:1, :] += dshift
        dmod_ref[0, 1:2, :] += dscale
        dg_ref[0:1, :] += dg
        dx_ref[...] = dxo_t + dx

    tok = lambda cols: pl.BlockSpec((tm, cols), lambda i: (i, 0))
    blk3 = lambda n: pl.BlockSpec((n, tm, fb), lambda i: (0, i, 0))
    modspec = pl.BlockSpec((1, 8, d), lambda i: (i // n_seq_tiles, 0, 0))
    in_specs = [tok(d), tok(d), blk3(N_DEV), tok(d), modspec, pl.BlockSpec(g_ffn.shape, lambda i: (0, 0)), ANY, ANY]
    out_shape = [
        jax.ShapeDtypeStruct((t_all, d), F32), jax.ShapeDtypeStruct((t_all, d), BF16),
        jax.ShapeDtypeStruct((N_DEV, t_all, fb), BF16), jax.ShapeDtypeStruct(modl.shape, F32),
        jax.ShapeDtypeStruct((8, d), F32),
    ]
    out_specs = [tok(d), tok(d), blk3(N_DEV), modspec, pl.BlockSpec((8, d), lambda i: (0, 0))]
    scratch = [pltpu.VMEM((N_DEV, d, fb), BF16), pltpu.VMEM((n_hid * fb, d), BF16), pltpu.SemaphoreType.DMA((2,))]
    return pl.pallas_call(
        body, name=f"ffn_bwd_{layer}", grid=(t_all // tm,), in_specs=in_specs, out_specs=out_specs, out_shape=out_shape,
        scratch_shapes=scratch, compiler_params=_cparams(dimension_semantics=("arbitrary",)),
    )(dxo, xmid, ab, f, modl, g_ffn, w13, w2)


def _mixer_bwd(dxm, x, z, ycat, mo, modl, g_mix, ln_g, ln_b, w_s, bexp, pool_w, pool_scale, conv_w,
               win, wpa, wpb, wpc, wo, layer, seq):
    t_all, d = x.shape
    tm = min(TOKEN_TILE, seq)
    n_seq_tiles = seq // tm
    blk = win.shape[-1]
    n_win = len(POOL_WINDOWS)

    def tile_of(i):
        return (i // n_seq_tiles) * n_seq_tiles + (n_seq_tiles - 1 - i % n_seq_tiles)

    def halo_row_block(i):
        return jnp.maximum(tile_of(i) * (tm // POOL_HALO) - 1, 0)

    def body(dxm_ref, x_ref, z_ref, zpb_ref, zpc_ref, ycat_ref, mo_ref, mod_ref, gmix_ref, lng_ref, lnb_ref, ws_ref,
             bexp_ref, pw_ref, ps_ref, cw_ref, win_hbm, wpa_hbm, wpb_hbm, wpc_hbm, wo_hbm,
             dx_ref, dz_ref, dycat_ref, dmo_ref, dmod_ref, dg_ref, sm_ref, dws_ref, dssum_ref, dpw_ref,
             win_v, wpa_v, wpb_v, wpc_v, wo_v, xbext, zzext, rext, dyext, s_scr, dvn_scr, sem):
        i = pl.program_id(0)
        pairs = [(win_hbm.at[j], win_v.at[:, pl.ds(blk * j, blk)]) for j in range(N_DEV)] + [(wo_hbm, wo_v)]
        pairs += _wp_pairs(wpa_hbm, wpa_v) + _wp_pairs(wpb_hbm, wpb_v) + _wp_pairs(wpc_hbm, wpc_v)
        _load_weights(i, pairs, sem)
        tile_in_seq = n_seq_tiles - 1 - i % n_seq_tiles
        first_of_seq = tile_in_seq == 0

        @pl.when(i == 0)
        def _():
            for r in (dg_ref, sm_ref, dws_ref, dssum_ref, dpw_ref):
                r[...] = jnp.zeros(r.shape, F32)

        @pl.when(i % n_seq_tiles == 0)
        def _():
            dmod_ref[...] = jnp.zeros(dmod_ref.shape, F32)
            rext[tm:tm + POOL_HALO, :] = jnp.zeros((POOL_HALO, BR_W), F32)
            dyext[tm:tm + CONV_HALO, :] = jnp.zeros((CONV_HALO, BR_W), F32)

        shift1, scale1, gate1 = mod_ref[0, 0:1, :], mod_ref[0, 1:2, :], mod_ref[0, 2:3, :]
        g = gmix_ref[...]
        x_t = x_ref[...]
        rstd = lax.rsqrt(jnp.mean(x_t * x_t, axis=-1, keepdims=True) + EPS)
        xn = x_t * rstd
        dxm_t = dxm_ref[...]
        dmod_ref[0, 2:3, :] += _colsum(dxm_t * mo_ref[...].astype(F32))
        dmo = (dxm_t * gate1).astype(BF16)
        dmo_ref[...] = dmo
        dmerged = _dot_nt(dmo, wo_v[...])

        dh_parts = []

        def emit_dz(lo, hi, value):
            vb = value.astype(BF16)
            dz_ref[:, lo:hi] = vb
            dh_parts.append(_dot_nt(vb, win_v[:, lo:hi]))

        dys = []
        dmerged = dmerged.astype(BF16)
        for n in range(3):
            gt = _sigmoid(z_ref[:, 3 * d + n * d:3 * d + (n + 1) * d])
            dyn = dmerged * gt
            emit_dz(3 * d + n * d, 3 * d + (n + 1) * d, dyn * ycat_ref[:, n * d:(n + 1) * d] * (1.0 - gt))
            dycat_ref[:, n * d:(n + 1) * d] = dyn
            dys.append(dyn)

        doa = _dot_nt(dys[0], wpa_v[...])
        u = z_ref[:, 0:BR_W].astype(F32)
        v = z_ref[:, BR_W:2 * BR_W].astype(F32)
        gu, tu, u2 = _gelu(u)
        gv, tv, v2 = _gelu(v)
        mu = jnp.mean(gv, axis=-1, keepdims=True)
        cen = gv - mu
        rs = lax.rsqrt(jnp.mean(cen * cen, axis=-1, keepdims=True) + EPS)
        vhat = cen * rs
        lng = lng_ref[...]
        vn_b = (vhat * lng + lnb_ref[...]).astype(BF16)
        mask = _tril_mask()
        wmask = [ws_ref[hh] * mask for hh in range(HEADS)]
        s = _gmlp_s(vn_b, [w.astype(BF16) for w in wmask], bexp_ref[...], s_scr, tm)
        du = (doa * s) * _gelu_grad(u, tu, u2)
        ds = doa * gu
        ds_b = ds.astype(BF16)
        dssum = jnp.zeros((CHUNK, BR_W), F32)
        for ch in range(tm // CHUNK):
            rows = slice(ch * CHUNK, (ch + 1) * CHUNK)
            dssum = dssum + ds[rows, :]
            for hh in range(HEADS):
                cols = slice(hh * HEAD_DIM, (hh + 1) * HEAD_DIM)
                dvn_scr[rows, cols] = _dot_tn(wmask[hh].astype(BF16), ds_b[rows, cols])
                dws_ref[hh] += _dot_nt(ds_b[rows, cols], vn_b[rows, cols]) * mask
        dssum_ref[...] += dssum
        dvn = dvn_scr[...]
        sm_ref[0:1, :] += _colsum(dvn * vhat)
        sm_ref[1:2, :] += _colsum(dvn)
        dvhat = dvn * lng
        dgv = rs * (dvhat - jnp.mean(dvhat, axis=-1, keepdims=True) - vhat * jnp.mean(dvhat * vhat, axis=-1, keepdims=True))
        dv = dgv * _gelu_grad(v, tv, v2)
        emit_dz(0, 2 * BR_W, jnp.concatenate([du, dv], axis=1))

        dob = _dot_nt(dys[1], wpb_v[...])
        xb = z_ref[:, 2 * BR_W:3 * BR_W].astype(F32)
        xbext[0:POOL_HALO, :] = jnp.where(first_of_seq, 0.0, zpb_ref[...].astype(F32))
        xbext[POOL_HALO:POOL_HALO + tm, :] = xb
        pos = tile_in_seq * tm + lax.broadcasted_iota(jnp.int32, (tm, 1), 0)
        ps = _pool_p(xb, xbext, pos, tm)
        scale_b = ps_ref[...]
        dq = dob * scale_b
        qs, dps = [], []
        for gi, win_len in enumerate(POOL_WINDOWS):
            cols = slice(gi * HEAD_DIM, (gi + 1) * HEAD_DIM)
            pw_b = pw_ref[gi].astype(BF16)
            p_b = ps[gi].astype(BF16)
            dq_b = dq[:, cols].astype(BF16)
            qs.append(_dot(p_b, pw_b))
            dpw_ref[gi] += _dot_tn(p_b, dq_b)
            dp = _dot_nt(dq_b, pw_b)
            dps.append(dp)
            rext[0:tm, cols] = dp * _inv_count(pos, win_len)
        sm_ref[2:3, :] += _colsum(dob * jnp.concatenate(qs, axis=1))
        dxbs = [acc - dp for acc, dp in zip(_window_sums(rext, tm, False), dps)]
        emit_dz(2 * BR_W, 3 * BR_W, jnp.concatenate(dxbs, axis=1))
        rext[tm:tm + POOL_HALO, :] = rext[0:POOL_HALO, :]

        doc = _dot_nt(dys[2], wpc_v[...])
        bg = z_ref[:, 3 * BR_W:4 * BR_W].astype(F32)
        cg = z_ref[:, 4 * BR_W:5 * BR_W].astype(F32)
        hc = z_ref[:, 5 * BR_W:6 * BR_W].astype(F32)
        zz = cg * hc
        zprev = zpc_ref[POOL_HALO - CONV_HALO:POOL_HALO, :].astype(F32)
        zzext[0:CONV_HALO, :] = jnp.where(first_of_seq, 0.0, zprev[:, 0:BR_W] * zprev[:, BR_W:2 * BR_W])
        zzext[CONV_HALO:CONV_HALO + tm, :] = zz
        zm2 = zzext[pl.ds(CONV_HALO - 2, tm), :]
        zm1 = zzext[pl.ds(CONV_HALO - 1, tm), :]
        w0, w1, w2c = cw_ref[0:1, :], cw_ref[1:2, :], cw_ref[2:3, :]
        yconv = w0 * zm2 + w1 * zm1 + w2c * zz
        dyc = doc * bg
        sm_ref[3:4, :] += _colsum(dyc * zm2)
        sm_ref[4:5, :] += _colsum(dyc * zm1)
        sm_ref[5:6, :] += _colsum(dyc * zz)
        dyext[0:tm, :] = dyc
        dzz = w2c * dyc + w1 * dyext[pl.ds(1, tm), :] + w0 * dyext[pl.ds(2, tm), :]
        dyext[tm:tm + CONV_HALO, :] = dyext[0:CONV_HALO, :]
        emit_dz(3 * BR_W, 6 * BR_W, jnp.concatenate([doc * yconv, dzz * hc, dzz * cg], axis=1))

        dh = dh_parts[0]
        for part in dh_parts[1:]:
            dh = dh + part
        dx, dshift, dscale, dg = _rms_mod_bwd(dh, xn, rstd, g, scale1)
        dmod_ref[0, 0:1, :] += dshift
        dmod_ref[0, 1:2, :] += dscale
        dg_ref[0:1, :] += dg
        dx_ref[...] = dxm_t + dx

    tok = lambda cols: pl.BlockSpec((tm, cols), lambda i: (tile_of(i), 0))
    const2 = lambda a: pl.BlockSpec(a.shape, lambda i: (0,) * a.ndim)
    modspec = pl.BlockSpec((1, 8, d), lambda i: (i // n_seq_tiles, 0, 0))
    in_specs = [
        tok(d), tok(d), tok(N_DEV * blk),
        pl.BlockSpec((POOL_HALO, BR_W), lambda i: (halo_row_block(i), 2)),
        pl.BlockSpec((POOL_HALO, 2 * BR_W), lambda i: (halo_row_block(i), 2)),
        tok(3 * d), tok(d), modspec,
        const2(g_mix), const2(ln_g), const2(ln_b), const2(w_s), const2(bexp), const2(pool_w), const2(pool_scale),
        const2(conv_w), ANY, ANY, ANY, ANY, ANY,
    ]
    acc = lambda shape: pl.BlockSpec(shape, lambda i: (0,) * len(shape))
    out_shape = [
        jax.ShapeDtypeStruct((t_all, d), F32), jax.ShapeDtypeStruct((t_all, N_DEV * blk), BF16),
        jax.ShapeDtypeStruct((t_all, 3 * d), BF16), jax.ShapeDtypeStruct((t_all, d), BF16),
        jax.ShapeDtypeStruct(modl.shape, F32), jax.ShapeDtypeStruct((8, d), F32), jax.ShapeDtypeStruct((8, BR_W), F32),
        jax.ShapeDtypeStruct((HEADS, CHUNK, CHUNK), F32), jax.ShapeDtypeStruct((CHUNK, BR_W), F32),
        jax.ShapeDtypeStruct((n_win, HEAD_DIM, HEAD_DIM), F32),
    ]
    out_specs = [tok(d), tok(N_DEV * blk), tok(3 * d), tok(d), modspec, acc((8, d)), acc((8, BR_W)),
                 acc((HEADS, CHUNK, CHUNK)), acc((CHUNK, BR_W)), acc((n_win, HEAD_DIM, HEAD_DIM))]
    scratch = [
        pltpu.VMEM((d, N_DEV * blk), BF16), pltpu.VMEM((BR_W, d), BF16), pltpu.VMEM((BR_W, d), BF16),
        pltpu.VMEM((BR_W, d), BF16), pltpu.VMEM((d, d), BF16),
        pltpu.VMEM((tm + POOL_HALO, BR_W), F32), pltpu.VMEM((tm + CONV_HALO, BR_W), F32),
        pltpu.VMEM((tm + POOL_HALO, BR_W), F32), pltpu.VMEM((tm + CONV_HALO, BR_W), F32),
        pltpu.VMEM((tm, BR_W), F32), pltpu.VMEM((tm, BR_W), F32),
        pltpu.SemaphoreType.DMA((1 + 4 * N_DEV,)),
    ]
    return pl.pallas_call(
        body, name=f"mixer_bwd_{layer}", grid=(t_all // tm,), in_specs=in_specs, out_specs=out_specs, out_shape=out_shape,
        scratch_shapes=scratch, compiler_params=_cparams(dimension_semantics=("arbitrary",)),
    )(dxm, x, z, z, z, ycat, mo, modl, g_mix, ln_g, ln_b, w_s, bexp, pool_w, pool_scale, conv_w, win, wpa, wpb, wpc, wo)


def _wgrad(a, b, a_spec, b_spec, out_struct, out_spec, grid_kn, tk, tn, split, name):
    t_all = a.shape[-2]
    tt = min(WGRAD_TOKENS, t_all)
    n_t = t_all // tt

    def body(a_ref, b_ref, o_ref, acc):
        t = pl.program_id(2)

        @pl.when(t == 0)
        def _():
            acc[...] = jnp.zeros(acc.shape, F32)

        acc[...] += _dot_tn(a_ref[...], b_ref[...])

        @pl.when(t == n_t - 1)
        def _():
            if split:
                for j in range(split):
                    w = tn // split
                    o_ref[j] = acc[:, j * w:(j + 1) * w].astype(o_ref.dtype)
            else:
                o_ref[...] = acc[...].astype(o_ref.dtype)

    return pl.pallas_call(
        body, name=name, grid=(*grid_kn, n_t), in_specs=[a_spec(tt), b_spec(tt)], out_specs=out_spec, out_shape=out_struct,
        scratch_shapes=[pltpu.VMEM((tk, tn), F32)],
        compiler_params=_cparams(dimension_semantics=("arbitrary", "arbitrary", "arbitrary")),
    )(a, b)


def _mixer_wgrads(h, dz, mrg, dmo, ocat, dycat, layer):
    d = h.shape[1]
    blk = dz.shape[1] // N_DEV
    g_win = _wgrad(
        h, dz, lambda tt: pl.BlockSpec((tt, d), lambda k, n, t: (t, 0)), lambda tt: pl.BlockSpec((tt, blk), lambda k, n, t: (t, n)),
        jax.ShapeDtypeStruct((N_DEV, d, blk), BF16), pl.BlockSpec((None, d, blk), lambda k, n, t: (n, 0, 0)),
        (1, N_DEV), d, blk, 0, f"wgrad_in_{layer}")
    g_wo = _wgrad(
        mrg, dmo, lambda tt: pl.BlockSpec((tt, d), lambda k, n, t: (t, 0)), lambda tt: pl.BlockSpec((tt, d), lambda k, n, t: (t, 0)),
        jax.ShapeDtypeStruct((d, d), BF16), pl.BlockSpec((d, d), lambda k, n, t: (0, 0)), (1, 1), d, d, 0, f"wgrad_o_{layer}")
    g_wp = []
    for n, nm in enumerate("abc"):
        g_wp.append(_wgrad(
            ocat, dycat, lambda tt, n=n: pl.BlockSpec((tt, BR_W), lambda k, nn, t: (t, n)),
            lambda tt, n=n: pl.BlockSpec((tt, d), lambda k, nn, t: (t, n)),
            jax.ShapeDtypeStruct((N_DEV, BR_W, d // N_DEV), BF16),
            pl.BlockSpec((N_DEV, BR_W, d // N_DEV), lambda k, nn, t: (0, 0, 0)), (1, 1), BR_W, d, N_DEV, f"wgrad_p{nm}_{layer}"))
    return [g_win, g_wo.reshape(N_DEV, d // N_DEV, d), *g_wp]


def _ffn_wgrads(h2, dab, hid, df, layer):
    d = h2.shape[1]
    fb = dab.shape[-1]
    n_hid = N_DEV // 2
    g_w13 = _wgrad(
        dab, h2, lambda tt: pl.BlockSpec((None, tt, fb), lambda k, n, t: (k, t, 0)),
        lambda tt: pl.BlockSpec((tt, d), lambda k, n, t: (t, 0)),
        jax.ShapeDtypeStruct((N_DEV, fb, d), BF16), pl.BlockSpec((None, fb, d), lambda k, n, t: (k, 0, 0)),
        (N_DEV, 1), fb, d, 0, f"wgrad_13_{layer}")
    g_w2 = _wgrad(
        hid, df, lambda tt: pl.BlockSpec((None, tt, fb), lambda k, n, t: (k, t, 0)),
        lambda tt: pl.BlockSpec((tt, d), lambda k, n, t: (t, 0)),
        jax.ShapeDtypeStruct((n_hid * fb, d), BF16), pl.BlockSpec((fb, d), lambda k, n, t: (k, 0)),
        (n_hid, 1), fb, d, 0, f"wgrad_2_{layer}")
    return [g_w13, g_w2.reshape(N_DEV, fb // 2, d)]


def _remote(src, dst, send_sem, recv_sem, dev):
    return pltpu.make_async_remote_copy(src_ref=src, dst_ref=dst, send_sem=send_sem, recv_sem=recv_sem, device_id=dev,
                                        device_id_type=MESH_ID)


HBM_SPEC = pl.BlockSpec(memory_space=pltpu.HBM)
SEM_SPEC = pl.BlockSpec(memory_space=pltpu.SEMAPHORE)
DATAFLOW = pltpu.SideEffectType.DATAFLOW_SIDE_EFFECTING
GATHER, SCATTER = "gather", "scatter"


def _xfer_sem(a, k):
    return a * (N_DEV - 1) + k - 1


def _gather_zones(weights, layer):
    me = _my_index()
    return [lax.dynamic_update_index_in_dim(lax.empty((N_DEV, *w.shape[1:]), BF16), w[layer].astype(BF16)[None], me, 0)
            for w in weights]


def _scatter_zones(partials):
    me = _my_index()
    return [lax.dynamic_update_index_in_dim(lax.empty(p.shape, p.dtype), lax.dynamic_index_in_dim(p, me, 0, keepdims=True), me, 0)
            for p in partials]


def _xfer_src(kind, src, land, a, me, idx):
    return land[a].at[me] if kind == GATHER else src[a].at[idx]


def _xfer_start(srcs, lands, kind, name, after):
    n, n_src = len(lands), len(srcs)

    def body(*refs):
        src, land = refs[:n_src], refs[n_src:n_src + n]
        send_sems, recv_sems = refs[n_src + n + 1], refs[n_src + n + 2]
        token = refs[-1]
        me = _my_index()
        for k in range(1, N_DEV):
            dev, idx = _peer(k)
            for a in range(n):
                q = _xfer_sem(a, k)
                _remote(_xfer_src(kind, src, land, a, me, idx), land[a].at[me], send_sems.at[q], recv_sems.at[q], dev).start()
        token[...] = jnp.zeros(token.shape, token.dtype)

    both = [*srcs, *lands]
    sems = pltpu.SemaphoreType.DMA((n * (N_DEV - 1),))
    out_shape = (sems, sems, *[pltpu.HBM(a.shape, a.dtype) for a in both], jax.ShapeDtypeStruct((8, 128), F32))
    outs = pl.pallas_call(
        body, name=name, in_specs=[HBM_SPEC] * len(both) + [ANY],
        out_specs=(SEM_SPEC, SEM_SPEC, *[HBM_SPEC] * len(both), VMEM_SPEC),
        out_shape=out_shape, input_output_aliases={i: 2 + i for i in range(len(both))},
        compiler_params=pltpu.CompilerParams(has_side_effects=DATAFLOW),
    )(*[pltpu.with_memory_space_constraint(a, pltpu.HBM) for a in both], after)
    return outs[0], outs[1], list(outs[2:2 + n_src]), list(outs[2 + n_src:2 + n_src + n]), outs[-1]


def _gather_two_level(zones, name):
    n = len(zones)

    def body(*refs):
        out = refs[n:2 * n]
        send_sems, recv_sems = refs[2 * n], refs[2 * n + 1]
        x, y, c = lax.axis_index("x"), lax.axis_index("y"), lax.axis_index("c")
        index = lambda px, py, pc: 4 * px + 2 * py + pc
        me, sibling = (x, y, c), (x, y, 1 - c)
        chips = [(1 - x, y), (x, 1 - y), (1 - x, 1 - y)]

        def copy(a, k, block, to):
            rows = out[a].at[index(*block)]
            return _remote(rows, rows, send_sems.at[a * (N_DEV - 1) + k], recv_sems.at[a * (N_DEV - 1) + k], to)

        first = [copy(a, 0, me, sibling) for a in range(n)]
        first += [copy(a, 1 + j, me, (*chip, c)) for j, chip in enumerate(chips) for a in range(n)]
        for cp in first:
            cp.start()
        passed = []
        for j, chip in enumerate(chips):
            for a in range(n):
                copy(a, 1 + j, (*chip, c), me).wait_recv()
                passed.append(copy(a, 4 + j, (*chip, c), sibling))
                passed[-1].start()
        for a in range(n):
            copy(a, 0, sibling, me).wait_recv()
            for j, chip in enumerate(chips):
                copy(a, 4 + j, (*chip, 1 - c), me).wait_recv()
        for cp in first + passed:
            cp.wait_send()

    sems = pltpu.SemaphoreType.DMA((n * (N_DEV - 1),))
    return pl.pallas_call(
        body, name=name, in_specs=[ANY] * n, out_specs=[ANY] * n, out_shape=[jax.ShapeDtypeStruct(z.shape, z.dtype) for z in zones],
        input_output_aliases={i: i for i in range(n)}, scratch_shapes=[sems, sems],
        compiler_params=pltpu.CompilerParams(has_side_effects=True),
    )(*zones)


def _xfer_wait(send_sems, recv_sems, srcs, lands, rows, after, kind, name):
    n, n_src = len(lands), len(srcs)

    def body(*refs):
        src, land = refs[:n_src], refs[n_src:n_src + n]
        send_sems, recv_sems = refs[n_src + n], refs[n_src + n + 1]
        me = _my_index()
        for k in range(1, N_DEV):
            dev, idx = _peer(k)
            for a in range(n):
                q = _xfer_sem(rows[a], k)
                cp = _remote(_xfer_src(kind, src, land, a, me, idx), land[a].at[idx], send_sems.at[q], recv_sems.at[q], dev)
                cp.wait_send()
                cp.wait_recv()

    both = [*srcs, *lands]
    outs = pl.pallas_call(
        body, name=name, in_specs=[HBM_SPEC] * len(both) + [SEM_SPEC, SEM_SPEC, ANY], out_specs=[HBM_SPEC] * len(both),
        out_shape=[pltpu.HBM(a.shape, a.dtype) for a in both], input_output_aliases={i: i for i in range(len(both))},
        compiler_params=pltpu.CompilerParams(has_side_effects=DATAFLOW),
    )(*both, send_sems, recv_sems, after)
    return list(outs[n_src:])


def _pre(c_pad, conv_pad, w_mod, b_mod_mine):
    n_layers, d, blk = w_mod.shape

    def body(c_ref, conv_ref, wmod_ref, bmod_ref, cact_ref, mod_ref, convall_ref, cact_mine, msh, send_sems, recv_sems):
        me = _my_index()
        c = c_ref[...]
        cact_mine[...] = c * _sigmoid(c)
        cact_ref[me] = cact_mine[...]
        convall_ref[me] = conv_ref[...]
        sends = []
        for k in range(1, N_DEV):
            dev, _ = _peer(k)
            sends.append(_remote(cact_mine, cact_ref.at[me], send_sems.at[0, k - 1], recv_sems.at[0, k - 1], dev))
            sends.append(_remote(conv_ref, convall_ref.at[me], send_sems.at[1, k - 1], recv_sems.at[1, k - 1], dev))
        for cp in sends:
            cp.start()
        for k in range(1, N_DEV):
            dev, idx = _peer(k)
            _remote(cact_mine, cact_ref.at[idx], send_sems.at[0, k - 1], recv_sems.at[0, k - 1], dev).wait_recv()
            _remote(conv_ref, convall_ref.at[idx], send_sems.at[1, k - 1], recv_sems.at[1, k - 1], dev).wait_recv()
        for cp in sends:
            cp.wait_send()
        cact_b = cact_ref[...].reshape(N_DEV * 8, d).astype(BF16)
        for l in range(n_layers):
            m = _dot(cact_b, wmod_ref[l].astype(BF16)) + bmod_ref[l]
            msh[l] = m.reshape(N_DEV, 8, blk)
        mod_ref[me] = msh[:, me]
        sends = []
        for k in range(1, N_DEV):
            dev, idx = _peer(k)
            sends.append(_remote(msh.at[:, idx], mod_ref.at[me], send_sems.at[2, k - 1], recv_sems.at[2, k - 1], dev))
        for cp in sends:
            cp.start()
        for k in range(1, N_DEV):
            dev, idx = _peer(k)
            _remote(msh.at[:, idx], mod_ref.at[idx], send_sems.at[2, k - 1], recv_sems.at[2, k - 1], dev).wait_recv()
        for cp in sends:
            cp.wait_send()

    out_shape = [jax.ShapeDtypeStruct((N_DEV, 8, d), F32), jax.ShapeDtypeStruct((N_DEV, n_layers, 8, blk), F32),
                 jax.ShapeDtypeStruct((N_DEV, *conv_pad.shape), F32)]
    return pl.pallas_call(
        body, name="pre", in_specs=[VMEM_SPEC] * 4, out_specs=[VMEM_SPEC] * 3, out_shape=out_shape,
        scratch_shapes=[pltpu.VMEM((8, d), F32), pltpu.VMEM((n_layers, N_DEV, 8, blk), F32),
                        pltpu.SemaphoreType.DMA((3, N_DEV - 1)), pltpu.SemaphoreType.DMA((3, N_DEV - 1))],
        compiler_params=_cparams(has_side_effects=True),
    )(c_pad, conv_pad, w_mod, b_mod_mine)


def _small(buf, dmod_blocks, cact_all):
    rows = buf.shape[0]
    seg = rows // N_DEV
    _, n_layers, _, blk = dmod_blocks.shape
    d = cact_all.shape[-1]

    def body(buf_ref, dmod_ref, cact_ref, out_ref, gw_ref, rs_recv, red, drecv, send_sems, recv_sems):
        me = _my_index()
        mine = pl.ds(pl.multiple_of(me * seg, 8), seg)
        sends = []
        for k in range(1, N_DEV):
            dev, idx = _peer(k)
            theirs = pl.ds(pl.multiple_of(idx * seg, 8), seg)
            sends.append(_remote(buf_ref.at[theirs], rs_recv.at[k - 1], send_sems.at[0, k - 1], recv_sems.at[0, k - 1], dev))
            sends.append(_remote(dmod_ref.at[idx], drecv.at[me], send_sems.at[1, k - 1], recv_sems.at[1, k - 1], dev))
        for cp in sends:
            cp.start()
        drecv[me] = dmod_ref[me]
        for k in range(1, N_DEV):
            dev, idx = _peer(k)
            _remote(buf_ref.at[mine], rs_recv.at[k - 1], send_sems.at[0, k - 1], recv_sems.at[0, k - 1], dev).wait_recv()
            _remote(dmod_ref.at[idx], drecv.at[idx], send_sems.at[1, k - 1], recv_sems.at[1, k - 1], dev).wait_recv()
        for cp in sends:
            cp.wait_send()
        total = buf_ref[mine, :]
        for k in range(1, N_DEV):
            total = total + rs_recv[k - 1]
        red[...] = total
        out_ref[mine, :] = total
        sends = []
        for k in range(1, N_DEV):
            dev, _ = _peer(k)
            sends.append(_remote(red, out_ref.at[mine], send_sems.at[2, k - 1], recv_sems.at[2, k - 1], dev))
        for cp in sends:
            cp.start()
        cact_b = cact_ref[...].reshape(N_DEV * 8, d).astype(BF16)
        for l in range(n_layers):
            gw_ref[l] = _dot_tn(cact_b, drecv[:, l].reshape(N_DEV * 8, blk).astype(BF16))
        for k in range(1, N_DEV):
            dev, idx = _peer(k)
            theirs = pl.ds(pl.multiple_of(idx * seg, 8), seg)
            _remote(red, out_ref.at[theirs], send_sems.at[2, k - 1], recv_sems.at[2, k - 1], dev).wait_recv()
        for cp in sends:
            cp.wait_send()

    out_shape = [jax.ShapeDtypeStruct(buf.shape, F32), jax.ShapeDtypeStruct((n_layers, d, blk), F32)]
    return pl.pallas_call(
        body, name="small_allreduce", in_specs=[VMEM_SPEC] * 3, out_specs=[VMEM_SPEC] * 2, out_shape=out_shape,
        scratch_shapes=[pltpu.VMEM((N_DEV - 1, seg, 128), F32), pltpu.VMEM((seg, 128), F32),
                        pltpu.VMEM(dmod_blocks.shape, F32),
                        pltpu.SemaphoreType.DMA((3, N_DEV - 1)), pltpu.SemaphoreType.DMA((3, N_DEV - 1))],
        compiler_params=_cparams(has_side_effects=True),
    )(buf, dmod_blocks, cact_all)


def _adamw_math(w, g, m, v):
    m = ADAM_B1 * m + (1.0 - ADAM_B1) * g
    v = ADAM_B2 * v + (1.0 - ADAM_B2) * (g * g)
    m_hat = m / (1.0 - ADAM_B1 ** ADAM_STEP)
    v_hat = v / (1.0 - ADAM_B2 ** ADAM_STEP)
    delta = -ADAM_LR * (m_hat / (jnp.sqrt(v_hat) + ADAM_EPS) + ADAM_WD * w)
    return delta, m, v


def _adamw(parts, w, m, v, name, first=0, earlier=None):
    n_layers = len(parts)
    n_parts, rows, cols = parts[0].shape
    tr = rows
    while tr * cols * 4 > (1 << 20) and tr % 32 == 0:
        tr //= 2
    n_r = rows // tr
    n_earlier = 0 if earlier is None else 4

    def body(*refs):
        p_refs = refs[:n_layers]
        w_ref, m_ref, v_ref = refs[n_layers:n_layers + 3]
        g_out, d_out, m_out, v_out = refs[n_layers + 3 + n_earlier:]
        layer = pl.program_id(0)
        for q in range(n_layers):

            @pl.when(layer == q)
            def _(q=q):
                g = p_refs[q][0].astype(F32)
                for p in range(1, n_parts):
                    g = g + p_refs[q][p].astype(F32)
                delta, m_new, v_new = _adamw_math(w_ref[...], g, m_ref[...], v_ref[...])
                g_out[...] = g
                d_out[...] = delta
                m_out[...] = m_new
                v_out[...] = v_new

    def parts_spec(q):
        return pl.BlockSpec((n_parts, tr, cols), lambda l, r: (0, jnp.where(l == q, r, jnp.where(l < q, 0, n_r - 1)), 0))

    spec = pl.BlockSpec((None, tr, cols), lambda l, r: (first + l, r, 0))
    out = jax.ShapeDtypeStruct(w.shape, F32)
    n_in = n_layers + 3
    return pl.pallas_call(
        body, name=name, grid=(n_layers, n_r),
        in_specs=[parts_spec(q) for q in range(n_layers)] + [spec, spec, spec] + [ANY] * n_earlier,
        out_specs=[spec] * 4, out_shape=[out] * 4, input_output_aliases={n_in + i: i for i in range(n_earlier)},
        compiler_params=_cparams(dimension_semantics=("arbitrary", "arbitrary")),
    )(*parts, w, m, v, *(earlier or ()))


def _adamw_flat(g, w, m, v):
    def body(g_ref, w_ref, m_ref, v_ref, d_out, m_out, v_out):
        delta, m_new, v_new = _adamw_math(w_ref[...], g_ref[...], m_ref[...], v_ref[...])
        d_out[...] = delta
        m_out[...] = m_new
        v_out[...] = v_new

    out = jax.ShapeDtypeStruct(g.shape, F32)
    return pl.pallas_call(body, name="adamw_small", in_specs=[VMEM_SPEC] * 4, out_specs=[VMEM_SPEC] * 3, out_shape=[out] * 3,
                          compiler_params=_cparams())(g, w, m, v)


def _pack(arrays, rows_multiple):
    flat = jnp.concatenate([a.reshape(-1) for a in arrays])
    per = 128 * rows_multiple
    total = -(-flat.shape[0] // per) * per
    return jnp.pad(flat, (0, total - flat.shape[0])).reshape(total // 128, 128)


def _unpack(buf, like):
    flat = buf.reshape(-1)
    out, off = [], 0
    for a in like:
        out.append(flat[off:off + a.size].reshape(a.shape))
        off += a.size
    return out


def kernel(x, c, w_mod, b_mod, g_mix, w_in, gm_ln_g, gm_ln_b, gm_w_s, gm_b_s, w_pa, pool_w, pool_scale, w_pb, conv_w, w_pc, w_o, g_ffn, w_13, w_2, g_final, loss_target, m_w_mod, m_b_mod, m_g_mix, m_w_in, m_gm_ln_g, m_gm_ln_b, m_gm_w_s, m_gm_b_s, m_w_pa, m_pool_w, m_pool_scale, m_w_pb, m_conv_w, m_w_pc, m_w_o, m_g_ffn, m_w_13, m_w_2, m_g_final, v_w_mod, v_b_mod, v_g_mix, v_w_in, v_gm_ln_g, v_gm_ln_b, v_gm_w_s, v_gm_b_s, v_w_pa, v_pool_w, v_pool_scale, v_w_pb, v_conv_w, v_w_pc, v_w_o, v_g_ffn, v_w_13, v_w_2, v_g_final):
    nb, seq, d = x.shape
    n_layers = w_in.shape[0]
    t_all = nb * seq
    blk = w_in.shape[-1]
    me = _my_index()
    conv_shard = conv_w.shape[-1]

    c_pad = jnp.pad(c, ((0, 8 - nb), (0, 0)))
    conv_pad = jnp.pad(conv_w.reshape(n_layers * 3, conv_shard), ((0, 16 - n_layers * 3), (0, 128 - conv_shard)))
    b_mod_mine = lax.dynamic_slice_in_dim(b_mod, me * blk, blk, axis=1).reshape(n_layers, 1, blk)
    cact_all, mod_blocks, conv_all = _pre(c_pad, conv_pad, w_mod, b_mod_mine)
    mod = jnp.transpose(mod_blocks, (1, 2, 0, 3)).reshape(n_layers, 8, N_MOD, d)[:, :nb]
    mod = jnp.pad(mod, ((0, 0), (0, 0), (0, 8 - N_MOD), (0, 0)))
    conv_full = jnp.transpose(conv_all[:, :n_layers * 3, :conv_shard].reshape(N_DEV, n_layers, 3, conv_shard), (1, 2, 0, 3))
    conv_full = jnp.pad(conv_full.reshape(n_layers, 3, N_DEV * conv_shard), ((0, 0), (0, 5), (0, 0)))
    bexp = jnp.repeat(jnp.transpose(gm_b_s, (0, 2, 1)), HEAD_DIM, axis=2)

    mixer_w, ffn_w = [w_in, w_o, w_pa, w_pb, w_pc], [w_13, w_2]
    n_mix = len(mixer_w)

    def send_weights(l, ws, name, after):
        return _xfer_start([], _gather_zones(ws, l), GATHER, name, after)

    def send_grads(partials, name, after):
        return _xfer_start(partials, _scatter_zones(partials), SCATTER, name, after)

    def arrived(flight, after, kind, name):
        send_sems, recv_sems, srcs, zones, _ = flight
        return _xfer_wait(send_sems, recv_sems, srcs, zones, list(range(len(zones))), after, kind, name)

    row = lambda a, l: a[l].reshape(1, -1)
    xs = x.reshape(t_all, d)
    saved, weights = [], []
    after = mod
    for l in range(n_layers):
        if l == 0:
            win_g, wo_g, wpa_g, wpb_g, wpc_g = _gather_two_level(_gather_zones(mixer_w, 0), "gather_mixer_0")
        else:
            win_g, wo_g, wpa_g, wpb_g, wpc_g = arrived(flight, after, GATHER, f"gather_wait_mixer_{l}")
        flight = send_weights(l, ffn_w, f"gather_start_ffn_{l}", win_g)
        wo_g = wo_g.reshape(d, d)
        h, z, ycat, ocat, mrg, mo, xmid = _mixer_fwd(
            xs, mod[l] + flight[4][0, 0], row(g_mix, l), row(gm_ln_g, l), row(gm_ln_b, l), gm_w_s[l], bexp[l], pool_w[l],
            row(pool_scale, l), conv_full[l], win_g, wpa_g, wpb_g, wpc_g, wo_g, l, seq)
        w13_g, w2_g = arrived(flight, xmid, GATHER, f"gather_wait_ffn_{l}")
        tok = jnp.zeros((), F32)
        if l + 1 < n_layers:
            flight = send_weights(l + 1, mixer_w, f"gather_start_mixer_{l + 1}", w13_g)
            tok = flight[4][0, 0]
        w2_g = w2_g.reshape(N_DEV * w_2.shape[1], d)
        h2, ab, hid, f, xo = _ffn_fwd(xmid, mod[l] + tok, row(g_ffn, l), w13_g, w2_g, l, seq)
        saved.append((xs, h, z, ycat, ocat, mrg, mo, xmid, h2, ab, hid, f))
        weights.append((win_g, wpa_g, wpb_g, wpc_g, wo_g, w13_g, w2_g))
        xs = after = xo

    dx, loss_blk, dgf_blk = _final_loss(xs, g_final.reshape(1, d), loss_target.reshape(t_all, d), seq)
    loss = lax.psum(loss_blk[0, 0], ("x", "y", "c"))

    ffn_flight = [None] * n_layers
    mix_flight = [None] * n_layers
    small_grads = [None] * n_layers
    dmods = [None] * n_layers
    tok = jnp.zeros((), F32)
    for l in reversed(range(n_layers)):
        x_in, h, z, ycat, ocat, mrg, mo, xmid, h2, ab, hid, f = saved[l]
        win_g, wpa_g, wpb_g, wpc_g, wo_g, w13_g, w2_g = weights[l]
        dxm, df, dab, dmod2, dg_ffn = _ffn_bwd(dx, xmid, ab, f, mod[l] + tok, row(g_ffn, l), w13_g, w2_g, l, seq)
        ffn_flight[l] = send_grads(_ffn_wgrads(h2, dab, hid, df, l), f"grads_start_ffn_{l}", dxm)
        dx, dz, dycat, dmo, dmod1, dg_mix, sm, dws, dssum, dpw = _mixer_bwd(
            dxm, x_in, z, ycat, mo, mod[l] + ffn_flight[l][4][0, 0], row(g_mix, l), row(gm_ln_g, l), row(gm_ln_b, l),
            gm_w_s[l], bexp[l], pool_w[l], row(pool_scale, l), conv_full[l], win_g, wpa_g, wpb_g, wpc_g, wo_g, l, seq)
        if l > 0:
            mix_flight[l] = send_grads(_mixer_wgrads(h, dz, mrg, dmo, ocat, dycat, l), f"grads_start_mixer_{l}", dx)
            tok = mix_flight[l][4][0, 0]
        dmod = jnp.concatenate([dmod1[:, 0:3], dmod2[:, 0:3]], axis=1).reshape(nb, N_MOD * d)
        dmods[l] = dmod
        db_s = jnp.transpose(jnp.sum(dssum.reshape(CHUNK, HEADS, HEAD_DIM), axis=2))
        small_grads[l] = [jnp.sum(dmod, axis=0), dg_mix[0], sm[0], sm[1], dws, db_s, dpw, sm[2], sm[3:6], dg_ffn[0]]
    grad_x = dx.reshape(nb, seq, d)

    names = ["b_mod", "g_mix", "ln_g", "ln_b", "w_s", "b_s", "pool_w", "pool_scale", "conv_w", "g_ffn"]
    per_name = [jnp.stack([small_grads[l][n] for l in range(n_layers)]) for n in range(len(names))] + [dgf_blk[0]]
    buf = _pack(per_name, 8 * N_DEV)
    dmod_all = jnp.pad(jnp.stack(dmods), ((0, 0), (0, 8 - nb), (0, 0)))
    dmod_blocks = jnp.transpose(dmod_all.reshape(n_layers, 8, N_DEV, blk), (2, 0, 1, 3))
    red, grad_w_mod = _small(buf, dmod_blocks, cact_all)
    (g_b_mod, g_g_mix, g_ln_g, g_ln_b, g_w_s, g_b_s, g_pool_w, g_pool_scale, g_conv_full, g_g_ffn, g_g_final) = _unpack(red, per_name)
    g_conv = lax.dynamic_slice_in_dim(g_conv_full, me * conv_shard, conv_shard, axis=2)

    mix_flight[0] = send_grads(_mixer_wgrads(h, dz, mrg, dmo, ocat, dycat, 0), "grads_start_mixer_0", red)
    results = {}
    results["w_mod"] = _adamw([grad_w_mod[l][None] for l in range(n_layers)], w_mod, m_w_mod, v_w_mod, "adamw_w_mod")

    small_w =[b_mod, g_mix, gm_ln_g, gm_ln_b, gm_w_s, gm_b_s, pool_w, pool_scale, conv_w, g_ffn, g_final]
    small_m = [m_b_mod, m_g_mix, m_gm_ln_g, m_gm_ln_b, m_gm_w_s, m_gm_b_s, m_pool_w, m_pool_scale, m_conv_w, m_g_ffn, m_g_final]
    small_v = [v_b_mod, v_g_mix, v_gm_ln_g, v_gm_ln_b, v_gm_w_s, v_gm_b_s, v_pool_w, v_pool_scale, v_conv_w, v_g_ffn, v_g_final]
    small_g = [g_b_mod, g_g_mix, g_ln_g, g_ln_b, g_w_s, g_b_s, g_pool_w, g_pool_scale, g_conv, g_g_ffn, g_g_final]
    small_names = ["b_mod", "g_mix", "gm_ln_g", "gm_ln_b", "gm_w_s", "gm_b_s", "pool_w", "pool_scale", "conv_w", "g_ffn", "g_final"]
    sd_buf, sm_new, sv_new = _adamw_flat(_pack(small_g, 8), _pack(small_w, 8), _pack(small_m, 8), _pack(small_v, 8))
    sd, sm_new, sv_new = _unpack(sd_buf, small_w), _unpack(sm_new, small_w), _unpack(sv_new, small_w)
    for n, nm in enumerate(small_names):
        results[nm] = (small_g[n], sd[n], sm_new[n], sv_new[n])

    layers = list(range(n_layers))
    done = (results["w_mod"][1][0, 0, 0] + sd_buf[0, 0]).reshape(1)
    ffn_recv = [arrived(ffn_flight[l], done, SCATTER, f"grads_wait_ffn_{l}") for l in reversed(layers)][::-1]
    mix_recv = [None] + [arrived(mix_flight[l], done, SCATTER, f"grads_wait_mixer_{l}") for l in reversed(layers[1:])][::-1]
    swap = lambda a: jnp.swapaxes(a, 1, 2)
    results["w_13"] = [swap(r) for r in _adamw([ffn_recv[l][0] for l in layers], swap(w_13), swap(m_w_13), swap(v_w_13), "adamw_w_13")]
    results["w_2"] = _adamw([ffn_recv[l][1] for l in layers], w_2, m_w_2, v_w_2, "adamw_w_2")
    mix_m = [m_w_in, m_w_o, m_w_pa, m_w_pb, m_w_pc]
    mix_v = [v_w_in, v_w_o, v_w_pa, v_w_pb, v_w_pc]
    mix_names = ["w_in", "w_o", "w_pa", "w_pb", "w_pc"]
    early = [_adamw([mix_recv[l][a] for l in layers[1:]], mixer_w[a], mix_m[a], mix_v[a], f"adamw_{mix_names[a]}_later", first=1)
             for a in range(n_mix)]
    done = (results["w_13"][1][0, 0, 0] + results["w_2"][1][0, 0, 0] + sum(e[1][1, 0, 0] for e in early)).reshape(1)
    mix_recv[0] = arrived(mix_flight[0], done, SCATTER, "grads_wait_mixer_0")
    for a in range(n_mix):
        results[mix_names[a]] = _adamw([mix_recv[0][a]], mixer_w[a], mix_m[a], mix_v[a], f"adamw_{mix_names[a]}_first",
                                       earlier=early[a])

    order = ["w_mod", "b_mod", "g_mix", "w_in", "gm_ln_g", "gm_ln_b", "gm_w_s", "gm_b_s", "w_pa", "pool_w", "pool_scale",
             "w_pb", "conv_w", "w_pc", "w_o", "g_ffn", "w_13", "w_2", "g_final"]
    return (loss, grad_x, *[results[nm][0] for nm in order], *[results[nm][1] for nm in order],
            *[results[nm][2] for nm in order], *[results[nm][3] for nm in order])
```

```python
import functools

import jax
import jax.numpy as jnp
from jax import lax
from jax.experimental import pallas as pl
from jax.experimental.pallas import tpu as pltpu

F32 = jnp.float32
BF16 = jnp.bfloat16
MESH_ID = pl.DeviceIdType.MESH

N_DEV = 8
EPS = 1e-6
CHUNK = 128
HEADS = 4
HEAD_DIM = 128
BR_W = 512
POOL_WINDOWS = (2, 4, 8, 16)
POOL_HALO = 16
CONV_HALO = 8
N_MOD = 6
ADAM_LR = 0.001
ADAM_B1 = 0.9
ADAM_B2 = 0.999
ADAM_EPS = 1e-08
ADAM_WD = 0.01
ADAM_STEP = 10

TOKEN_TILE = 256
WGRAD_TOKENS = 2048
VMEM_LIMIT = 56 * 1024 * 1024
GELU_K = 0.7978845608028654
GELU_C = 0.044715

ANY = pl.BlockSpec(memory_space=pl.ANY)
VMEM_SPEC = pl.BlockSpec(memory_space=pltpu.VMEM)


def _cparams(**kw):
    return pltpu.CompilerParams(vmem_limit_bytes=VMEM_LIMIT, **kw)


def _dot(a, b):
    return jnp.dot(a, b, preferred_element_type=F32)


def _dot_nt(a, b):
    return lax.dot_general(a, b, (((1,), (1,)), ((), ())), preferred_element_type=F32)


def _dot_tn(a, b):
    return lax.dot_general(a, b, (((0,), (0,)), ((), ())), preferred_element_type=F32)


def _colsum(a):
    return jnp.sum(a, axis=0, keepdims=True)


def _sigmoid(x):
    return 0.5 * jnp.tanh(0.5 * x) + 0.5


def _gelu(x):
    x2 = x * x
    t = jnp.tanh(x * (GELU_K + (GELU_K * GELU_C) * x2))
    return (0.5 * x) * (1.0 + t), t, x2


def _gelu_grad(x, t, x2):
    one_t = 1.0 + t
    return 0.5 * one_t + (0.5 * x) * (one_t * (1.0 - t)) * (GELU_K + (3.0 * GELU_K * GELU_C) * x2)


def _tril_mask():
    r = lax.broadcasted_iota(jnp.int32, (CHUNK, CHUNK), 0)
    c = lax.broadcasted_iota(jnp.int32, (CHUNK, CHUNK), 1)
    return (r >= c).astype(F32)


def _my_index():
    return 4 * lax.axis_index("x") + 2 * lax.axis_index("y") + lax.axis_index("c")


def _peer(k):
    x, y, c = lax.axis_index("x"), lax.axis_index("y"), lax.axis_index("c")
    px = 1 - x if (k >> 2) & 1 else x
    py = 1 - y if (k >> 1) & 1 else y
    pc = 1 - c if k & 1 else c
    return (px, py, pc), 4 * px + 2 * py + pc


def _load_weights(step, pairs, sem):
    @pl.when(step == 0)
    def _():
        copies = [pltpu.make_async_copy(src, dst, sem.at[n]) for n, (src, dst) in enumerate(pairs)]
        for cp in copies:
            cp.start()
        for cp in copies:
            cp.wait()


def _wp_pairs(wp_hbm, wp_v):
    return [(wp_hbm.at[j], wp_v.at[:, pl.ds(HEAD_DIM * j, HEAD_DIM)]) for j in range(N_DEV)]


def _layer_spec(a, layer):
    return pl.BlockSpec((None, *a.shape[1:]), lambda i: (layer,) + (0,) * (a.ndim - 1))


def _mod_spec(mod, layer, n_seq_tiles):
    return pl.BlockSpec((None, 1, *mod.shape[2:]), lambda i: (layer, i // n_seq_tiles, 0, 0))


def _rms_mod(x, g, shift, scale):
    rstd = lax.rsqrt(jnp.mean(x * x, axis=-1, keepdims=True) + EPS)
    xn = x * rstd
    return xn, rstd, (xn * g) * (1.0 + scale) + shift


def _rms_mod_bwd(dh, xn, rstd, g, scale):
    dxn = dh * (1.0 + scale) * g
    dx = rstd * (dxn - xn * jnp.mean(dxn * xn, axis=-1, keepdims=True))
    return dx, _colsum(dh), _colsum(dh * (xn * g)), _colsum(dh * (1.0 + scale) * xn)


def _gmlp_s(vn_b, wmask_b, bexp, s_scr, tm):
    for ch in range(tm // CHUNK):
        rows = slice(ch * CHUNK, (ch + 1) * CHUNK)
        for hh in range(HEADS):
            cols = slice(hh * HEAD_DIM, (hh + 1) * HEAD_DIM)
            s_scr[rows, cols] = _dot(wmask_b[hh], vn_b[rows, cols]) + bexp[:, cols]
    return s_scr[...]


def _inv_count(pos, win):
    return 1.0 / jnp.minimum(pos + 1, win).astype(F32)


def _window_sums(ext, tm, trailing):
    n = tm + POOL_HALO
    sums = []
    for g, win in enumerate(POOL_WINDOWS):
        s = ext[:, g * HEAD_DIM:(g + 1) * HEAD_DIM]
        span = 1
        while span < win:
            s = s + pltpu.roll(s, span if trailing else n - span, 0)
            span *= 2
        sums.append(s[POOL_HALO:POOL_HALO + tm] if trailing else s[0:tm])
    return sums


def _pool_p(xb, xbext, pos, tm):
    sums = _window_sums(xbext, tm, True)
    return [sums[g] * _inv_count(pos, win) - xb[:, g * HEAD_DIM:(g + 1) * HEAD_DIM] for g, win in enumerate(POOL_WINDOWS)]


def _mixer_fwd(x, mod, g_mix, ln_g, ln_b, w_s, bexp, pool_w, pool_scale, conv_w, win, wpa, wpb, wpc, wo, dep, layer, seq):
    t_all, d = x.shape
    tm = min(TOKEN_TILE, seq)
    n_seq_tiles = seq // tm
    blk = win.shape[-1]

    def body(x_ref, mod_ref, gmix_ref, lng_ref, lnb_ref, ws_ref, bexp_ref, pw_ref, ps_ref, cw_ref,
             win_hbm, wpa_hbm, wpb_hbm, wpc_hbm, wo_hbm, dep_ref,
             h_ref, z_ref, ycat_ref, ocat_ref, mrg_ref, mo_ref, xmid_ref,
             win_v, wpa_v, wpb_v, wpc_v, wo_v, xbext, zcext, s_scr, sem):
        i = pl.program_id(0)
        pairs = [(win_hbm, win_v), (wo_hbm, wo_v)]
        pairs += _wp_pairs(wpa_hbm, wpa_v) + _wp_pairs(wpb_hbm, wpb_v) + _wp_pairs(wpc_hbm, wpc_v)
        _load_weights(i, pairs, sem)
        tile_in_seq = i % n_seq_tiles

        @pl.when(tile_in_seq == 0)
        def _():
            xbext[0:POOL_HALO, :] = jnp.zeros((POOL_HALO, BR_W), F32)
            zcext[0:CONV_HALO, :] = jnp.zeros((CONV_HALO, BR_W), F32)

        x_t = x_ref[...]
        shift1, scale1, gate1 = mod_ref[0, 0:1, :], mod_ref[0, 1:2, :], mod_ref[0, 2:3, :]
        _, _, h = _rms_mod(x_t, gmix_ref[...], shift1, scale1)
        hb = h.astype(BF16)
        h_ref[...] = hb
        def project(j):
            zj = _dot(hb, win_v[j])
            z_ref[:, j * blk:(j + 1) * blk] = zj.astype(BF16)
            return zj

        z0, z1 = project(0), project(1)
        u = z0[:, 0:BR_W]
        v = jnp.concatenate([z0[:, BR_W:blk], z1[:, 0:2 * BR_W - blk]], axis=1)
        xb = z1[:, 2 * BR_W - blk:blk]

        gu = _gelu(u)[0]
        gv = _gelu(v)[0]
        mu = jnp.mean(gv, axis=-1, keepdims=True)
        cen = gv - mu
        rs = lax.rsqrt(jnp.mean(cen * cen, axis=-1, keepdims=True) + EPS)
        vn = (cen * rs) * lng_ref[...] + lnb_ref[...]
        mask = _tril_mask()
        wmask_b = [(ws_ref[hh] * mask).astype(BF16) for hh in range(HEADS)]
        s = _gmlp_s(vn.astype(BF16), wmask_b, bexp_ref[...], s_scr, tm)
        oa = (gu * s).astype(BF16)
        ya = _dot(oa, wpa_v[...])

        xbext[POOL_HALO:POOL_HALO + tm, :] = xb
        pos = tile_in_seq * tm + lax.broadcasted_iota(jnp.int32, (tm, 1), 0)
        ps = _pool_p(xb, xbext, pos, tm)
        qs = [_dot(ps[g].astype(BF16), pw_ref[g].astype(BF16)) for g in range(len(POOL_WINDOWS))]
        ob = (jnp.concatenate(qs, axis=1) * ps_ref[...]).astype(BF16)
        yb = _dot(ob, wpb_v[...])
        xbext[0:POOL_HALO, :] = xbext[tm:tm + POOL_HALO, :]

        z2, z3 = project(2), project(3)
        bg = z2[:, 0:BR_W]
        cg = jnp.concatenate([z2[:, BR_W:blk], z3[:, 0:2 * BR_W - blk]], axis=1)
        hc = z3[:, 2 * BR_W - blk:blk]
        zz = cg * hc
        zcext[CONV_HALO:CONV_HALO + tm, :] = zz
        yconv = (cw_ref[0:1, :] * zcext[pl.ds(CONV_HALO - 2, tm), :] + cw_ref[1:2, :] * zcext[pl.ds(CONV_HALO - 1, tm), :]
                 + cw_ref[2:3, :] * zz)
        oc = (bg * yconv).astype(BF16)
        yc = _dot(oc, wpc_v[...])
        zcext[0:CONV_HALO, :] = zcext[tm:tm + CONV_HALO, :]

        ocat_ref[:, 0:BR_W] = oa
        ocat_ref[:, BR_W:2 * BR_W] = ob
        ocat_ref[:, 2 * BR_W:3 * BR_W] = oc
        ycat_ref[:, 0:d] = ya.astype(BF16)
        ycat_ref[:, d:2 * d] = yb.astype(BF16)
        ycat_ref[:, 2 * d:3 * d] = yc.astype(BF16)

        ys = (ya, yb, yc)
        zg = jnp.concatenate([project(j).astype(BF16) for j in range(4, N_DEV)], axis=1)
        mb = _sigmoid(zg[:, 0:d]) * ys[0].astype(BF16)
        for n in range(1, 3):
            mb = mb + _sigmoid(zg[:, n * d:(n + 1) * d]) * ys[n].astype(BF16)
        mrg_ref[...] = mb
        mo = _dot(mb, wo_v[...])
        mo_ref[...] = mo.astype(BF16)
        xmid_ref[...] = x_t + gate1 * mo

    tok = lambda cols: pl.BlockSpec((tm, cols), lambda i: (i, 0))
    in_specs = [
        tok(d),
        _mod_spec(mod, layer, n_seq_tiles),
        *[_layer_spec(a, layer) for a in (g_mix, ln_g, ln_b, w_s, bexp, pool_w, pool_scale, conv_w)],
        ANY, ANY, ANY, ANY, ANY, ANY,
    ]
    out_shape = [
        jax.ShapeDtypeStruct((t_all, d), BF16),
        jax.ShapeDtypeStruct((t_all, N_DEV * blk), BF16),
        jax.ShapeDtypeStruct((t_all, 3 * d), BF16),
        jax.ShapeDtypeStruct((t_all, 3 * BR_W), BF16),
        jax.ShapeDtypeStruct((t_all, d), BF16),
        jax.ShapeDtypeStruct((t_all, d), BF16),
        jax.ShapeDtypeStruct((t_all, d), F32),
    ]
    out_specs = [tok(d), tok(N_DEV * blk), tok(3 * d), tok(3 * BR_W), tok(d), tok(d), tok(d)]
    scratch = [
        pltpu.VMEM((N_DEV, d, blk), BF16), pltpu.VMEM((BR_W, d), BF16), pltpu.VMEM((BR_W, d), BF16),
        pltpu.VMEM((BR_W, d), BF16), pltpu.VMEM((d, d), BF16),
        pltpu.VMEM((tm + POOL_HALO, BR_W), F32), pltpu.VMEM((tm + CONV_HALO, BR_W), F32), pltpu.VMEM((tm, BR_W), F32),
        pltpu.SemaphoreType.DMA((2 + 3 * N_DEV,)),
    ]
    return pl.pallas_call(
        body, name=f"mixer_fwd_{layer}", grid=(t_all // tm,), in_specs=in_specs, out_specs=out_specs, out_shape=out_shape,
        scratch_shapes=scratch, compiler_params=_cparams(dimension_semantics=("arbitrary",)),
    )(x, mod, g_mix, ln_g, ln_b, w_s, bexp, pool_w, pool_scale, conv_w, win, wpa, wpb, wpc, wo, dep)


def _ffn_fwd(x, mod, g_ffn, w13, w2, dep, layer, seq):
    t_all, d = x.shape
    tm = min(TOKEN_TILE, seq)
    n_seq_tiles = seq // tm
    fb = w13.shape[-1]
    n_hid = N_DEV // 2

    def body(x_ref, mod_ref, g_ref, w13_hbm, w2_hbm, dep_ref, h_ref, ab_ref, hid_ref, f_ref, xo_ref, w13_v, w2_v, sem):
        i = pl.program_id(0)
        _load_weights(i, [(w13_hbm, w13_v), (w2_hbm, w2_v)], sem)
        x_t = x_ref[...]
        shift2, scale2, gate2 = mod_ref[0, 3:4, :], mod_ref[0, 4:5, :], mod_ref[0, 5:6, :]
        _, _, h = _rms_mod(x_t, g_ref[...], shift2, scale2)
        hb = h.astype(BF16)
        h_ref[...] = hb
        f = jnp.zeros((tm, d), F32)
        for k in range(n_hid):
            a = _dot(hb, w13_v[k])
            b = _dot(hb, w13_v[n_hid + k])
            a, b = a.astype(BF16), b.astype(BF16)
            ab_ref[k] = a
            ab_ref[n_hid + k] = b
            hid = (a * _sigmoid(a)) * b
            hid_ref[k] = hid
            f = f + _dot(hid, w2_v[k * fb:(k + 1) * fb, :])
        f_ref[...] = f.astype(BF16)
        xo_ref[...] = x_t + gate2 * f

    tok = lambda cols: pl.BlockSpec((tm, cols), lambda i: (i, 0))
    blk3 = lambda n: pl.BlockSpec((n, tm, fb), lambda i: (0, i, 0))
    in_specs = [tok(d), _mod_spec(mod, layer, n_seq_tiles), _layer_spec(g_ffn, layer), ANY, ANY, ANY]
    out_shape = [
        jax.ShapeDtypeStruct((t_all, d), BF16),
        jax.ShapeDtypeStruct((N_DEV, t_all, fb), BF16),
        jax.ShapeDtypeStruct((n_hid, t_all, fb), BF16),
        jax.ShapeDtypeStruct((t_all, d), BF16),
        jax.ShapeDtypeStruct((t_all, d), F32),
    ]
    out_specs = [tok(d), blk3(N_DEV), blk3(n_hid), tok(d), tok(d)]
    scratch = [pltpu.VMEM((N_DEV, d, fb), BF16), pltpu.VMEM((n_hid * fb, d), BF16), pltpu.SemaphoreType.DMA((2,))]
    return pl.pallas_call(
        body, name=f"ffn_fwd_{layer}", grid=(t_all // tm,), in_specs=in_specs, out_specs=out_specs, out_shape=out_shape,
        scratch_shapes=scratch, compiler_params=_cparams(dimension_semantics=("arbitrary",)),
    )(x, mod, g_ffn, w13, w2, dep)


def _final_loss(x, g_final, target, seq):
    t_all, d = x.shape
    tm = min(TOKEN_TILE, seq)

    def body(x_ref, g_ref, t_ref, dx_ref, loss_ref, dg_ref):
        i = pl.program_id(0)

        @pl.when(i == 0)
        def _():
            loss_ref[...] = jnp.zeros(loss_ref.shape, F32)
            dg_ref[...] = jnp.zeros(dg_ref.shape, F32)

        x_t = x_ref[...]
        g = g_ref[...]
        rstd = lax.rsqrt(jnp.mean(x_t * x_t, axis=-1, keepdims=True) + EPS)
        xn = x_t * rstd
        err = xn * g - t_ref[...]
        loss_ref[0:1, :] += _colsum(err * err) * (0.5 / d)
        dy = err * (1.0 / d)
        dg_ref[0:1, :] += _colsum(dy * xn)
        dxn = dy * g
        dx_ref[...] = rstd * (dxn - xn * jnp.mean(dxn * xn, axis=-1, keepdims=True))

        @pl.when(i == pl.num_programs(0) - 1)
        def _():
            loss_ref[...] = jnp.broadcast_to(jnp.sum(loss_ref[0:1, :], axis=1, keepdims=True), loss_ref.shape)

    tok = pl.BlockSpec((tm, d), lambda i: (i, 0))
    acc = pl.BlockSpec((8, d), lambda i: (0, 0))
    return pl.pallas_call(
        body, name="final_loss", grid=(t_all // tm,),
        in_specs=[tok, pl.BlockSpec((1, d), lambda i: (0, 0)), tok], out_specs=[tok, acc, acc],
        out_shape=[jax.ShapeDtypeStruct((t_all, d), F32), jax.ShapeDtypeStruct((8, d), F32), jax.ShapeDtypeStruct((8, d), F32)],
        compiler_params=_cparams(dimension_semantics=("arbitrary",)),
    )(x, g_final, target)


def _ffn_bwd(dxo, xmid, ab, f, mod, g_ffn, w13, w2, dep, layer, seq):
    t_all, d = xmid.shape
    tm = min(TOKEN_TILE, seq)
    n_seq_tiles = seq // tm
    fb = w13.shape[-1]
    n_hid = N_DEV // 2

    def body(dxo_ref, x_ref, ab_ref, f_ref, mod_ref, g_ref, w13_hbm, w2_hbm, dep_ref,
             dx_ref, df_ref, dab_ref, dmod_ref, dg_ref, w13_v, w2_v, sem):
        i = pl.program_id(0)
        _load_weights(i, [(w13_hbm, w13_v), (w2_hbm, w2_v)], sem)

        @pl.when(i == 0)
        def _():
            dg_ref[...] = jnp.zeros(dg_ref.shape, F32)

        @pl.when(i % n_seq_tiles == 0)
        def _():
            dmod_ref[...] = jnp.zeros(dmod_ref.shape, F32)

        scale2, gate2 = mod_ref[0, 4:5, :], mod_ref[0, 5:6, :]
        g = g_ref[...]
        x_t = x_ref[...]
        rstd = lax.rsqrt(jnp.mean(x_t * x_t, axis=-1, keepdims=True) + EPS)
        xn = x_t * rstd
        dxo_t = dxo_ref[...]
        dmod_ref[0, 2:3, :] += _colsum(dxo_t * f_ref[...].astype(F32))
        dfb = (dxo_t * gate2).astype(BF16)
        df_ref[...] = dfb
        dh = jnp.zeros((tm, d), F32)
        for k in range(n_hid):
            dhid = _dot_nt(dfb, w2_v[k * fb:(k + 1) * fb, :]).astype(BF16)
            a = ab_ref[k]
            b = ab_ref[n_hid + k]
            sg = _sigmoid(a)
            da = dhid * b * (sg * (1.0 + a * (1.0 - sg)))
            db = dhid * (a * sg)
            dab_ref[k] = da
            dab_ref[n_hid + k] = db
            dh = dh + _dot_nt(da, w13_v[k]) + _dot_nt(db, w13_v[n_hid + k])
        dx, dshift, dscale, dg = _rms_mod_bwd(dh, xn, rstd, g, scale2)
        dmod_ref[0, 0:1, :] += dshift
        dmod_ref[0, 1:2, :] += dscale
        dg_ref[0:1, :] += dg
        dx_ref[...] = dxo_t + dx

    tok = lambda cols: pl.BlockSpec((tm, cols), lambda i: (i, 0))
    blk3 = lambda n: pl.BlockSpec((n, tm, fb), lambda i: (0, i, 0))
    modspec = pl.BlockSpec((1, 8, d), lambda i: (i // n_seq_tiles, 0, 0))
    in_specs = [tok(d), tok(d), blk3(N_DEV), tok(d), _mod_spec(mod, layer, n_seq_tiles), _layer_spec(g_ffn, layer), ANY, ANY, ANY]
    out_shape = [
        jax.ShapeDtypeStruct((t_all, d), F32), jax.ShapeDtypeStruct((t_all, d), BF16),
        jax.ShapeDtypeStruct((N_DEV, t_all, fb), BF16), jax.ShapeDtypeStruct(mod.shape[1:], F32),
        jax.ShapeDtypeStruct((8, d), F32),
    ]
    out_specs = [tok(d), tok(d), blk3(N_DEV), modspec, pl.BlockSpec((8, d), lambda i: (0, 0))]
    scratch = [pltpu.VMEM((N_DEV, d, fb), BF16), pltpu.VMEM((n_hid * fb, d), BF16), pltpu.SemaphoreType.DMA((2,))]
    return pl.pallas_call(
        body, name=f"ffn_bwd_{layer}", grid=(t_all // tm,), in_specs=in_specs, out_specs=out_specs, out_shape=out_shape,
        scratch_shapes=scratch, compiler_params=_cparams(dimension_semantics=("arbitrary",)),
    )(dxo, xmid, ab, f, mod, g_ffn, w13, w2, dep)


def _mixer_bwd(dxm, x, z, ycat, mo, mod, g_mix, ln_g, ln_b, w_s, bexp, pool_w, pool_scale, conv_w,
               win, wpa, wpb, wpc, wo, dep, layer, seq):
    t_all, d = x.shape
    tm = min(TOKEN_TILE, seq)
    n_seq_tiles = seq // tm
    blk = win.shape[-1]
    n_win = len(POOL_WINDOWS)

    def tile_of(i):
        return (i // n_seq_tiles) * n_seq_tiles + (n_seq_tiles - 1 - i % n_seq_tiles)

    def halo_row_block(i):
        return jnp.maximum(tile_of(i) * (tm // POOL_HALO) - 1, 0)

    def body(dxm_ref, x_ref, z_ref, zpb_ref, zpc_ref, ycat_ref, mo_ref, mod_ref, gmix_ref, lng_ref, lnb_ref, ws_ref,
             bexp_ref, pw_ref, ps_ref, cw_ref, win_hbm, wpa_hbm, wpb_hbm, wpc_hbm, wo_hbm, dep_ref,
             dx_ref, dz_ref, dycat_ref, dmo_ref, dmod_ref, dg_ref, sm_ref, dws_ref, dssum_ref, dpw_ref,
             win_v, wpa_v, wpb_v, wpc_v, wo_v, xbext, zzext, rext, dyext, s_scr, dvn_scr, sem):
        i = pl.program_id(0)
        pairs = [(win_hbm.at[j], win_v.at[:, pl.ds(blk * j, blk)]) for j in range(N_DEV)] + [(wo_hbm, wo_v)]
        pairs += _wp_pairs(wpa_hbm, wpa_v) + _wp_pairs(wpb_hbm, wpb_v) + _wp_pairs(wpc_hbm, wpc_v)
        _load_weights(i, pairs, sem)
        tile_in_seq = n_seq_tiles - 1 - i % n_seq_tiles
        first_of_seq = tile_in_seq == 0

        @pl.when(i == 0)
        def _():
            for r in (dg_ref, sm_ref, dws_ref, dssum_ref, dpw_ref):
                r[...] = jnp.zeros(r.shape, F32)

        @pl.when(i % n_seq_tiles == 0)
        def _():
            dmod_ref[...] = jnp.zeros(dmod_ref.shape, F32)
            rext[tm:tm + POOL_HALO, :] = jnp.zeros((POOL_HALO, BR_W), F32)
            dyext[tm:tm + CONV_HALO, :] = jnp.zeros((CONV_HALO, BR_W), F32)

        shift1, scale1, gate1 = mod_ref[0, 0:1, :], mod_ref[0, 1:2, :], mod_ref[0, 2:3, :]
        g = gmix_ref[...]
        x_t = x_ref[...]
        rstd = lax.rsqrt(jnp.mean(x_t * x_t, axis=-1, keepdims=True) + EPS)
        xn = x_t * rstd
        dxm_t = dxm_ref[...]
        dmod_ref[0, 2:3, :] += _colsum(dxm_t * mo_ref[...].astype(F32))
        dmo = (dxm_t * gate1).astype(BF16)
        dmo_ref[...] = dmo
        dmerged = _dot_nt(dmo, wo_v[...])

        dh_parts = []

        def emit_dz(lo, hi, value):
            vb = value.astype(BF16)
            dz_ref[:, lo:hi] = vb
            dh_parts.append(_dot_nt(vb, win_v[:, lo:hi]))

        dys = []
        dmerged = dmerged.astype(BF16)
        for n in range(3):
            gt = _sigmoid(z_ref[:, 3 * d + n * d:3 * d + (n + 1) * d])
            dyn = dmerged * gt
            emit_dz(3 * d + n * d, 3 * d + (n + 1) * d, dyn * ycat_ref[:, n * d:(n + 1) * d] * (1.0 - gt))
            dycat_ref[:, n * d:(n + 1) * d] = dyn
            dys.append(dyn)

        doa = _dot_nt(dys[0], wpa_v[...])
        u = z_ref[:, 0:BR_W].astype(F32)
        v = z_ref[:, BR_W:2 * BR_W].astype(F32)
        gu, tu, u2 = _gelu(u)
        gv, tv, v2 = _gelu(v)
        mu = jnp.mean(gv, axis=-1, keepdims=True)
        cen = gv - mu
        rs = lax.rsqrt(jnp.mean(cen * cen, axis=-1, keepdims=True) + EPS)
        vhat = cen * rs
        lng = lng_ref[...]
        vn_b = (vhat * lng + lnb_ref[...]).astype(BF16)
        mask = _tril_mask()
        wmask = [ws_ref[hh] * mask for hh in range(HEADS)]
        s = _gmlp_s(vn_b, [w.astype(BF16) for w in wmask], bexp_ref[...], s_scr, tm)
        du = (doa * s) * _gelu_grad(u, tu, u2)
        ds = doa * gu
        ds_b = ds.astype(BF16)
        dssum = jnp.zeros((CHUNK, BR_W), F32)
        for ch in range(tm // CHUNK):
            rows = slice(ch * CHUNK, (ch + 1) * CHUNK)
            dssum = dssum + ds[rows, :]
            for hh in range(HEADS):
                cols = slice(hh * HEAD_DIM, (hh + 1) * HEAD_DIM)
                dvn_scr[rows, cols] = _dot_tn(wmask[hh].astype(BF16), ds_b[rows, cols])
                dws_ref[hh] += _dot_nt(ds_b[rows, cols], vn_b[rows, cols]) * mask
        dssum_ref[...] += dssum
        dvn = dvn_scr[...]
        sm_ref[0:1, :] += _colsum(dvn * vhat)
        sm_ref[1:2, :] += _colsum(dvn)
        dvhat = dvn * lng
        dgv = rs * (dvhat - jnp.mean(dvhat, axis=-1, keepdims=True) - vhat * jnp.mean(dvhat * vhat, axis=-1, keepdims=True))
        dv = dgv * _gelu_grad(v, tv, v2)
        emit_dz(0, 2 * BR_W, jnp.concatenate([du, dv], axis=1))

        dob = _dot_nt(dys[1], wpb_v[...])
        xb = z_ref[:, 2 * BR_W:3 * BR_W].astype(F32)
        xbext[0:POOL_HALO, :] = jnp.where(first_of_seq, 0.0, zpb_ref[...].astype(F32))
        xbext[POOL_HALO:POOL_HALO + tm, :] = xb
        pos = tile_in_seq * tm + lax.broadcasted_iota(jnp.int32, (tm, 1), 0)
        ps = _pool_p(xb, xbext, pos, tm)
        scale_b = ps_ref[...]
        dq = dob * scale_b
        qs, dps = [], []
        for gi, win_len in enumerate(POOL_WINDOWS):
            cols = slice(gi * HEAD_DIM, (gi + 1) * HEAD_DIM)
            pw_b = pw_ref[gi].astype(BF16)
            p_b = ps[gi].astype(BF16)
            dq_b = dq[:, cols].astype(BF16)
            qs.append(_dot(p_b, pw_b))
            dpw_ref[gi] += _dot_tn(p_b, dq_b)
            dp = _dot_nt(dq_b, pw_b)
            dps.append(dp)
            rext[0:tm, cols] = dp * _inv_count(pos, win_len)
        sm_ref[2:3, :] += _colsum(dob * jnp.concatenate(qs, axis=1))
        dxbs = [acc - dp for acc, dp in zip(_window_sums(rext, tm, False), dps)]
        emit_dz(2 * BR_W, 3 * BR_W, jnp.concatenate(dxbs, axis=1))
        rext[tm:tm + POOL_HALO, :] = rext[0:POOL_HALO, :]

        doc = _dot_nt(dys[2], wpc_v[...])
        bg = z_ref[:, 3 * BR_W:4 * BR_W].astype(F32)
        cg = z_ref[:, 4 * BR_W:5 * BR_W].astype(F32)
        hc = z_ref[:, 5 * BR_W:6 * BR_W].astype(F32)
        zz = cg * hc
        zprev = zpc_ref[POOL_HALO - CONV_HALO:POOL_HALO, :].astype(F32)
        zzext[0:CONV_HALO, :] = jnp.where(first_of_seq, 0.0, zprev[:, 0:BR_W] * zprev[:, BR_W:2 * BR_W])
        zzext[CONV_HALO:CONV_HALO + tm, :] = zz
        zm2 = zzext[pl.ds(CONV_HALO - 2, tm), :]
        zm1 = zzext[pl.ds(CONV_HALO - 1, tm), :]
        w0, w1, w2c = cw_ref[0:1, :], cw_ref[1:2, :], cw_ref[2:3, :]
        yconv = w0 * zm2 + w1 * zm1 + w2c * zz
        dyc = doc * bg
        sm_ref[3:4, :] += _colsum(dyc * zm2)
        sm_ref[4:5, :] += _colsum(dyc * zm1)
        sm_ref[5:6, :] += _colsum(dyc * zz)
        dyext[0:tm, :] = dyc
        dzz = w2c * dyc + w1 * dyext[pl.ds(1, tm), :] + w0 * dyext[pl.ds(2, tm), :]
        dyext[tm:tm + CONV_HALO, :] = dyext[0:CONV_HALO, :]
        emit_dz(3 * BR_W, 6 * BR_W, jnp.concatenate([doc * yconv, dzz * hc, dzz * cg], axis=1))

        dh = dh_parts[0]
        for part in dh_parts[1:]:
            dh = dh + part
        dx, dshift, dscale, dg = _rms_mod_bwd(dh, xn, rstd, g, scale1)
        dmod_ref[0, 0:1, :] += dshift
        dmod_ref[0, 1:2, :] += dscale
        dg_ref[0:1, :] += dg
        dx_ref[...] = dxm_t + dx

    tok = lambda cols: pl.BlockSpec((tm, cols), lambda i: (tile_of(i), 0))
    modspec = pl.BlockSpec((1, 8, d), lambda i: (i // n_seq_tiles, 0, 0))
    in_specs = [
        tok(d), tok(d), tok(N_DEV * blk),
        pl.BlockSpec((POOL_HALO, BR_W), lambda i: (halo_row_block(i), 2)),
        pl.BlockSpec((POOL_HALO, 2 * BR_W), lambda i: (halo_row_block(i), 2)),
        tok(3 * d), tok(d), _mod_spec(mod, layer, n_seq_tiles),
        *[_layer_spec(a, layer) for a in (g_mix, ln_g, ln_b, w_s, bexp, pool_w, pool_scale, conv_w)],
        ANY, ANY, ANY, ANY, ANY, ANY,
    ]
    acc = lambda shape: pl.BlockSpec(shape, lambda i: (0,) * len(shape))
    out_shape = [
        jax.ShapeDtypeStruct((t_all, d), F32), jax.ShapeDtypeStruct((t_all, N_DEV * blk), BF16),
        jax.ShapeDtypeStruct((t_all, 3 * d), BF16), jax.ShapeDtypeStruct((t_all, d), BF16),
        jax.ShapeDtypeStruct(mod.shape[1:], F32), jax.ShapeDtypeStruct((8, d), F32), jax.ShapeDtypeStruct((8, BR_W), F32),
        jax.ShapeDtypeStruct((HEADS, CHUNK, CHUNK), F32), jax.ShapeDtypeStruct((CHUNK, BR_W), F32),
        jax.ShapeDtypeStruct((n_win, HEAD_DIM, HEAD_DIM), F32),
    ]
    out_specs = [tok(d), tok(N_DEV * blk), tok(3 * d), tok(d), modspec, acc((8, d)), acc((8, BR_W)),
                 acc((HEADS, CHUNK, CHUNK)), acc((CHUNK, BR_W)), acc((n_win, HEAD_DIM, HEAD_DIM))]
    scratch = [
        pltpu.VMEM((d, N_DEV * blk), BF16), pltpu.VMEM((BR_W, d), BF16), pltpu.VMEM((BR_W, d), BF16),
        pltpu.VMEM((BR_W, d), BF16), pltpu.VMEM((d, d), BF16),
        pltpu.VMEM((tm + POOL_HALO, BR_W), F32), pltpu.VMEM((tm + CONV_HALO, BR_W), F32),
        pltpu.VMEM((tm + POOL_HALO, BR_W), F32), pltpu.VMEM((tm + CONV_HALO, BR_W), F32),
        pltpu.VMEM((tm, BR_W), F32), pltpu.VMEM((tm, BR_W), F32),
        pltpu.SemaphoreType.DMA((1 + 4 * N_DEV,)),
    ]
    return pl.pallas_call(
        body, name=f"mixer_bwd_{layer}", grid=(t_all // tm,), in_specs=in_specs, out_specs=out_specs, out_shape=out_shape,
        scratch_shapes=scratch, compiler_params=_cparams(dimension_semantics=("arbitrary",)),
    )(dxm, x, z, z, z, ycat, mo, mod, g_mix, ln_g, ln_b, w_s, bexp, pool_w, pool_scale, conv_w, win, wpa, wpb, wpc, wo, dep)


def _wgrad(a, b, a_spec, b_spec, out_struct, out_spec, grid_kn, tk, tn, split, name):
    t_all = a.shape[-2]
    tt = min(WGRAD_TOKENS, t_all)
    n_t = t_all // tt

    def body(a_ref, b_ref, o_ref, acc):
        t = pl.program_id(2)

        @pl.when(t == 0)
        def _():
            acc[...] = jnp.zeros(acc.shape, F32)

        acc[...] += _dot_tn(a_ref[...], b_ref[...])

        @pl.when(t == n_t - 1)
        def _():
            if split:
                for j in range(split):
                    w = tn // split
                    o_ref[j] = acc[:, j * w:(j + 1) * w].astype(o_ref.dtype)
            else:
                o_ref[...] = acc[...].astype(o_ref.dtype)

    return pl.pallas_call(
        body, name=name, grid=(*grid_kn, n_t), in_specs=[a_spec(tt), b_spec(tt)], out_specs=out_spec, out_shape=out_struct,
        scratch_shapes=[pltpu.VMEM((tk, tn), F32)],
        compiler_params=_cparams(dimension_semantics=("arbitrary", "arbitrary", "arbitrary")),
    )(a, b)


def _mixer_wgrads(h, dz, mrg, dmo, ocat, dycat, layer):
    d = h.shape[1]
    blk = dz.shape[1] // N_DEV
    g_win = _wgrad(
        h, dz, lambda tt: pl.BlockSpec((tt, d), lambda k, n, t: (t, 0)), lambda tt: pl.BlockSpec((tt, blk), lambda k, n, t: (t, n)),
        jax.ShapeDtypeStruct((N_DEV, d, blk), BF16), pl.BlockSpec((None, d, blk), lambda k, n, t: (n, 0, 0)),
        (1, N_DEV), d, blk, 0, f"wgrad_in_{layer}")
    g_wo = _wgrad(
        mrg, dmo, lambda tt: pl.BlockSpec((tt, d), lambda k, n, t: (t, 0)), lambda tt: pl.BlockSpec((tt, d), lambda k, n, t: (t, 0)),
        jax.ShapeDtypeStruct((d, d), BF16), pl.BlockSpec((d, d), lambda k, n, t: (0, 0)), (1, 1), d, d, 0, f"wgrad_o_{layer}")
    g_wp = []
    for n, nm in enumerate("abc"):
        g_wp.append(_wgrad(
            ocat, dycat, lambda tt, n=n: pl.BlockSpec((tt, BR_W), lambda k, nn, t: (t, n)),
            lambda tt, n=n: pl.BlockSpec((tt, d), lambda k, nn, t: (t, n)),
            jax.ShapeDtypeStruct((N_DEV, BR_W, d // N_DEV), BF16),
            pl.BlockSpec((N_DEV, BR_W, d // N_DEV), lambda k, nn, t: (0, 0, 0)), (1, 1), BR_W, d, N_DEV, f"wgrad_p{nm}_{layer}"))
    return [g_win, g_wo.reshape(N_DEV, d // N_DEV, d), *g_wp]


def _ffn_wgrads(h2, dab, hid, df, layer):
    d = h2.shape[1]
    fb = dab.shape[-1]
    n_hid = N_DEV // 2
    g_w13 = _wgrad(
        dab, h2, lambda tt: pl.BlockSpec((None, tt, fb), lambda k, n, t: (k, t, 0)),
        lambda tt: pl.BlockSpec((tt, d), lambda k, n, t: (t, 0)),
        jax.ShapeDtypeStruct((N_DEV, fb, d), BF16), pl.BlockSpec((None, fb, d), lambda k, n, t: (k, 0, 0)),
        (N_DEV, 1), fb, d, 0, f"wgrad_13_{layer}")
    g_w2 = _wgrad(
        hid, df, lambda tt: pl.BlockSpec((None, tt, fb), lambda k, n, t: (k, t, 0)),
        lambda tt: pl.BlockSpec((tt, d), lambda k, n, t: (t, 0)),
        jax.ShapeDtypeStruct((n_hid * fb, d), BF16), pl.BlockSpec((fb, d), lambda k, n, t: (k, 0)),
        (n_hid, 1), fb, d, 0, f"wgrad_2_{layer}")
    return [g_w13, g_w2.reshape(N_DEV, fb // 2, d)]


def _remote(src, dst, send_sem, recv_sem, dev):
    return pltpu.make_async_remote_copy(src_ref=src, dst_ref=dst, send_sem=send_sem, recv_sem=recv_sem, device_id=dev,
                                        device_id_type=MESH_ID)


HBM_SPEC = pl.BlockSpec(memory_space=pltpu.HBM)
SEM_SPEC = pl.BlockSpec(memory_space=pltpu.SEMAPHORE)
DATAFLOW = pltpu.SideEffectType.DATAFLOW_SIDE_EFFECTING
GATHER, SCATTER = "gather", "scatter"


def _xfer_sem(a, k):
    return a * (N_DEV - 1) + k - 1


def _gather_zones(weights, layer):
    me = _my_index()
    return [lax.dynamic_update_index_in_dim(lax.empty((N_DEV, *w.shape[1:]), BF16), w[layer].astype(BF16)[None], me, 0)
            for w in weights]


def _scatter_zones(partials):
    me = _my_index()
    return [lax.dynamic_update_index_in_dim(lax.empty(p.shape, p.dtype), lax.dynamic_index_in_dim(p, me, 0, keepdims=True), me, 0)
            for p in partials]


def _xfer_src(kind, src, land, a, me, idx):
    return land[a].at[me] if kind == GATHER else src[a].at[idx]


def _xfer_start(srcs, lands, kind, name, after):
    n, n_src = len(lands), len(srcs)

    def body(*refs):
        src, land = refs[:n_src], refs[n_src:n_src + n]
        send_sems, recv_sems = refs[n_src + n + 1], refs[n_src + n + 2]
        token = refs[-1]
        me = _my_index()
        for k in range(1, N_DEV):
            dev, idx = _peer(k)
            for a in range(n):
                q = _xfer_sem(a, k)
                _remote(_xfer_src(kind, src, land, a, me, idx), land[a].at[me], send_sems.at[q], recv_sems.at[q], dev).start()
        token[...] = jnp.zeros(token.shape, token.dtype)

    both = [*srcs, *lands]
    sems = pltpu.SemaphoreType.DMA((n * (N_DEV - 1),))
    out_shape = (sems, sems, *[pltpu.HBM(a.shape, a.dtype) for a in both], jax.ShapeDtypeStruct((8, 128), F32))
    outs = pl.pallas_call(
        body, name=name, in_specs=[HBM_SPEC] * len(both) + [ANY],
        out_specs=(SEM_SPEC, SEM_SPEC, *[HBM_SPEC] * len(both), VMEM_SPEC),
        out_shape=out_shape, input_output_aliases={i: 2 + i for i in range(len(both))},
        compiler_params=pltpu.CompilerParams(has_side_effects=DATAFLOW),
    )(*[pltpu.with_memory_space_constraint(a, pltpu.HBM) for a in both], after)
    return outs[0], outs[1], list(outs[2:2 + n_src]), list(outs[2 + n_src:2 + n_src + n]), outs[-1]


def _gather_two_level(zones, name, after):
    n = len(zones)

    def body(*refs):
        out = refs[n + 1:2 * n + 1]
        send_sems, recv_sems = refs[2 * n + 1], refs[2 * n + 2]
        x, y, c = lax.axis_index("x"), lax.axis_index("y"), lax.axis_index("c")
        index = lambda px, py, pc: 4 * px + 2 * py + pc
        me, sibling = (x, y, c), (x, y, 1 - c)
        chips = [(1 - x, y), (x, 1 - y), (1 - x, 1 - y)]

        def copy(a, k, block, to):
            rows = out[a].at[index(*block)]
            return _remote(rows, rows, send_sems.at[a * (N_DEV - 1) + k], recv_sems.at[a * (N_DEV - 1) + k], to)

        first = [copy(a, 0, me, sibling) for a in range(n)]
        first += [copy(a, 1 + j, me, (*chip, c)) for j, chip in enumerate(chips) for a in range(n)]
        for cp in first:
            cp.start()
        passed = []
        for j, chip in enumerate(chips):
            for a in range(n):
                copy(a, 1 + j, (*chip, c), me).wait_recv()
                passed.append(copy(a, 4 + j, (*chip, c), sibling))
                passed[-1].start()
        for a in range(n):
            copy(a, 0, sibling, me).wait_recv()
            for j, chip in enumerate(chips):
                copy(a, 4 + j, (*chip, 1 - c), me).wait_recv()
        for cp in first + passed:
            cp.wait_send()

    sems = pltpu.SemaphoreType.DMA((n * (N_DEV - 1),))
    return pl.pallas_call(
        body, name=name, in_specs=[ANY] * (n + 1), out_specs=[ANY] * n, out_shape=[jax.ShapeDtypeStruct(z.shape, z.dtype) for z in zones],
        input_output_aliases={i: i for i in range(n)}, scratch_shapes=[sems, sems],
        compiler_params=pltpu.CompilerParams(has_side_effects=True),
    )(*zones, after)


def _xfer_wait(send_sems, recv_sems, srcs, lands, rows, after, kind, name):
    n, n_src = len(lands), len(srcs)

    def body(*refs):
        src, land = refs[:n_src], refs[n_src:n_src + n]
        send_sems, recv_sems = refs[n_src + n], refs[n_src + n + 1]
        me = _my_index()
        for k in range(1, N_DEV):
            dev, idx = _peer(k)
            for a in range(n):
                q = _xfer_sem(rows[a], k)
                cp = _remote(_xfer_src(kind, src, land, a, me, idx), land[a].at[idx], send_sems.at[q], recv_sems.at[q], dev)
                cp.wait_send()
                cp.wait_recv()

    both = [*srcs, *lands]
    outs = pl.pallas_call(
        body, name=name, in_specs=[HBM_SPEC] * len(both) + [SEM_SPEC, SEM_SPEC, ANY], out_specs=[HBM_SPEC] * len(both),
        out_shape=[pltpu.HBM(a.shape, a.dtype) for a in both], input_output_aliases={i: i for i in range(len(both))},
        compiler_params=pltpu.CompilerParams(has_side_effects=DATAFLOW),
    )(*both, send_sems, recv_sems, after)
    return list(outs[n_src:])


def _pre(c_pad, conv_pad, w_mod, b_mod_mine):
    n_layers, d, blk = w_mod.shape

    def body(c_ref, conv_ref, wmod_ref, bmod_ref, cact_ref, mod_ref, convall_ref, cact_mine, msh, send_sems, recv_sems):
        me = _my_index()
        c = c_ref[...]
        cact_mine[...] = c * _sigmoid(c)
        cact_ref[me] = cact_mine[...]
        convall_ref[me] = conv_ref[...]
        sends = []
        for k in range(1, N_DEV):
            dev, _ = _peer(k)
            sends.append(_remote(cact_mine, cact_ref.at[me], send_sems.at[0, k - 1], recv_sems.at[0, k - 1], dev))
            sends.append(_remote(conv_ref, convall_ref.at[me], send_sems.at[1, k - 1], recv_sems.at[1, k - 1], dev))
        for cp in sends:
            cp.start()
        for k in range(1, N_DEV):
            dev, idx = _peer(k)
            _remote(cact_mine, cact_ref.at[idx], send_sems.at[0, k - 1], recv_sems.at[0, k - 1], dev).wait_recv()
            _remote(conv_ref, convall_ref.at[idx], send_sems.at[1, k - 1], recv_sems.at[1, k - 1], dev).wait_recv()
        for cp in sends:
            cp.wait_send()
        cact_b = cact_ref[...].reshape(N_DEV * 8, d).astype(BF16)
        for l in range(n_layers):
            m = _dot(cact_b, wmod_ref[l].astype(BF16)) + bmod_ref[l]
            msh[l] = m.reshape(N_DEV, 8, blk)
        mod_ref[me] = msh[:, me]
        sends = []
        for k in range(1, N_DEV):
            dev, idx = _peer(k)
            sends.append(_remote(msh.at[:, idx], mod_ref.at[me], send_sems.at[2, k - 1], recv_sems.at[2, k - 1], dev))
        for cp in sends:
            cp.start()
        for k in range(1, N_DEV):
            dev, idx = _peer(k)
            _remote(msh.at[:, idx], mod_ref.at[idx], send_sems.at[2, k - 1], recv_sems.at[2, k - 1], dev).wait_recv()
        for cp in sends:
            cp.wait_send()

    out_shape = [jax.ShapeDtypeStruct((N_DEV, 8, d), F32), jax.ShapeDtypeStruct((N_DEV, n_layers, 8, blk), F32),
                 jax.ShapeDtypeStruct((N_DEV, *conv_pad.shape), F32)]
    return pl.pallas_call(
        body, name="pre", in_specs=[VMEM_SPEC] * 4, out_specs=[VMEM_SPEC] * 3, out_shape=out_shape,
        scratch_shapes=[pltpu.VMEM((8, d), F32), pltpu.VMEM((n_layers, N_DEV, 8, blk), F32),
                        pltpu.SemaphoreType.DMA((3, N_DEV - 1)), pltpu.SemaphoreType.DMA((3, N_DEV - 1))],
        compiler_params=_cparams(has_side_effects=True),
    )(c_pad, conv_pad, w_mod, b_mod_mine)


def _small(buf, dmod_blocks, cact_all):
    rows = buf.shape[0]
    seg = rows // N_DEV
    _, n_layers, _, blk = dmod_blocks.shape
    d = cact_all.shape[-1]

    def body(buf_ref, dmod_ref, cact_ref, out_ref, gw_ref, rs_recv, red, drecv, send_sems, recv_sems):
        me = _my_index()
        mine = pl.ds(pl.multiple_of(me * seg, 8), seg)
        sends = []
        for k in range(1, N_DEV):
            dev, idx = _peer(k)
            theirs = pl.ds(pl.multiple_of(idx * seg, 8), seg)
            sends.append(_remote(buf_ref.at[theirs], rs_recv.at[k - 1], send_sems.at[0, k - 1], recv_sems.at[0, k - 1], dev))
            sends.append(_remote(dmod_ref.at[idx], drecv.at[me], send_sems.at[1, k - 1], recv_sems.at[1, k - 1], dev))
        for cp in sends:
            cp.start()
        drecv[me] = dmod_ref[me]
        for k in range(1, N_DEV):
            dev, idx = _peer(k)
            _remote(buf_ref.at[mine], rs_recv.at[k - 1], send_sems.at[0, k - 1], recv_sems.at[0, k - 1], dev).wait_recv()
            _remote(dmod_ref.at[idx], drecv.at[idx], send_sems.at[1, k - 1], recv_sems.at[1, k - 1], dev).wait_recv()
        for cp in sends:
            cp.wait_send()
        total = buf_ref[mine, :]
        for k in range(1, N_DEV):
            total = total + rs_recv[k - 1]
        red[...] = total
        out_ref[mine, :] = total
        sends = []
        for k in range(1, N_DEV):
            dev, _ = _peer(k)
            sends.append(_remote(red, out_ref.at[mine], send_sems.at[2, k - 1], recv_sems.at[2, k - 1], dev))
        for cp in sends:
            cp.start()
        cact_b = cact_ref[...].reshape(N_DEV * 8, d).astype(BF16)
        for l in range(n_layers):
            gw_ref[l] = _dot_tn(cact_b, drecv[:, l].reshape(N_DEV * 8, blk).astype(BF16))
        for k in range(1, N_DEV):
            dev, idx = _peer(k)
            theirs = pl.ds(pl.multiple_of(idx * seg, 8), seg)
            _remote(red, out_ref.at[theirs], send_sems.at[2, k - 1], recv_sems.at[2, k - 1], dev).wait_recv()
        for cp in sends:
            cp.wait_send()

    out_shape = [jax.ShapeDtypeStruct(buf.shape, F32), jax.ShapeDtypeStruct((n_layers, d, blk), F32)]
    return pl.pallas_call(
        body, name="small_allreduce", in_specs=[VMEM_SPEC] * 3, out_specs=[VMEM_SPEC] * 2, out_shape=out_shape,
        scratch_shapes=[pltpu.VMEM((N_DEV - 1, seg, 128), F32), pltpu.VMEM((seg, 128), F32),
                        pltpu.VMEM(dmod_blocks.shape, F32),
                        pltpu.SemaphoreType.DMA((3, N_DEV - 1)), pltpu.SemaphoreType.DMA((3, N_DEV - 1))],
        compiler_params=_cparams(has_side_effects=True),
    )(buf, dmod_blocks, cact_all)


def _adamw_math(w, g, m, v):
    m = ADAM_B1 * m + (1.0 - ADAM_B1) * g
    v = ADAM_B2 * v + (1.0 - ADAM_B2) * (g * g)
    m_hat = m / (1.0 - ADAM_B1 ** ADAM_STEP)
    v_hat = v / (1.0 - ADAM_B2 ** ADAM_STEP)
    delta = -ADAM_LR * (m_hat / (jnp.sqrt(v_hat) + ADAM_EPS) + ADAM_WD * w)
    return delta, m, v


def _adamw(parts, w, m, v, name, first=0, earlier=None):
    n_layers = len(parts)
    n_parts, rows, cols = parts[0].shape
    tr = rows
    while tr * cols * 4 > (1 << 20) and tr % 32 == 0:
        tr //= 2
    n_r = rows // tr
    n_earlier = 0 if earlier is None else 4

    def body(*refs):
        p_refs = refs[:n_layers]
        w_ref, m_ref, v_ref = refs[n_layers:n_layers + 3]
        g_out, d_out, m_out, v_out = refs[n_layers + 3 + n_earlier:]
        layer = pl.program_id(0)
        for q in range(n_layers):

            @pl.when(layer == q)
            def _(q=q):
                g = p_refs[q][0].astype(F32)
                for p in range(1, n_parts):
                    g = g + p_refs[q][p].astype(F32)
                delta, m_new, v_new = _adamw_math(w_ref[...], g, m_ref[...], v_ref[...])
                g_out[...] = g
                d_out[...] = delta
                m_out[...] = m_new
                v_out[...] = v_new

    def parts_spec(q):
        return pl.BlockSpec((n_parts, tr, cols), lambda l, r: (0, jnp.where(l == q, r, jnp.where(l < q, 0, n_r - 1)), 0))

    spec = pl.BlockSpec((None, tr, cols), lambda l, r: (first + l, r, 0))
    out = jax.ShapeDtypeStruct(w.shape, F32)
    n_in = n_layers + 3
    return pl.pallas_call(
        body, name=name, grid=(n_layers, n_r),
        in_specs=[parts_spec(q) for q in range(n_layers)] + [spec, spec, spec] + [ANY] * n_earlier,
        out_specs=[spec] * 4, out_shape=[out] * 4, input_output_aliases={n_in + i: i for i in range(n_earlier)},
        compiler_params=_cparams(dimension_semantics=("arbitrary", "arbitrary")),
    )(*parts, w, m, v, *(earlier or ()))


def _adamw_flat(g, w, m, v):
    def body(g_ref, w_ref, m_ref, v_ref, d_out, m_out, v_out):
        delta, m_new, v_new = _adamw_math(w_ref[...], g_ref[...], m_ref[...], v_ref[...])
        d_out[...] = delta
        m_out[...] = m_new
        v_out[...] = v_new

    out = jax.ShapeDtypeStruct(g.shape, F32)
    return pl.pallas_call(body, name="adamw_small", in_specs=[VMEM_SPEC] * 4, out_specs=[VMEM_SPEC] * 3, out_shape=[out] * 3,
                          compiler_params=_cparams())(g, w, m, v)


def _pack(arrays, rows_multiple):
    flat = jnp.concatenate([a.reshape(-1) for a in arrays])
    per = 128 * rows_multiple
    total = -(-flat.shape[0] // per) * per
    return jnp.pad(flat, (0, total - flat.shape[0])).reshape(total // 128, 128)


def _unpack(buf, like):
    flat = buf.reshape(-1)
    out, off = [], 0
    for a in like:
        out.append(flat[off:off + a.size].reshape(a.shape))
        off += a.size
    return out


def kernel(x, c, w_mod, b_mod, g_mix, w_in, gm_ln_g, gm_ln_b, gm_w_s, gm_b_s, w_pa, pool_w, pool_scale, w_pb, conv_w, w_pc, w_o, g_ffn, w_13, w_2, g_final, loss_target, m_w_mod, m_b_mod, m_g_mix, m_w_in, m_gm_ln_g, m_gm_ln_b, m_gm_w_s, m_gm_b_s, m_w_pa, m_pool_w, m_pool_scale, m_w_pb, m_conv_w, m_w_pc, m_w_o, m_g_ffn, m_w_13, m_w_2, m_g_final, v_w_mod, v_b_mod, v_g_mix, v_w_in, v_gm_ln_g, v_gm_ln_b, v_gm_w_s, v_gm_b_s, v_w_pa, v_pool_w, v_pool_scale, v_w_pb, v_conv_w, v_w_pc, v_w_o, v_g_ffn, v_w_13, v_w_2, v_g_final):
    nb, seq, d = x.shape
    n_layers = w_in.shape[0]
    t_all = nb * seq
    blk = w_in.shape[-1]
    me = _my_index()
    conv_shard = conv_w.shape[-1]

    c_pad = jnp.pad(c, ((0, 8 - nb), (0, 0)))
    conv_pad = jnp.pad(conv_w.reshape(n_layers * 3, conv_shard), ((0, 16 - n_layers * 3), (0, 128 - conv_shard)))
    b_mod_mine = lax.dynamic_slice_in_dim(b_mod, me * blk, blk, axis=1).reshape(n_layers, 1, blk)
    cact_all, mod_blocks, conv_all = _pre(c_pad, conv_pad, w_mod, b_mod_mine)
    mod = jnp.transpose(mod_blocks, (1, 2, 0, 3)).reshape(n_layers, 8, N_MOD, d)[:, :nb]
    mod = jnp.pad(mod, ((0, 0), (0, 0), (0, 8 - N_MOD), (0, 0)))
    conv_full = jnp.transpose(conv_all[:, :n_layers * 3, :conv_shard].reshape(N_DEV, n_layers, 3, conv_shard), (1, 2, 0, 3))
    conv_full = jnp.pad(conv_full.reshape(n_layers, 3, N_DEV * conv_shard), ((0, 0), (0, 5), (0, 0)))
    bexp = jnp.repeat(jnp.transpose(gm_b_s, (0, 2, 1)), HEAD_DIM, axis=2)

    mixer_w, ffn_w = [w_in, w_o, w_pa, w_pb, w_pc], [w_13, w_2]
    n_mix = len(mixer_w)

    def send_weights(l, ws, name, after):
        return _xfer_start([], _gather_zones(ws, l), GATHER, name, after)

    def send_grads(partials, name, after):
        return _xfer_start(partials, _scatter_zones(partials), SCATTER, name, after)

    def arrived(flight, after, kind, name):
        send_sems, recv_sems, srcs, zones, _ = flight
        return _xfer_wait(send_sems, recv_sems, srcs, zones, list(range(len(zones))), after, kind, name)

    rows = lambda a: a.reshape(n_layers, 1, -1)
    mix_small = (rows(g_mix), rows(gm_ln_g), rows(gm_ln_b), gm_w_s, bexp, pool_w, rows(pool_scale), conv_full)
    g_ffn_rows = rows(g_ffn)
    xs = x.reshape(t_all, d)
    saved, weights = [], []
    after = mod
    for l in range(n_layers):
        if l == 0:
            win_g, wo_g, wpa_g, wpb_g, wpc_g = _gather_two_level(_gather_zones(mixer_w, 0), "gather_mixer_0", cact_all)
        else:
            win_g, wo_g, wpa_g, wpb_g, wpc_g = arrived(flight, after, GATHER, f"gather_wait_mixer_{l}")
        flight = send_weights(l, ffn_w, f"gather_start_ffn_{l}", win_g)
        wo_g = wo_g.reshape(d, d)
        h, z, ycat, ocat, mrg, mo, xmid = _mixer_fwd(xs, mod, *mix_small, win_g, wpa_g, wpb_g, wpc_g, wo_g, flight[4], l, seq)
        w13_g, w2_g = arrived(flight, xmid, GATHER, f"gather_wait_ffn_{l}")
        dep = w13_g
        if l + 1 < n_layers:
            flight = send_weights(l + 1, mixer_w, f"gather_start_mixer_{l + 1}", w13_g)
            dep = flight[4]
        w2_g = w2_g.reshape(N_DEV * w_2.shape[1], d)
        h2, ab, hid, f, xo = _ffn_fwd(xmid, mod, g_ffn_rows, w13_g, w2_g, dep, l, seq)
        saved.append((xs, h, z, ycat, ocat, mrg, mo, xmid, h2, ab, hid, f))
        weights.append((win_g, wpa_g, wpb_g, wpc_g, wo_g, w13_g, w2_g))
        xs = after = xo

    dx, loss_blk, dgf_blk = _final_loss(xs, g_final.reshape(1, d), loss_target.reshape(t_all, d), seq)
    loss = lax.psum(loss_blk[0, 0], ("x", "y", "c"))

    ffn_flight = [None] * n_layers
    mix_flight = [None] * n_layers
    small_grads = [None] * n_layers
    dmods = [None] * n_layers
    dep = dx
    for l in reversed(range(n_layers)):
        x_in, h, z, ycat, ocat, mrg, mo, xmid, h2, ab, hid, f = saved[l]
        win_g, wpa_g, wpb_g, wpc_g, wo_g, w13_g, w2_g = weights[l]
        dxm, df, dab, dmod2, dg_ffn = _ffn_bwd(dx, xmid, ab, f, mod, g_ffn_rows, w13_g, w2_g, dep, l, seq)
        ffn_flight[l] = send_grads(_ffn_wgrads(h2, dab, hid, df, l), f"grads_start_ffn_{l}", dxm)
        dx, dz, dycat, dmo, dmod1, dg_mix, sm, dws, dssum, dpw = _mixer_bwd(
            dxm, x_in, z, ycat, mo, mod, *mix_small, win_g, wpa_g, wpb_g, wpc_g, wo_g, ffn_flight[l][4], l, seq)
        if l > 0:
            mix_flight[l] = send_grads(_mixer_wgrads(h, dz, mrg, dmo, ocat, dycat, l), f"grads_start_mixer_{l}", dx)
            dep = mix_flight[l][4]
        dmod = jnp.concatenate([dmod1[:, 0:3], dmod2[:, 0:3]], axis=1).reshape(nb, N_MOD * d)
        dmods[l] = dmod
        db_s = jnp.transpose(jnp.sum(dssum.reshape(CHUNK, HEADS, HEAD_DIM), axis=2))
        small_grads[l] = [jnp.sum(dmod, axis=0), dg_mix[0], sm[0], sm[1], dws, db_s, dpw, sm[2], sm[3:6], dg_ffn[0]]
    grad_x = dx.reshape(nb, seq, d)

    names = ["b_mod", "g_mix", "ln_g", "ln_b", "w_s", "b_s", "pool_w", "pool_scale", "conv_w", "g_ffn"]
    per_name = [jnp.stack([small_grads[l][n] for l in range(n_layers)]) for n in range(len(names))] + [dgf_blk[0]]
    buf = _pack(per_name, 8 * N_DEV)
    dmod_all = jnp.pad(jnp.stack(dmods), ((0, 0), (0, 8 - nb), (0, 0)))
    dmod_blocks = jnp.transpose(dmod_all.reshape(n_layers, 8, N_DEV, blk), (2, 0, 1, 3))
    red, grad_w_mod = _small(buf, dmod_blocks, cact_all)
    (g_b_mod, g_g_mix, g_ln_g, g_ln_b, g_w_s, g_b_s, g_pool_w, g_pool_scale, g_conv_full, g_g_ffn, g_g_final) = _unpack(red, per_name)
    g_conv = lax.dynamic_slice_in_dim(g_conv_full, me * conv_shard, conv_shard, axis=2)

    mix_flight[0] = send_grads(_mixer_wgrads(h, dz, mrg, dmo, ocat, dycat, 0), "grads_start_mixer_0", red)
    results = {}
    results["w_mod"] = _adamw([grad_w_mod[l][None] for l in range(n_layers)], w_mod, m_w_mod, v_w_mod, "adamw_w_mod")

    small_w =[b_mod, g_mix, gm_ln_g, gm_ln_b, gm_w_s, gm_b_s, pool_w, pool_scale, conv_w, g_ffn, g_final]
    small_m = [m_b_mod, m_g_mix, m_gm_ln_g, m_gm_ln_b, m_gm_w_s, m_gm_b_s, m_pool_w, m_pool_scale, m_conv_w, m_g_ffn, m_g_final]
    small_v = [v_b_mod, v_g_mix, v_gm_ln_g, v_gm_ln_b, v_gm_w_s, v_gm_b_s, v_pool_w, v_pool_scale, v_conv_w, v_g_ffn, v_g_final]
    small_g = [g_b_mod, g_g_mix, g_ln_g, g_ln_b, g_w_s, g_b_s, g_pool_w, g_pool_scale, g_conv, g_g_ffn, g_g_final]
    small_names = ["b_mod", "g_mix", "gm_ln_g", "gm_ln_b", "gm_w_s", "gm_b_s", "pool_w", "pool_scale", "conv_w", "g_ffn", "g_final"]
    sd_buf, sm_new, sv_new = _adamw_flat(_pack(small_g, 8), _pack(small_w, 8), _pack(small_m, 8), _pack(small_v, 8))
    sd, sm_new, sv_new = _unpack(sd_buf, small_w), _unpack(sm_new, small_w), _unpack(sv_new, small_w)
    for n, nm in enumerate(small_names):
        results[nm] = (small_g[n], sd[n], sm_new[n], sv_new[n])

    layers = list(range(n_layers))
    done = (results["w_mod"][1][0, 0, 0] + sd_buf[0, 0]).reshape(1)
    ffn_recv = [arrived(ffn_flight[l], done, SCATTER, f"grads_wait_ffn_{l}") for l in reversed(layers)][::-1]
    mix_recv = [None] + [arrived(mix_flight[l], done, SCATTER, f"grads_wait_mixer_{l}") for l in reversed(layers[1:])][::-1]
    swap = lambda a: jnp.swapaxes(a, 1, 2)
    results["w_13"] = [swap(r) for r in _adamw([ffn_recv[l][0] for l in layers], swap(w_13), swap(m_w_13), swap(v_w_13), "adamw_w_13")]
    results["w_2"] = _adamw([ffn_recv[l][1] for l in layers], w_2, m_w_2, v_w_2, "adamw_w_2")
    mix_m = [m_w_in, m_w_o, m_w_pa, m_w_pb, m_w_pc]
    mix_v = [v_w_in, v_w_o, v_w_pa, v_w_pb, v_w_pc]
    mix_names = ["w_in", "w_o", "w_pa", "w_pb", "w_pc"]
    early = [_adamw([mix_recv[l][a] for l in layers[1:]], mixer_w[a], mix_m[a], mix_v[a], f"adamw_{mix_names[a]}_later", first=1)
             for a in range(n_mix)]
    done = (results["w_13"][1][0, 0, 0] + results["w_2"][1][0, 0, 0] + sum(e[1][1, 0, 0] for e in early)).reshape(1)
    mix_recv[0] = arrived(mix_flight[0], done, SCATTER, "grads_wait_mixer_0")
    for a in range(n_mix):
        results[mix_names[a]] = _adamw([mix_recv[0][a]], mixer_w[a], mix_m[a], mix_v[a], f"adamw_{mix_names[a]}_first",
                                       earlier=early[a])

    order = ["w_mod", "b_mod", "g_mix", "w_in", "gm_ln_g", "gm_ln_b", "gm_w_s", "gm_b_s", "w_pa", "pool_w", "pool_scale",
             "w_pb", "conv_w", "w_pc", "w_o", "g_ffn", "w_13", "w_2", "g_final"]
    return (loss, grad_x, *[results[nm][0] for nm in order], *[results[nm][1] for nm in order],
            *[results[nm][2] for nm in order], *[results[nm][3] for nm in order])
```

```python
import functools

import jax
import jax.numpy as jnp
from jax import lax
from jax.experimental import pallas as pl
from jax.experimental.pallas import tpu as pltpu

F32 = jnp.float32
BF16 = jnp.bfloat16
MESH_ID = pl.DeviceIdType.MESH

N_DEV = 8
EPS = 1e-6
CHUNK = 128
HEADS = 4
HEAD_DIM = 128
BR_W = 512
POOL_WINDOWS = (2, 4, 8, 16)
POOL_HALO = 16
CONV_HALO = 8
N_MOD = 6
ADAM_LR = 0.001
ADAM_B1 = 0.9
ADAM_B2 = 0.999
ADAM_EPS = 1e-08
ADAM_WD = 0.01
ADAM_STEP = 10

TOKEN_TILE = 256
ELEMENTWISE_TILE = 1024
WGRAD_TOKENS = 2048
VMEM_LIMIT = 56 * 1024 * 1024
GELU_K = 0.7978845608028654
GELU_C = 0.044715

ANY = pl.BlockSpec(memory_space=pl.ANY)
VMEM_SPEC = pl.BlockSpec(memory_space=pltpu.VMEM)


def _cparams(**kw):
    return pltpu.CompilerParams(vmem_limit_bytes=VMEM_LIMIT, **kw)


def _dot(a, b):
    return jnp.dot(a, b, preferred_element_type=F32)


def _dot_nt(a, b):
    return lax.dot_general(a, b, (((1,), (1,)), ((), ())), preferred_element_type=F32)


def _dot_tn(a, b):
    return lax.dot_general(a, b, (((0,), (0,)), ((), ())), preferred_element_type=F32)


def _colsum(a):
    return jnp.sum(a, axis=0, keepdims=True)


def _sigmoid(x):
    return 0.5 * jnp.tanh(0.5 * x) + 0.5


def _gelu(x):
    x2 = x * x
    t = jnp.tanh(x * (GELU_K + (GELU_K * GELU_C) * x2))
    return (0.5 * x) * (1.0 + t), t, x2


def _gelu_grad(x, t, x2):
    one_t = 1.0 + t
    return 0.5 * one_t + (0.5 * x) * (one_t * (1.0 - t)) * (GELU_K + (3.0 * GELU_K * GELU_C) * x2)


def _tril_mask():
    r = lax.broadcasted_iota(jnp.int32, (CHUNK, CHUNK), 0)
    c = lax.broadcasted_iota(jnp.int32, (CHUNK, CHUNK), 1)
    return (r >= c).astype(F32)


def _my_index():
    return 4 * lax.axis_index("x") + 2 * lax.axis_index("y") + lax.axis_index("c")


def _peer(k):
    x, y, c = lax.axis_index("x"), lax.axis_index("y"), lax.axis_index("c")
    px = 1 - x if (k >> 2) & 1 else x
    py = 1 - y if (k >> 1) & 1 else y
    pc = 1 - c if k & 1 else c
    return (px, py, pc), 4 * px + 2 * py + pc


def _load_weights(step, pairs, sem):
    @pl.when(step == 0)
    def _():
        copies = [pltpu.make_async_copy(src, dst, sem.at[n]) for n, (src, dst) in enumerate(pairs)]
        for cp in copies:
            cp.start()
        for cp in copies:
            cp.wait()


def _wp_pairs(wp_hbm, wp_v):
    return [(wp_hbm.at[j], wp_v.at[:, pl.ds(HEAD_DIM * j, HEAD_DIM)]) for j in range(N_DEV)]


def _layer_spec(a, layer):
    return pl.BlockSpec((None, *a.shape[1:]), lambda i: (layer,) + (0,) * (a.ndim - 1))


def _mod_spec(mod, layer, n_seq_tiles):
    return pl.BlockSpec((None, 1, *mod.shape[2:]), lambda i: (layer, i // n_seq_tiles, 0, 0))


def _rms_mod(x, g, shift, scale):
    rstd = lax.rsqrt(jnp.mean(x * x, axis=-1, keepdims=True) + EPS)
    xn = x * rstd
    return xn, rstd, (xn * g) * (1.0 + scale) + shift


def _rms_mod_bwd(dh, xn, rstd, g, scale):
    dxn = dh * (1.0 + scale) * g
    dx = rstd * (dxn - xn * jnp.mean(dxn * xn, axis=-1, keepdims=True))
    return dx, _colsum(dh), _colsum(dh * (xn * g)), _colsum(dh * (1.0 + scale) * xn)


def _gmlp_s(vn_b, wmask_b, bexp, s_scr, tm):
    for ch in range(tm // CHUNK):
        rows = slice(ch * CHUNK, (ch + 1) * CHUNK)
        for hh in range(HEADS):
            cols = slice(hh * HEAD_DIM, (hh + 1) * HEAD_DIM)
            s_scr[rows, cols] = _dot(wmask_b[hh], vn_b[rows, cols]) + bexp[:, cols]
    return s_scr[...]


def _inv_count(pos, win):
    return 1.0 / jnp.minimum(pos + 1, win).astype(F32)


def _window_sums(ext, tm, trailing):
    n = tm + POOL_HALO
    sums = []
    for g, win in enumerate(POOL_WINDOWS):
        s = ext[:, g * HEAD_DIM:(g + 1) * HEAD_DIM]
        span = 1
        while span < win:
            s = s + pltpu.roll(s, span if trailing else n - span, 0)
            span *= 2
        sums.append(s[POOL_HALO:POOL_HALO + tm] if trailing else s[0:tm])
    return sums


def _pool_p(xb, xbext, pos, tm):
    sums = _window_sums(xbext, tm, True)
    return [sums[g] * _inv_count(pos, win) - xb[:, g * HEAD_DIM:(g + 1) * HEAD_DIM] for g, win in enumerate(POOL_WINDOWS)]


def _mixer_fwd(x, mod, g_mix, ln_g, ln_b, w_s, bexp, pool_w, pool_scale, conv_w, win, wpa, wpb, wpc, wo, dep, layer, seq):
    t_all, d = x.shape
    tm = min(TOKEN_TILE, seq)
    n_seq_tiles = seq // tm
    blk = win.shape[-1]

    def body(x_ref, mod_ref, gmix_ref, lng_ref, lnb_ref, ws_ref, bexp_ref, pw_ref, ps_ref, cw_ref,
             win_hbm, wpa_hbm, wpb_hbm, wpc_hbm, wo_hbm, dep_ref,
             h_ref, z_ref, ycat_ref, ocat_ref, mrg_ref, mo_ref, xmid_ref,
             win_v, wpa_v, wpb_v, wpc_v, wo_v, xbext, zcext, s_scr, sem):
        i = pl.program_id(0)
        pairs = [(win_hbm, win_v), (wo_hbm, wo_v)]
        pairs += _wp_pairs(wpa_hbm, wpa_v) + _wp_pairs(wpb_hbm, wpb_v) + _wp_pairs(wpc_hbm, wpc_v)
        _load_weights(i, pairs, sem)
        tile_in_seq = i % n_seq_tiles

        @pl.when(tile_in_seq == 0)
        def _():
            xbext[0:POOL_HALO, :] = jnp.zeros((POOL_HALO, BR_W), F32)
            zcext[0:CONV_HALO, :] = jnp.zeros((CONV_HALO, BR_W), F32)

        x_t = x_ref[...]
        shift1, scale1, gate1 = mod_ref[0, 0:1, :], mod_ref[0, 1:2, :], mod_ref[0, 2:3, :]
        _, _, h = _rms_mod(x_t, gmix_ref[...], shift1, scale1)
        hb = h.astype(BF16)
        h_ref[...] = hb
        def project(j):
            zj = _dot(hb, win_v[j])
            z_ref[:, j * blk:(j + 1) * blk] = zj.astype(BF16)
            return zj

        z0, z1 = project(0), project(1)
        u = z0[:, 0:BR_W]
        v = jnp.concatenate([z0[:, BR_W:blk], z1[:, 0:2 * BR_W - blk]], axis=1)
        xb = z1[:, 2 * BR_W - blk:blk]

        gu = _gelu(u)[0]
        gv = _gelu(v)[0]
        mu = jnp.mean(gv, axis=-1, keepdims=True)
        cen = gv - mu
        rs = lax.rsqrt(jnp.mean(cen * cen, axis=-1, keepdims=True) + EPS)
        vn = (cen * rs) * lng_ref[...] + lnb_ref[...]
        mask = _tril_mask()
        wmask_b = [(ws_ref[hh] * mask).astype(BF16) for hh in range(HEADS)]
        s = _gmlp_s(vn.astype(BF16), wmask_b, bexp_ref[...], s_scr, tm)
        oa = (gu * s).astype(BF16)
        ya = _dot(oa, wpa_v[...])

        xbext[POOL_HALO:POOL_HALO + tm, :] = xb
        pos = tile_in_seq * tm + lax.broadcasted_iota(jnp.int32, (tm, 1), 0)
        ps = _pool_p(xb, xbext, pos, tm)
        qs = [_dot(ps[g].astype(BF16), pw_ref[g].astype(BF16)) for g in range(len(POOL_WINDOWS))]
        ob = (jnp.concatenate(qs, axis=1) * ps_ref[...]).astype(BF16)
        yb = _dot(ob, wpb_v[...])
        xbext[0:POOL_HALO, :] = xbext[tm:tm + POOL_HALO, :]

        z2, z3 = project(2), project(3)
        bg = z2[:, 0:BR_W]
        cg = jnp.concatenate([z2[:, BR_W:blk], z3[:, 0:2 * BR_W - blk]], axis=1)
        hc = z3[:, 2 * BR_W - blk:blk]
        zz = cg * hc
        zcext[CONV_HALO:CONV_HALO + tm, :] = zz
        yconv = (cw_ref[0:1, :] * zcext[pl.ds(CONV_HALO - 2, tm), :] + cw_ref[1:2, :] * zcext[pl.ds(CONV_HALO - 1, tm), :]
                 + cw_ref[2:3, :] * zz)
        oc = (bg * yconv).astype(BF16)
        yc = _dot(oc, wpc_v[...])
        zcext[0:CONV_HALO, :] = zcext[tm:tm + CONV_HALO, :]

        ocat_ref[:, 0:BR_W] = oa
        ocat_ref[:, BR_W:2 * BR_W] = ob
        ocat_ref[:, 2 * BR_W:3 * BR_W] = oc
        ycat_ref[:, 0:d] = ya.astype(BF16)
        ycat_ref[:, d:2 * d] = yb.astype(BF16)
        ycat_ref[:, 2 * d:3 * d] = yc.astype(BF16)

        ys = (ya, yb, yc)
        zg = jnp.concatenate([project(j).astype(BF16) for j in range(4, N_DEV)], axis=1)
        mb = _sigmoid(zg[:, 0:d]) * ys[0].astype(BF16)
        for n in range(1, 3):
            mb = mb + _sigmoid(zg[:, n * d:(n + 1) * d]) * ys[n].astype(BF16)
        mrg_ref[...] = mb
        mo = _dot(mb, wo_v[...])
        mo_ref[...] = mo.astype(BF16)
        xmid_ref[...] = x_t + gate1 * mo

    tok = lambda cols: pl.BlockSpec((tm, cols), lambda i: (i, 0))
    in_specs = [
        tok(d),
        _mod_spec(mod, layer, n_seq_tiles),
        *[_layer_spec(a, layer) for a in (g_mix, ln_g, ln_b, w_s, bexp, pool_w, pool_scale, conv_w)],
        ANY, ANY, ANY, ANY, ANY, ANY,
    ]
    out_shape = [
        jax.ShapeDtypeStruct((t_all, d), BF16),
        jax.ShapeDtypeStruct((t_all, N_DEV * blk), BF16),
        jax.ShapeDtypeStruct((t_all, 3 * d), BF16),
        jax.ShapeDtypeStruct((t_all, 3 * BR_W), BF16),
        jax.ShapeDtypeStruct((t_all, d), BF16),
        jax.ShapeDtypeStruct((t_all, d), BF16),
        jax.ShapeDtypeStruct((t_all, d), F32),
    ]
    out_specs = [tok(d), tok(N_DEV * blk), tok(3 * d), tok(3 * BR_W), tok(d), tok(d), tok(d)]
    scratch = [
        pltpu.VMEM((N_DEV, d, blk), BF16), pltpu.VMEM((BR_W, d), BF16), pltpu.VMEM((BR_W, d), BF16),
        pltpu.VMEM((BR_W, d), BF16), pltpu.VMEM((d, d), BF16),
        pltpu.VMEM((tm + POOL_HALO, BR_W), F32), pltpu.VMEM((tm + CONV_HALO, BR_W), F32), pltpu.VMEM((tm, BR_W), F32),
        pltpu.SemaphoreType.DMA((2 + 3 * N_DEV,)),
    ]
    return pl.pallas_call(
        body, name=f"mixer_fwd_{layer}", grid=(t_all // tm,), in_specs=in_specs, out_specs=out_specs, out_shape=out_shape,
        scratch_shapes=scratch, compiler_params=_cparams(dimension_semantics=("arbitrary",)),
    )(x, mod, g_mix, ln_g, ln_b, w_s, bexp, pool_w, pool_scale, conv_w, win, wpa, wpb, wpc, wo, dep)


def _ffn_fwd(x, mod, g_ffn, w13, w2, dep, layer, seq):
    t_all, d = x.shape
    tm = min(TOKEN_TILE, seq)
    n_seq_tiles = seq // tm
    fb = w13.shape[-1]
    n_hid = N_DEV // 2

    def body(x_ref, mod_ref, g_ref, w13_hbm, w2_hbm, dep_ref, h_ref, ab_ref, hid_ref, f_ref, xo_ref, w13_v, w2_v, sem):
        i = pl.program_id(0)
        _load_weights(i, [(w13_hbm, w13_v), (w2_hbm, w2_v)], sem)
        x_t = x_ref[...]
        shift2, scale2, gate2 = mod_ref[0, 3:4, :], mod_ref[0, 4:5, :], mod_ref[0, 5:6, :]
        _, _, h = _rms_mod(x_t, g_ref[...], shift2, scale2)
        hb = h.astype(BF16)
        h_ref[...] = hb
        f = jnp.zeros((tm, d), F32)
        for k in range(n_hid):
            a = _dot(hb, w13_v[k])
            b = _dot(hb, w13_v[n_hid + k])
            a, b = a.astype(BF16), b.astype(BF16)
            ab_ref[k] = a
            ab_ref[n_hid + k] = b
            hid = (a * _sigmoid(a)) * b
            hid_ref[k] = hid
            f = f + _dot(hid, w2_v[k * fb:(k + 1) * fb, :])
        f_ref[...] = f.astype(BF16)
        xo_ref[...] = x_t + gate2 * f

    tok = lambda cols: pl.BlockSpec((tm, cols), lambda i: (i, 0))
    blk3 = lambda n: pl.BlockSpec((n, tm, fb), lambda i: (0, i, 0))
    in_specs = [tok(d), _mod_spec(mod, layer, n_seq_tiles), _layer_spec(g_ffn, layer), ANY, ANY, ANY]
    out_shape = [
        jax.ShapeDtypeStruct((t_all, d), BF16),
        jax.ShapeDtypeStruct((N_DEV, t_all, fb), BF16),
        jax.ShapeDtypeStruct((n_hid, t_all, fb), BF16),
        jax.ShapeDtypeStruct((t_all, d), BF16),
        jax.ShapeDtypeStruct((t_all, d), F32),
    ]
    out_specs = [tok(d), blk3(N_DEV), blk3(n_hid), tok(d), tok(d)]
    scratch = [pltpu.VMEM((N_DEV, d, fb), BF16), pltpu.VMEM((n_hid * fb, d), BF16), pltpu.SemaphoreType.DMA((2,))]
    return pl.pallas_call(
        body, name=f"ffn_fwd_{layer}", grid=(t_all // tm,), in_specs=in_specs, out_specs=out_specs, out_shape=out_shape,
        scratch_shapes=scratch, compiler_params=_cparams(dimension_semantics=("arbitrary",)),
    )(x, mod, g_ffn, w13, w2, dep)


def _final_loss(x, g_final, target, seq):
    t_all, d = x.shape
    tm = min(ELEMENTWISE_TILE, seq)

    def body(x_ref, g_ref, t_ref, dx_ref, loss_ref, dg_ref):
        i = pl.program_id(0)

        @pl.when(i == 0)
        def _():
            loss_ref[...] = jnp.zeros(loss_ref.shape, F32)
            dg_ref[...] = jnp.zeros(dg_ref.shape, F32)

        x_t = x_ref[...]
        g = g_ref[...]
        rstd = lax.rsqrt(jnp.mean(x_t * x_t, axis=-1, keepdims=True) + EPS)
        xn = x_t * rstd
        err = xn * g - t_ref[...]
        loss_ref[0:1, :] += _colsum(err * err) * (0.5 / d)
        dy = err * (1.0 / d)
        dg_ref[0:1, :] += _colsum(dy * xn)
        dxn = dy * g
        dx_ref[...] = rstd * (dxn - xn * jnp.mean(dxn * xn, axis=-1, keepdims=True))

        @pl.when(i == pl.num_programs(0) - 1)
        def _():
            loss_ref[...] = jnp.broadcast_to(jnp.sum(loss_ref[0:1, :], axis=1, keepdims=True), loss_ref.shape)

    tok = pl.BlockSpec((tm, d), lambda i: (i, 0))
    acc = pl.BlockSpec((8, d), lambda i: (0, 0))
    return pl.pallas_call(
        body, name="final_loss", grid=(t_all // tm,),
        in_specs=[tok, pl.BlockSpec((1, d), lambda i: (0, 0)), tok], out_specs=[tok, acc, acc],
        out_shape=[jax.ShapeDtypeStruct((t_all, d), F32), jax.ShapeDtypeStruct((8, d), F32), jax.ShapeDtypeStruct((8, d), F32)],
        compiler_params=_cparams(dimension_semantics=("arbitrary",)),
    )(x, g_final, target)


def _ffn_bwd(dxo, xmid, ab, f, mod, g_ffn, w13, w2, dep, layer, seq):
    t_all, d = xmid.shape
    tm = min(TOKEN_TILE, seq)
    n_seq_tiles = seq // tm
    fb = w13.shape[-1]
    n_hid = N_DEV // 2

    def body(dxo_ref, x_ref, ab_ref, f_ref, mod_ref, g_ref, w13_hbm, w2_hbm, dep_ref,
             dx_ref, df_ref, dab_ref, dmod_ref, dg_ref, w13_v, w2_v, sem):
        i = pl.program_id(0)
        _load_weights(i, [(w13_hbm, w13_v), (w2_hbm, w2_v)], sem)

        @pl.when(i == 0)
        def _():
            dg_ref[...] = jnp.zeros(dg_ref.shape, F32)

        @pl.when(i % n_seq_tiles == 0)
        def _():
            dmod_ref[...] = jnp.zeros(dmod_ref.shape, F32)

        scale2, gate2 = mod_ref[0, 4:5, :], mod_ref[0, 5:6, :]
        g = g_ref[...]
        x_t = x_ref[...]
        rstd = lax.rsqrt(jnp.mean(x_t * x_t, axis=-1, keepdims=True) + EPS)
        xn = x_t * rstd
        dxo_t = dxo_ref[...]
        dmod_ref[0, 2:3, :] += _colsum(dxo_t * f_ref[...].astype(F32))
        dfb = (dxo_t * gate2).astype(BF16)
        df_ref[...] = dfb
        dh = jnp.zeros((tm, d), F32)
        for k in range(n_hid):
            dhid = _dot_nt(dfb, w2_v[k * fb:(k + 1) * fb, :]).astype(BF16)
            a = ab_ref[k]
            b = ab_ref[n_hid + k]
            sg = _sigmoid(a)
            da = dhid * b * (sg * (1.0 + a * (1.0 - sg)))
            db = dhid * (a * sg)
            dab_ref[k] = da
            dab_ref[n_hid + k] = db
            dh = dh + _dot_nt(da, w13_v[k]) + _dot_nt(db, w13_v[n_hid + k])
        dx, dshift, dscale, dg = _rms_mod_bwd(dh, xn, rstd, g, scale2)
        dmod_ref[0, 0:1, :] += dshift
        dmod_ref[0, 1:2, :] += dscale
        dg_ref[0:1, :] += dg
        dx_ref[...] = dxo_t + dx

    tok = lambda cols: pl.BlockSpec((tm, cols), lambda i: (i, 0))
    blk3 = lambda n: pl.BlockSpec((n, tm, fb), lambda i: (0, i, 0))
    modspec = pl.BlockSpec((1, 8, d), lambda i: (i // n_seq_tiles, 0, 0))
    in_specs = [tok(d), tok(d), blk3(N_DEV), tok(d), _mod_spec(mod, layer, n_seq_tiles), _layer_spec(g_ffn, layer), ANY, ANY, ANY]
    out_shape = [
        jax.ShapeDtypeStruct((t_all, d), F32), jax.ShapeDtypeStruct((t_all, d), BF16),
        jax.ShapeDtypeStruct((N_DEV, t_all, fb), BF16), jax.ShapeDtypeStruct(mod.shape[1:], F32),
        jax.ShapeDtypeStruct((8, d), F32),
    ]
    out_specs = [tok(d), tok(d), blk3(N_DEV), modspec, pl.BlockSpec((8, d), lambda i: (0, 0))]
    scratch = [pltpu.VMEM((N_DEV, d, fb), BF16), pltpu.VMEM((n_hid * fb, d), BF16), pltpu.SemaphoreType.DMA((2,))]
    return pl.pallas_call(
        body, name=f"ffn_bwd_{layer}", grid=(t_all // tm,), in_specs=in_specs, out_specs=out_specs, out_shape=out_shape,
        scratch_shapes=scratch, compiler_params=_cparams(dimension_semantics=("arbitrary",)),
    )(dxo, xmid, ab, f, mod, g_ffn, w13, w2, dep)


def _mixer_bwd(dxm, x, z, ycat, mo, mod, g_mix, ln_g, ln_b, w_s, bexp, pool_w, pool_scale, conv_w,
               win, wpa, wpb, wpc, wo, dep, layer, seq):
    t_all, d = x.shape
    tm = min(TOKEN_TILE, seq)
    n_seq_tiles = seq // tm
    blk = win.shape[-1]
    n_win = len(POOL_WINDOWS)

    def tile_of(i):
        return (i // n_seq_tiles) * n_seq_tiles + (n_seq_tiles - 1 - i % n_seq_tiles)

    def halo_row_block(i):
        return jnp.maximum(tile_of(i) * (tm // POOL_HALO) - 1, 0)

    def body(dxm_ref, x_ref, z_ref, zpb_ref, zpc_ref, ycat_ref, mo_ref, mod_ref, gmix_ref, lng_ref, lnb_ref, ws_ref,
             bexp_ref, pw_ref, ps_ref, cw_ref, win_hbm, wpa_hbm, wpb_hbm, wpc_hbm, wo_hbm, dep_ref,
             dx_ref, dz_ref, dycat_ref, dmo_ref, dmod_ref, dg_ref, sm_ref, dws_ref, dssum_ref, dpw_ref,
             win_v, wpa_v, wpb_v, wpc_v, wo_v, xbext, zzext, rext, dyext, s_scr, dvn_scr, sem):
        i = pl.program_id(0)
        pairs = [(win_hbm.at[j], win_v.at[:, pl.ds(blk * j, blk)]) for j in range(N_DEV)] + [(wo_hbm, wo_v)]
        pairs += _wp_pairs(wpa_hbm, wpa_v) + _wp_pairs(wpb_hbm, wpb_v) + _wp_pairs(wpc_hbm, wpc_v)
        _load_weights(i, pairs, sem)
        tile_in_seq = n_seq_tiles - 1 - i % n_seq_tiles
        first_of_seq = tile_in_seq == 0

        @pl.when(i == 0)
        def _():
            for r in (dg_ref, sm_ref, dws_ref, dssum_ref, dpw_ref):
                r[...] = jnp.zeros(r.shape, F32)

        @pl.when(i % n_seq_tiles == 0)
        def _():
            dmod_ref[...] = jnp.zeros(dmod_ref.shape, F32)
            rext[tm:tm + POOL_HALO, :] = jnp.zeros((POOL_HALO, BR_W), F32)
            dyext[tm:tm + CONV_HALO, :] = jnp.zeros((CONV_HALO, BR_W), F32)

        shift1, scale1, gate1 = mod_ref[0, 0:1, :], mod_ref[0, 1:2, :], mod_ref[0, 2:3, :]
        g = gmix_ref[...]
        x_t = x_ref[...]
        rstd = lax.rsqrt(jnp.mean(x_t * x_t, axis=-1, keepdims=True) + EPS)
        xn = x_t * rstd
        dxm_t = dxm_ref[...]
        dmod_ref[0, 2:3, :] += _colsum(dxm_t * mo_ref[...].astype(F32))
        dmo = (dxm_t * gate1).astype(BF16)
        dmo_ref[...] = dmo
        dmerged = _dot_nt(dmo, wo_v[...])

        dh_parts = []

        def emit_dz(lo, hi, value):
            vb = value.astype(BF16)
            dz_ref[:, lo:hi] = vb
            dh_parts.append(_dot_nt(vb, win_v[:, lo:hi]))

        dys = []
        dmerged = dmerged.astype(BF16)
        for n in range(3):
            gt = _sigmoid(z_ref[:, 3 * d + n * d:3 * d + (n + 1) * d])
            dyn = dmerged * gt
            emit_dz(3 * d + n * d, 3 * d + (n + 1) * d, dyn * ycat_ref[:, n * d:(n + 1) * d] * (1.0 - gt))
            dycat_ref[:, n * d:(n + 1) * d] = dyn
            dys.append(dyn)

        doa = _dot_nt(dys[0], wpa_v[...])
        u = z_ref[:, 0:BR_W].astype(F32)
        v = z_ref[:, BR_W:2 * BR_W].astype(F32)
        gu, tu, u2 = _gelu(u)
        gv, tv, v2 = _gelu(v)
        mu = jnp.mean(gv, axis=-1, keepdims=True)
        cen = gv - mu
        rs = lax.rsqrt(jnp.mean(cen * cen, axis=-1, keepdims=True) + EPS)
        vhat = cen * rs
        lng = lng_ref[...]
        vn_b = (vhat * lng + lnb_ref[...]).astype(BF16)
        mask = _tril_mask()
        wmask = [ws_ref[hh] * mask for hh in range(HEADS)]
        s = _gmlp_s(vn_b, [w.astype(BF16) for w in wmask], bexp_ref[...], s_scr, tm)
        du = (doa * s) * _gelu_grad(u, tu, u2)
        ds = doa * gu
        ds_b = ds.astype(BF16)
        dssum = jnp.zeros((CHUNK, BR_W), F32)
        for ch in range(tm // CHUNK):
            rows = slice(ch * CHUNK, (ch + 1) * CHUNK)
            dssum = dssum + ds[rows, :]
            for hh in range(HEADS):
                cols = slice(hh * HEAD_DIM, (hh + 1) * HEAD_DIM)
                dvn_scr[rows, cols] = _dot_tn(wmask[hh].astype(BF16), ds_b[rows, cols])
                dws_ref[hh] += _dot_nt(ds_b[rows, cols], vn_b[rows, cols]) * mask
        dssum_ref[...] += dssum
        dvn = dvn_scr[...]
        sm_ref[0:1, :] += _colsum(dvn * vhat)
        sm_ref[1:2, :] += _colsum(dvn)
        dvhat = dvn * lng
        dgv = rs * (dvhat - jnp.mean(dvhat, axis=-1, keepdims=True) - vhat * jnp.mean(dvhat * vhat, axis=-1, keepdims=True))
        dv = dgv * _gelu_grad(v, tv, v2)
        emit_dz(0, 2 * BR_W, jnp.concatenate([du, dv], axis=1))

        dob = _dot_nt(dys[1], wpb_v[...])
        xb = z_ref[:, 2 * BR_W:3 * BR_W].astype(F32)
        xbext[0:POOL_HALO, :] = jnp.where(first_of_seq, 0.0, zpb_ref[...].astype(F32))
        xbext[POOL_HALO:POOL_HALO + tm, :] = xb
        pos = tile_in_seq * tm + lax.broadcasted_iota(jnp.int32, (tm, 1), 0)
        ps = _pool_p(xb, xbext, pos, tm)
        scale_b = ps_ref[...]
        dq = dob * scale_b
        qs, dps = [], []
        for gi, win_len in enumerate(POOL_WINDOWS):
            cols = slice(gi * HEAD_DIM, (gi + 1) * HEAD_DIM)
            pw_b = pw_ref[gi].astype(BF16)
            p_b = ps[gi].astype(BF16)
            dq_b = dq[:, cols].astype(BF16)
            qs.append(_dot(p_b, pw_b))
            dpw_ref[gi] += _dot_tn(p_b, dq_b)
            dp = _dot_nt(dq_b, pw_b)
            dps.append(dp)
            rext[0:tm, cols] = dp * _inv_count(pos, win_len)
        sm_ref[2:3, :] += _colsum(dob * jnp.concatenate(qs, axis=1))
        dxbs = [acc - dp for acc, dp in zip(_window_sums(rext, tm, False), dps)]
        emit_dz(2 * BR_W, 3 * BR_W, jnp.concatenate(dxbs, axis=1))
        rext[tm:tm + POOL_HALO, :] = rext[0:POOL_HALO, :]

        doc = _dot_nt(dys[2], wpc_v[...])
        bg = z_ref[:, 3 * BR_W:4 * BR_W].astype(F32)
        cg = z_ref[:, 4 * BR_W:5 * BR_W].astype(F32)
        hc = z_ref[:, 5 * BR_W:6 * BR_W].astype(F32)
        zz = cg * hc
        zprev = zpc_ref[POOL_HALO - CONV_HALO:POOL_HALO, :].astype(F32)
        zzext[0:CONV_HALO, :] = jnp.where(first_of_seq, 0.0, zprev[:, 0:BR_W] * zprev[:, BR_W:2 * BR_W])
        zzext[CONV_HALO:CONV_HALO + tm, :] = zz
        zm2 = zzext[pl.ds(CONV_HALO - 2, tm), :]
        zm1 = zzext[pl.ds(CONV_HALO - 1, tm), :]
        w0, w1, w2c = cw_ref[0:1, :], cw_ref[1:2, :], cw_ref[2:3, :]
        yconv = w0 * zm2 + w1 * zm1 + w2c * zz
        dyc = doc * bg
        sm_ref[3:4, :] += _colsum(dyc * zm2)
        sm_ref[4:5, :] += _colsum(dyc * zm1)
        sm_ref[5:6, :] += _colsum(dyc * zz)
        dyext[0:tm, :] = dyc
        dzz = w2c * dyc + w1 * dyext[pl.ds(1, tm), :] + w0 * dyext[pl.ds(2, tm), :]
        dyext[tm:tm + CONV_HALO, :] = dyext[0:CONV_HALO, :]
        emit_dz(3 * BR_W, 6 * BR_W, jnp.concatenate([doc * yconv, dzz * hc, dzz * cg], axis=1))

        dh = dh_parts[0]
        for part in dh_parts[1:]:
            dh = dh + part
        dx, dshift, dscale, dg = _rms_mod_bwd(dh, xn, rstd, g, scale1)
        dmod_ref[0, 0:1, :] += dshift
        dmod_ref[0, 1:2, :] += dscale
        dg_ref[0:1, :] += dg
        dx_ref[...] = dxm_t + dx

    tok = lambda cols: pl.BlockSpec((tm, cols), lambda i: (tile_of(i), 0))
    modspec = pl.BlockSpec((1, 8, d), lambda i: (i // n_seq_tiles, 0, 0))
    in_specs = [
        tok(d), tok(d), tok(N_DEV * blk),
        pl.BlockSpec((POOL_HALO, BR_W), lambda i: (halo_row_block(i), 2)),
        pl.BlockSpec((POOL_HALO, 2 * BR_W), lambda i: (halo_row_block(i), 2)),
        tok(3 * d), tok(d), _mod_spec(mod, layer, n_seq_tiles),
        *[_layer_spec(a, layer) for a in (g_mix, ln_g, ln_b, w_s, bexp, pool_w, pool_scale, conv_w)],
        ANY, ANY, ANY, ANY, ANY, ANY,
    ]
    acc = lambda shape: pl.BlockSpec(shape, lambda i: (0,) * len(shape))
    out_shape = [
        jax.ShapeDtypeStruct((t_all, d), F32), jax.ShapeDtypeStruct((t_all, N_DEV * blk), BF16),
        jax.ShapeDtypeStruct((t_all, 3 * d), BF16), jax.ShapeDtypeStruct((t_all, d), BF16),
        jax.ShapeDtypeStruct(mod.shape[1:], F32), jax.ShapeDtypeStruct((8, d), F32), jax.ShapeDtypeStruct((8, BR_W), F32),
        jax.ShapeDtypeStruct((HEADS, CHUNK, CHUNK), F32), jax.ShapeDtypeStruct((CHUNK, BR_W), F32),
        jax.ShapeDtypeStruct((n_win, HEAD_DIM, HEAD_DIM), F32),
    ]
    out_specs = [tok(d), tok(N_DEV * blk), tok(3 * d), tok(d), modspec, acc((8, d)), acc((8, BR_W)),
                 acc((HEADS, CHUNK, CHUNK)), acc((CHUNK, BR_W)), acc((n_win, HEAD_DIM, HEAD_DIM))]
    scratch = [
        pltpu.VMEM((d, N_DEV * blk), BF16), pltpu.VMEM((BR_W, d), BF16), pltpu.VMEM((BR_W, d), BF16),
        pltpu.VMEM((BR_W, d), BF16), pltpu.VMEM((d, d), BF16),
        pltpu.VMEM((tm + POOL_HALO, BR_W), F32), pltpu.VMEM((tm + CONV_HALO, BR_W), F32),
        pltpu.VMEM((tm + POOL_HALO, BR_W), F32), pltpu.VMEM((tm + CONV_HALO, BR_W), F32),
        pltpu.VMEM((tm, BR_W), F32), pltpu.VMEM((tm, BR_W), F32),
        pltpu.SemaphoreType.DMA((1 + 4 * N_DEV,)),
    ]
    return pl.pallas_call(
        body, name=f"mixer_bwd_{layer}", grid=(t_all // tm,), in_specs=in_specs, out_specs=out_specs, out_shape=out_shape,
        scratch_shapes=scratch, compiler_params=_cparams(dimension_semantics=("arbitrary",)),
    )(dxm, x, z, z, z, ycat, mo, mod, g_mix, ln_g, ln_b, w_s, bexp, pool_w, pool_scale, conv_w, win, wpa, wpb, wpc, wo, dep)


def _wgrad(a, b, a_spec, b_spec, out_struct, out_spec, grid_kn, tk, tn, split, name):
    t_all = a.shape[-2]
    tt = min(WGRAD_TOKENS, t_all)
    n_t = t_all // tt

    def body(a_ref, b_ref, o_ref, acc):
        t = pl.program_id(2)

        @pl.when(t == 0)
        def _():
            acc[...] = jnp.zeros(acc.shape, F32)

        acc[...] += _dot_tn(a_ref[...], b_ref[...])

        @pl.when(t == n_t - 1)
        def _():
            if split:
                for j in range(split):
                    w = tn // split
                    o_ref[j] = acc[:, j * w:(j + 1) * w].astype(o_ref.dtype)
            else:
                o_ref[...] = acc[...].astype(o_ref.dtype)

    return pl.pallas_call(
        body, name=name, grid=(*grid_kn, n_t), in_specs=[a_spec(tt), b_spec(tt)], out_specs=out_spec, out_shape=out_struct,
        scratch_shapes=[pltpu.VMEM((tk, tn), F32)],
        compiler_params=_cparams(dimension_semantics=("arbitrary", "arbitrary", "arbitrary")),
    )(a, b)


def _mixer_wgrads(h, dz, mrg, dmo, ocat, dycat, layer):
    d = h.shape[1]
    blk = dz.shape[1] // N_DEV
    g_win = _wgrad(
        h, dz, lambda tt: pl.BlockSpec((tt, d), lambda k, n, t: (t, 0)), lambda tt: pl.BlockSpec((tt, blk), lambda k, n, t: (t, n)),
        jax.ShapeDtypeStruct((N_DEV, d, blk), BF16), pl.BlockSpec((None, d, blk), lambda k, n, t: (n, 0, 0)),
        (1, N_DEV), d, blk, 0, f"wgrad_in_{layer}")
    g_wo = _wgrad(
        mrg, dmo, lambda tt: pl.BlockSpec((tt, d), lambda k, n, t: (t, 0)), lambda tt: pl.BlockSpec((tt, d), lambda k, n, t: (t, 0)),
        jax.ShapeDtypeStruct((d, d), BF16), pl.BlockSpec((d, d), lambda k, n, t: (0, 0)), (1, 1), d, d, 0, f"wgrad_o_{layer}")
    g_wp = []
    for n, nm in enumerate("abc"):
        g_wp.append(_wgrad(
            ocat, dycat, lambda tt, n=n: pl.BlockSpec((tt, BR_W), lambda k, nn, t: (t, n)),
            lambda tt, n=n: pl.BlockSpec((tt, d), lambda k, nn, t: (t, n)),
            jax.ShapeDtypeStruct((N_DEV, BR_W, d // N_DEV), BF16),
            pl.BlockSpec((N_DEV, BR_W, d // N_DEV), lambda k, nn, t: (0, 0, 0)), (1, 1), BR_W, d, N_DEV, f"wgrad_p{nm}_{layer}"))
    return [g_win, g_wo.reshape(N_DEV, d // N_DEV, d), *g_wp]


def _ffn_wgrads(h2, dab, hid, df, layer):
    d = h2.shape[1]
    fb = dab.shape[-1]
    n_hid = N_DEV // 2
    g_w13 = _wgrad(
        dab, h2, lambda tt: pl.BlockSpec((None, tt, fb), lambda k, n, t: (k, t, 0)),
        lambda tt: pl.BlockSpec((tt, d), lambda k, n, t: (t, 0)),
        jax.ShapeDtypeStruct((N_DEV, fb, d), BF16), pl.BlockSpec((None, fb, d), lambda k, n, t: (k, 0, 0)),
        (N_DEV, 1), fb, d, 0, f"wgrad_13_{layer}")
    g_w2 = _wgrad(
        hid, df, lambda tt: pl.BlockSpec((None, tt, fb), lambda k, n, t: (k, t, 0)),
        lambda tt: pl.BlockSpec((tt, d), lambda k, n, t: (t, 0)),
        jax.ShapeDtypeStruct((n_hid * fb, d), BF16), pl.BlockSpec((fb, d), lambda k, n, t: (k, 0)),
        (n_hid, 1), fb, d, 0, f"wgrad_2_{layer}")
    return [g_w13, g_w2.reshape(N_DEV, fb // 2, d)]


def _remote(src, dst, send_sem, recv_sem, dev):
    return pltpu.make_async_remote_copy(src_ref=src, dst_ref=dst, send_sem=send_sem, recv_sem=recv_sem, device_id=dev,
                                        device_id_type=MESH_ID)


HBM_SPEC = pl.BlockSpec(memory_space=pltpu.HBM)
SEM_SPEC = pl.BlockSpec(memory_space=pltpu.SEMAPHORE)
DATAFLOW = pltpu.SideEffectType.DATAFLOW_SIDE_EFFECTING
GATHER, SCATTER = "gather", "scatter"


def _xfer_sem(a, k):
    return a * (N_DEV - 1) + k - 1


def _gather_zones(weights, layer):
    me = _my_index()
    return [lax.dynamic_update_index_in_dim(lax.empty((N_DEV, *w.shape[1:]), BF16), w[layer].astype(BF16)[None], me, 0)
            for w in weights]


def _scatter_zones(partials):
    return [lax.empty(p.shape, p.dtype) for p in partials]


def _xfer_src(kind, src, land, a, me, idx):
    return land[a].at[me] if kind == GATHER else src[a].at[idx]


def _xfer_start(srcs, lands, kind, name, after):
    n, n_src = len(lands), len(srcs)

    def body(*refs):
        src, land = refs[:n_src], refs[n_src:n_src + n]
        send_sems, recv_sems = refs[n_src + n + 1], refs[n_src + n + 2]
        token = refs[-1]
        me = _my_index()
        for k in range(1, N_DEV):
            dev, idx = _peer(k)
            for a in range(n):
                q = _xfer_sem(a, k)
                _remote(_xfer_src(kind, src, land, a, me, idx), land[a].at[me], send_sems.at[q], recv_sems.at[q], dev).start()
        token[...] = jnp.zeros(token.shape, token.dtype)

    both = [*srcs, *lands]
    sems = pltpu.SemaphoreType.DMA((n * (N_DEV - 1),))
    out_shape = (sems, sems, *[pltpu.HBM(a.shape, a.dtype) for a in both], jax.ShapeDtypeStruct((8, 128), F32))
    outs = pl.pallas_call(
        body, name=name, in_specs=[HBM_SPEC] * len(both) + [ANY],
        out_specs=(SEM_SPEC, SEM_SPEC, *[HBM_SPEC] * len(both), VMEM_SPEC),
        out_shape=out_shape, input_output_aliases={i: 2 + i for i in range(len(both))},
        compiler_params=pltpu.CompilerParams(has_side_effects=DATAFLOW),
    )(*[pltpu.with_memory_space_constraint(a, pltpu.HBM) for a in both], after)
    return outs[0], outs[1], list(outs[2:2 + n_src]), list(outs[2 + n_src:2 + n_src + n]), outs[-1]


def _gather_two_level(zones, name, after):
    n = len(zones)

    def body(*refs):
        out = refs[n + 1:2 * n + 1]
        send_sems, recv_sems = refs[2 * n + 1], refs[2 * n + 2]
        x, y, c = lax.axis_index("x"), lax.axis_index("y"), lax.axis_index("c")
        index = lambda px, py, pc: 4 * px + 2 * py + pc
        me, sibling = (x, y, c), (x, y, 1 - c)
        chips = [(1 - x, y), (x, 1 - y), (1 - x, 1 - y)]

        def copy(a, k, block, to):
            rows = out[a].at[index(*block)]
            return _remote(rows, rows, send_sems.at[a * (N_DEV - 1) + k], recv_sems.at[a * (N_DEV - 1) + k], to)

        first = [copy(a, 0, me, sibling) for a in range(n)]
        first += [copy(a, 1 + j, me, (*chip, c)) for j, chip in enumerate(chips) for a in range(n)]
        for cp in first:
            cp.start()
        passed = []
        for j, chip in enumerate(chips):
            for a in range(n):
                copy(a, 1 + j, (*chip, c), me).wait_recv()
                passed.append(copy(a, 4 + j, (*chip, c), sibling))
                passed[-1].start()
        for a in range(n):
            copy(a, 0, sibling, me).wait_recv()
            for j, chip in enumerate(chips):
                copy(a, 4 + j, (*chip, 1 - c), me).wait_recv()
        for cp in first + passed:
            cp.wait_send()

    sems = pltpu.SemaphoreType.DMA((n * (N_DEV - 1),))
    return pl.pallas_call(
        body, name=name, in_specs=[ANY] * (n + 1), out_specs=[ANY] * n, out_shape=[jax.ShapeDtypeStruct(z.shape, z.dtype) for z in zones],
        input_output_aliases={i: i for i in range(n)}, scratch_shapes=[sems, sems],
        compiler_params=pltpu.CompilerParams(has_side_effects=True),
    )(*zones, after)


def _xfer_wait(send_sems, recv_sems, srcs, lands, rows, after, kind, name):
    n, n_src = len(lands), len(srcs)

    def body(*refs):
        src, land = refs[:n_src], refs[n_src:n_src + n]
        send_sems, recv_sems = refs[n_src + n], refs[n_src + n + 1]
        me = _my_index()
        for k in range(1, N_DEV):
            dev, idx = _peer(k)
            for a in range(n):
                q = _xfer_sem(rows[a], k)
                cp = _remote(_xfer_src(kind, src, land, a, me, idx), land[a].at[idx], send_sems.at[q], recv_sems.at[q], dev)
                cp.wait_send()
                cp.wait_recv()

    both = [*srcs, *lands]
    outs = pl.pallas_call(
        body, name=name, in_specs=[HBM_SPEC] * len(both) + [SEM_SPEC, SEM_SPEC, ANY], out_specs=[HBM_SPEC] * len(both),
        out_shape=[pltpu.HBM(a.shape, a.dtype) for a in both], input_output_aliases={i: i for i in range(len(both))},
        compiler_params=pltpu.CompilerParams(has_side_effects=DATAFLOW),
    )(*both, send_sems, recv_sems, after)
    return list(outs[:n_src]), list(outs[n_src:])


def _pre(c_pad, conv_pad, w_mod, b_mod_mine):
    n_layers, d, blk = w_mod.shape

    def body(c_ref, conv_ref, wmod_ref, bmod_ref, cact_ref, mod_ref, convall_ref, cact_mine, msh, send_sems, recv_sems):
        me = _my_index()
        c = c_ref[...]
        cact_mine[...] = c * _sigmoid(c)
        cact_ref[me] = cact_mine[...]
        convall_ref[me] = conv_ref[...]
        sends = []
        for k in range(1, N_DEV):
            dev, _ = _peer(k)
            sends.append(_remote(cact_mine, cact_ref.at[me], send_sems.at[0, k - 1], recv_sems.at[0, k - 1], dev))
            sends.append(_remote(conv_ref, convall_ref.at[me], send_sems.at[1, k - 1], recv_sems.at[1, k - 1], dev))
        for cp in sends:
            cp.start()
        for k in range(1, N_DEV):
            dev, idx = _peer(k)
            _remote(cact_mine, cact_ref.at[idx], send_sems.at[0, k - 1], recv_sems.at[0, k - 1], dev).wait_recv()
            _remote(conv_ref, convall_ref.at[idx], send_sems.at[1, k - 1], recv_sems.at[1, k - 1], dev).wait_recv()
        for cp in sends:
            cp.wait_send()
        cact_b = cact_ref[...].reshape(N_DEV * 8, d).astype(BF16)
        for l in range(n_layers):
            m = _dot(cact_b, wmod_ref[l].astype(BF16)) + bmod_ref[l]
            msh[l] = m.reshape(N_DEV, 8, blk)
        mod_ref[me] = msh[:, me]
        sends = []
        for k in range(1, N_DEV):
            dev, idx = _peer(k)
            sends.append(_remote(msh.at[:, idx], mod_ref.at[me], send_sems.at[2, k - 1], recv_sems.at[2, k - 1], dev))
        for cp in sends:
            cp.start()
        for k in range(1, N_DEV):
            dev, idx = _peer(k)
            _remote(msh.at[:, idx], mod_ref.at[idx], send_sems.at[2, k - 1], recv_sems.at[2, k - 1], dev).wait_recv()
        for cp in sends:
            cp.wait_send()

    out_shape = [jax.ShapeDtypeStruct((N_DEV, 8, d), F32), jax.ShapeDtypeStruct((N_DEV, n_layers, 8, blk), F32),
                 jax.ShapeDtypeStruct((N_DEV, *conv_pad.shape), F32)]
    return pl.pallas_call(
        body, name="pre", in_specs=[VMEM_SPEC] * 4, out_specs=[VMEM_SPEC] * 3, out_shape=out_shape,
        scratch_shapes=[pltpu.VMEM((8, d), F32), pltpu.VMEM((n_layers, N_DEV, 8, blk), F32),
                        pltpu.SemaphoreType.DMA((3, N_DEV - 1)), pltpu.SemaphoreType.DMA((3, N_DEV - 1))],
        compiler_params=_cparams(has_side_effects=True),
    )(c_pad, conv_pad, w_mod, b_mod_mine)


def _small(buf, dmod_blocks, cact_all):
    rows = buf.shape[0]
    seg = rows // N_DEV
    _, n_layers, _, blk = dmod_blocks.shape
    d = cact_all.shape[-1]

    def body(buf_ref, dmod_ref, cact_ref, out_ref, gw_ref, rs_recv, red, drecv, send_sems, recv_sems):
        me = _my_index()
        mine = pl.ds(pl.multiple_of(me * seg, 8), seg)
        sends = []
        for k in range(1, N_DEV):
            dev, idx = _peer(k)
            theirs = pl.ds(pl.multiple_of(idx * seg, 8), seg)
            sends.append(_remote(buf_ref.at[theirs], rs_recv.at[k - 1], send_sems.at[0, k - 1], recv_sems.at[0, k - 1], dev))
            sends.append(_remote(dmod_ref.at[idx], drecv.at[me], send_sems.at[1, k - 1], recv_sems.at[1, k - 1], dev))
        for cp in sends:
            cp.start()
        drecv[me] = dmod_ref[me]
        for k in range(1, N_DEV):
            dev, idx = _peer(k)
            _remote(buf_ref.at[mine], rs_recv.at[k - 1], send_sems.at[0, k - 1], recv_sems.at[0, k - 1], dev).wait_recv()
            _remote(dmod_ref.at[idx], drecv.at[idx], send_sems.at[1, k - 1], recv_sems.at[1, k - 1], dev).wait_recv()
        for cp in sends:
            cp.wait_send()
        total = buf_ref[mine, :]
        for k in range(1, N_DEV):
            total = total + rs_recv[k - 1]
        red[...] = total
        out_ref[mine, :] = total
        sends = []
        for k in range(1, N_DEV):
            dev, _ = _peer(k)
            sends.append(_remote(red, out_ref.at[mine], send_sems.at[2, k - 1], recv_sems.at[2, k - 1], dev))
        for cp in sends:
            cp.start()
        cact_b = cact_ref[...].reshape(N_DEV * 8, d).astype(BF16)
        for l in range(n_layers):
            gw_ref[l] = _dot_tn(cact_b, drecv[:, l].reshape(N_DEV * 8, blk).astype(BF16))
        for k in range(1, N_DEV):
            dev, idx = _peer(k)
            theirs = pl.ds(pl.multiple_of(idx * seg, 8), seg)
            _remote(red, out_ref.at[theirs], send_sems.at[2, k - 1], recv_sems.at[2, k - 1], dev).wait_recv()
        for cp in sends:
            cp.wait_send()

    out_shape = [jax.ShapeDtypeStruct(buf.shape, F32), jax.ShapeDtypeStruct((n_layers, d, blk), F32)]
    return pl.pallas_call(
        body, name="small_allreduce", in_specs=[VMEM_SPEC] * 3, out_specs=[VMEM_SPEC] * 2, out_shape=out_shape,
        scratch_shapes=[pltpu.VMEM((N_DEV - 1, seg, 128), F32), pltpu.VMEM((seg, 128), F32),
                        pltpu.VMEM(dmod_blocks.shape, F32),
                        pltpu.SemaphoreType.DMA((3, N_DEV - 1)), pltpu.SemaphoreType.DMA((3, N_DEV - 1))],
        compiler_params=_cparams(has_side_effects=True),
    )(buf, dmod_blocks, cact_all)


def _adamw_math(w, g, m, v):
    m = ADAM_B1 * m + (1.0 - ADAM_B1) * g
    v = ADAM_B2 * v + (1.0 - ADAM_B2) * (g * g)
    m_hat = m / (1.0 - ADAM_B1 ** ADAM_STEP)
    v_hat = v / (1.0 - ADAM_B2 ** ADAM_STEP)
    delta = -ADAM_LR * (m_hat / (jnp.sqrt(v_hat) + ADAM_EPS) + ADAM_WD * w)
    return delta, m, v


def _adamw(parts, w, m, v, name, first=0, earlier=None, own=None):
    n_layers = len(parts)
    n_parts, rows, cols = parts[0].shape
    tr = rows
    while tr * cols * 4 > (1 << 20) and tr % 32 == 0:
        tr //= 2
    n_r = rows // tr
    n_earlier = 0 if earlier is None else 4
    n_own = 0 if own is None else n_layers
    n_in = n_layers + n_own + 3

    def body(me_ref, *refs):
        p_refs, o_refs = refs[:n_layers], refs[n_layers:n_layers + n_own]
        w_ref, m_ref, v_ref = refs[n_layers + n_own:n_in]
        g_out, d_out, m_out, v_out = refs[n_in + n_earlier:]
        layer = pl.program_id(0)

        def part(q, p):
            value = p_refs[q][p]
            if n_own:
                value = jnp.where(me_ref[0] == p, o_refs[q][...], value)
            return value.astype(F32)

        for q in range(n_layers):

            @pl.when(layer == q)
            def _(q=q):
                g = part(q, 0)
                for p in range(1, n_parts):
                    g = g + part(q, p)
                delta, m_new, v_new = _adamw_math(w_ref[...], g, m_ref[...], v_ref[...])
                g_out[...] = g
                d_out[...] = delta
                m_out[...] = m_new
                v_out[...] = v_new

    def moving(q, l, r):
        return jnp.where(l == q, r, jnp.where(l < q, 0, n_r - 1))

    parts_spec = lambda q: pl.BlockSpec((n_parts, tr, cols), lambda l, r, me: (0, moving(q, l, r), 0))
    own_spec = lambda q: pl.BlockSpec((None, tr, cols), lambda l, r, me: (me[0], moving(q, l, r), 0))
    spec = pl.BlockSpec((None, tr, cols), lambda l, r, me: (first + l, r, 0))
    out = jax.ShapeDtypeStruct(w.shape, F32)
    grid_spec = pltpu.PrefetchScalarGridSpec(
        num_scalar_prefetch=1, grid=(n_layers, n_r),
        in_specs=[parts_spec(q) for q in range(n_layers)] + [own_spec(q) for q in range(n_own)] + [spec, spec, spec]
        + [ANY] * n_earlier,
        out_specs=[spec] * 4)
    return pl.pallas_call(
        body, name=name, grid_spec=grid_spec, out_shape=[out] * 4,
        input_output_aliases={1 + n_in + i: i for i in range(n_earlier)},
        compiler_params=_cparams(dimension_semantics=("arbitrary", "arbitrary")),
    )(_my_index().astype(jnp.int32).reshape(1), *parts, *(own or ()), w, m, v, *(earlier or ()))


def _adamw_small(gs, ws, ms, vs):
    n = len(ws)

    def body(*refs):
        g, w, m, v = (refs[i * n:(i + 1) * n] for i in range(4))
        d_out, m_out, v_out = (refs[i * n:(i + 1) * n] for i in range(4, 7))
        for i in range(n):
            delta, m_new, v_new = _adamw_math(w[i][...], g[i][...], m[i][...], v[i][...])
            d_out[i][...] = delta
            m_out[i][...] = m_new
            v_out[i][...] = v_new

    out = [jax.ShapeDtypeStruct(w.shape, F32) for w in ws]
    res = pl.pallas_call(body, name="adamw_small", in_specs=[VMEM_SPEC] * (4 * n), out_specs=[VMEM_SPEC] * (3 * n),
                         out_shape=out * 3, compiler_params=_cparams())(*gs, *ws, *ms, *vs)
    return res[:n], res[n:2 * n], res[2 * n:]


def _pack(arrays, rows_multiple):
    flat = jnp.concatenate([a.reshape(-1) for a in arrays])
    per = 128 * rows_multiple
    total = -(-flat.shape[0] // per) * per
    return jnp.pad(flat, (0, total - flat.shape[0])).reshape(total // 128, 128)


def _unpack(buf, like):
    flat = buf.reshape(-1)
    out, off = [], 0
    for a in like:
        out.append(flat[off:off + a.size].reshape(a.shape))
        off += a.size
    return out


def kernel(x, c, w_mod, b_mod, g_mix, w_in, gm_ln_g, gm_ln_b, gm_w_s, gm_b_s, w_pa, pool_w, pool_scale, w_pb, conv_w, w_pc, w_o, g_ffn, w_13, w_2, g_final, loss_target, m_w_mod, m_b_mod, m_g_mix, m_w_in, m_gm_ln_g, m_gm_ln_b, m_gm_w_s, m_gm_b_s, m_w_pa, m_pool_w, m_pool_scale, m_w_pb, m_conv_w, m_w_pc, m_w_o, m_g_ffn, m_w_13, m_w_2, m_g_final, v_w_mod, v_b_mod, v_g_mix, v_w_in, v_gm_ln_g, v_gm_ln_b, v_gm_w_s, v_gm_b_s, v_w_pa, v_pool_w, v_pool_scale, v_w_pb, v_conv_w, v_w_pc, v_w_o, v_g_ffn, v_w_13, v_w_2, v_g_final):
    nb, seq, d = x.shape
    n_layers = w_in.shape[0]
    t_all = nb * seq
    blk = w_in.shape[-1]
    me = _my_index()
    conv_shard = conv_w.shape[-1]

    c_pad = jnp.pad(c, ((0, 8 - nb), (0, 0)))
    conv_pad = jnp.pad(conv_w.reshape(n_layers * 3, conv_shard), ((0, 16 - n_layers * 3), (0, 128 - conv_shard)))
    b_mod_mine = lax.dynamic_slice_in_dim(b_mod, me * blk, blk, axis=1).reshape(n_layers, 1, blk)
    cact_all, mod_blocks, conv_all = _pre(c_pad, conv_pad, w_mod, b_mod_mine)
    mod = jnp.transpose(mod_blocks, (1, 2, 0, 3)).reshape(n_layers, 8, N_MOD, d)[:, :nb]
    mod = jnp.pad(mod, ((0, 0), (0, 0), (0, 8 - N_MOD), (0, 0)))
    conv_full = jnp.transpose(conv_all[:, :n_layers * 3, :conv_shard].reshape(N_DEV, n_layers, 3, conv_shard), (1, 2, 0, 3))
    conv_full = jnp.pad(conv_full.reshape(n_layers, 3, N_DEV * conv_shard), ((0, 0), (0, 5), (0, 0)))
    bexp = jnp.repeat(jnp.transpose(gm_b_s, (0, 2, 1)), HEAD_DIM, axis=2)

    mixer_w, ffn_w = [w_in, w_o, w_pa, w_pb, w_pc], [w_13, w_2]
    n_mix = len(mixer_w)

    def send_weights(l, ws, name, after):
        return _xfer_start([], _gather_zones(ws, l), GATHER, name, after)

    def send_grads(partials, name, after):
        return _xfer_start(partials, _scatter_zones(partials), SCATTER, name, after)

    def arrived(flight, after, kind, name):
        send_sems, recv_sems, srcs, zones, _ = flight
        return _xfer_wait(send_sems, recv_sems, srcs, zones, list(range(len(zones))), after, kind, name)

    rows = lambda a: a.reshape(n_layers, 1, -1)
    mix_small = (rows(g_mix), rows(gm_ln_g), rows(gm_ln_b), gm_w_s, bexp, pool_w, rows(pool_scale), conv_full)
    g_ffn_rows = rows(g_ffn)
    xs = x.reshape(t_all, d)
    saved, weights = [], []
    after = mod
    for l in range(n_layers):
        if l == 0:
            win_g, wo_g, wpa_g, wpb_g, wpc_g = _gather_two_level(_gather_zones(mixer_w, 0), "gather_mixer_0", cact_all)
        else:
            win_g, wo_g, wpa_g, wpb_g, wpc_g = arrived(flight, after, GATHER, f"gather_wait_mixer_{l}")[1]
        flight = send_weights(l, ffn_w, f"gather_start_ffn_{l}", win_g)
        wo_g = wo_g.reshape(d, d)
        h, z, ycat, ocat, mrg, mo, xmid = _mixer_fwd(xs, mod, *mix_small, win_g, wpa_g, wpb_g, wpc_g, wo_g, flight[4], l, seq)
        w13_g, w2_g = arrived(flight, xmid, GATHER, f"gather_wait_ffn_{l}")[1]
        dep = w13_g
        if l + 1 < n_layers:
            flight = send_weights(l + 1, mixer_w, f"gather_start_mixer_{l + 1}", w13_g)
            dep = flight[4]
        w2_g = w2_g.reshape(N_DEV * w_2.shape[1], d)
        h2, ab, hid, f, xo = _ffn_fwd(xmid, mod, g_ffn_rows, w13_g, w2_g, dep, l, seq)
        saved.append((xs, h, z, ycat, ocat, mrg, mo, xmid, h2, ab, hid, f))
        weights.append((win_g, wpa_g, wpb_g, wpc_g, wo_g, w13_g, w2_g))
        xs = after = xo

    dx, loss_blk, dgf_blk = _final_loss(xs, g_final.reshape(1, d), loss_target.reshape(t_all, d), seq)
    loss = lax.psum(loss_blk[0, 0], ("x", "y", "c"))

    ffn_flight = [None] * n_layers
    mix_flight = [None] * n_layers
    small_grads = [None] * n_layers
    dmods = [None] * n_layers
    dep = dx
    for l in reversed(range(n_layers)):
        x_in, h, z, ycat, ocat, mrg, mo, xmid, h2, ab, hid, f = saved[l]
        win_g, wpa_g, wpb_g, wpc_g, wo_g, w13_g, w2_g = weights[l]
        dxm, df, dab, dmod2, dg_ffn = _ffn_bwd(dx, xmid, ab, f, mod, g_ffn_rows, w13_g, w2_g, dep, l, seq)
        ffn_flight[l] = send_grads(_ffn_wgrads(h2, dab, hid, df, l), f"grads_start_ffn_{l}", dxm)
        dx, dz, dycat, dmo, dmod1, dg_mix, sm, dws, dssum, dpw = _mixer_bwd(
            dxm, x_in, z, ycat, mo, mod, *mix_small, win_g, wpa_g, wpb_g, wpc_g, wo_g, ffn_flight[l][4], l, seq)
        if l > 0:
            mix_flight[l] = send_grads(_mixer_wgrads(h, dz, mrg, dmo, ocat, dycat, l), f"grads_start_mixer_{l}", dx)
            dep = mix_flight[l][4]
        dmod = jnp.concatenate([dmod1[:, 0:3], dmod2[:, 0:3]], axis=1).reshape(nb, N_MOD * d)
        dmods[l] = dmod
        db_s = jnp.transpose(jnp.sum(dssum.reshape(CHUNK, HEADS, HEAD_DIM), axis=2))
        small_grads[l] = [jnp.sum(dmod, axis=0), dg_mix[0], sm[0], sm[1], dws, db_s, dpw, sm[2], sm[3:6], dg_ffn[0]]
    grad_x = dx.reshape(nb, seq, d)

    names = ["b_mod", "g_mix", "ln_g", "ln_b", "w_s", "b_s", "pool_w", "pool_scale", "conv_w", "g_ffn"]
    per_name = [jnp.stack([small_grads[l][n] for l in range(n_layers)]) for n in range(len(names))] + [dgf_blk[0]]
    buf = _pack(per_name, 8 * N_DEV)
    dmod_all = jnp.pad(jnp.stack(dmods), ((0, 0), (0, 8 - nb), (0, 0)))
    dmod_blocks = jnp.transpose(dmod_all.reshape(n_layers, 8, N_DEV, blk), (2, 0, 1, 3))
    red, grad_w_mod = _small(buf, dmod_blocks, cact_all)
    (g_b_mod, g_g_mix, g_ln_g, g_ln_b, g_w_s, g_b_s, g_pool_w, g_pool_scale, g_conv_full, g_g_ffn, g_g_final) = _unpack(red, per_name)
    g_conv = lax.dynamic_slice_in_dim(g_conv_full, me * conv_shard, conv_shard, axis=2)

    mix_flight[0] = send_grads(_mixer_wgrads(h, dz, mrg, dmo, ocat, dycat, 0), "grads_start_mixer_0", red)
    results = {}
    results["w_mod"] = _adamw([grad_w_mod[l][None] for l in range(n_layers)], w_mod, m_w_mod, v_w_mod, "adamw_w_mod")

    small_w =[b_mod, g_mix, gm_ln_g, gm_ln_b, gm_w_s, gm_b_s, pool_w, pool_scale, conv_w, g_ffn, g_final]
    small_m = [m_b_mod, m_g_mix, m_gm_ln_g, m_gm_ln_b, m_gm_w_s, m_gm_b_s, m_pool_w, m_pool_scale, m_conv_w, m_g_ffn, m_g_final]
    small_v = [v_b_mod, v_g_mix, v_gm_ln_g, v_gm_ln_b, v_gm_w_s, v_gm_b_s, v_pool_w, v_pool_scale, v_conv_w, v_g_ffn, v_g_final]
    small_g = [g_b_mod, g_g_mix, g_ln_g, g_ln_b, g_w_s, g_b_s, g_pool_w, g_pool_scale, g_conv, g_g_ffn, g_g_final]
    small_names = ["b_mod", "g_mix", "gm_ln_g", "gm_ln_b", "gm_w_s", "gm_b_s", "pool_w", "pool_scale", "conv_w", "g_ffn", "g_final"]
    sd, sm_new, sv_new = _adamw_small(small_g, small_w, small_m, small_v)
    for n, nm in enumerate(small_names):
        results[nm] = (small_g[n], sd[n], sm_new[n], sv_new[n])

    layers = list(range(n_layers))
    done = (results["w_mod"][1][0, 0, 0] + sd[-1][0]).reshape(1)
    ffn_recv = [arrived(ffn_flight[l], done, SCATTER, f"grads_wait_ffn_{l}") for l in reversed(layers)][::-1]
    mix_recv = [None] + [arrived(mix_flight[l], done, SCATTER, f"grads_wait_mixer_{l}") for l in reversed(layers[1:])][::-1]
    swap = lambda a: jnp.swapaxes(a, 1, 2)
    got = lambda recv, ls, a: dict(parts=[recv[l][1][a] for l in ls], own=[recv[l][0][a] for l in ls])
    results["w_13"] = [swap(r) for r in _adamw(w=swap(w_13), m=swap(m_w_13), v=swap(v_w_13), name="adamw_w_13", **got(ffn_recv, layers, 0))]
    results["w_2"] = _adamw(w=w_2, m=m_w_2, v=v_w_2, name="adamw_w_2", **got(ffn_recv, layers, 1))
    mix_m = [m_w_in, m_w_o, m_w_pa, m_w_pb, m_w_pc]
    mix_v = [v_w_in, v_w_o, v_w_pa, v_w_pb, v_w_pc]
    mix_names = ["w_in", "w_o", "w_pa", "w_pb", "w_pc"]
    early = [_adamw(w=mixer_w[a], m=mix_m[a], v=mix_v[a], name=f"adamw_{mix_names[a]}_later", first=1, **got(mix_recv, layers[1:], a))
             for a in range(n_mix)]
    done = (results["w_13"][1][0, 0, 0] + results["w_2"][1][0, 0, 0] + sum(e[1][1, 0, 0] for e in early)).reshape(1)
    mix_recv[0] = arrived(mix_flight[0], done, SCATTER, "grads_wait_mixer_0")
    for a in range(n_mix):
        results[mix_names[a]] = _adamw(w=mixer_w[a], m=mix_m[a], v=mix_v[a], name=f"adamw_{mix_names[a]}_first", earlier=early[a],
                                       **got(mix_recv, [0], a))

    order = ["w_mod", "b_mod", "g_mix", "w_in", "gm_ln_g", "gm_ln_b", "gm_w_s", "gm_b_s", "w_pa", "pool_w", "pool_scale",
             "w_pb", "conv_w", "w_pc", "w_o", "g_ffn", "w_13", "w_2", "g_final"]
    return (loss, grad_x, *[results[nm][0] for nm in order], *[results[nm][1] for nm in order],
            *[results[nm][2] for nm in order], *[results[nm][3] for nm in order])
```

```python
import functools

import jax
import jax.numpy as jnp
from jax import lax
from jax.experimental import pallas as pl
from jax.experimental.pallas import tpu as pltpu

F32 = jnp.float32
BF16 = jnp.bfloat16
MESH_ID = pl.DeviceIdType.MESH

N_DEV = 8
EPS = 1e-6
CHUNK = 128
HEADS = 4
HEAD_DIM = 128
BR_W = 512
POOL_WINDOWS = (2, 4, 8, 16)
POOL_HALO = 16
CONV_HALO = 8
N_MOD = 6
ADAM_LR = 0.001
ADAM_B1 = 0.9
ADAM_B2 = 0.999
ADAM_EPS = 1e-08
ADAM_WD = 0.01
ADAM_STEP = 10

TOKEN_TILE = 256
ELEMENTWISE_TILE = 1024
WGRAD_TOKENS = 2048
VMEM_LIMIT = 56 * 1024 * 1024
GELU_K = 0.7978845608028654
GELU_C = 0.044715

ANY = pl.BlockSpec(memory_space=pl.ANY)
VMEM_SPEC = pl.BlockSpec(memory_space=pltpu.VMEM)


def _cparams(**kw):
    return pltpu.CompilerParams(vmem_limit_bytes=VMEM_LIMIT, **kw)


def _dot(a, b):
    return jnp.dot(a, b, preferred_element_type=F32)


def _dot_nt(a, b):
    return lax.dot_general(a, b, (((1,), (1,)), ((), ())), preferred_element_type=F32)


def _dot_tn(a, b):
    return lax.dot_general(a, b, (((0,), (0,)), ((), ())), preferred_element_type=F32)


def _colsum(a):
    return jnp.sum(a, axis=0, keepdims=True)


def _sigmoid(x):
    return 0.5 * jnp.tanh(0.5 * x) + 0.5


def _gelu(x):
    x2 = x * x
    t = jnp.tanh(x * (GELU_K + (GELU_K * GELU_C) * x2))
    return (0.5 * x) * (1.0 + t), t, x2


def _gelu_grad(x, t, x2):
    one_t = 1.0 + t
    return 0.5 * one_t + (0.5 * x) * (one_t * (1.0 - t)) * (GELU_K + (3.0 * GELU_K * GELU_C) * x2)


def _tril_mask():
    r = lax.broadcasted_iota(jnp.int32, (CHUNK, CHUNK), 0)
    c = lax.broadcasted_iota(jnp.int32, (CHUNK, CHUNK), 1)
    return (r >= c).astype(F32)


def _my_index():
    return 4 * lax.axis_index("x") + 2 * lax.axis_index("y") + lax.axis_index("c")


def _peer(k):
    x, y, c = lax.axis_index("x"), lax.axis_index("y"), lax.axis_index("c")
    px = 1 - x if (k >> 2) & 1 else x
    py = 1 - y if (k >> 1) & 1 else y
    pc = 1 - c if k & 1 else c
    return (px, py, pc), 4 * px + 2 * py + pc


def _load_weights(step, pairs, sem):
    @pl.when(step == 0)
    def _():
        copies = [pltpu.make_async_copy(src, dst, sem.at[n]) for n, (src, dst) in enumerate(pairs)]
        for cp in copies:
            cp.start()
        for cp in copies:
            cp.wait()


def _wp_pairs(wp_hbm, wp_v):
    return [(wp_hbm.at[j], wp_v.at[:, pl.ds(HEAD_DIM * j, HEAD_DIM)]) for j in range(N_DEV)]


def _layer_spec(a, layer):
    return pl.BlockSpec((None, *a.shape[1:]), lambda i: (layer,) + (0,) * (a.ndim - 1))


def _mod_spec(mod, layer, n_seq_tiles):
    return pl.BlockSpec((None, 1, *mod.shape[2:]), lambda i: (layer, i // n_seq_tiles, 0, 0))


def _rms_mod(x, g, shift, scale):
    rstd = lax.rsqrt(jnp.mean(x * x, axis=-1, keepdims=True) + EPS)
    xn = x * rstd
    return xn, rstd, (xn * g) * (1.0 + scale) + shift


def _rms_mod_bwd(dh, xn, rstd, g, scale):
    dxn = dh * (1.0 + scale) * g
    dx = rstd * (dxn - xn * jnp.mean(dxn * xn, axis=-1, keepdims=True))
    return dx, _colsum(dh), _colsum(dh * (xn * g)), _colsum(dh * (1.0 + scale) * xn)


def _gmlp_s(vn_b, wmask_b, bexp, s_scr, tm):
    for ch in range(tm // CHUNK):
        rows = slice(ch * CHUNK, (ch + 1) * CHUNK)
        for hh in range(HEADS):
            cols = slice(hh * HEAD_DIM, (hh + 1) * HEAD_DIM)
            s_scr[rows, cols] = _dot(wmask_b[hh], vn_b[rows, cols]) + bexp[:, cols]
    return s_scr[...]


def _inv_count(pos, win):
    return 1.0 / jnp.minimum(pos + 1, win).astype(F32)


def _window_sums(ext, tm, trailing):
    n = tm + POOL_HALO
    sums = []
    for g, win in enumerate(POOL_WINDOWS):
        s = ext[:, g * HEAD_DIM:(g + 1) * HEAD_DIM]
        span = 1
        while span < win:
            s = s + pltpu.roll(s, span if trailing else n - span, 0)
            span *= 2
        sums.append(s[POOL_HALO:POOL_HALO + tm] if trailing else s[0:tm])
    return sums


def _pool_p(xb, xbext, pos, tm):
    sums = _window_sums(xbext, tm, True)
    return [sums[g] * _inv_count(pos, win) - xb[:, g * HEAD_DIM:(g + 1) * HEAD_DIM] for g, win in enumerate(POOL_WINDOWS)]


def _mixer_fwd(x, mod, g_mix, ln_g, ln_b, w_s, bexp, pool_w, pool_scale, conv_w, win, wpa, wpb, wpc, wo, dep, layer, seq):
    t_all, d = x.shape
    tm = min(TOKEN_TILE, seq)
    n_seq_tiles = seq // tm
    blk = win.shape[-1]

    def body(x_ref, mod_ref, gmix_ref, lng_ref, lnb_ref, ws_ref, bexp_ref, pw_ref, ps_ref, cw_ref,
             win_hbm, wpa_hbm, wpb_hbm, wpc_hbm, wo_hbm, dep_ref,
             h_ref, z_ref, ycat_ref, ocat_ref, mrg_ref, mo_ref, xmid_ref,
             win_v, wpa_v, wpb_v, wpc_v, wo_v, xbext, zcext, s_scr, sem):
        i = pl.program_id(0)
        pairs = [(win_hbm, win_v), (wo_hbm, wo_v)]
        pairs += _wp_pairs(wpa_hbm, wpa_v) + _wp_pairs(wpb_hbm, wpb_v) + _wp_pairs(wpc_hbm, wpc_v)
        _load_weights(i, pairs, sem)
        tile_in_seq = i % n_seq_tiles

        @pl.when(tile_in_seq == 0)
        def _():
            xbext[0:POOL_HALO, :] = jnp.zeros((POOL_HALO, BR_W), F32)
            zcext[0:CONV_HALO, :] = jnp.zeros((CONV_HALO, BR_W), F32)

        x_t = x_ref[...]
        shift1, scale1, gate1 = mod_ref[0, 0:1, :], mod_ref[0, 1:2, :], mod_ref[0, 2:3, :]
        _, _, h = _rms_mod(x_t, gmix_ref[...], shift1, scale1)
        hb = h.astype(BF16)
        h_ref[...] = hb
        def project(j):
            zj = _dot(hb, win_v[j])
            z_ref[:, j * blk:(j + 1) * blk] = zj.astype(BF16)
            return zj

        z0, z1 = project(0), project(1)
        u = z0[:, 0:BR_W]
        v = jnp.concatenate([z0[:, BR_W:blk], z1[:, 0:2 * BR_W - blk]], axis=1)
        xb = z1[:, 2 * BR_W - blk:blk]

        gu = _gelu(u)[0]
        gv = _gelu(v)[0]
        mu = jnp.mean(gv, axis=-1, keepdims=True)
        cen = gv - mu
        rs = lax.rsqrt(jnp.mean(cen * cen, axis=-1, keepdims=True) + EPS)
        vn = (cen * rs) * lng_ref[...] + lnb_ref[...]
        mask = _tril_mask()
        wmask_b = [(ws_ref[hh] * mask).astype(BF16) for hh in range(HEADS)]
        s = _gmlp_s(vn.astype(BF16), wmask_b, bexp_ref[...], s_scr, tm)
        oa = (gu * s).astype(BF16)
        ya = _dot(oa, wpa_v[...])

        xbext[POOL_HALO:POOL_HALO + tm, :] = xb
        pos = tile_in_seq * tm + lax.broadcasted_iota(jnp.int32, (tm, 1), 0)
        ps = _pool_p(xb, xbext, pos, tm)
        qs = [_dot(ps[g].astype(BF16), pw_ref[g].astype(BF16)) for g in range(len(POOL_WINDOWS))]
        ob = (jnp.concatenate(qs, axis=1) * ps_ref[...]).astype(BF16)
        yb = _dot(ob, wpb_v[...])
        xbext[0:POOL_HALO, :] = xbext[tm:tm + POOL_HALO, :]

        z2, z3 = project(2), project(3)
        bg = z2[:, 0:BR_W]
        cg = jnp.concatenate([z2[:, BR_W:blk], z3[:, 0:2 * BR_W - blk]], axis=1)
        hc = z3[:, 2 * BR_W - blk:blk]
        zz = cg * hc
        zcext[CONV_HALO:CONV_HALO + tm, :] = zz
        yconv = (cw_ref[0:1, :] * zcext[pl.ds(CONV_HALO - 2, tm), :] + cw_ref[1:2, :] * zcext[pl.ds(CONV_HALO - 1, tm), :]
                 + cw_ref[2:3, :] * zz)
        oc = (bg * yconv).astype(BF16)
        yc = _dot(oc, wpc_v[...])
        zcext[0:CONV_HALO, :] = zcext[tm:tm + CONV_HALO, :]

        ocat_ref[:, 0:BR_W] = oa
        ocat_ref[:, BR_W:2 * BR_W] = ob
        ocat_ref[:, 2 * BR_W:3 * BR_W] = oc
        ycat_ref[:, 0:d] = ya.astype(BF16)
        ycat_ref[:, d:2 * d] = yb.astype(BF16)
        ycat_ref[:, 2 * d:3 * d] = yc.astype(BF16)

        ys = (ya, yb, yc)
        zg = jnp.concatenate([project(j).astype(BF16) for j in range(4, N_DEV)], axis=1)
        mb = _sigmoid(zg[:, 0:d]) * ys[0].astype(BF16)
        for n in range(1, 3):
            mb = mb + _sigmoid(zg[:, n * d:(n + 1) * d]) * ys[n].astype(BF16)
        mrg_ref[...] = mb
        mo = _dot(mb, wo_v[...])
        mo_ref[...] = mo.astype(BF16)
        xmid_ref[...] = x_t + gate1 * mo

    tok = lambda cols: pl.BlockSpec((tm, cols), lambda i: (i, 0))
    in_specs = [
        tok(d),
        _mod_spec(mod, layer, n_seq_tiles),
        *[_layer_spec(a, layer) for a in (g_mix, ln_g, ln_b, w_s, bexp, pool_w, pool_scale, conv_w)],
        ANY, ANY, ANY, ANY, ANY, ANY,
    ]
    out_shape = [
        jax.ShapeDtypeStruct((t_all, d), BF16),
        jax.ShapeDtypeStruct((t_all, N_DEV * blk), BF16),
        jax.ShapeDtypeStruct((t_all, 3 * d), BF16),
        jax.ShapeDtypeStruct((t_all, 3 * BR_W), BF16),
        jax.ShapeDtypeStruct((t_all, d), BF16),
        jax.ShapeDtypeStruct((t_all, d), BF16),
        jax.ShapeDtypeStruct((t_all, d), F32),
    ]
    out_specs = [tok(d), tok(N_DEV * blk), tok(3 * d), tok(3 * BR_W), tok(d), tok(d), tok(d)]
    scratch = [
        pltpu.VMEM((N_DEV, d, blk), BF16), pltpu.VMEM((BR_W, d), BF16), pltpu.VMEM((BR_W, d), BF16),
        pltpu.VMEM((BR_W, d), BF16), pltpu.VMEM((d, d), BF16),
        pltpu.VMEM((tm + POOL_HALO, BR_W), F32), pltpu.VMEM((tm + CONV_HALO, BR_W), F32), pltpu.VMEM((tm, BR_W), F32),
        pltpu.SemaphoreType.DMA((2 + 3 * N_DEV,)),
    ]
    return pl.pallas_call(
        body, name=f"mixer_fwd_{layer}", grid=(t_all // tm,), in_specs=in_specs, out_specs=out_specs, out_shape=out_shape,
        scratch_shapes=scratch, compiler_params=_cparams(dimension_semantics=("arbitrary",)),
    )(x, mod, g_mix, ln_g, ln_b, w_s, bexp, pool_w, pool_scale, conv_w, win, wpa, wpb, wpc, wo, dep)


def _ffn_fwd(x, mod, g_ffn, w13, w2, dep, layer, seq):
    t_all, d = x.shape
    tm = min(TOKEN_TILE, seq)
    n_seq_tiles = seq // tm
    fb = w13.shape[-1]
    n_hid = N_DEV // 2

    def body(x_ref, mod_ref, g_ref, w13_hbm, w2_hbm, dep_ref, h_ref, ab_ref, hid_ref, f_ref, xo_ref, w13_v, w2_v, sem):
        i = pl.program_id(0)
        _load_weights(i, [(w13_hbm, w13_v), (w2_hbm, w2_v)], sem)
        x_t = x_ref[...]
        shift2, scale2, gate2 = mod_ref[0, 3:4, :], mod_ref[0, 4:5, :], mod_ref[0, 5:6, :]
        _, _, h = _rms_mod(x_t, g_ref[...], shift2, scale2)
        hb = h.astype(BF16)
        h_ref[...] = hb
        f = jnp.zeros((tm, d), F32)
        for k in range(n_hid):
            a = _dot(hb, w13_v[k])
            b = _dot(hb, w13_v[n_hid + k])
            a, b = a.astype(BF16), b.astype(BF16)
            ab_ref[k] = a
            ab_ref[n_hid + k] = b
            hid = (a * _sigmoid(a)) * b
            hid_ref[k] = hid
            f = f + _dot(hid, w2_v[k * fb:(k + 1) * fb, :])
        f_ref[...] = f.astype(BF16)
        xo_ref[...] = x_t + gate2 * f

    tok = lambda cols: pl.BlockSpec((tm, cols), lambda i: (i, 0))
    blk3 = lambda n: pl.BlockSpec((n, tm, fb), lambda i: (0, i, 0))
    in_specs = [tok(d), _mod_spec(mod, layer, n_seq_tiles), _layer_spec(g_ffn, layer), ANY, ANY, ANY]
    out_shape = [
        jax.ShapeDtypeStruct((t_all, d), BF16),
        jax.ShapeDtypeStruct((N_DEV, t_all, fb), BF16),
        jax.ShapeDtypeStruct((n_hid, t_all, fb), BF16),
        jax.ShapeDtypeStruct((t_all, d), BF16),
        jax.ShapeDtypeStruct((t_all, d), F32),
    ]
    out_specs = [tok(d), blk3(N_DEV), blk3(n_hid), tok(d), tok(d)]
    scratch = [pltpu.VMEM((N_DEV, d, fb), BF16), pltpu.VMEM((n_hid * fb, d), BF16), pltpu.SemaphoreType.DMA((2,))]
    return pl.pallas_call(
        body, name=f"ffn_fwd_{layer}", grid=(t_all // tm,), in_specs=in_specs, out_specs=out_specs, out_shape=out_shape,
        scratch_shapes=scratch, compiler_params=_cparams(dimension_semantics=("arbitrary",)),
    )(x, mod, g_ffn, w13, w2, dep)


def _final_loss(x, g_final, target, seq):
    t_all, d = x.shape
    tm = min(ELEMENTWISE_TILE, seq)

    def body(x_ref, g_ref, t_ref, dx_ref, loss_ref, dg_ref):
        i = pl.program_id(0)

        @pl.when(i == 0)
        def _():
            loss_ref[...] = jnp.zeros(loss_ref.shape, F32)
            dg_ref[...] = jnp.zeros(dg_ref.shape, F32)

        x_t = x_ref[...]
        g = g_ref[...]
        rstd = lax.rsqrt(jnp.mean(x_t * x_t, axis=-1, keepdims=True) + EPS)
        xn = x_t * rstd
        err = xn * g - t_ref[...]
        loss_ref[0:1, :] += _colsum(err * err) * (0.5 / d)
        dy = err * (1.0 / d)
        dg_ref[0:1, :] += _colsum(dy * xn)
        dxn = dy * g
        dx_ref[...] = rstd * (dxn - xn * jnp.mean(dxn * xn, axis=-1, keepdims=True))

        @pl.when(i == pl.num_programs(0) - 1)
        def _():
            loss_ref[...] = jnp.broadcast_to(jnp.sum(loss_ref[0:1, :], axis=1, keepdims=True), loss_ref.shape)

    tok = pl.BlockSpec((tm, d), lambda i: (i, 0))
    acc = pl.BlockSpec((8, d), lambda i: (0, 0))
    return pl.pallas_call(
        body, name="final_loss", grid=(t_all // tm,),
        in_specs=[tok, pl.BlockSpec((1, d), lambda i: (0, 0)), tok], out_specs=[tok, acc, acc],
        out_shape=[jax.ShapeDtypeStruct((t_all, d), F32), jax.ShapeDtypeStruct((8, d), F32), jax.ShapeDtypeStruct((8, d), F32)],
        compiler_params=_cparams(dimension_semantics=("arbitrary",)),
    )(x, g_final, target)


def _ffn_bwd(dxo, xmid, ab, f, mod, g_ffn, w13, w2, dep, layer, seq):
    t_all, d = xmid.shape
    tm = min(TOKEN_TILE, seq)
    n_seq_tiles = seq // tm
    fb = w13.shape[-1]
    n_hid = N_DEV // 2

    def body(dxo_ref, x_ref, ab_ref, f_ref, mod_ref, g_ref, w13_hbm, w2_hbm, dep_ref,
             dx_ref, df_ref, dab_ref, dmod_ref, dg_ref, w13_v, w2_v, sem):
        i = pl.program_id(0)
        _load_weights(i, [(w13_hbm, w13_v), (w2_hbm, w2_v)], sem)

        @pl.when(i == 0)
        def _():
            dg_ref[...] = jnp.zeros(dg_ref.shape, F32)

        @pl.when(i % n_seq_tiles == 0)
        def _():
            dmod_ref[...] = jnp.zeros(dmod_ref.shape, F32)

        scale2, gate2 = mod_ref[0, 4:5, :], mod_ref[0, 5:6, :]
        g = g_ref[...]
        x_t = x_ref[...]
        rstd = lax.rsqrt(jnp.mean(x_t * x_t, axis=-1, keepdims=True) + EPS)
        xn = x_t * rstd
        dxo_t = dxo_ref[...]
        dmod_ref[0, 2:3, :] += _colsum(dxo_t * f_ref[...].astype(F32))
        dfb = (dxo_t * gate2).astype(BF16)
        df_ref[...] = dfb
        dh = jnp.zeros((tm, d), F32)
        for k in range(n_hid):
            dhid = _dot_nt(dfb, w2_v[k * fb:(k + 1) * fb, :]).astype(BF16)
            a = ab_ref[k]
            b = ab_ref[n_hid + k]
            sg = _sigmoid(a)
            da = dhid * b * (sg * (1.0 + a * (1.0 - sg)))
            db = dhid * (a * sg)
            dab_ref[k] = da
            dab_ref[n_hid + k] = db
            dh = dh + _dot_nt(da, w13_v[k]) + _dot_nt(db, w13_v[n_hid + k])
        dx, dshift, dscale, dg = _rms_mod_bwd(dh, xn, rstd, g, scale2)
        dmod_ref[0, 0:1, :] += dshift
        dmod_ref[0, 1:2, :] += dscale
        dg_ref[0:1, :] += dg
        dx_ref[...] = dxo_t + dx

    tok = lambda cols: pl.BlockSpec((tm, cols), lambda i: (i, 0))
    blk3 = lambda n: pl.BlockSpec((n, tm, fb), lambda i: (0, i, 0))
    modspec = pl.BlockSpec((1, 8, d), lambda i: (i // n_seq_tiles, 0, 0))
    in_specs = [tok(d), tok(d), blk3(N_DEV), tok(d), _mod_spec(mod, layer, n_seq_tiles), _layer_spec(g_ffn, layer), ANY, ANY, ANY]
    out_shape = [
        jax.ShapeDtypeStruct((t_all, d), F32), jax.ShapeDtypeStruct((t_all, d), BF16),
        jax.ShapeDtypeStruct((N_DEV, t_all, fb), BF16), jax.ShapeDtypeStruct(mod.shape[1:], F32),
        jax.ShapeDtypeStruct((8, d), F32),
    ]
    out_specs = [tok(d), tok(d), blk3(N_DEV), modspec, pl.BlockSpec((8, d), lambda i: (0, 0))]
    scratch = [pltpu.VMEM((N_DEV, d, fb), BF16), pltpu.VMEM((n_hid * fb, d), BF16), pltpu.SemaphoreType.DMA((2,))]
    return pl.pallas_call(
        body, name=f"ffn_bwd_{layer}", grid=(t_all // tm,), in_specs=in_specs, out_specs=out_specs, out_shape=out_shape,
        scratch_shapes=scratch, compiler_params=_cparams(dimension_semantics=("arbitrary",)),
    )(dxo, xmid, ab, f, mod, g_ffn, w13, w2, dep)


def _mixer_bwd(dxm, x, z, ycat, mo, mod, g_mix, ln_g, ln_b, w_s, bexp, pool_w, pool_scale, conv_w,
               win, wpa, wpb, wpc, wo, dep, layer, seq):
    t_all, d = x.shape
    tm = min(TOKEN_TILE, seq)
    n_seq_tiles = seq // tm
    blk = win.shape[-1]
    n_win = len(POOL_WINDOWS)

    def tile_of(i):
        return (i // n_seq_tiles) * n_seq_tiles + (n_seq_tiles - 1 - i % n_seq_tiles)

    def halo_row_block(i):
        return jnp.maximum(tile_of(i) * (tm // POOL_HALO) - 1, 0)

    def body(dxm_ref, x_ref, z_ref, zpb_ref, zpc_ref, ycat_ref, mo_ref, mod_ref, gmix_ref, lng_ref, lnb_ref, ws_ref,
             bexp_ref, pw_ref, ps_ref, cw_ref, win_hbm, wpa_hbm, wpb_hbm, wpc_hbm, wo_hbm, dep_ref,
             dx_ref, dz_ref, dycat_ref, dmo_ref, dmod_ref, dg_ref, sm_ref, dws_ref, dssum_ref, dpw_ref,
             win_v, wpa_v, wpb_v, wpc_v, wo_v, xbext, zzext, rext, dyext, s_scr, dvn_scr, sem):
        i = pl.program_id(0)
        pairs = [(win_hbm.at[j], win_v.at[:, pl.ds(blk * j, blk)]) for j in range(N_DEV)] + [(wo_hbm, wo_v)]
        pairs += _wp_pairs(wpa_hbm, wpa_v) + _wp_pairs(wpb_hbm, wpb_v) + _wp_pairs(wpc_hbm, wpc_v)
        _load_weights(i, pairs, sem)
        tile_in_seq = n_seq_tiles - 1 - i % n_seq_tiles
        first_of_seq = tile_in_seq == 0

        @pl.when(i == 0)
        def _():
            for r in (dg_ref, sm_ref, dws_ref, dssum_ref, dpw_ref):
                r[...] = jnp.zeros(r.shape, F32)

        @pl.when(i % n_seq_tiles == 0)
        def _():
            dmod_ref[...] = jnp.zeros(dmod_ref.shape, F32)
            rext[tm:tm + POOL_HALO, :] = jnp.zeros((POOL_HALO, BR_W), F32)
            dyext[tm:tm + CONV_HALO, :] = jnp.zeros((CONV_HALO, BR_W), F32)

        shift1, scale1, gate1 = mod_ref[0, 0:1, :], mod_ref[0, 1:2, :], mod_ref[0, 2:3, :]
        dxm_t = dxm_ref[...]
        dmo = (dxm_t * gate1).astype(BF16)
        dmo_ref[...] = dmo
        dmerged = _dot_nt(dmo, wo_v[...])
        dmod_ref[0, 2:3, :] += _colsum(dxm_t * mo_ref[...].astype(F32))

        dh_parts = []

        def emit_dz(lo, hi, value):
            vb = value.astype(BF16)
            dz_ref[:, lo:hi] = vb
            dh_parts.append(_dot_nt(vb, win_v[:, lo:hi]))

        dys = []
        dmerged = dmerged.astype(BF16)
        for n in range(3):
            gt = _sigmoid(z_ref[:, 3 * d + n * d:3 * d + (n + 1) * d])
            dyn = dmerged * gt
            emit_dz(3 * d + n * d, 3 * d + (n + 1) * d, dyn * ycat_ref[:, n * d:(n + 1) * d] * (1.0 - gt))
            dycat_ref[:, n * d:(n + 1) * d] = dyn
            dys.append(dyn)

        doa = _dot_nt(dys[0], wpa_v[...])
        dob = _dot_nt(dys[1], wpb_v[...])
        doc = _dot_nt(dys[2], wpc_v[...])

        u = z_ref[:, 0:BR_W].astype(F32)
        v = z_ref[:, BR_W:2 * BR_W].astype(F32)
        gu, tu, u2 = _gelu(u)
        gv, tv, v2 = _gelu(v)
        mu = jnp.mean(gv, axis=-1, keepdims=True)
        cen = gv - mu
        rs = lax.rsqrt(jnp.mean(cen * cen, axis=-1, keepdims=True) + EPS)
        vhat = cen * rs
        lng = lng_ref[...]
        vn_b = (vhat * lng + lnb_ref[...]).astype(BF16)
        mask = _tril_mask()
        wmask = [ws_ref[hh] * mask for hh in range(HEADS)]
        s = _gmlp_s(vn_b, [w.astype(BF16) for w in wmask], bexp_ref[...], s_scr, tm)
        du = (doa * s) * _gelu_grad(u, tu, u2)
        ds = doa * gu
        ds_b = ds.astype(BF16)
        dssum = jnp.zeros((CHUNK, BR_W), F32)
        for ch in range(tm // CHUNK):
            rows = slice(ch * CHUNK, (ch + 1) * CHUNK)
            dssum = dssum + ds[rows, :]
            for hh in range(HEADS):
                cols = slice(hh * HEAD_DIM, (hh + 1) * HEAD_DIM)
                dvn_scr[rows, cols] = _dot_tn(wmask[hh].astype(BF16), ds_b[rows, cols])
                dws_ref[hh] += _dot_nt(ds_b[rows, cols], vn_b[rows, cols]) * mask
        dssum_ref[...] += dssum
        dvn = dvn_scr[...]
        sm_ref[0:1, :] += _colsum(dvn * vhat)
        sm_ref[1:2, :] += _colsum(dvn)
        dvhat = dvn * lng
        dgv = rs * (dvhat - jnp.mean(dvhat, axis=-1, keepdims=True) - vhat * jnp.mean(dvhat * vhat, axis=-1, keepdims=True))
        dv = dgv * _gelu_grad(v, tv, v2)
        emit_dz(0, 2 * BR_W, jnp.concatenate([du, dv], axis=1))

        xb = z_ref[:, 2 * BR_W:3 * BR_W].astype(F32)
        xbext[0:POOL_HALO, :] = jnp.where(first_of_seq, 0.0, zpb_ref[...].astype(F32))
        xbext[POOL_HALO:POOL_HALO + tm, :] = xb
        pos = tile_in_seq * tm + lax.broadcasted_iota(jnp.int32, (tm, 1), 0)
        ps = _pool_p(xb, xbext, pos, tm)
        scale_b = ps_ref[...]
        dq = dob * scale_b
        qs, dps = [], []
        for gi, win_len in enumerate(POOL_WINDOWS):
            cols = slice(gi * HEAD_DIM, (gi + 1) * HEAD_DIM)
            pw_b = pw_ref[gi].astype(BF16)
            p_b = ps[gi].astype(BF16)
            dq_b = dq[:, cols].astype(BF16)
            qs.append(_dot(p_b, pw_b))
            dpw_ref[gi] += _dot_tn(p_b, dq_b)
            dp = _dot_nt(dq_b, pw_b)
            dps.append(dp)
            rext[0:tm, cols] = dp * _inv_count(pos, win_len)
        sm_ref[2:3, :] += _colsum(dob * jnp.concatenate(qs, axis=1))
        dxbs = [acc - dp for acc, dp in zip(_window_sums(rext, tm, False), dps)]
        emit_dz(2 * BR_W, 3 * BR_W, jnp.concatenate(dxbs, axis=1))
        rext[tm:tm + POOL_HALO, :] = rext[0:POOL_HALO, :]

        bg = z_ref[:, 3 * BR_W:4 * BR_W].astype(F32)
        cg = z_ref[:, 4 * BR_W:5 * BR_W].astype(F32)
        hc = z_ref[:, 5 * BR_W:6 * BR_W].astype(F32)
        zz = cg * hc
        zprev = zpc_ref[POOL_HALO - CONV_HALO:POOL_HALO, :].astype(F32)
        zzext[0:CONV_HALO, :] = jnp.where(first_of_seq, 0.0, zprev[:, 0:BR_W] * zprev[:, BR_W:2 * BR_W])
        zzext[CONV_HALO:CONV_HALO + tm, :] = zz
        zm2 = zzext[pl.ds(CONV_HALO - 2, tm), :]
        zm1 = zzext[pl.ds(CONV_HALO - 1, tm), :]
        w0, w1, w2c = cw_ref[0:1, :], cw_ref[1:2, :], cw_ref[2:3, :]
        yconv = w0 * zm2 + w1 * zm1 + w2c * zz
        dyc = doc * bg
        sm_ref[3:4, :] += _colsum(dyc * zm2)
        sm_ref[4:5, :] += _colsum(dyc * zm1)
        sm_ref[5:6, :] += _colsum(dyc * zz)
        dyext[0:tm, :] = dyc
        dzz = w2c * dyc + w1 * dyext[pl.ds(1, tm), :] + w0 * dyext[pl.ds(2, tm), :]
        dyext[tm:tm + CONV_HALO, :] = dyext[0:CONV_HALO, :]
        emit_dz(3 * BR_W, 6 * BR_W, jnp.concatenate([doc * yconv, dzz * hc, dzz * cg], axis=1))

        dh = dh_parts[0]
        for part in dh_parts[1:]:
            dh = dh + part
        x_t = x_ref[...]
        rstd = lax.rsqrt(jnp.mean(x_t * x_t, axis=-1, keepdims=True) + EPS)
        dx, dshift, dscale, dg = _rms_mod_bwd(dh, x_t * rstd, rstd, gmix_ref[...], scale1)
        dmod_ref[0, 0:1, :] += dshift
        dmod_ref[0, 1:2, :] += dscale
        dg_ref[0:1, :] += dg
        dx_ref[...] = dxm_ref[...] + dx

    tok = lambda cols: pl.BlockSpec((tm, cols), lambda i: (tile_of(i), 0))
    modspec = pl.BlockSpec((1, 8, d), lambda i: (i // n_seq_tiles, 0, 0))
    in_specs = [
        tok(d), tok(d), tok(N_DEV * blk),
        pl.BlockSpec((POOL_HALO, BR_W), lambda i: (halo_row_block(i), 2)),
        pl.BlockSpec((POOL_HALO, 2 * BR_W), lambda i: (halo_row_block(i), 2)),
        tok(3 * d), tok(d), _mod_spec(mod, layer, n_seq_tiles),
        *[_layer_spec(a, layer) for a in (g_mix, ln_g, ln_b, w_s, bexp, pool_w, pool_scale, conv_w)],
        ANY, ANY, ANY, ANY, ANY, ANY,
    ]
    acc = lambda shape: pl.BlockSpec(shape, lambda i: (0,) * len(shape))
    out_shape = [
        jax.ShapeDtypeStruct((t_all, d), F32), jax.ShapeDtypeStruct((t_all, N_DEV * blk), BF16),
        jax.ShapeDtypeStruct((t_all, 3 * d), BF16), jax.ShapeDtypeStruct((t_all, d), BF16),
        jax.ShapeDtypeStruct(mod.shape[1:], F32), jax.ShapeDtypeStruct((8, d), F32), jax.ShapeDtypeStruct((8, BR_W), F32),
        jax.ShapeDtypeStruct((HEADS, CHUNK, CHUNK), F32), jax.ShapeDtypeStruct((CHUNK, BR_W), F32),
        jax.ShapeDtypeStruct((n_win, HEAD_DIM, HEAD_DIM), F32),
    ]
    out_specs = [tok(d), tok(N_DEV * blk), tok(3 * d), tok(d), modspec, acc((8, d)), acc((8, BR_W)),
                 acc((HEADS, CHUNK, CHUNK)), acc((CHUNK, BR_W)), acc((n_win, HEAD_DIM, HEAD_DIM))]
    scratch = [
        pltpu.VMEM((d, N_DEV * blk), BF16), pltpu.VMEM((BR_W, d), BF16), pltpu.VMEM((BR_W, d), BF16),
        pltpu.VMEM((BR_W, d), BF16), pltpu.VMEM((d, d), BF16),
        pltpu.VMEM((tm + POOL_HALO, BR_W), F32), pltpu.VMEM((tm + CONV_HALO, BR_W), F32),
        pltpu.VMEM((tm + POOL_HALO, BR_W), F32), pltpu.VMEM((tm + CONV_HALO, BR_W), F32),
        pltpu.VMEM((tm, BR_W), F32), pltpu.VMEM((tm, BR_W), F32),
        pltpu.SemaphoreType.DMA((1 + 4 * N_DEV,)),
    ]
    return pl.pallas_call(
        body, name=f"mixer_bwd_{layer}", grid=(t_all // tm,), in_specs=in_specs, out_specs=out_specs, out_shape=out_shape,
        scratch_shapes=scratch, compiler_params=_cparams(dimension_semantics=("arbitrary",)),
    )(dxm, x, z, z, z, ycat, mo, mod, g_mix, ln_g, ln_b, w_s, bexp, pool_w, pool_scale, conv_w, win, wpa, wpb, wpc, wo, dep)


def _wgrad(a, b, a_spec, b_spec, out_struct, out_spec, grid_kn, tk, tn, split, name):
    t_all = a.shape[-2]
    tt = min(WGRAD_TOKENS, t_all)
    n_t = t_all // tt

    def body(a_ref, b_ref, o_ref, acc):
        t = pl.program_id(2)

        @pl.when(t == 0)
        def _():
            acc[...] = jnp.zeros(acc.shape, F32)

        acc[...] += _dot_tn(a_ref[...], b_ref[...])

        @pl.when(t == n_t - 1)
        def _():
            if split:
                for j in range(split):
                    w = tn // split
                    o_ref[j] = acc[:, j * w:(j + 1) * w].astype(o_ref.dtype)
            else:
                o_ref[...] = acc[...].astype(o_ref.dtype)

    return pl.pallas_call(
        body, name=name, grid=(*grid_kn, n_t), in_specs=[a_spec(tt), b_spec(tt)], out_specs=out_spec, out_shape=out_struct,
        scratch_shapes=[pltpu.VMEM((tk, tn), F32)],
        compiler_params=_cparams(dimension_semantics=("arbitrary", "arbitrary", "arbitrary")),
    )(a, b)


def _mixer_wgrads(h, dz, mrg, dmo, ocat, dycat, layer):
    d = h.shape[1]
    blk = dz.shape[1] // N_DEV
    g_win = _wgrad(
        h, dz, lambda tt: pl.BlockSpec((tt, d), lambda k, n, t: (t, 0)), lambda tt: pl.BlockSpec((tt, blk), lambda k, n, t: (t, n)),
        jax.ShapeDtypeStruct((N_DEV, d, blk), BF16), pl.BlockSpec((None, d, blk), lambda k, n, t: (n, 0, 0)),
        (1, N_DEV), d, blk, 0, f"wgrad_in_{layer}")
    g_wo = _wgrad(
        mrg, dmo, lambda tt: pl.BlockSpec((tt, d), lambda k, n, t: (t, 0)), lambda tt: pl.BlockSpec((tt, d), lambda k, n, t: (t, 0)),
        jax.ShapeDtypeStruct((d, d), BF16), pl.BlockSpec((d, d), lambda k, n, t: (0, 0)), (1, 1), d, d, 0, f"wgrad_o_{layer}")
    g_wp = []
    for n, nm in enumerate("abc"):
        g_wp.append(_wgrad(
            ocat, dycat, lambda tt, n=n: pl.BlockSpec((tt, BR_W), lambda k, nn, t: (t, n)),
            lambda tt, n=n: pl.BlockSpec((tt, d), lambda k, nn, t: (t, n)),
            jax.ShapeDtypeStruct((N_DEV, BR_W, d // N_DEV), BF16),
            pl.BlockSpec((N_DEV, BR_W, d // N_DEV), lambda k, nn, t: (0, 0, 0)), (1, 1), BR_W, d, N_DEV, f"wgrad_p{nm}_{layer}"))
    return [g_win, g_wo.reshape(N_DEV, d // N_DEV, d), *g_wp]


def _ffn_wgrads(h2, dab, hid, df, layer):
    d = h2.shape[1]
    fb = dab.shape[-1]
    n_hid = N_DEV // 2
    g_w13 = _wgrad(
        dab, h2, lambda tt: pl.BlockSpec((None, tt, fb), lambda k, n, t: (k, t, 0)),
        lambda tt: pl.BlockSpec((tt, d), lambda k, n, t: (t, 0)),
        jax.ShapeDtypeStruct((N_DEV, fb, d), BF16), pl.BlockSpec((None, fb, d), lambda k, n, t: (k, 0, 0)),
        (N_DEV, 1), fb, d, 0, f"wgrad_13_{layer}")
    g_w2 = _wgrad(
        hid, df, lambda tt: pl.BlockSpec((None, tt, fb), lambda k, n, t: (k, t, 0)),
        lambda tt: pl.BlockSpec((tt, d), lambda k, n, t: (t, 0)),
        jax.ShapeDtypeStruct((n_hid * fb, d), BF16), pl.BlockSpec((fb, d), lambda k, n, t: (k, 0)),
        (n_hid, 1), fb, d, 0, f"wgrad_2_{layer}")
    return [g_w13, g_w2.reshape(N_DEV, fb // 2, d)]


def _remote(src, dst, send_sem, recv_sem, dev):
    return pltpu.make_async_remote_copy(src_ref=src, dst_ref=dst, send_sem=send_sem, recv_sem=recv_sem, device_id=dev,
                                        device_id_type=MESH_ID)


HBM_SPEC = pl.BlockSpec(memory_space=pltpu.HBM)
SEM_SPEC = pl.BlockSpec(memory_space=pltpu.SEMAPHORE)
DATAFLOW = pltpu.SideEffectType.DATAFLOW_SIDE_EFFECTING
GATHER, SCATTER = "gather", "scatter"


def _xfer_sem(a, k):
    return a * (N_DEV - 1) + k - 1


def _gather_zones(weights, layer):
    me = _my_index()
    return [lax.dynamic_update_index_in_dim(lax.empty((N_DEV, *w.shape[1:]), BF16), w[layer].astype(BF16)[None], me, 0)
            for w in weights]


def _scatter_zones(partials):
    return [lax.empty(p.shape, p.dtype) for p in partials]


def _xfer_src(kind, src, land, a, me, idx):
    return land[a].at[me] if kind == GATHER else src[a].at[idx]


def _xfer_start(srcs, lands, kind, name, after):
    n, n_src = len(lands), len(srcs)

    def body(*refs):
        src, land = refs[:n_src], refs[n_src:n_src + n]
        send_sems, recv_sems = refs[n_src + n + 1], refs[n_src + n + 2]
        token = refs[-1]
        me = _my_index()
        for k in range(1, N_DEV):
            dev, idx = _peer(k)
            for a in range(n):
                q = _xfer_sem(a, k)
                _remote(_xfer_src(kind, src, land, a, me, idx), land[a].at[me], send_sems.at[q], recv_sems.at[q], dev).start()
        token[...] = jnp.zeros(token.shape, token.dtype)

    both = [*srcs, *lands]
    sems = pltpu.SemaphoreType.DMA((n * (N_DEV - 1),))
    out_shape = (sems, sems, *[pltpu.HBM(a.shape, a.dtype) for a in both], jax.ShapeDtypeStruct((8, 128), F32))
    outs = pl.pallas_call(
        body, name=name, in_specs=[HBM_SPEC] * len(both) + [ANY],
        out_specs=(SEM_SPEC, SEM_SPEC, *[HBM_SPEC] * len(both), VMEM_SPEC),
        out_shape=out_shape, input_output_aliases={i: 2 + i for i in range(len(both))},
        compiler_params=pltpu.CompilerParams(has_side_effects=DATAFLOW),
    )(*[pltpu.with_memory_space_constraint(a, pltpu.HBM) for a in both], after)
    return outs[0], outs[1], list(outs[2:2 + n_src]), list(outs[2 + n_src:2 + n_src + n]), outs[-1]


def _gather_two_level(zones, name, after):
    n = len(zones)

    def body(*refs):
        out = refs[n + 1:2 * n + 1]
        send_sems, recv_sems = refs[2 * n + 1], refs[2 * n + 2]
        x, y, c = lax.axis_index("x"), lax.axis_index("y"), lax.axis_index("c")
        index = lambda px, py, pc: 4 * px + 2 * py + pc
        me, sibling = (x, y, c), (x, y, 1 - c)
        chips = [(1 - x, y), (x, 1 - y), (1 - x, 1 - y)]

        def copy(a, k, block, to):
            rows = out[a].at[index(*block)]
            return _remote(rows, rows, send_sems.at[a * (N_DEV - 1) + k], recv_sems.at[a * (N_DEV - 1) + k], to)

        first = [copy(a, 0, me, sibling) for a in range(n)]
        first += [copy(a, 1 + j, me, (*chip, c)) for j, chip in enumerate(chips) for a in range(n)]
        for cp in first:
            cp.start()
        passed = []
        for j, chip in enumerate(chips):
            for a in range(n):
                copy(a, 1 + j, (*chip, c), me).wait_recv()
                passed.append(copy(a, 4 + j, (*chip, c), sibling))
                passed[-1].start()
        for a in range(n):
            copy(a, 0, sibling, me).wait_recv()
            for j, chip in enumerate(chips):
                copy(a, 4 + j, (*chip, 1 - c), me).wait_recv()
        for cp in first + passed:
            cp.wait_send()

    sems = pltpu.SemaphoreType.DMA((n * (N_DEV - 1),))
    return pl.pallas_call(
        body, name=name, in_specs=[ANY] * (n + 1), out_specs=[ANY] * n, out_shape=[jax.ShapeDtypeStruct(z.shape, z.dtype) for z in zones],
        input_output_aliases={i: i for i in range(n)}, scratch_shapes=[sems, sems],
        compiler_params=pltpu.CompilerParams(has_side_effects=True),
    )(*zones, after)


def _xfer_wait(send_sems, recv_sems, srcs, lands, rows, after, kind, name):
    n, n_src = len(lands), len(srcs)

    def body(*refs):
        src, land = refs[:n_src], refs[n_src:n_src + n]
        send_sems, recv_sems = refs[n_src + n], refs[n_src + n + 1]
        me = _my_index()
        for k in range(1, N_DEV):
            dev, idx = _peer(k)
            for a in range(n):
                q = _xfer_sem(rows[a], k)
                cp = _remote(_xfer_src(kind, src, land, a, me, idx), land[a].at[idx], send_sems.at[q], recv_sems.at[q], dev)
                cp.wait_send()
                cp.wait_recv()

    both = [*srcs, *lands]
    outs = pl.pallas_call(
        body, name=name, in_specs=[HBM_SPEC] * len(both) + [SEM_SPEC, SEM_SPEC, ANY], out_specs=[HBM_SPEC] * len(both),
        out_shape=[pltpu.HBM(a.shape, a.dtype) for a in both], input_output_aliases={i: i for i in range(len(both))},
        compiler_params=pltpu.CompilerParams(has_side_effects=DATAFLOW),
    )(*both, send_sems, recv_sems, after)
    return list(outs[:n_src]), list(outs[n_src:])


def _pre(c_pad, conv_pad, w_mod, b_mod_mine):
    n_layers, d, blk = w_mod.shape

    def body(c_ref, conv_ref, wmod_ref, bmod_ref, cact_ref, mod_ref, convall_ref, cact_mine, msh, send_sems, recv_sems):
        me = _my_index()
        c = c_ref[...]
        cact_mine[...] = c * _sigmoid(c)
        cact_ref[me] = cact_mine[...]
        convall_ref[me] = conv_ref[...]
        sends = []
        for k in range(1, N_DEV):
            dev, _ = _peer(k)
            sends.append(_remote(cact_mine, cact_ref.at[me], send_sems.at[0, k - 1], recv_sems.at[0, k - 1], dev))
            sends.append(_remote(conv_ref, convall_ref.at[me], send_sems.at[1, k - 1], recv_sems.at[1, k - 1], dev))
        for cp in sends:
            cp.start()
        for k in range(1, N_DEV):
            dev, idx = _peer(k)
            _remote(cact_mine, cact_ref.at[idx], send_sems.at[0, k - 1], recv_sems.at[0, k - 1], dev).wait_recv()
            _remote(conv_ref, convall_ref.at[idx], send_sems.at[1, k - 1], recv_sems.at[1, k - 1], dev).wait_recv()
        for cp in sends:
            cp.wait_send()
        cact_b = cact_ref[...].reshape(N_DEV * 8, d).astype(BF16)
        for l in range(n_layers):
            m = _dot(cact_b, wmod_ref[l].astype(BF16)) + bmod_ref[l]
            msh[l] = m.reshape(N_DEV, 8, blk)
        mod_ref[me] = msh[:, me]
        sends = []
        for k in range(1, N_DEV):
            dev, idx = _peer(k)
            sends.append(_remote(msh.at[:, idx], mod_ref.at[me], send_sems.at[2, k - 1], recv_sems.at[2, k - 1], dev))
        for cp in sends:
            cp.start()
        for k in range(1, N_DEV):
            dev, idx = _peer(k)
            _remote(msh.at[:, idx], mod_ref.at[idx], send_sems.at[2, k - 1], recv_sems.at[2, k - 1], dev).wait_recv()
        for cp in sends:
            cp.wait_send()

    out_shape = [jax.ShapeDtypeStruct((N_DEV, 8, d), F32), jax.ShapeDtypeStruct((N_DEV, n_layers, 8, blk), F32),
                 jax.ShapeDtypeStruct((N_DEV, *conv_pad.shape), F32)]
    return pl.pallas_call(
        body, name="pre", in_specs=[VMEM_SPEC] * 4, out_specs=[VMEM_SPEC] * 3, out_shape=out_shape,
        scratch_shapes=[pltpu.VMEM((8, d), F32), pltpu.VMEM((n_layers, N_DEV, 8, blk), F32),
                        pltpu.SemaphoreType.DMA((3, N_DEV - 1)), pltpu.SemaphoreType.DMA((3, N_DEV - 1))],
        compiler_params=_cparams(has_side_effects=True),
    )(c_pad, conv_pad, w_mod, b_mod_mine)


def _small_reduce(zone_buf, own_buf, zone_d, own_d, cact_all):
    _, seg, _ = zone_buf.shape
    _, n_layers, _, blk = zone_d.shape
    d = cact_all.shape[-1]

    def body(zb_ref, ob_ref, zd_ref, od_ref, cact_ref, red_ref, gw_ref, dcols):
        me = _my_index()
        total = jnp.where(me == 0, ob_ref[0], zb_ref[0])
        for p in range(1, N_DEV):
            total = total + jnp.where(me == p, ob_ref[p], zb_ref[p])
        red_ref[...] = total
        for p in range(N_DEV):
            dcols[p] = jnp.where(me == p, od_ref[p], zd_ref[p])
        cact_b = cact_ref[...].reshape(N_DEV * 8, d).astype(BF16)
        for l in range(n_layers):
            gw_ref[l] = _dot_tn(cact_b, dcols[:, l].reshape(N_DEV * 8, blk).astype(BF16))

    out_shape = [jax.ShapeDtypeStruct((seg, 128), F32), jax.ShapeDtypeStruct((n_layers, d, blk), F32)]
    return pl.pallas_call(
        body, name="small_reduce", in_specs=[VMEM_SPEC] * 5, out_specs=[VMEM_SPEC] * 2, out_shape=out_shape,
        scratch_shapes=[pltpu.VMEM(zone_d.shape, F32)], compiler_params=_cparams(),
    )(zone_buf, own_buf, zone_d, own_d, cact_all)


def _adamw_math(w, g, m, v):
    m = ADAM_B1 * m + (1.0 - ADAM_B1) * g
    v = ADAM_B2 * v + (1.0 - ADAM_B2) * (g * g)
    m_hat = m / (1.0 - ADAM_B1 ** ADAM_STEP)
    v_hat = v / (1.0 - ADAM_B2 ** ADAM_STEP)
    delta = -ADAM_LR * (m_hat / (jnp.sqrt(v_hat) + ADAM_EPS) + ADAM_WD * w)
    return delta, m, v


def _adamw(parts, w, m, v, name, first=0, earlier=None, own=None):
    n_layers = len(parts)
    n_parts, rows, cols = parts[0].shape
    tr = rows
    while tr * cols * 4 > (1 << 20) and tr % 32 == 0:
        tr //= 2
    n_r = rows // tr
    n_earlier = 0 if earlier is None else 4
    n_own = 0 if own is None else n_layers
    n_in = n_layers + n_own + 3

    def body(me_ref, *refs):
        p_refs, o_refs = refs[:n_layers], refs[n_layers:n_layers + n_own]
        w_ref, m_ref, v_ref = refs[n_layers + n_own:n_in]
        g_out, d_out, m_out, v_out = refs[n_in + n_earlier:]
        layer = pl.program_id(0)

        def part(q, p):
            value = p_refs[q][p]
            if n_own:
                value = jnp.where(me_ref[0] == p, o_refs[q][...], value)
            return value.astype(F32)

        for q in range(n_layers):

            @pl.when(layer == q)
            def _(q=q):
                g = part(q, 0)
                for p in range(1, n_parts):
                    g = g + part(q, p)
                delta, m_new, v_new = _adamw_math(w_ref[...], g, m_ref[...], v_ref[...])
                g_out[...] = g
                d_out[...] = delta
                m_out[...] = m_new
                v_out[...] = v_new

    def moving(q, l, r):
        return jnp.where(l == q, r, jnp.where(l < q, 0, n_r - 1))

    parts_spec = lambda q: pl.BlockSpec((n_parts, tr, cols), lambda l, r, me: (0, moving(q, l, r), 0))
    own_spec = lambda q: pl.BlockSpec((None, tr, cols), lambda l, r, me: (me[0], moving(q, l, r), 0))
    spec = pl.BlockSpec((None, tr, cols), lambda l, r, me: (first + l, r, 0))
    out = jax.ShapeDtypeStruct(w.shape, F32)
    grid_spec = pltpu.PrefetchScalarGridSpec(
        num_scalar_prefetch=1, grid=(n_layers, n_r),
        in_specs=[parts_spec(q) for q in range(n_layers)] + [own_spec(q) for q in range(n_own)] + [spec, spec, spec]
        + [ANY] * n_earlier,
        out_specs=[spec] * 4)
    return pl.pallas_call(
        body, name=name, grid_spec=grid_spec, out_shape=[out] * 4,
        input_output_aliases={1 + n_in + i: i for i in range(n_earlier)},
        compiler_params=_cparams(dimension_semantics=("arbitrary", "arbitrary")),
    )(_my_index().astype(jnp.int32).reshape(1), *parts, *(own or ()), w, m, v, *(earlier or ()))


def _adamw_small(gs, ws, ms, vs):
    n = len(ws)

    def body(*refs):
        g, w, m, v = (refs[i * n:(i + 1) * n] for i in range(4))
        d_out, m_out, v_out = (refs[i * n:(i + 1) * n] for i in range(4, 7))
        for i in range(n):
            delta, m_new, v_new = _adamw_math(w[i][...], g[i][...], m[i][...], v[i][...])
            d_out[i][...] = delta
            m_out[i][...] = m_new
            v_out[i][...] = v_new

    out = [jax.ShapeDtypeStruct(w.shape, F32) for w in ws]
    res = pl.pallas_call(body, name="adamw_small", in_specs=[VMEM_SPEC] * (4 * n), out_specs=[VMEM_SPEC] * (3 * n),
                         out_shape=out * 3, compiler_params=_cparams())(*gs, *ws, *ms, *vs)
    return res[:n], res[n:2 * n], res[2 * n:]


def _pack(arrays, rows_multiple):
    flat = jnp.concatenate([a.reshape(-1) for a in arrays])
    per = 128 * rows_multiple
    total = -(-flat.shape[0] // per) * per
    return jnp.pad(flat, (0, total - flat.shape[0])).reshape(total // 128, 128)


def _unpack(buf, like):
    flat = buf.reshape(-1)
    out, off = [], 0
    for a in like:
        out.append(flat[off:off + a.size].reshape(a.shape))
        off += a.size
    return out


def kernel(x, c, w_mod, b_mod, g_mix, w_in, gm_ln_g, gm_ln_b, gm_w_s, gm_b_s, w_pa, pool_w, pool_scale, w_pb, conv_w, w_pc, w_o, g_ffn, w_13, w_2, g_final, loss_target, m_w_mod, m_b_mod, m_g_mix, m_w_in, m_gm_ln_g, m_gm_ln_b, m_gm_w_s, m_gm_b_s, m_w_pa, m_pool_w, m_pool_scale, m_w_pb, m_conv_w, m_w_pc, m_w_o, m_g_ffn, m_w_13, m_w_2, m_g_final, v_w_mod, v_b_mod, v_g_mix, v_w_in, v_gm_ln_g, v_gm_ln_b, v_gm_w_s, v_gm_b_s, v_w_pa, v_pool_w, v_pool_scale, v_w_pb, v_conv_w, v_w_pc, v_w_o, v_g_ffn, v_w_13, v_w_2, v_g_final):
    nb, seq, d = x.shape
    n_layers = w_in.shape[0]
    t_all = nb * seq
    blk = w_in.shape[-1]
    me = _my_index()
    conv_shard = conv_w.shape[-1]

    c_pad = jnp.pad(c, ((0, 8 - nb), (0, 0)))
    conv_pad = jnp.pad(conv_w.reshape(n_layers * 3, conv_shard), ((0, 16 - n_layers * 3), (0, 128 - conv_shard)))
    b_mod_mine = lax.dynamic_slice_in_dim(b_mod, me * blk, blk, axis=1).reshape(n_layers, 1, blk)
    cact_all, mod_blocks, conv_all = _pre(c_pad, conv_pad, w_mod, b_mod_mine)
    mod = jnp.transpose(mod_blocks, (1, 2, 0, 3)).reshape(n_layers, 8, N_MOD, d)[:, :nb]
    mod = jnp.pad(mod, ((0, 0), (0, 0), (0, 8 - N_MOD), (0, 0)))
    conv_full = jnp.transpose(conv_all[:, :n_layers * 3, :conv_shard].reshape(N_DEV, n_layers, 3, conv_shard), (1, 2, 0, 3))
    conv_full = jnp.pad(conv_full.reshape(n_layers, 3, N_DEV * conv_shard), ((0, 0), (0, 5), (0, 0)))
    bexp = jnp.repeat(jnp.transpose(gm_b_s, (0, 2, 1)), HEAD_DIM, axis=2)

    mixer_w, ffn_w = [w_in, w_o, w_pa, w_pb, w_pc], [w_13, w_2]
    n_mix = len(mixer_w)

    def send_weights(l, ws, name, after):
        return _xfer_start([], _gather_zones(ws, l), GATHER, name, after)

    def send_grads(partials, name, after):
        return _xfer_start(partials, _scatter_zones(partials), SCATTER, name, after)

    def arrived(flight, after, kind, name):
        send_sems, recv_sems, srcs, zones, _ = flight
        return _xfer_wait(send_sems, recv_sems, srcs, zones, list(range(len(zones))), after, kind, name)

    rows = lambda a: a.reshape(n_layers, 1, -1)
    mix_small = (rows(g_mix), rows(gm_ln_g), rows(gm_ln_b), gm_w_s, bexp, pool_w, rows(pool_scale), conv_full)
    g_ffn_rows = rows(g_ffn)
    xs = x.reshape(t_all, d)
    saved, weights = [], []
    after = mod
    for l in range(n_layers):
        if l == 0:
            win_g, wo_g, wpa_g, wpb_g, wpc_g = _gather_two_level(_gather_zones(mixer_w, 0), "gather_mixer_0", cact_all)
        else:
            win_g, wo_g, wpa_g, wpb_g, wpc_g = arrived(flight, after, GATHER, f"gather_wait_mixer_{l}")[1]
        flight = send_weights(l, ffn_w, f"gather_start_ffn_{l}", win_g)
        wo_g = wo_g.reshape(d, d)
        h, z, ycat, ocat, mrg, mo, xmid = _mixer_fwd(xs, mod, *mix_small, win_g, wpa_g, wpb_g, wpc_g, wo_g, flight[4], l, seq)
        w13_g, w2_g = arrived(flight, xmid, GATHER, f"gather_wait_ffn_{l}")[1]
        dep = w13_g
        if l + 1 < n_layers:
            flight = send_weights(l + 1, mixer_w, f"gather_start_mixer_{l + 1}", w13_g)
            dep = flight[4]
        w2_g = w2_g.reshape(N_DEV * w_2.shape[1], d)
        h2, ab, hid, f, xo = _ffn_fwd(xmid, mod, g_ffn_rows, w13_g, w2_g, dep, l, seq)
        saved.append((xs, h, z, ycat, ocat, mrg, mo, xmid, h2, ab, hid, f))
        weights.append((win_g, wpa_g, wpb_g, wpc_g, wo_g, w13_g, w2_g))
        xs = after = xo

    dx, loss_blk, dgf_blk = _final_loss(xs, g_final.reshape(1, d), loss_target.reshape(t_all, d), seq)
    loss = lax.psum(loss_blk[0, 0], ("x", "y", "c"))

    ffn_flight = [None] * n_layers
    mix_flight = [None] * n_layers
    small_grads = [None] * n_layers
    dmods = [None] * n_layers
    dep = dx
    for l in reversed(range(n_layers)):
        x_in, h, z, ycat, ocat, mrg, mo, xmid, h2, ab, hid, f = saved[l]
        win_g, wpa_g, wpb_g, wpc_g, wo_g, w13_g, w2_g = weights[l]
        dxm, df, dab, dmod2, dg_ffn = _ffn_bwd(dx, xmid, ab, f, mod, g_ffn_rows, w13_g, w2_g, dep, l, seq)
        ffn_flight[l] = send_grads(_ffn_wgrads(h2, dab, hid, df, l), f"grads_start_ffn_{l}", dxm)
        dx, dz, dycat, dmo, dmod1, dg_mix, sm, dws, dssum, dpw = _mixer_bwd(
            dxm, x_in, z, ycat, mo, mod, *mix_small, win_g, wpa_g, wpb_g, wpc_g, wo_g, ffn_flight[l][4], l, seq)
        if l > 0:
            mix_flight[l] = send_grads(_mixer_wgrads(h, dz, mrg, dmo, ocat, dycat, l), f"grads_start_mixer_{l}", dx)
            dep = mix_flight[l][4]
        dmod = jnp.concatenate([dmod1[:, 0:3], dmod2[:, 0:3]], axis=1).reshape(nb, N_MOD * d)
        dmods[l] = dmod
        db_s = jnp.transpose(jnp.sum(dssum.reshape(CHUNK, HEADS, HEAD_DIM), axis=2))
        small_grads[l] = [jnp.sum(dmod, axis=0), dg_mix[0], sm[0], sm[1], dws, db_s, dpw, sm[2], sm[3:6], dg_ffn[0]]
    grad_x = dx.reshape(nb, seq, d)

    names = ["b_mod", "g_mix", "ln_g", "ln_b", "w_s", "b_s", "pool_w", "pool_scale", "conv_w", "g_ffn"]
    per_name = [jnp.stack([small_grads[l][n] for l in range(n_layers)]) for n in range(len(names))] + [dgf_blk[0]]
    buf = _pack(per_name, 8 * N_DEV)
    dmod_all = jnp.pad(jnp.stack(dmods), ((0, 0), (0, 8 - nb), (0, 0)))
    dmod_blocks = jnp.transpose(dmod_all.reshape(n_layers, 8, N_DEV, blk), (2, 0, 1, 3))
    seg = buf.shape[0] // N_DEV
    small_flight = send_grads([buf.reshape(N_DEV, seg, 128), dmod_blocks], "small_start", buf)
    mixer_partials = _mixer_wgrads(h, dz, mrg, dmo, ocat, dycat, 0)
    (own_buf, own_d), (zone_buf, zone_d) = arrived(small_flight, mixer_partials[0], SCATTER, "small_wait")
    red_seg, grad_w_mod = _small_reduce(zone_buf, own_buf, zone_d, own_d, cact_all)
    red_zone = lax.dynamic_update_index_in_dim(lax.empty((N_DEV, seg, 128), F32), red_seg[None], me, 0)
    red_flight = _xfer_start([], [red_zone], GATHER, "small_gather_start", red_seg)

    mix_flight[0] = send_grads(mixer_partials, "grads_start_mixer_0", red_flight[4])
    results = {}
    results["w_mod"] = _adamw([grad_w_mod[l][None] for l in range(n_layers)], w_mod, m_w_mod, v_w_mod, "adamw_w_mod")
    red = arrived(red_flight, results["w_mod"][1], GATHER, "small_gather_wait")[1][0].reshape(buf.shape)
    (g_b_mod, g_g_mix, g_ln_g, g_ln_b, g_w_s, g_b_s, g_pool_w, g_pool_scale, g_conv_full, g_g_ffn, g_g_final) = _unpack(red, per_name)
    g_conv = lax.dynamic_slice_in_dim(g_conv_full, me * conv_shard, conv_shard, axis=2)

    small_w =[b_mod, g_mix, gm_ln_g, gm_ln_b, gm_w_s, gm_b_s, pool_w, pool_scale, conv_w, g_ffn, g_final]
    small_m = [m_b_mod, m_g_mix, m_gm_ln_g, m_gm_ln_b, m_gm_w_s, m_gm_b_s, m_pool_w, m_pool_scale, m_conv_w, m_g_ffn, m_g_final]
    small_v = [v_b_mod, v_g_mix, v_gm_ln_g, v_gm_ln_b, v_gm_w_s, v_gm_b_s, v_pool_w, v_pool_scale, v_conv_w, v_g_ffn, v_g_final]
    small_g = [g_b_mod, g_g_mix, g_ln_g, g_ln_b, g_w_s, g_b_s, g_pool_w, g_pool_scale, g_conv, g_g_ffn, g_g_final]
    small_names = ["b_mod", "g_mix", "gm_ln_g", "gm_ln_b", "gm_w_s", "gm_b_s", "pool_w", "pool_scale", "conv_w", "g_ffn", "g_final"]
    sd, sm_new, sv_new = _adamw_small(small_g, small_w, small_m, small_v)
    for n, nm in enumerate(small_names):
        results[nm] = (small_g[n], sd[n], sm_new[n], sv_new[n])

    layers = list(range(n_layers))
    done = (results["w_mod"][1][0, 0, 0] + sd[-1][0]).reshape(1)
    ffn_recv = [arrived(ffn_flight[l], done, SCATTER, f"grads_wait_ffn_{l}") for l in reversed(layers)][::-1]
    mix_recv = [None] + [arrived(mix_flight[l], done, SCATTER, f"grads_wait_mixer_{l}") for l in reversed(layers[1:])][::-1]
    swap = lambda a: jnp.swapaxes(a, 1, 2)
    got = lambda recv, ls, a: dict(parts=[recv[l][1][a] for l in ls], own=[recv[l][0][a] for l in ls])
    results["w_13"] = [swap(r) for r in _adamw(w=swap(w_13), m=swap(m_w_13), v=swap(v_w_13), name="adamw_w_13", **got(ffn_recv, layers, 0))]
    results["w_2"] = _adamw(w=w_2, m=m_w_2, v=v_w_2, name="adamw_w_2", **got(ffn_recv, layers, 1))
    mix_m = [m_w_in, m_w_o, m_w_pa, m_w_pb, m_w_pc]
    mix_v = [v_w_in, v_w_o, v_w_pa, v_w_pb, v_w_pc]
    mix_names = ["w_in", "w_o", "w_pa", "w_pb", "w_pc"]
    early = [_adamw(w=mixer_w[a], m=mix_m[a], v=mix_v[a], name=f"adamw_{mix_names[a]}_later", first=1, **got(mix_recv, layers[1:], a))
             for a in range(n_mix)]
    done = (results["w_13"][1][0, 0, 0] + results["w_2"][1][0, 0, 0] + sum(e[1][1, 0, 0] for e in early)).reshape(1)
    mix_recv[0] = arrived(mix_flight[0], done, SCATTER, "grads_wait_mixer_0")
    for a in range(n_mix):
        results[mix_names[a]] = _adamw(w=mixer_w[a], m=mix_m[a], v=mix_v[a], name=f"adamw_{mix_names[a]}_first", earlier=early[a],
                                       **got(mix_recv, [0], a))

    order = ["w_mod", "b_mod", "g_mix", "w_in", "gm_ln_g", "gm_ln_b", "gm_w_s", "gm_b_s", "w_pa", "pool_w", "pool_scale",
             "w_pb", "conv_w", "w_pc", "w_o", "g_ffn", "w_13", "w_2", "g_final"]
    return (loss, grad_x, *[results[nm][0] for nm in order], *[results[nm][1] for nm in order],
            *[results[nm][2] for nm in order], *[results[nm][3] for nm in order])
```

```python
import functools

import jax
import jax.numpy as jnp
from jax import lax
from jax.experimental import pallas as pl
from jax.experimental.pallas import tpu as pltpu

F32 = jnp.float32
BF16 = jnp.bfloat16
MESH_ID = pl.DeviceIdType.MESH

N_DEV = 8
EPS = 1e-6
CHUNK = 128
HEADS = 4
HEAD_DIM = 128
BR_W = 512
POOL_WINDOWS = (2, 4, 8, 16)
POOL_HALO = 16
CONV_HALO = 8
N_MOD = 6
ADAM_LR = 0.001
ADAM_B1 = 0.9
ADAM_B2 = 0.999
ADAM_EPS = 1e-08
ADAM_WD = 0.01
ADAM_STEP = 10

TOKEN_TILE = 256
ELEMENTWISE_TILE = 1024
WGRAD_TOKENS = 2048
VMEM_LIMIT = 56 * 1024 * 1024
GELU_K = 0.7978845608028654
GELU_C = 0.044715

ANY = pl.BlockSpec(memory_space=pl.ANY)
VMEM_SPEC = pl.BlockSpec(memory_space=pltpu.VMEM)


def _cparams(**kw):
    return pltpu.CompilerParams(vmem_limit_bytes=VMEM_LIMIT, **kw)


def _dot(a, b):
    return jnp.dot(a, b, preferred_element_type=F32)


def _dot_nt(a, b):
    return lax.dot_general(a, b, (((1,), (1,)), ((), ())), preferred_element_type=F32)


def _dot_tn(a, b):
    return lax.dot_general(a, b, (((0,), (0,)), ((), ())), preferred_element_type=F32)


def _colsum(a):
    return jnp.sum(a, axis=0, keepdims=True)


def _sigmoid(x):
    return 0.5 * jnp.tanh(0.5 * x) + 0.5


def _gelu(x):
    x2 = x * x
    t = jnp.tanh(x * (GELU_K + (GELU_K * GELU_C) * x2))
    return (0.5 * x) * (1.0 + t), t, x2


def _gelu_grad(x, t, x2):
    one_t = 1.0 + t
    return 0.5 * one_t + (0.5 * x) * (one_t * (1.0 - t)) * (GELU_K + (3.0 * GELU_K * GELU_C) * x2)


def _tril_mask():
    r = lax.broadcasted_iota(jnp.int32, (CHUNK, CHUNK), 0)
    c = lax.broadcasted_iota(jnp.int32, (CHUNK, CHUNK), 1)
    return (r >= c).astype(F32)


def _my_index():
    return 4 * lax.axis_index("x") + 2 * lax.axis_index("y") + lax.axis_index("c")


def _peer(k):
    x, y, c = lax.axis_index("x"), lax.axis_index("y"), lax.axis_index("c")
    px = 1 - x if (k >> 2) & 1 else x
    py = 1 - y if (k >> 1) & 1 else y
    pc = 1 - c if k & 1 else c
    return (px, py, pc), 4 * px + 2 * py + pc


def _load_weights(step, pairs, sem):
    @pl.when(step == 0)
    def _():
        copies = [pltpu.make_async_copy(src, dst, sem.at[n]) for n, (src, dst) in enumerate(pairs)]
        for cp in copies:
            cp.start()
        for cp in copies:
            cp.wait()


def _wp_pairs(wp_hbm, wp_v):
    return [(wp_hbm.at[j], wp_v.at[:, pl.ds(HEAD_DIM * j, HEAD_DIM)]) for j in range(N_DEV)]


def _layer_spec(a, layer):
    return pl.BlockSpec((None, *a.shape[1:]), lambda i: (layer,) + (0,) * (a.ndim - 1))


def _mod_spec(mod, layer, n_seq_tiles):
    return pl.BlockSpec((None, 1, *mod.shape[2:]), lambda i: (layer, i // n_seq_tiles, 0, 0))


def _rms_mod(x, g, shift, scale):
    rstd = lax.rsqrt(jnp.mean(x * x, axis=-1, keepdims=True) + EPS)
    xn = x * rstd
    return xn, rstd, (xn * g) * (1.0 + scale) + shift


def _rms_mod_bwd(dh, xn, rstd, g, scale):
    dxn = dh * (1.0 + scale) * g
    dx = rstd * (dxn - xn * jnp.mean(dxn * xn, axis=-1, keepdims=True))
    return dx, _colsum(dh), _colsum(dh * (xn * g)), _colsum(dh * (1.0 + scale) * xn)


def _gmlp_s(vn_b, wmask_b, bexp, s_scr, tm):
    for ch in range(tm // CHUNK):
        rows = slice(ch * CHUNK, (ch + 1) * CHUNK)
        for hh in range(HEADS):
            cols = slice(hh * HEAD_DIM, (hh + 1) * HEAD_DIM)
            s_scr[rows, cols] = _dot(wmask_b[hh], vn_b[rows, cols]) + bexp[:, cols]
    return s_scr[...]


def _inv_count(pos, win):
    return 1.0 / jnp.minimum(pos + 1, win).astype(F32)


def _window_sums(ext, tm, trailing):
    n = tm + POOL_HALO
    sums = []
    for g, win in enumerate(POOL_WINDOWS):
        s = ext[:, g * HEAD_DIM:(g + 1) * HEAD_DIM]
        span = 1
        while span < win:
            s = s + pltpu.roll(s, span if trailing else n - span, 0)
            span *= 2
        sums.append(s[POOL_HALO:POOL_HALO + tm] if trailing else s[0:tm])
    return sums


def _pool_p(xb, xbext, pos, tm):
    sums = _window_sums(xbext, tm, True)
    return [sums[g] * _inv_count(pos, win) - xb[:, g * HEAD_DIM:(g + 1) * HEAD_DIM] for g, win in enumerate(POOL_WINDOWS)]


def _mixer_fwd(x, mod, g_mix, ln_g, ln_b, w_s, bexp, pool_w, pool_scale, conv_w, win, wpa, wpb, wpc, wo, dep, layer, seq):
    t_all, d = x.shape
    tm = min(TOKEN_TILE, seq)
    n_seq_tiles = seq // tm
    blk = win.shape[-1]

    def body(x_ref, mod_ref, gmix_ref, lng_ref, lnb_ref, ws_ref, bexp_ref, pw_ref, ps_ref, cw_ref,
             win_hbm, wpa_hbm, wpb_hbm, wpc_hbm, wo_hbm, dep_ref,
             h_ref, z_ref, ycat_ref, ocat_ref, mrg_ref, mo_ref, xmid_ref,
             win_v, wpa_v, wpb_v, wpc_v, wo_v, xbext, zcext, s_scr, sem):
        i = pl.program_id(0)
        pairs = [(win_hbm, win_v), (wo_hbm, wo_v)]
        pairs += _wp_pairs(wpa_hbm, wpa_v) + _wp_pairs(wpb_hbm, wpb_v) + _wp_pairs(wpc_hbm, wpc_v)
        _load_weights(i, pairs, sem)
        tile_in_seq = i % n_seq_tiles

        @pl.when(tile_in_seq == 0)
        def _():
            xbext[0:POOL_HALO, :] = jnp.zeros((POOL_HALO, BR_W), F32)
            zcext[0:CONV_HALO, :] = jnp.zeros((CONV_HALO, BR_W), F32)

        x_t = x_ref[...]
        shift1, scale1, gate1 = mod_ref[0, 0:1, :], mod_ref[0, 1:2, :], mod_ref[0, 2:3, :]
        _, _, h = _rms_mod(x_t, gmix_ref[...], shift1, scale1)
        hb = h.astype(BF16)
        h_ref[...] = hb
        def project(j):
            zj = _dot(hb, win_v[j])
            z_ref[:, j * blk:(j + 1) * blk] = zj.astype(BF16)
            return zj

        z0, z1 = project(0), project(1)
        u = z0[:, 0:BR_W]
        v = jnp.concatenate([z0[:, BR_W:blk], z1[:, 0:2 * BR_W - blk]], axis=1)
        xb = z1[:, 2 * BR_W - blk:blk]

        gu = _gelu(u)[0]
        gv = _gelu(v)[0]
        mu = jnp.mean(gv, axis=-1, keepdims=True)
        cen = gv - mu
        rs = lax.rsqrt(jnp.mean(cen * cen, axis=-1, keepdims=True) + EPS)
        vn = (cen * rs) * lng_ref[...] + lnb_ref[...]
        mask = _tril_mask()
        wmask_b = [(ws_ref[hh] * mask).astype(BF16) for hh in range(HEADS)]
        s = _gmlp_s(vn.astype(BF16), wmask_b, bexp_ref[...], s_scr, tm)
        oa = (gu * s).astype(BF16)
        ya = _dot(oa, wpa_v[...])

        xbext[POOL_HALO:POOL_HALO + tm, :] = xb
        pos = tile_in_seq * tm + lax.broadcasted_iota(jnp.int32, (tm, 1), 0)
        ps = _pool_p(xb, xbext, pos, tm)
        qs = [_dot(ps[g].astype(BF16), pw_ref[g].astype(BF16)) for g in range(len(POOL_WINDOWS))]
        ob = (jnp.concatenate(qs, axis=1) * ps_ref[...]).astype(BF16)
        yb = _dot(ob, wpb_v[...])
        xbext[0:POOL_HALO, :] = xbext[tm:tm + POOL_HALO, :]

        z2, z3 = project(2), project(3)
        bg = z2[:, 0:BR_W]
        cg = jnp.concatenate([z2[:, BR_W:blk], z3[:, 0:2 * BR_W - blk]], axis=1)
        hc = z3[:, 2 * BR_W - blk:blk]
        zz = cg * hc
        zcext[CONV_HALO:CONV_HALO + tm, :] = zz
        yconv = (cw_ref[0:1, :] * zcext[pl.ds(CONV_HALO - 2, tm), :] + cw_ref[1:2, :] * zcext[pl.ds(CONV_HALO - 1, tm), :]
                 + cw_ref[2:3, :] * zz)
        oc = (bg * yconv).astype(BF16)
        yc = _dot(oc, wpc_v[...])
        zcext[0:CONV_HALO, :] = zcext[tm:tm + CONV_HALO, :]

        ocat_ref[:, 0:BR_W] = oa
        ocat_ref[:, BR_W:2 * BR_W] = ob
        ocat_ref[:, 2 * BR_W:3 * BR_W] = oc
        ycat_ref[:, 0:d] = ya.astype(BF16)
        ycat_ref[:, d:2 * d] = yb.astype(BF16)
        ycat_ref[:, 2 * d:3 * d] = yc.astype(BF16)

        ys = (ya, yb, yc)
        zg = jnp.concatenate([project(j).astype(BF16) for j in range(4, N_DEV)], axis=1)
        mb = _sigmoid(zg[:, 0:d]) * ys[0].astype(BF16)
        for n in range(1, 3):
            mb = mb + _sigmoid(zg[:, n * d:(n + 1) * d]) * ys[n].astype(BF16)
        mrg_ref[...] = mb
        mo = _dot(mb, wo_v[...])
        mo_ref[...] = mo.astype(BF16)
        xmid_ref[...] = x_t + gate1 * mo

    tok = lambda cols: pl.BlockSpec((tm, cols), lambda i: (i, 0))
    in_specs = [
        tok(d),
        _mod_spec(mod, layer, n_seq_tiles),
        *[_layer_spec(a, layer) for a in (g_mix, ln_g, ln_b, w_s, bexp, pool_w, pool_scale, conv_w)],
        ANY, ANY, ANY, ANY, ANY, ANY,
    ]
    out_shape = [
        jax.ShapeDtypeStruct((t_all, d), BF16),
        jax.ShapeDtypeStruct((t_all, N_DEV * blk), BF16),
        jax.ShapeDtypeStruct((t_all, 3 * d), BF16),
        jax.ShapeDtypeStruct((t_all, 3 * BR_W), BF16),
        jax.ShapeDtypeStruct((t_all, d), BF16),
        jax.ShapeDtypeStruct((t_all, d), BF16),
        jax.ShapeDtypeStruct((t_all, d), F32),
    ]
    out_specs = [tok(d), tok(N_DEV * blk), tok(3 * d), tok(3 * BR_W), tok(d), tok(d), tok(d)]
    scratch = [
        pltpu.VMEM((N_DEV, d, blk), BF16), pltpu.VMEM((BR_W, d), BF16), pltpu.VMEM((BR_W, d), BF16),
        pltpu.VMEM((BR_W, d), BF16), pltpu.VMEM((d, d), BF16),
        pltpu.VMEM((tm + POOL_HALO, BR_W), F32), pltpu.VMEM((tm + CONV_HALO, BR_W), F32), pltpu.VMEM((tm, BR_W), F32),
        pltpu.SemaphoreType.DMA((2 + 3 * N_DEV,)),
    ]
    return pl.pallas_call(
        body, name=f"mixer_fwd_{layer}", grid=(t_all // tm,), in_specs=in_specs, out_specs=out_specs, out_shape=out_shape,
        scratch_shapes=scratch, compiler_params=_cparams(dimension_semantics=("arbitrary",)),
    )(x, mod, g_mix, ln_g, ln_b, w_s, bexp, pool_w, pool_scale, conv_w, win, wpa, wpb, wpc, wo, dep)


def _ffn_fwd(x, mod, g_ffn, w13, w2, dep, layer, seq):
    t_all, d = x.shape
    tm = min(TOKEN_TILE, seq)
    n_seq_tiles = seq // tm
    fb = w13.shape[-1]
    n_hid = N_DEV // 2

    def body(x_ref, mod_ref, g_ref, w13_hbm, w2_hbm, dep_ref, h_ref, ab_ref, hid_ref, f_ref, xo_ref, w13_v, w2_v, sem):
        i = pl.program_id(0)
        _load_weights(i, [(w13_hbm, w13_v), (w2_hbm, w2_v)], sem)
        x_t = x_ref[...]
        shift2, scale2, gate2 = mod_ref[0, 3:4, :], mod_ref[0, 4:5, :], mod_ref[0, 5:6, :]
        _, _, h = _rms_mod(x_t, g_ref[...], shift2, scale2)
        hb = h.astype(BF16)
        h_ref[...] = hb
        f = jnp.zeros((tm, d), F32)
        for k in range(n_hid):
            a = _dot(hb, w13_v[k])
            b = _dot(hb, w13_v[n_hid + k])
            a, b = a.astype(BF16), b.astype(BF16)
            ab_ref[k] = a
            ab_ref[n_hid + k] = b
            hid = (a * _sigmoid(a)) * b
            hid_ref[k] = hid
            f = f + _dot(hid, w2_v[k * fb:(k + 1) * fb, :])
        f_ref[...] = f.astype(BF16)
        xo_ref[...] = x_t + gate2 * f

    tok = lambda cols: pl.BlockSpec((tm, cols), lambda i: (i, 0))
    blk3 = lambda n: pl.BlockSpec((n, tm, fb), lambda i: (0, i, 0))
    in_specs = [tok(d), _mod_spec(mod, layer, n_seq_tiles), _layer_spec(g_ffn, layer), ANY, ANY, ANY]
    out_shape = [
        jax.ShapeDtypeStruct((t_all, d), BF16),
        jax.ShapeDtypeStruct((N_DEV, t_all, fb), BF16),
        jax.ShapeDtypeStruct((n_hid, t_all, fb), BF16),
        jax.ShapeDtypeStruct((t_all, d), BF16),
        jax.ShapeDtypeStruct((t_all, d), F32),
    ]
    out_specs = [tok(d), blk3(N_DEV), blk3(n_hid), tok(d), tok(d)]
    scratch = [pltpu.VMEM((N_DEV, d, fb), BF16), pltpu.VMEM((n_hid * fb, d), BF16), pltpu.SemaphoreType.DMA((2,))]
    return pl.pallas_call(
        body, name=f"ffn_fwd_{layer}", grid=(t_all // tm,), in_specs=in_specs, out_specs=out_specs, out_shape=out_shape,
        scratch_shapes=scratch, compiler_params=_cparams(dimension_semantics=("arbitrary",)),
    )(x, mod, g_ffn, w13, w2, dep)


def _final_loss(x, g_final, target, seq):
    t_all, d = x.shape
    tm = min(ELEMENTWISE_TILE, seq)

    def body(x_ref, g_ref, t_ref, dx_ref, loss_ref, dg_ref):
        i = pl.program_id(0)

        @pl.when(i == 0)
        def _():
            loss_ref[...] = jnp.zeros(loss_ref.shape, F32)
            dg_ref[...] = jnp.zeros(dg_ref.shape, F32)

        x_t = x_ref[...]
        g = g_ref[...]
        rstd = lax.rsqrt(jnp.mean(x_t * x_t, axis=-1, keepdims=True) + EPS)
        xn = x_t * rstd
        err = xn * g - t_ref[...]
        loss_ref[0:1, :] += _colsum(err * err) * (0.5 / d)
        dy = err * (1.0 / d)
        dg_ref[0:1, :] += _colsum(dy * xn)
        dxn = dy * g
        dx_ref[...] = rstd * (dxn - xn * jnp.mean(dxn * xn, axis=-1, keepdims=True))

        @pl.when(i == pl.num_programs(0) - 1)
        def _():
            loss_ref[...] = jnp.broadcast_to(jnp.sum(loss_ref[0:1, :], axis=1, keepdims=True), loss_ref.shape)

    tok = pl.BlockSpec((tm, d), lambda i: (i, 0))
    acc = pl.BlockSpec((8, d), lambda i: (0, 0))
    return pl.pallas_call(
        body, name="final_loss", grid=(t_all // tm,),
        in_specs=[tok, pl.BlockSpec((1, d), lambda i: (0, 0)), tok], out_specs=[tok, acc, acc],
        out_shape=[jax.ShapeDtypeStruct((t_all, d), F32), jax.ShapeDtypeStruct((8, d), F32), jax.ShapeDtypeStruct((8, d), F32)],
        compiler_params=_cparams(dimension_semantics=("arbitrary",)),
    )(x, g_final, target)


def _ffn_bwd(dxo, xmid, ab, f, mod, g_ffn, w13, w2, dep, layer, seq):
    t_all, d = xmid.shape
    tm = min(TOKEN_TILE, seq)
    n_seq_tiles = seq // tm
    fb = w13.shape[-1]
    n_hid = N_DEV // 2

    def body(dxo_ref, x_ref, ab_ref, f_ref, mod_ref, g_ref, w13_hbm, w2_hbm, dep_ref,
             dx_ref, df_ref, dab_ref, dmod_ref, dg_ref, w13_v, w2_v, sem):
        i = pl.program_id(0)
        _load_weights(i, [(w13_hbm, w13_v), (w2_hbm, w2_v)], sem)

        @pl.when(i == 0)
        def _():
            dg_ref[...] = jnp.zeros(dg_ref.shape, F32)

        @pl.when(i % n_seq_tiles == 0)
        def _():
            dmod_ref[...] = jnp.zeros(dmod_ref.shape, F32)

        scale2, gate2 = mod_ref[0, 4:5, :], mod_ref[0, 5:6, :]
        g = g_ref[...]
        x_t = x_ref[...]
        rstd = lax.rsqrt(jnp.mean(x_t * x_t, axis=-1, keepdims=True) + EPS)
        xn = x_t * rstd
        dxo_t = dxo_ref[...]
        dmod_ref[0, 2:3, :] += _colsum(dxo_t * f_ref[...].astype(F32))
        dfb = (dxo_t * gate2).astype(BF16)
        df_ref[...] = dfb
        dh = jnp.zeros((tm, d), F32)
        for k in range(n_hid):
            dhid = _dot_nt(dfb, w2_v[k * fb:(k + 1) * fb, :]).astype(BF16)
            a = ab_ref[k]
            b = ab_ref[n_hid + k]
            sg = _sigmoid(a)
            da = dhid * b * (sg * (1.0 + a * (1.0 - sg)))
            db = dhid * (a * sg)
            dab_ref[k] = da
            dab_ref[n_hid + k] = db
            dh = dh + _dot_nt(da, w13_v[k]) + _dot_nt(db, w13_v[n_hid + k])
        dx, dshift, dscale, dg = _rms_mod_bwd(dh, xn, rstd, g, scale2)
        dmod_ref[0, 0:1, :] += dshift
        dmod_ref[0, 1:2, :] += dscale
        dg_ref[0:1, :] += dg
        dx_ref[...] = dxo_t + dx

    tok = lambda cols: pl.BlockSpec((tm, cols), lambda i: (i, 0))
    blk3 = lambda n: pl.BlockSpec((n, tm, fb), lambda i: (0, i, 0))
    modspec = pl.BlockSpec((1, 8, d), lambda i: (i // n_seq_tiles, 0, 0))
    in_specs = [tok(d), tok(d), blk3(N_DEV), tok(d), _mod_spec(mod, layer, n_seq_tiles), _layer_spec(g_ffn, layer), ANY, ANY, ANY]
    out_shape = [
        jax.ShapeDtypeStruct((t_all, d), F32), jax.ShapeDtypeStruct((t_all, d), BF16),
        jax.ShapeDtypeStruct((N_DEV, t_all, fb), BF16), jax.ShapeDtypeStruct(mod.shape[1:], F32),
        jax.ShapeDtypeStruct((8, d), F32),
    ]
    out_specs = [tok(d), tok(d), blk3(N_DEV), modspec, pl.BlockSpec((8, d), lambda i: (0, 0))]
    scratch = [pltpu.VMEM((N_DEV, d, fb), BF16), pltpu.VMEM((n_hid * fb, d), BF16), pltpu.SemaphoreType.DMA((2,))]
    return pl.pallas_call(
        body, name=f"ffn_bwd_{layer}", grid=(t_all // tm,), in_specs=in_specs, out_specs=out_specs, out_shape=out_shape,
        scratch_shapes=scratch, compiler_params=_cparams(dimension_semantics=("arbitrary",)),
    )(dxo, xmid, ab, f, mod, g_ffn, w13, w2, dep)


def _mixer_bwd(dxm, x, z, ycat, mo, mod, g_mix, ln_g, ln_b, w_s, bexp, pool_w, pool_scale, conv_w,
               win, wpa, wpb, wpc, wo, dep, layer, seq):
    t_all, d = x.shape
    tm = min(TOKEN_TILE, seq)
    n_seq_tiles = seq // tm
    blk = win.shape[-1]
    n_win = len(POOL_WINDOWS)

    def tile_of(i):
        return (i // n_seq_tiles) * n_seq_tiles + (n_seq_tiles - 1 - i % n_seq_tiles)

    def halo_row_block(i):
        return jnp.maximum(tile_of(i) * (tm // POOL_HALO) - 1, 0)

    def body(dxm_ref, x_ref, z_ref, zpb_ref, zpc_ref, ycat_ref, mo_ref, mod_ref, gmix_ref, lng_ref, lnb_ref, ws_ref,
             bexp_ref, pw_ref, ps_ref, cw_ref, win_hbm, wpa_hbm, wpb_hbm, wpc_hbm, wo_hbm, dep_ref,
             dx_ref, dz_ref, dycat_ref, dmo_ref, dmod_ref, dg_ref, sm_ref, dws_ref, dssum_ref, dpw_ref,
             win_v, wpa_v, wpb_v, wpc_v, wo_v, xbext, zzext, rext, dyext, s_scr, dvn_scr, sem):
        i = pl.program_id(0)
        pairs = [(win_hbm.at[j], win_v.at[:, pl.ds(blk * j, blk)]) for j in range(N_DEV)] + [(wo_hbm, wo_v)]
        pairs += _wp_pairs(wpa_hbm, wpa_v) + _wp_pairs(wpb_hbm, wpb_v) + _wp_pairs(wpc_hbm, wpc_v)
        _load_weights(i, pairs, sem)
        tile_in_seq = n_seq_tiles - 1 - i % n_seq_tiles
        first_of_seq = tile_in_seq == 0

        @pl.when(i == 0)
        def _():
            for r in (dg_ref, sm_ref, dws_ref, dssum_ref, dpw_ref):
                r[...] = jnp.zeros(r.shape, F32)

        @pl.when(i % n_seq_tiles == 0)
        def _():
            dmod_ref[...] = jnp.zeros(dmod_ref.shape, F32)
            rext[tm:tm + POOL_HALO, :] = jnp.zeros((POOL_HALO, BR_W), F32)
            dyext[tm:tm + CONV_HALO, :] = jnp.zeros((CONV_HALO, BR_W), F32)

        shift1, scale1, gate1 = mod_ref[0, 0:1, :], mod_ref[0, 1:2, :], mod_ref[0, 2:3, :]
        dxm_t = dxm_ref[...]
        dmo = (dxm_t * gate1).astype(BF16)
        dmo_ref[...] = dmo
        dmerged = _dot_nt(dmo, wo_v[...])
        dmod_ref[0, 2:3, :] += _colsum(dxm_t * mo_ref[...].astype(F32))

        dh_parts = []

        def emit_dz(lo, hi, value):
            vb = value.astype(BF16)
            dz_ref[:, lo:hi] = vb
            dh_parts.append(_dot_nt(vb, win_v[:, lo:hi]))

        dys = []
        dmerged = dmerged.astype(BF16)
        for n in range(3):
            gt = _sigmoid(z_ref[:, 3 * d + n * d:3 * d + (n + 1) * d])
            dyn = dmerged * gt
            emit_dz(3 * d + n * d, 3 * d + (n + 1) * d, dyn * ycat_ref[:, n * d:(n + 1) * d] * (1.0 - gt))
            dycat_ref[:, n * d:(n + 1) * d] = dyn
            dys.append(dyn)

        doa = _dot_nt(dys[0], wpa_v[...])
        dob = _dot_nt(dys[1], wpb_v[...])
        doc = _dot_nt(dys[2], wpc_v[...])

        u = z_ref[:, 0:BR_W].astype(F32)
        v = z_ref[:, BR_W:2 * BR_W].astype(F32)
        gu, tu, u2 = _gelu(u)
        gv, tv, v2 = _gelu(v)
        mu = jnp.mean(gv, axis=-1, keepdims=True)
        cen = gv - mu
        rs = lax.rsqrt(jnp.mean(cen * cen, axis=-1, keepdims=True) + EPS)
        vhat = cen * rs
        lng = lng_ref[...]
        vn_b = (vhat * lng + lnb_ref[...]).astype(BF16)
        mask = _tril_mask()
        wmask = [ws_ref[hh] * mask for hh in range(HEADS)]
        s = _gmlp_s(vn_b, [w.astype(BF16) for w in wmask], bexp_ref[...], s_scr, tm)
        du = (doa * s) * _gelu_grad(u, tu, u2)
        ds = doa * gu
        ds_b = ds.astype(BF16)
        dssum = jnp.zeros((CHUNK, BR_W), F32)
        for ch in range(tm // CHUNK):
            rows = slice(ch * CHUNK, (ch + 1) * CHUNK)
            dssum = dssum + ds[rows, :]
            for hh in range(HEADS):
                cols = slice(hh * HEAD_DIM, (hh + 1) * HEAD_DIM)
                dvn_scr[rows, cols] = _dot_tn(wmask[hh].astype(BF16), ds_b[rows, cols])
                dws_ref[hh] += _dot_nt(ds_b[rows, cols], vn_b[rows, cols]) * mask
        dssum_ref[...] += dssum
        dvn = dvn_scr[...]
        sm_ref[0:1, :] += _colsum(dvn * vhat)
        sm_ref[1:2, :] += _colsum(dvn)
        dvhat = dvn * lng
        dgv = rs * (dvhat - jnp.mean(dvhat, axis=-1, keepdims=True) - vhat * jnp.mean(dvhat * vhat, axis=-1, keepdims=True))
        dv = dgv * _gelu_grad(v, tv, v2)
        emit_dz(0, 2 * BR_W, jnp.concatenate([du, dv], axis=1))

        xb = z_ref[:, 2 * BR_W:3 * BR_W].astype(F32)
        xbext[0:POOL_HALO, :] = jnp.where(first_of_seq, 0.0, zpb_ref[...].astype(F32))
        xbext[POOL_HALO:POOL_HALO + tm, :] = xb
        pos = tile_in_seq * tm + lax.broadcasted_iota(jnp.int32, (tm, 1), 0)
        ps = _pool_p(xb, xbext, pos, tm)
        scale_b = ps_ref[...]
        dq = dob * scale_b
        qs, dps = [], []
        for gi, win_len in enumerate(POOL_WINDOWS):
            cols = slice(gi * HEAD_DIM, (gi + 1) * HEAD_DIM)
            pw_b = pw_ref[gi].astype(BF16)
            p_b = ps[gi].astype(BF16)
            dq_b = dq[:, cols].astype(BF16)
            qs.append(_dot(p_b, pw_b))
            dpw_ref[gi] += _dot_tn(p_b, dq_b)
            dp = _dot_nt(dq_b, pw_b)
            dps.append(dp)
            rext[0:tm, cols] = dp * _inv_count(pos, win_len)
        sm_ref[2:3, :] += _colsum(dob * jnp.concatenate(qs, axis=1))
        dxbs = [acc - dp for acc, dp in zip(_window_sums(rext, tm, False), dps)]
        emit_dz(2 * BR_W, 3 * BR_W, jnp.concatenate(dxbs, axis=1))
        rext[tm:tm + POOL_HALO, :] = rext[0:POOL_HALO, :]

        bg = z_ref[:, 3 * BR_W:4 * BR_W].astype(F32)
        cg = z_ref[:, 4 * BR_W:5 * BR_W].astype(F32)
        hc = z_ref[:, 5 * BR_W:6 * BR_W].astype(F32)
        zz = cg * hc
        zprev = zpc_ref[POOL_HALO - CONV_HALO:POOL_HALO, :].astype(F32)
        zzext[0:CONV_HALO, :] = jnp.where(first_of_seq, 0.0, zprev[:, 0:BR_W] * zprev[:, BR_W:2 * BR_W])
        zzext[CONV_HALO:CONV_HALO + tm, :] = zz
        zm2 = zzext[pl.ds(CONV_HALO - 2, tm), :]
        zm1 = zzext[pl.ds(CONV_HALO - 1, tm), :]
        w0, w1, w2c = cw_ref[0:1, :], cw_ref[1:2, :], cw_ref[2:3, :]
        yconv = w0 * zm2 + w1 * zm1 + w2c * zz
        dyc = doc * bg
        sm_ref[3:4, :] += _colsum(dyc * zm2)
        sm_ref[4:5, :] += _colsum(dyc * zm1)
        sm_ref[5:6, :] += _colsum(dyc * zz)
        dyext[0:tm, :] = dyc
        dzz = w2c * dyc + w1 * dyext[pl.ds(1, tm), :] + w0 * dyext[pl.ds(2, tm), :]
        dyext[tm:tm + CONV_HALO, :] = dyext[0:CONV_HALO, :]
        emit_dz(3 * BR_W, 6 * BR_W, jnp.concatenate([doc * yconv, dzz * hc, dzz * cg], axis=1))

        dh = dh_parts[0]
        for part in dh_parts[1:]:
            dh = dh + part
        x_t = x_ref[...]
        rstd = lax.rsqrt(jnp.mean(x_t * x_t, axis=-1, keepdims=True) + EPS)
        dx, dshift, dscale, dg = _rms_mod_bwd(dh, x_t * rstd, rstd, gmix_ref[...], scale1)
        dmod_ref[0, 0:1, :] += dshift
        dmod_ref[0, 1:2, :] += dscale
        dg_ref[0:1, :] += dg
        dx_ref[...] = dxm_ref[...] + dx

    tok = lambda cols: pl.BlockSpec((tm, cols), lambda i: (tile_of(i), 0))
    modspec = pl.BlockSpec((1, 8, d), lambda i: (i // n_seq_tiles, 0, 0))
    in_specs = [
        tok(d), tok(d), tok(N_DEV * blk),
        pl.BlockSpec((POOL_HALO, BR_W), lambda i: (halo_row_block(i), 2)),
        pl.BlockSpec((POOL_HALO, 2 * BR_W), lambda i: (halo_row_block(i), 2)),
        tok(3 * d), tok(d), _mod_spec(mod, layer, n_seq_tiles),
        *[_layer_spec(a, layer) for a in (g_mix, ln_g, ln_b, w_s, bexp, pool_w, pool_scale, conv_w)],
        ANY, ANY, ANY, ANY, ANY, ANY,
    ]
    acc = lambda shape: pl.BlockSpec(shape, lambda i: (0,) * len(shape))
    out_shape = [
        jax.ShapeDtypeStruct((t_all, d), F32), jax.ShapeDtypeStruct((t_all, N_DEV * blk), BF16),
        jax.ShapeDtypeStruct((t_all, 3 * d), BF16), jax.ShapeDtypeStruct((t_all, d), BF16),
        jax.ShapeDtypeStruct(mod.shape[1:], F32), jax.ShapeDtypeStruct((8, d), F32), jax.ShapeDtypeStruct((8, BR_W), F32),
        jax.ShapeDtypeStruct((HEADS, CHUNK, CHUNK), F32), jax.ShapeDtypeStruct((CHUNK, BR_W), F32),
        jax.ShapeDtypeStruct((n_win, HEAD_DIM, HEAD_DIM), F32),
    ]
    out_specs = [tok(d), tok(N_DEV * blk), tok(3 * d), tok(d), modspec, acc((8, d)), acc((8, BR_W)),
                 acc((HEADS, CHUNK, CHUNK)), acc((CHUNK, BR_W)), acc((n_win, HEAD_DIM, HEAD_DIM))]
    scratch = [
        pltpu.VMEM((d, N_DEV * blk), BF16), pltpu.VMEM((BR_W, d), BF16), pltpu.VMEM((BR_W, d), BF16),
        pltpu.VMEM((BR_W, d), BF16), pltpu.VMEM((d, d), BF16),
        pltpu.VMEM((tm + POOL_HALO, BR_W), F32), pltpu.VMEM((tm + CONV_HALO, BR_W), F32),
        pltpu.VMEM((tm + POOL_HALO, BR_W), F32), pltpu.VMEM((tm + CONV_HALO, BR_W), F32),
        pltpu.VMEM((tm, BR_W), F32), pltpu.VMEM((tm, BR_W), F32),
        pltpu.SemaphoreType.DMA((1 + 4 * N_DEV,)),
    ]
    return pl.pallas_call(
        body, name=f"mixer_bwd_{layer}", grid=(t_all // tm,), in_specs=in_specs, out_specs=out_specs, out_shape=out_shape,
        scratch_shapes=scratch, compiler_params=_cparams(dimension_semantics=("arbitrary",)),
    )(dxm, x, z, z, z, ycat, mo, mod, g_mix, ln_g, ln_b, w_s, bexp, pool_w, pool_scale, conv_w, win, wpa, wpb, wpc, wo, dep)


def _wgrad(a, b, a_spec, b_spec, out_struct, out_spec, grid_kn, tk, tn, split, name):
    t_all = a.shape[-2]
    tt = min(WGRAD_TOKENS, t_all)
    n_t = t_all // tt

    def body(a_ref, b_ref, o_ref, acc):
        t = pl.program_id(2)

        @pl.when(t == 0)
        def _():
            acc[...] = jnp.zeros(acc.shape, F32)

        acc[...] += _dot_tn(a_ref[...], b_ref[...])

        @pl.when(t == n_t - 1)
        def _():
            if split:
                for j in range(split):
                    w = tn // split
                    o_ref[j] = acc[:, j * w:(j + 1) * w].astype(o_ref.dtype)
            else:
                o_ref[...] = acc[...].astype(o_ref.dtype)

    return pl.pallas_call(
        body, name=name, grid=(*grid_kn, n_t), in_specs=[a_spec(tt), b_spec(tt)], out_specs=out_spec, out_shape=out_struct,
        scratch_shapes=[pltpu.VMEM((tk, tn), F32)],
        compiler_params=_cparams(dimension_semantics=("arbitrary", "arbitrary", "arbitrary")),
    )(a, b)


def _mixer_wgrads(h, dz, mrg, dmo, ocat, dycat, layer):
    d = h.shape[1]
    blk = dz.shape[1] // N_DEV
    g_win = _wgrad(
        h, dz, lambda tt: pl.BlockSpec((tt, d), lambda k, n, t: (t, 0)), lambda tt: pl.BlockSpec((tt, blk), lambda k, n, t: (t, n)),
        jax.ShapeDtypeStruct((N_DEV, d, blk), BF16), pl.BlockSpec((None, d, blk), lambda k, n, t: (n, 0, 0)),
        (1, N_DEV), d, blk, 0, f"wgrad_in_{layer}")
    g_wo = _wgrad(
        mrg, dmo, lambda tt: pl.BlockSpec((tt, d), lambda k, n, t: (t, 0)), lambda tt: pl.BlockSpec((tt, d), lambda k, n, t: (t, 0)),
        jax.ShapeDtypeStruct((d, d), BF16), pl.BlockSpec((d, d), lambda k, n, t: (0, 0)), (1, 1), d, d, 0, f"wgrad_o_{layer}")
    g_wp = []
    for n, nm in enumerate("abc"):
        g_wp.append(_wgrad(
            ocat, dycat, lambda tt, n=n: pl.BlockSpec((tt, BR_W), lambda k, nn, t: (t, n)),
            lambda tt, n=n: pl.BlockSpec((tt, d), lambda k, nn, t: (t, n)),
            jax.ShapeDtypeStruct((N_DEV, BR_W, d // N_DEV), BF16),
            pl.BlockSpec((N_DEV, BR_W, d // N_DEV), lambda k, nn, t: (0, 0, 0)), (1, 1), BR_W, d, N_DEV, f"wgrad_p{nm}_{layer}"))
    return [g_win, g_wo.reshape(N_DEV, d // N_DEV, d), *g_wp]


def _ffn_wgrads(h2, dab, hid, df, layer):
    d = h2.shape[1]
    fb = dab.shape[-1]
    n_hid = N_DEV // 2
    g_w13 = _wgrad(
        dab, h2, lambda tt: pl.BlockSpec((None, tt, fb), lambda k, n, t: (k, t, 0)),
        lambda tt: pl.BlockSpec((tt, d), lambda k, n, t: (t, 0)),
        jax.ShapeDtypeStruct((N_DEV, fb, d), BF16), pl.BlockSpec((None, fb, d), lambda k, n, t: (k, 0, 0)),
        (N_DEV, 1), fb, d, 0, f"wgrad_13_{layer}")
    g_w2 = _wgrad(
        hid, df, lambda tt: pl.BlockSpec((None, tt, fb), lambda k, n, t: (k, t, 0)),
        lambda tt: pl.BlockSpec((tt, d), lambda k, n, t: (t, 0)),
        jax.ShapeDtypeStruct((n_hid * fb, d), BF16), pl.BlockSpec((fb, d), lambda k, n, t: (k, 0)),
        (n_hid, 1), fb, d, 0, f"wgrad_2_{layer}")
    return [g_w13, g_w2.reshape(N_DEV, fb // 2, d)]


def _remote(src, dst, send_sem, recv_sem, dev):
    return pltpu.make_async_remote_copy(src_ref=src, dst_ref=dst, send_sem=send_sem, recv_sem=recv_sem, device_id=dev,
                                        device_id_type=MESH_ID)


HBM_SPEC = pl.BlockSpec(memory_space=pltpu.HBM)
SEM_SPEC = pl.BlockSpec(memory_space=pltpu.SEMAPHORE)
DATAFLOW = pltpu.SideEffectType.DATAFLOW_SIDE_EFFECTING
GATHER, SCATTER = "gather", "scatter"


def _xfer_sem(a, k):
    return a * (N_DEV - 1) + k - 1


def _gather_zones(weights, layer):
    me = _my_index()
    return [lax.dynamic_update_index_in_dim(lax.empty((N_DEV, *w.shape[1:]), BF16), w[layer].astype(BF16)[None], me, 0)
            for w in weights]


def _scatter_zones(partials):
    return [lax.empty(p.shape, p.dtype) for p in partials]


def _xfer_src(kind, src, land, a, me, idx):
    return land[a].at[me] if kind == GATHER else src[a].at[idx]


def _xfer_start(srcs, lands, kind, name, after):
    n, n_src = len(lands), len(srcs)

    def body(*refs):
        src, land = refs[:n_src], refs[n_src:n_src + n]
        send_sems, recv_sems = refs[n_src + n + 1], refs[n_src + n + 2]
        token = refs[-1]
        me = _my_index()
        for k in range(1, N_DEV):
            dev, idx = _peer(k)
            for a in range(n):
                q = _xfer_sem(a, k)
                _remote(_xfer_src(kind, src, land, a, me, idx), land[a].at[me], send_sems.at[q], recv_sems.at[q], dev).start()
        token[...] = jnp.zeros(token.shape, token.dtype)

    both = [*srcs, *lands]
    sems = pltpu.SemaphoreType.DMA((n * (N_DEV - 1),))
    out_shape = (sems, sems, *[pltpu.HBM(a.shape, a.dtype) for a in both], jax.ShapeDtypeStruct((8, 128), F32))
    outs = pl.pallas_call(
        body, name=name, in_specs=[HBM_SPEC] * len(both) + [ANY],
        out_specs=(SEM_SPEC, SEM_SPEC, *[HBM_SPEC] * len(both), VMEM_SPEC),
        out_shape=out_shape, input_output_aliases={i: 2 + i for i in range(len(both))},
        compiler_params=pltpu.CompilerParams(has_side_effects=DATAFLOW),
    )(*[pltpu.with_memory_space_constraint(a, pltpu.HBM) for a in both], after)
    return outs[0], outs[1], list(outs[2:2 + n_src]), list(outs[2 + n_src:2 + n_src + n]), outs[-1]


def _gather_two_level(zones, name, after):
    n = len(zones)

    def body(*refs):
        out = refs[n + 1:2 * n + 1]
        send_sems, recv_sems = refs[2 * n + 1], refs[2 * n + 2]
        x, y, c = lax.axis_index("x"), lax.axis_index("y"), lax.axis_index("c")
        index = lambda px, py, pc: 4 * px + 2 * py + pc
        me, sibling = (x, y, c), (x, y, 1 - c)
        chips = [(1 - x, y), (x, 1 - y), (1 - x, 1 - y)]

        def copy(a, k, block, to):
            rows = out[a].at[index(*block)]
            return _remote(rows, rows, send_sems.at[a * (N_DEV - 1) + k], recv_sems.at[a * (N_DEV - 1) + k], to)

        first = [copy(a, 0, me, sibling) for a in range(n)]
        first += [copy(a, 1 + j, me, (*chip, c)) for j, chip in enumerate(chips) for a in range(n)]
        for cp in first:
            cp.start()
        passed = []
        for j, chip in enumerate(chips):
            for a in range(n):
                copy(a, 1 + j, (*chip, c), me).wait_recv()
                passed.append(copy(a, 4 + j, (*chip, c), sibling))
                passed[-1].start()
        for a in range(n):
            copy(a, 0, sibling, me).wait_recv()
            for j, chip in enumerate(chips):
                copy(a, 4 + j, (*chip, 1 - c), me).wait_recv()
        for cp in first + passed:
            cp.wait_send()

    sems = pltpu.SemaphoreType.DMA((n * (N_DEV - 1),))
    return pl.pallas_call(
        body, name=name, in_specs=[ANY] * (n + 1), out_specs=[ANY] * n, out_shape=[jax.ShapeDtypeStruct(z.shape, z.dtype) for z in zones],
        input_output_aliases={i: i for i in range(n)}, scratch_shapes=[sems, sems],
        compiler_params=pltpu.CompilerParams(has_side_effects=True),
    )(*zones, after)


def _xfer_wait(send_sems, recv_sems, srcs, lands, rows, after, kind, name):
    n, n_src = len(lands), len(srcs)

    def body(*refs):
        src, land = refs[:n_src], refs[n_src:n_src + n]
        send_sems, recv_sems = refs[n_src + n], refs[n_src + n + 1]
        me = _my_index()
        for k in range(1, N_DEV):
            dev, idx = _peer(k)
            for a in range(n):
                q = _xfer_sem(rows[a], k)
                cp = _remote(_xfer_src(kind, src, land, a, me, idx), land[a].at[idx], send_sems.at[q], recv_sems.at[q], dev)
                cp.wait_send()
                cp.wait_recv()

    both = [*srcs, *lands]
    outs = pl.pallas_call(
        body, name=name, in_specs=[HBM_SPEC] * len(both) + [SEM_SPEC, SEM_SPEC, ANY], out_specs=[HBM_SPEC] * len(both),
        out_shape=[pltpu.HBM(a.shape, a.dtype) for a in both], input_output_aliases={i: i for i in range(len(both))},
        compiler_params=pltpu.CompilerParams(has_side_effects=DATAFLOW),
    )(*both, send_sems, recv_sems, after)
    return list(outs[:n_src]), list(outs[n_src:])


def _pre(c_pad, conv_pad, w_mod, b_mod_mine):
    n_layers, d, blk = w_mod.shape

    def body(c_ref, conv_ref, wmod_ref, bmod_ref, cact_ref, mod_ref, convall_ref, cact_mine, msh, send_sems, recv_sems):
        me = _my_index()
        c = c_ref[...]
        cact_mine[...] = c * _sigmoid(c)
        cact_ref[me] = cact_mine[...]
        convall_ref[me] = conv_ref[...]
        sends = []
        for k in range(1, N_DEV):
            dev, _ = _peer(k)
            sends.append(_remote(cact_mine, cact_ref.at[me], send_sems.at[0, k - 1], recv_sems.at[0, k - 1], dev))
            sends.append(_remote(conv_ref, convall_ref.at[me], send_sems.at[1, k - 1], recv_sems.at[1, k - 1], dev))
        for cp in sends:
            cp.start()
        for k in range(1, N_DEV):
            dev, idx = _peer(k)
            _remote(cact_mine, cact_ref.at[idx], send_sems.at[0, k - 1], recv_sems.at[0, k - 1], dev).wait_recv()
            _remote(conv_ref, convall_ref.at[idx], send_sems.at[1, k - 1], recv_sems.at[1, k - 1], dev).wait_recv()
        for cp in sends:
            cp.wait_send()
        cact_b = cact_ref[...].reshape(N_DEV * 8, d).astype(BF16)
        for l in range(n_layers):
            m = _dot(cact_b, wmod_ref[l].astype(BF16)) + bmod_ref[l]
            msh[l] = m.reshape(N_DEV, 8, blk)
        mod_ref[me] = msh[:, me]
        sends = []
        for k in range(1, N_DEV):
            dev, idx = _peer(k)
            sends.append(_remote(msh.at[:, idx], mod_ref.at[me], send_sems.at[2, k - 1], recv_sems.at[2, k - 1], dev))
        for cp in sends:
            cp.start()
        for k in range(1, N_DEV):
            dev, idx = _peer(k)
            _remote(msh.at[:, idx], mod_ref.at[idx], send_sems.at[2, k - 1], recv_sems.at[2, k - 1], dev).wait_recv()
        for cp in sends:
            cp.wait_send()

    out_shape = [jax.ShapeDtypeStruct((N_DEV, 8, d), F32), jax.ShapeDtypeStruct((N_DEV, n_layers, 8, blk), F32),
                 jax.ShapeDtypeStruct((N_DEV, *conv_pad.shape), F32)]
    return pl.pallas_call(
        body, name="pre", in_specs=[VMEM_SPEC] * 4, out_specs=[VMEM_SPEC] * 3, out_shape=out_shape,
        scratch_shapes=[pltpu.VMEM((8, d), F32), pltpu.VMEM((n_layers, N_DEV, 8, blk), F32),
                        pltpu.SemaphoreType.DMA((3, N_DEV - 1)), pltpu.SemaphoreType.DMA((3, N_DEV - 1))],
        compiler_params=_cparams(has_side_effects=True),
    )(c_pad, conv_pad, w_mod, b_mod_mine)


def _small_reduce(zone_buf, own_buf, zone_d, own_d, cact_all):
    _, seg, _ = zone_buf.shape
    _, n_layers, _, blk = zone_d.shape
    d = cact_all.shape[-1]

    def body(zb_ref, ob_ref, zd_ref, od_ref, cact_ref, red_ref, gw_ref, dcols):
        me = _my_index()
        total = jnp.where(me == 0, ob_ref[0], zb_ref[0])
        for p in range(1, N_DEV):
            total = total + jnp.where(me == p, ob_ref[p], zb_ref[p])
        red_ref[...] = total
        for p in range(N_DEV):
            dcols[p] = jnp.where(me == p, od_ref[p], zd_ref[p])
        cact_b = cact_ref[...].reshape(N_DEV * 8, d).astype(BF16)
        for l in range(n_layers):
            gw_ref[l] = _dot_tn(cact_b, dcols[:, l].reshape(N_DEV * 8, blk).astype(BF16))

    out_shape = [jax.ShapeDtypeStruct((seg, 128), F32), jax.ShapeDtypeStruct((n_layers, d, blk), F32)]
    return pl.pallas_call(
        body, name="small_reduce", in_specs=[VMEM_SPEC] * 5, out_specs=[VMEM_SPEC] * 2, out_shape=out_shape,
        scratch_shapes=[pltpu.VMEM(zone_d.shape, F32)], compiler_params=_cparams(),
    )(zone_buf, own_buf, zone_d, own_d, cact_all)


def _adamw_math(w, g, m, v):
    m = ADAM_B1 * m + (1.0 - ADAM_B1) * g
    v = ADAM_B2 * v + (1.0 - ADAM_B2) * (g * g)
    m_hat = m / (1.0 - ADAM_B1 ** ADAM_STEP)
    v_hat = v / (1.0 - ADAM_B2 ** ADAM_STEP)
    delta = -ADAM_LR * (m_hat / (jnp.sqrt(v_hat) + ADAM_EPS) + ADAM_WD * w)
    return delta, m, v


def _adamw(parts, w, m, v, name, first=0, earlier=None, own=None):
    n_layers = len(parts)
    n_parts, rows, cols = parts[0].shape
    tr = rows
    while tr * cols * 4 > (1 << 20) and tr % 32 == 0:
        tr //= 2
    n_r = rows // tr
    n_earlier = 0 if earlier is None else 4
    n_own = 0 if own is None else n_layers
    n_in = n_layers + n_own + 3

    def body(me_ref, *refs):
        p_refs, o_refs = refs[:n_layers], refs[n_layers:n_layers + n_own]
        w_ref, m_ref, v_ref = refs[n_layers + n_own:n_in]
        g_out, d_out, m_out, v_out = refs[n_in + n_earlier:]
        layer = pl.program_id(0)

        def part(q, p):
            value = p_refs[q][p]
            if n_own:
                value = jnp.where(me_ref[0] == p, o_refs[q][...], value)
            return value.astype(F32)

        for q in range(n_layers):

            @pl.when(layer == q)
            def _(q=q):
                g = part(q, 0)
                for p in range(1, n_parts):
                    g = g + part(q, p)
                delta, m_new, v_new = _adamw_math(w_ref[...], g, m_ref[...], v_ref[...])
                g_out[...] = g
                d_out[...] = delta
                m_out[...] = m_new
                v_out[...] = v_new

    def moving(q, l, r):
        return jnp.where(l == q, r, jnp.where(l < q, 0, n_r - 1))

    parts_spec = lambda q: pl.BlockSpec((n_parts, tr, cols), lambda l, r, me: (0, moving(q, l, r), 0))
    own_spec = lambda q: pl.BlockSpec((None, tr, cols), lambda l, r, me: (me[0], moving(q, l, r), 0))
    spec = pl.BlockSpec((None, tr, cols), lambda l, r, me: (first + l, r, 0))
    out = jax.ShapeDtypeStruct(w.shape, F32)
    grid_spec = pltpu.PrefetchScalarGridSpec(
        num_scalar_prefetch=1, grid=(n_layers, n_r),
        in_specs=[parts_spec(q) for q in range(n_layers)] + [own_spec(q) for q in range(n_own)] + [spec, spec, spec]
        + [ANY] * n_earlier,
        out_specs=[spec] * 4)
    return pl.pallas_call(
        body, name=name, grid_spec=grid_spec, out_shape=[out] * 4,
        input_output_aliases={1 + n_in + i: i for i in range(n_earlier)},
        compiler_params=_cparams(dimension_semantics=("arbitrary", "arbitrary")),
    )(_my_index().astype(jnp.int32).reshape(1), *parts, *(own or ()), w, m, v, *(earlier or ()))


def _adamw_small(gs, ws, ms, vs):
    n = len(ws)

    def body(*refs):
        g, w, m, v = (refs[i * n:(i + 1) * n] for i in range(4))
        d_out, m_out, v_out = (refs[i * n:(i + 1) * n] for i in range(4, 7))
        for i in range(n):
            delta, m_new, v_new = _adamw_math(w[i][...], g[i][...], m[i][...], v[i][...])
            d_out[i][...] = delta
            m_out[i][...] = m_new
            v_out[i][...] = v_new

    out = [jax.ShapeDtypeStruct(w.shape, F32) for w in ws]
    res = pl.pallas_call(body, name="adamw_small", in_specs=[VMEM_SPEC] * (4 * n), out_specs=[VMEM_SPEC] * (3 * n),
                         out_shape=out * 3, compiler_params=_cparams())(*gs, *ws, *ms, *vs)
    return res[:n], res[n:2 * n], res[2 * n:]


def _pack(arrays, rows_multiple):
    flat = jnp.concatenate([a.reshape(-1) for a in arrays])
    per = 128 * rows_multiple
    total = -(-flat.shape[0] // per) * per
    return jnp.pad(flat, (0, total - flat.shape[0])).reshape(total // 128, 128)


def _unpack(buf, like):
    flat = buf.reshape(-1)
    out, off = [], 0
    for a in like:
        out.append(flat[off:off + a.size].reshape(a.shape))
        off += a.size
    return out


def kernel(x, c, w_mod, b_mod, g_mix, w_in, gm_ln_g, gm_ln_b, gm_w_s, gm_b_s, w_pa, pool_w, pool_scale, w_pb, conv_w, w_pc, w_o, g_ffn, w_13, w_2, g_final, loss_target, m_w_mod, m_b_mod, m_g_mix, m_w_in, m_gm_ln_g, m_gm_ln_b, m_gm_w_s, m_gm_b_s, m_w_pa, m_pool_w, m_pool_scale, m_w_pb, m_conv_w, m_w_pc, m_w_o, m_g_ffn, m_w_13, m_w_2, m_g_final, v_w_mod, v_b_mod, v_g_mix, v_w_in, v_gm_ln_g, v_gm_ln_b, v_gm_w_s, v_gm_b_s, v_w_pa, v_pool_w, v_pool_scale, v_w_pb, v_conv_w, v_w_pc, v_w_o, v_g_ffn, v_w_13, v_w_2, v_g_final):
    nb, seq, d = x.shape
    n_layers = w_in.shape[0]
    t_all = nb * seq
    blk = w_in.shape[-1]
    me = _my_index()
    conv_shard = conv_w.shape[-1]

    c_pad = jnp.pad(c, ((0, 8 - nb), (0, 0)))
    conv_pad = jnp.pad(conv_w.reshape(n_layers * 3, conv_shard), ((0, 16 - n_layers * 3), (0, 128 - conv_shard)))
    b_mod_mine = lax.dynamic_slice_in_dim(b_mod, me * blk, blk, axis=1).reshape(n_layers, 1, blk)
    cact_all, mod_blocks, conv_all = _pre(c_pad, conv_pad, w_mod, b_mod_mine)
    mod = jnp.transpose(mod_blocks, (1, 2, 0, 3)).reshape(n_layers, 8, N_MOD, d)[:, :nb]
    mod = jnp.pad(mod, ((0, 0), (0, 0), (0, 8 - N_MOD), (0, 0)))
    conv_full = jnp.transpose(conv_all[:, :n_layers * 3, :conv_shard].reshape(N_DEV, n_layers, 3, conv_shard), (1, 2, 0, 3))
    conv_full = jnp.pad(conv_full.reshape(n_layers, 3, N_DEV * conv_shard), ((0, 0), (0, 5), (0, 0)))
    bexp = jnp.repeat(jnp.transpose(gm_b_s, (0, 2, 1)), HEAD_DIM, axis=2)

    mixer_w, ffn_w = [w_in, w_o, w_pa, w_pb, w_pc], [w_13, w_2]
    n_mix = len(mixer_w)

    def send_weights(l, ws, name, after):
        return _xfer_start([], _gather_zones(ws, l), GATHER, name, after)

    def send_grads(partials, name, after):
        return _xfer_start(partials, _scatter_zones(partials), SCATTER, name, after)

    def arrived(flight, after, kind, name):
        send_sems, recv_sems, srcs, zones, _ = flight
        return _xfer_wait(send_sems, recv_sems, srcs, zones, list(range(len(zones))), after, kind, name)

    rows = lambda a: a.reshape(n_layers, 1, -1)
    mix_small = (rows(g_mix), rows(gm_ln_g), rows(gm_ln_b), gm_w_s, bexp, pool_w, rows(pool_scale), conv_full)
    g_ffn_rows = rows(g_ffn)
    xs = x.reshape(t_all, d)
    saved, weights = [], []
    after = mod
    for l in range(n_layers):
        if l == 0:
            win_g, wo_g, wpa_g, wpb_g, wpc_g = _gather_two_level(_gather_zones(mixer_w, 0), "gather_mixer_0", cact_all)
        else:
            win_g, wo_g, wpa_g, wpb_g, wpc_g = arrived(flight, after, GATHER, f"gather_wait_mixer_{l}")[1]
        flight = send_weights(l, ffn_w, f"gather_start_ffn_{l}", win_g)
        wo_g = wo_g.reshape(d, d)
        h, z, ycat, ocat, mrg, mo, xmid = _mixer_fwd(xs, mod, *mix_small, win_g, wpa_g, wpb_g, wpc_g, wo_g, flight[4], l, seq)
        w13_g, w2_g = arrived(flight, xmid, GATHER, f"gather_wait_ffn_{l}")[1]
        dep = w13_g
        if l + 1 < n_layers:
            flight = send_weights(l + 1, mixer_w, f"gather_start_mixer_{l + 1}", w13_g)
            dep = flight[4]
        w2_g = w2_g.reshape(N_DEV * w_2.shape[1], d)
        h2, ab, hid, f, xo = _ffn_fwd(xmid, mod, g_ffn_rows, w13_g, w2_g, dep, l, seq)
        saved.append((xs, h, z, ycat, ocat, mrg, mo, xmid, h2, ab, hid, f))
        weights.append((win_g, wpa_g, wpb_g, wpc_g, wo_g, w13_g, w2_g))
        xs = after = xo

    dx, loss_blk, dgf_blk = _final_loss(xs, g_final.reshape(1, d), loss_target.reshape(t_all, d), seq)
    loss = lax.psum(loss_blk[0, 0], ("x", "y", "c"))

    ffn_flight = [None] * n_layers
    mix_flight = [None] * n_layers
    small_grads = [None] * n_layers
    dmods = [None] * n_layers
    dep = dx
    for l in reversed(range(n_layers)):
        x_in, h, z, ycat, ocat, mrg, mo, xmid, h2, ab, hid, f = saved[l]
        win_g, wpa_g, wpb_g, wpc_g, wo_g, w13_g, w2_g = weights[l]
        dxm, df, dab, dmod2, dg_ffn = _ffn_bwd(dx, xmid, ab, f, mod, g_ffn_rows, w13_g, w2_g, dep, l, seq)
        ffn_flight[l] = send_grads(_ffn_wgrads(h2, dab, hid, df, l), f"grads_start_ffn_{l}", dxm)
        dx, dz, dycat, dmo, dmod1, dg_mix, sm, dws, dssum, dpw = _mixer_bwd(
            dxm, x_in, z, ycat, mo, mod, *mix_small, win_g, wpa_g, wpb_g, wpc_g, wo_g, ffn_flight[l][4], l, seq)
        if l > 0:
            mix_flight[l] = send_grads(_mixer_wgrads(h, dz, mrg, dmo, ocat, dycat, l), f"grads_start_mixer_{l}", dx)
            dep = mix_flight[l][4]
        dmod = jnp.concatenate([dmod1[:, 0:3], dmod2[:, 0:3]], axis=1).reshape(nb, N_MOD * d)
        dmods[l] = dmod
        db_s = jnp.transpose(jnp.sum(dssum.reshape(CHUNK, HEADS, HEAD_DIM), axis=2))
        small_grads[l] = [jnp.sum(dmod, axis=0), dg_mix[0], sm[0], sm[1], dws, db_s, dpw, sm[2], sm[3:6], dg_ffn[0]]
    grad_x = dx.reshape(nb, seq, d)

    names = ["b_mod", "g_mix", "ln_g", "ln_b", "w_s", "b_s", "pool_w", "pool_scale", "conv_w", "g_ffn"]
    per_name = [jnp.stack([small_grads[l][n] for l in range(n_layers)]) for n in range(len(names))] + [dgf_blk[0]]
    buf = _pack(per_name, 8 * N_DEV)
    dmod_all = jnp.pad(jnp.stack(dmods), ((0, 0), (0, 8 - nb), (0, 0)))
    dmod_blocks = jnp.transpose(dmod_all.reshape(n_layers, 8, N_DEV, blk), (2, 0, 1, 3))
    seg = buf.shape[0] // N_DEV
    small_flight = send_grads([buf.reshape(N_DEV, seg, 128), dmod_blocks], "small_start", buf)
    mixer_partials = _mixer_wgrads(h, dz, mrg, dmo, ocat, dycat, 0)
    (own_buf, own_d), (zone_buf, zone_d) = arrived(small_flight, mixer_partials[0], SCATTER, "small_wait")
    red_seg, grad_w_mod = _small_reduce(zone_buf, own_buf, zone_d, own_d, cact_all)
    red_zone = lax.dynamic_update_index_in_dim(lax.empty((N_DEV, seg, 128), F32), red_seg[None], me, 0)
    red_flight = _xfer_start([], [red_zone], GATHER, "small_gather_start", red_seg)

    mix_flight[0] = send_grads(mixer_partials, "grads_start_mixer_0", red_flight[4])
    results = {}
    results["w_mod"] = _adamw([grad_w_mod[l][None] for l in range(n_layers)], w_mod, m_w_mod, v_w_mod, "adamw_w_mod")
    layers = list(range(n_layers))
    done = (results["w_mod"][1][0, 0, 0] + results["w_mod"][2][0, 0, 0]).reshape(1)
    ffn_recv = [arrived(ffn_flight[l], done, SCATTER, f"grads_wait_ffn_{l}") for l in reversed(layers)][::-1]
    mix_recv = [None] + [arrived(mix_flight[l], done, SCATTER, f"grads_wait_mixer_{l}") for l in reversed(layers[1:])][::-1]
    swap = lambda a: jnp.swapaxes(a, 1, 2)
    got = lambda recv, ls, a: dict(parts=[recv[l][1][a] for l in ls], own=[recv[l][0][a] for l in ls])
    results["w_13"] = [swap(r) for r in _adamw(w=swap(w_13), m=swap(m_w_13), v=swap(v_w_13), name="adamw_w_13", **got(ffn_recv, layers, 0))]
    results["w_2"] = _adamw(w=w_2, m=m_w_2, v=v_w_2, name="adamw_w_2", **got(ffn_recv, layers, 1))
    mix_m = [m_w_in, m_w_o, m_w_pa, m_w_pb, m_w_pc]
    mix_v = [v_w_in, v_w_o, v_w_pa, v_w_pb, v_w_pc]
    mix_names = ["w_in", "w_o", "w_pa", "w_pb", "w_pc"]
    early = [_adamw(w=mixer_w[a], m=mix_m[a], v=mix_v[a], name=f"adamw_{mix_names[a]}_later", first=1, **got(mix_recv, layers[1:], a))
             for a in range(n_mix)]

    red = arrived(red_flight, results["w_2"][1], GATHER, "small_gather_wait")[1][0].reshape(buf.shape)
    (g_b_mod, g_g_mix, g_ln_g, g_ln_b, g_w_s, g_b_s, g_pool_w, g_pool_scale, g_conv_full, g_g_ffn, g_g_final) = _unpack(red, per_name)
    g_conv = lax.dynamic_slice_in_dim(g_conv_full, me * conv_shard, conv_shard, axis=2)

    small_w =[b_mod, g_mix, gm_ln_g, gm_ln_b, gm_w_s, gm_b_s, pool_w, pool_scale, conv_w, g_ffn, g_final]
    small_m = [m_b_mod, m_g_mix, m_gm_ln_g, m_gm_ln_b, m_gm_w_s, m_gm_b_s, m_pool_w, m_pool_scale, m_conv_w, m_g_ffn, m_g_final]
    small_v = [v_b_mod, v_g_mix, v_gm_ln_g, v_gm_ln_b, v_gm_w_s, v_gm_b_s, v_pool_w, v_pool_scale, v_conv_w, v_g_ffn, v_g_final]
    small_g = [g_b_mod, g_g_mix, g_ln_g, g_ln_b, g_w_s, g_b_s, g_pool_w, g_pool_scale, g_conv, g_g_ffn, g_g_final]
    small_names = ["b_mod", "g_mix", "gm_ln_g", "gm_ln_b", "gm_w_s", "gm_b_s", "pool_w", "pool_scale", "conv_w", "g_ffn", "g_final"]
    sd, sm_new, sv_new = _adamw_small(small_g, small_w, small_m, small_v)
    for n, nm in enumerate(small_names):
        results[nm] = (small_g[n], sd[n], sm_new[n], sv_new[n])

    done = (results["w_13"][1][0, 0, 0] + sd[-1][0] + sum(e[1][1, 0, 0] for e in early)).reshape(1)
    mix_recv[0] = arrived(mix_flight[0], done, SCATTER, "grads_wait_mixer_0")
    for a in range(n_mix):
        results[mix_names[a]] = _adamw(w=mixer_w[a], m=mix_m[a], v=mix_v[a], name=f"adamw_{mix_names[a]}_first", earlier=early[a],
                                       **got(mix_recv, [0], a))

    order = ["w_mod", "b_mod", "g_mix", "w_in", "gm_ln_g", "gm_ln_b", "gm_w_s", "gm_b_s", "w_pa", "pool_w", "pool_scale",
             "w_pb", "conv_w", "w_pc", "w_o", "g_ffn", "w_13", "w_2", "g_final"]
    return (loss, grad_x, *[results[nm][0] for nm in order], *[results[nm][1] for nm in order],
            *[results[nm][2] for nm in order], *[results[nm][3] for nm in order])
```

```python
import functools

import jax
import jax.numpy as jnp
from jax import lax
from jax.experimental import pallas as pl
from jax.experimental.pallas import tpu as pltpu

F32 = jnp.float32
BF16 = jnp.bfloat16
MESH_ID = pl.DeviceIdType.MESH

N_DEV = 8
EPS = 1e-6
CHUNK = 128
HEADS = 4
HEAD_DIM = 128
BR_W = 512
POOL_WINDOWS = (2, 4, 8, 16)
POOL_HALO = 16
CONV_HALO = 8
N_MOD = 6
ADAM_LR = 0.001
ADAM_B1 = 0.9
ADAM_B2 = 0.999
ADAM_EPS = 1e-08
ADAM_WD = 0.01
ADAM_STEP = 10

TOKEN_TILE = 256
ELEMENTWISE_TILE = 1024
WGRAD_TOKENS = 2048
VMEM_LIMIT = 56 * 1024 * 1024
GELU_K = 0.7978845608028654
GELU_C = 0.044715

ANY = pl.BlockSpec(memory_space=pl.ANY)
VMEM_SPEC = pl.BlockSpec(memory_space=pltpu.VMEM)


def _cparams(**kw):
    return pltpu.CompilerParams(vmem_limit_bytes=VMEM_LIMIT, **kw)


def _dot(a, b):
    return jnp.dot(a, b, preferred_element_type=F32)


def _dot_nt(a, b):
    return lax.dot_general(a, b, (((1,), (1,)), ((), ())), preferred_element_type=F32)


def _dot_tn(a, b):
    return lax.dot_general(a, b, (((0,), (0,)), ((), ())), preferred_element_type=F32)


def _colsum(a):
    return jnp.sum(a, axis=0, keepdims=True)


def _sigmoid(x):
    return 0.5 * jnp.tanh(0.5 * x) + 0.5


def _gelu(x):
    x2 = x * x
    t = jnp.tanh(x * (GELU_K + (GELU_K * GELU_C) * x2))
    return (0.5 * x) * (1.0 + t), t, x2


def _gelu_grad(x, t, x2):
    one_t = 1.0 + t
    return 0.5 * one_t + (0.5 * x) * (one_t * (1.0 - t)) * (GELU_K + (3.0 * GELU_K * GELU_C) * x2)


def _tril_mask():
    r = lax.broadcasted_iota(jnp.int32, (CHUNK, CHUNK), 0)
    c = lax.broadcasted_iota(jnp.int32, (CHUNK, CHUNK), 1)
    return (r >= c).astype(F32)


def _my_index():
    return 4 * lax.axis_index("x") + 2 * lax.axis_index("y") + lax.axis_index("c")


def _peer(k):
    x, y, c = lax.axis_index("x"), lax.axis_index("y"), lax.axis_index("c")
    px = 1 - x if (k >> 2) & 1 else x
    py = 1 - y if (k >> 1) & 1 else y
    pc = 1 - c if k & 1 else c
    return (px, py, pc), 4 * px + 2 * py + pc


def _load_weights(step, pairs, sem):
    @pl.when(step == 0)
    def _():
        copies = [pltpu.make_async_copy(src, dst, sem.at[n]) for n, (src, dst) in enumerate(pairs)]
        for cp in copies:
            cp.start()
        for cp in copies:
            cp.wait()


def _wp_pairs(wp_hbm, wp_v):
    return [(wp_hbm.at[j], wp_v.at[:, pl.ds(HEAD_DIM * j, HEAD_DIM)]) for j in range(N_DEV)]


def _layer_spec(a, layer):
    return pl.BlockSpec((None, *a.shape[1:]), lambda i: (layer,) + (0,) * (a.ndim - 1))


def _mod_spec(mod, layer, n_seq_tiles):
    return pl.BlockSpec((None, 1, *mod.shape[2:]), lambda i: (layer, i // n_seq_tiles, 0, 0))


def _rms_mod(x, g, shift, scale):
    rstd = lax.rsqrt(jnp.mean(x * x, axis=-1, keepdims=True) + EPS)
    xn = x * rstd
    return xn, rstd, (xn * g) * (1.0 + scale) + shift


def _rms_mod_bwd(dh, xn, rstd, g, scale):
    dxn = dh * (1.0 + scale) * g
    dx = rstd * (dxn - xn * jnp.mean(dxn * xn, axis=-1, keepdims=True))
    return dx, _colsum(dh), _colsum(dh * (xn * g)), _colsum(dh * (1.0 + scale) * xn)


def _gmlp_s(vn_b, wmask_b, bexp, s_scr, tm):
    for ch in range(tm // CHUNK):
        rows = slice(ch * CHUNK, (ch + 1) * CHUNK)
        for hh in range(HEADS):
            cols = slice(hh * HEAD_DIM, (hh + 1) * HEAD_DIM)
            s_scr[rows, cols] = _dot(wmask_b[hh], vn_b[rows, cols]) + bexp[:, cols]
    return s_scr[...]


def _inv_count(pos, win):
    return 1.0 / jnp.minimum(pos + 1, win).astype(F32)


def _window_sums(ext, tm, trailing):
    n = tm + POOL_HALO
    sums = []
    for g, win in enumerate(POOL_WINDOWS):
        s = ext[:, g * HEAD_DIM:(g + 1) * HEAD_DIM]
        span = 1
        while span < win:
            s = s + pltpu.roll(s, span if trailing else n - span, 0)
            span *= 2
        sums.append(s[POOL_HALO:POOL_HALO + tm] if trailing else s[0:tm])
    return sums


def _pool_p(xb, xbext, pos, tm):
    sums = _window_sums(xbext, tm, True)
    return [sums[g] * _inv_count(pos, win) - xb[:, g * HEAD_DIM:(g + 1) * HEAD_DIM] for g, win in enumerate(POOL_WINDOWS)]


def _mixer_fwd(x, mod, g_mix, ln_g, ln_b, w_s, bexp, pool_w, pool_scale, conv_w, win, wpa, wpb, wpc, wo, dep, layer, seq):
    t_all, d = x.shape
    tm = min(TOKEN_TILE, seq)
    n_seq_tiles = seq // tm
    blk = win.shape[-1]

    def body(x_ref, mod_ref, gmix_ref, lng_ref, lnb_ref, ws_ref, bexp_ref, pw_ref, ps_ref, cw_ref,
             win_hbm, wpa_hbm, wpb_hbm, wpc_hbm, wo_hbm, dep_ref,
             h_ref, z_ref, ycat_ref, ocat_ref, mrg_ref, mo_ref, xmid_ref,
             win_v, wpa_v, wpb_v, wpc_v, wo_v, xbext, zcext, s_scr, sem):
        i = pl.program_id(0)
        pairs = [(win_hbm, win_v), (wo_hbm, wo_v)]
        pairs += _wp_pairs(wpa_hbm, wpa_v) + _wp_pairs(wpb_hbm, wpb_v) + _wp_pairs(wpc_hbm, wpc_v)
        _load_weights(i, pairs, sem)
        tile_in_seq = i % n_seq_tiles

        @pl.when(tile_in_seq == 0)
        def _():
            xbext[0:POOL_HALO, :] = jnp.zeros((POOL_HALO, BR_W), F32)
            zcext[0:CONV_HALO, :] = jnp.zeros((CONV_HALO, BR_W), F32)

        x_t = x_ref[...]
        shift1, scale1, gate1 = mod_ref[0, 0:1, :], mod_ref[0, 1:2, :], mod_ref[0, 2:3, :]
        _, _, h = _rms_mod(x_t, gmix_ref[...], shift1, scale1)
        hb = h.astype(BF16)
        h_ref[...] = hb
        def project(j):
            zj = _dot(hb, win_v[j])
            z_ref[:, j * blk:(j + 1) * blk] = zj.astype(BF16)
            return zj

        z0, z1 = project(0), project(1)
        u = z0[:, 0:BR_W]
        v = jnp.concatenate([z0[:, BR_W:blk], z1[:, 0:2 * BR_W - blk]], axis=1)
        xb = z1[:, 2 * BR_W - blk:blk]

        gu = _gelu(u)[0]
        gv = _gelu(v)[0]
        mu = jnp.mean(gv, axis=-1, keepdims=True)
        cen = gv - mu
        rs = lax.rsqrt(jnp.mean(cen * cen, axis=-1, keepdims=True) + EPS)
        vn = (cen * rs) * lng_ref[...] + lnb_ref[...]
        mask = _tril_mask()
        wmask_b = [(ws_ref[hh] * mask).astype(BF16) for hh in range(HEADS)]
        s = _gmlp_s(vn.astype(BF16), wmask_b, bexp_ref[...], s_scr, tm)
        oa = (gu * s).astype(BF16)
        ya = _dot(oa, wpa_v[...])

        xbext[POOL_HALO:POOL_HALO + tm, :] = xb
        pos = tile_in_seq * tm + lax.broadcasted_iota(jnp.int32, (tm, 1), 0)
        ps = _pool_p(xb, xbext, pos, tm)
        qs = [_dot(ps[g].astype(BF16), pw_ref[g].astype(BF16)) for g in range(len(POOL_WINDOWS))]
        ob = (jnp.concatenate(qs, axis=1) * ps_ref[...]).astype(BF16)
        yb = _dot(ob, wpb_v[...])
        xbext[0:POOL_HALO, :] = xbext[tm:tm + POOL_HALO, :]

        z2, z3 = project(2), project(3)
        bg = z2[:, 0:BR_W]
        cg = jnp.concatenate([z2[:, BR_W:blk], z3[:, 0:2 * BR_W - blk]], axis=1)
        hc = z3[:, 2 * BR_W - blk:blk]
        zz = cg * hc
        zcext[CONV_HALO:CONV_HALO + tm, :] = zz
        yconv = (cw_ref[0:1, :] * zcext[pl.ds(CONV_HALO - 2, tm), :] + cw_ref[1:2, :] * zcext[pl.ds(CONV_HALO - 1, tm), :]
                 + cw_ref[2:3, :] * zz)
        oc = (bg * yconv).astype(BF16)
        yc = _dot(oc, wpc_v[...])
        zcext[0:CONV_HALO, :] = zcext[tm:tm + CONV_HALO, :]

        ocat_ref[:, 0:BR_W] = oa
        ocat_ref[:, BR_W:2 * BR_W] = ob
        ocat_ref[:, 2 * BR_W:3 * BR_W] = oc
        ycat_ref[:, 0:d] = ya.astype(BF16)
        ycat_ref[:, d:2 * d] = yb.astype(BF16)
        ycat_ref[:, 2 * d:3 * d] = yc.astype(BF16)

        ys = (ya, yb, yc)
        zg = jnp.concatenate([project(j).astype(BF16) for j in range(4, N_DEV)], axis=1)
        mb = _sigmoid(zg[:, 0:d]) * ys[0].astype(BF16)
        for n in range(1, 3):
            mb = mb + _sigmoid(zg[:, n * d:(n + 1) * d]) * ys[n].astype(BF16)
        mrg_ref[...] = mb
        mo = _dot(mb, wo_v[...])
        mo_ref[...] = mo.astype(BF16)
        xmid_ref[...] = x_t + gate1 * mo

    tok = lambda cols: pl.BlockSpec((tm, cols), lambda i: (i, 0))
    in_specs = [
        tok(d),
        _mod_spec(mod, layer, n_seq_tiles),
        *[_layer_spec(a, layer) for a in (g_mix, ln_g, ln_b, w_s, bexp, pool_w, pool_scale, conv_w)],
        ANY, ANY, ANY, ANY, ANY, ANY,
    ]
    out_shape = [
        jax.ShapeDtypeStruct((t_all, d), BF16),
        jax.ShapeDtypeStruct((t_all, N_DEV * blk), BF16),
        jax.ShapeDtypeStruct((t_all, 3 * d), BF16),
        jax.ShapeDtypeStruct((t_all, 3 * BR_W), BF16),
        jax.ShapeDtypeStruct((t_all, d), BF16),
        jax.ShapeDtypeStruct((t_all, d), BF16),
        jax.ShapeDtypeStruct((t_all, d), F32),
    ]
    out_specs = [tok(d), tok(N_DEV * blk), tok(3 * d), tok(3 * BR_W), tok(d), tok(d), tok(d)]
    scratch = [
        pltpu.VMEM((N_DEV, d, blk), BF16), pltpu.VMEM((BR_W, d), BF16), pltpu.VMEM((BR_W, d), BF16),
        pltpu.VMEM((BR_W, d), BF16), pltpu.VMEM((d, d), BF16),
        pltpu.VMEM((tm + POOL_HALO, BR_W), F32), pltpu.VMEM((tm + CONV_HALO, BR_W), F32), pltpu.VMEM((tm, BR_W), F32),
        pltpu.SemaphoreType.DMA((2 + 3 * N_DEV,)),
    ]
    return pl.pallas_call(
        body, name=f"mixer_fwd_{layer}", grid=(t_all // tm,), in_specs=in_specs, out_specs=out_specs, out_shape=out_shape,
        scratch_shapes=scratch, compiler_params=_cparams(dimension_semantics=("arbitrary",)),
    )(x, mod, g_mix, ln_g, ln_b, w_s, bexp, pool_w, pool_scale, conv_w, win, wpa, wpb, wpc, wo, dep)


def _ffn_fwd(x, mod, g_ffn, w13, w2, dep, layer, seq):
    t_all, d = x.shape
    tm = min(TOKEN_TILE, seq)
    n_seq_tiles = seq // tm
    fb = w13.shape[-1]
    n_hid = N_DEV // 2

    def body(x_ref, mod_ref, g_ref, w13_hbm, w2_hbm, dep_ref, h_ref, ab_ref, hid_ref, f_ref, xo_ref, w13_v, w2_v, sem):
        i = pl.program_id(0)
        _load_weights(i, [(w13_hbm, w13_v), (w2_hbm, w2_v)], sem)
        x_t = x_ref[...]
        shift2, scale2, gate2 = mod_ref[0, 3:4, :], mod_ref[0, 4:5, :], mod_ref[0, 5:6, :]
        _, _, h = _rms_mod(x_t, g_ref[...], shift2, scale2)
        hb = h.astype(BF16)
        h_ref[...] = hb
        f = jnp.zeros((tm, d), F32)
        for k in range(n_hid):
            a = _dot(hb, w13_v[k])
            b = _dot(hb, w13_v[n_hid + k])
            a, b = a.astype(BF16), b.astype(BF16)
            ab_ref[k] = a
            ab_ref[n_hid + k] = b
            hid = (a * _sigmoid(a)) * b
            hid_ref[k] = hid
            f = f + _dot(hid, w2_v[k * fb:(k + 1) * fb, :])
        f_ref[...] = f.astype(BF16)
        xo_ref[...] = x_t + gate2 * f

    tok = lambda cols: pl.BlockSpec((tm, cols), lambda i: (i, 0))
    blk3 = lambda n: pl.BlockSpec((n, tm, fb), lambda i: (0, i, 0))
    in_specs = [tok(d), _mod_spec(mod, layer, n_seq_tiles), _layer_spec(g_ffn, layer), ANY, ANY, ANY]
    out_shape = [
        jax.ShapeDtypeStruct((t_all, d), BF16),
        jax.ShapeDtypeStruct((N_DEV, t_all, fb), BF16),
        jax.ShapeDtypeStruct((n_hid, t_all, fb), BF16),
        jax.ShapeDtypeStruct((t_all, d), BF16),
        jax.ShapeDtypeStruct((t_all, d), F32),
    ]
    out_specs = [tok(d), blk3(N_DEV), blk3(n_hid), tok(d), tok(d)]
    scratch = [pltpu.VMEM((N_DEV, d, fb), BF16), pltpu.VMEM((n_hid * fb, d), BF16), pltpu.SemaphoreType.DMA((2,))]
    return pl.pallas_call(
        body, name=f"ffn_fwd_{layer}", grid=(t_all // tm,), in_specs=in_specs, out_specs=out_specs, out_shape=out_shape,
        scratch_shapes=scratch, compiler_params=_cparams(dimension_semantics=("arbitrary",)),
    )(x, mod, g_ffn, w13, w2, dep)


def _final_loss(x, g_final, target, seq):
    t_all, d = x.shape
    tm = min(ELEMENTWISE_TILE, seq)

    def body(x_ref, g_ref, t_ref, dx_ref, loss_ref, dg_ref):
        i = pl.program_id(0)

        @pl.when(i == 0)
        def _():
            loss_ref[...] = jnp.zeros(loss_ref.shape, F32)
            dg_ref[...] = jnp.zeros(dg_ref.shape, F32)

        x_t = x_ref[...]
        g = g_ref[...]
        rstd = lax.rsqrt(jnp.mean(x_t * x_t, axis=-1, keepdims=True) + EPS)
        xn = x_t * rstd
        err = xn * g - t_ref[...]
        loss_ref[0:1, :] += _colsum(err * err) * (0.5 / d)
        dy = err * (1.0 / d)
        dg_ref[0:1, :] += _colsum(dy * xn)
        dxn = dy * g
        dx_ref[...] = rstd * (dxn - xn * jnp.mean(dxn * xn, axis=-1, keepdims=True))

        @pl.when(i == pl.num_programs(0) - 1)
        def _():
            loss_ref[...] = jnp.broadcast_to(jnp.sum(loss_ref[0:1, :], axis=1, keepdims=True), loss_ref.shape)

    tok = pl.BlockSpec((tm, d), lambda i: (i, 0))
    acc = pl.BlockSpec((8, d), lambda i: (0, 0))
    return pl.pallas_call(
        body, name="final_loss", grid=(t_all // tm,),
        in_specs=[tok, pl.BlockSpec((1, d), lambda i: (0, 0)), tok], out_specs=[tok, acc, acc],
        out_shape=[jax.ShapeDtypeStruct((t_all, d), F32), jax.ShapeDtypeStruct((8, d), F32), jax.ShapeDtypeStruct((8, d), F32)],
        compiler_params=_cparams(dimension_semantics=("arbitrary",)),
    )(x, g_final, target)


def _ffn_bwd(dxo, xmid, ab, f, mod, g_ffn, w13, w2, dep, layer, seq):
    t_all, d = xmid.shape
    tm = min(TOKEN_TILE, seq)
    n_seq_tiles = seq // tm
    fb = w13.shape[-1]
    n_hid = N_DEV // 2

    def body(dxo_ref, x_ref, ab_ref, f_ref, mod_ref, g_ref, w13_hbm, w2_hbm, dep_ref,
             dx_ref, df_ref, dab_ref, dmod_ref, dg_ref, w13_v, w2_v, sem):
        i = pl.program_id(0)
        _load_weights(i, [(w13_hbm, w13_v), (w2_hbm, w2_v)], sem)

        @pl.when(i == 0)
        def _():
            dg_ref[...] = jnp.zeros(dg_ref.shape, F32)

        @pl.when(i % n_seq_tiles == 0)
        def _():
            dmod_ref[...] = jnp.zeros(dmod_ref.shape, F32)

        scale2, gate2 = mod_ref[0, 4:5, :], mod_ref[0, 5:6, :]
        g = g_ref[...]
        x_t = x_ref[...]
        rstd = lax.rsqrt(jnp.mean(x_t * x_t, axis=-1, keepdims=True) + EPS)
        xn = x_t * rstd
        dxo_t = dxo_ref[...]
        dmod_ref[0, 2:3, :] += _colsum(dxo_t * f_ref[...].astype(F32))
        dfb = (dxo_t * gate2).astype(BF16)
        df_ref[...] = dfb
        dh = jnp.zeros((tm, d), F32)
        for k in range(n_hid):
            dhid = _dot_nt(dfb, w2_v[k * fb:(k + 1) * fb, :]).astype(BF16)
            a = ab_ref[k]
            b = ab_ref[n_hid + k]
            sg = _sigmoid(a)
            da = dhid * b * (sg * (1.0 + a * (1.0 - sg)))
            db = dhid * (a * sg)
            dab_ref[k] = da
            dab_ref[n_hid + k] = db
            dh = dh + _dot_nt(da, w13_v[k]) + _dot_nt(db, w13_v[n_hid + k])
        dx, dshift, dscale, dg = _rms_mod_bwd(dh, xn, rstd, g, scale2)
        dmod_ref[0, 0:1, :] += dshift
        dmod_ref[0, 1:2, :] += dscale
        dg_ref[0:1, :] += dg
        dx_ref[...] = dxo_t + dx

    tok = lambda cols: pl.BlockSpec((tm, cols), lambda i: (i, 0))
    blk3 = lambda n: pl.BlockSpec((n, tm, fb), lambda i: (0, i, 0))
    modspec = pl.BlockSpec((1, 8, d), lambda i: (i // n_seq_tiles, 0, 0))
    in_specs = [tok(d), tok(d), blk3(N_DEV), tok(d), _mod_spec(mod, layer, n_seq_tiles), _layer_spec(g_ffn, layer), ANY, ANY, ANY]
    out_shape = [
        jax.ShapeDtypeStruct((t_all, d), F32), jax.ShapeDtypeStruct((t_all, d), BF16),
        jax.ShapeDtypeStruct((N_DEV, t_all, fb), BF16), jax.ShapeDtypeStruct(mod.shape[1:], F32),
        jax.ShapeDtypeStruct((8, d), F32),
    ]
    out_specs = [tok(d), tok(d), blk3(N_DEV), modspec, pl.BlockSpec((8, d), lambda i: (0, 0))]
    scratch = [pltpu.VMEM((N_DEV, d, fb), BF16), pltpu.VMEM((n_hid * fb, d), BF16), pltpu.SemaphoreType.DMA((2,))]
    return pl.pallas_call(
        body, name=f"ffn_bwd_{layer}", grid=(t_all // tm,), in_specs=in_specs, out_specs=out_specs, out_shape=out_shape,
        scratch_shapes=scratch, compiler_params=_cparams(dimension_semantics=("arbitrary",)),
    )(dxo, xmid, ab, f, mod, g_ffn, w13, w2, dep)


def _mixer_bwd(dxm, x, z, ycat, mo, mod, g_mix, ln_g, ln_b, w_s, bexp, pool_w, pool_scale, conv_w,
               win, wpa, wpb, wpc, wo, dep, layer, seq):
    t_all, d = x.shape
    tm = min(TOKEN_TILE, seq)
    n_seq_tiles = seq // tm
    blk = win.shape[-1]
    n_win = len(POOL_WINDOWS)

    def tile_of(i):
        return (i // n_seq_tiles) * n_seq_tiles + (n_seq_tiles - 1 - i % n_seq_tiles)

    def halo_row_block(i):
        return jnp.maximum(tile_of(i) * (tm // POOL_HALO) - 1, 0)

    def body(dxm_ref, x_ref, z_ref, zpb_ref, zpc_ref, ycat_ref, mo_ref, mod_ref, gmix_ref, lng_ref, lnb_ref, ws_ref,
             bexp_ref, pw_ref, ps_ref, cw_ref, win_hbm, wpa_hbm, wpb_hbm, wpc_hbm, wo_hbm, dep_ref,
             dx_ref, dz_ref, dycat_ref, dmo_ref, dmod_ref, dg_ref, sm_ref, dws_ref, dssum_ref, dpw_ref,
             win_v, wpa_v, wpb_v, wpc_v, wo_v, xbext, zzext, rext, dyext, s_scr, dvn_scr, sem):
        i = pl.program_id(0)
        pairs = [(win_hbm.at[j], win_v.at[:, pl.ds(blk * j, blk)]) for j in range(N_DEV)] + [(wo_hbm, wo_v)]
        pairs += _wp_pairs(wpa_hbm, wpa_v) + _wp_pairs(wpb_hbm, wpb_v) + _wp_pairs(wpc_hbm, wpc_v)
        _load_weights(i, pairs, sem)
        tile_in_seq = n_seq_tiles - 1 - i % n_seq_tiles
        first_of_seq = tile_in_seq == 0

        @pl.when(i == 0)
        def _():
            for r in (dg_ref, sm_ref, dws_ref, dssum_ref, dpw_ref):
                r[...] = jnp.zeros(r.shape, F32)

        @pl.when(i % n_seq_tiles == 0)
        def _():
            dmod_ref[...] = jnp.zeros(dmod_ref.shape, F32)
            rext[tm:tm + POOL_HALO, :] = jnp.zeros((POOL_HALO, BR_W), F32)
            dyext[tm:tm + CONV_HALO, :] = jnp.zeros((CONV_HALO, BR_W), F32)

        shift1, scale1, gate1 = mod_ref[0, 0:1, :], mod_ref[0, 1:2, :], mod_ref[0, 2:3, :]
        dxm_t = dxm_ref[...]
        dmo = (dxm_t * gate1).astype(BF16)
        dmo_ref[...] = dmo
        dmerged = _dot_nt(dmo, wo_v[...])
        dmod_ref[0, 2:3, :] += _colsum(dxm_t * mo_ref[...].astype(F32))

        dh_parts = []

        def emit_dz(lo, hi, value):
            vb = value.astype(BF16)
            dz_ref[:, lo:hi] = vb
            dh_parts.append(_dot_nt(vb, win_v[:, lo:hi]))

        dys = []
        dmerged = dmerged.astype(BF16)
        for n in range(3):
            gt = _sigmoid(z_ref[:, 3 * d + n * d:3 * d + (n + 1) * d])
            dyn = dmerged * gt
            emit_dz(3 * d + n * d, 3 * d + (n + 1) * d, dyn * ycat_ref[:, n * d:(n + 1) * d] * (1.0 - gt))
            dycat_ref[:, n * d:(n + 1) * d] = dyn
            dys.append(dyn)

        doa = _dot_nt(dys[0], wpa_v[...])
        dob = _dot_nt(dys[1], wpb_v[...])
        doc = _dot_nt(dys[2], wpc_v[...])

        u = z_ref[:, 0:BR_W].astype(F32)
        v = z_ref[:, BR_W:2 * BR_W].astype(F32)
        gu, tu, u2 = _gelu(u)
        gv, tv, v2 = _gelu(v)
        mu = jnp.mean(gv, axis=-1, keepdims=True)
        cen = gv - mu
        rs = lax.rsqrt(jnp.mean(cen * cen, axis=-1, keepdims=True) + EPS)
        vhat = cen * rs
        lng = lng_ref[...]
        vn_b = (vhat * lng + lnb_ref[...]).astype(BF16)
        mask = _tril_mask()
        wmask = [ws_ref[hh] * mask for hh in range(HEADS)]
        s = _gmlp_s(vn_b, [w.astype(BF16) for w in wmask], bexp_ref[...], s_scr, tm)
        du = (doa * s) * _gelu_grad(u, tu, u2)
        ds = doa * gu
        ds_b = ds.astype(BF16)
        dssum = jnp.zeros((CHUNK, BR_W), F32)
        for ch in range(tm // CHUNK):
            rows = slice(ch * CHUNK, (ch + 1) * CHUNK)
            dssum = dssum + ds[rows, :]
            for hh in range(HEADS):
                cols = slice(hh * HEAD_DIM, (hh + 1) * HEAD_DIM)
                dvn_scr[rows, cols] = _dot_tn(wmask[hh].astype(BF16), ds_b[rows, cols])
                dws_ref[hh] += _dot_nt(ds_b[rows, cols], vn_b[rows, cols]) * mask
        dssum_ref[...] += dssum
        dvn = dvn_scr[...]
        sm_ref[0:1, :] += _colsum(dvn * vhat)
        sm_ref[1:2, :] += _colsum(dvn)
        dvhat = dvn * lng
        dgv = rs * (dvhat - jnp.mean(dvhat, axis=-1, keepdims=True) - vhat * jnp.mean(dvhat * vhat, axis=-1, keepdims=True))
        dv = dgv * _gelu_grad(v, tv, v2)
        emit_dz(0, 2 * BR_W, jnp.concatenate([du, dv], axis=1))

        xb = z_ref[:, 2 * BR_W:3 * BR_W].astype(F32)
        xbext[0:POOL_HALO, :] = jnp.where(first_of_seq, 0.0, zpb_ref[...].astype(F32))
        xbext[POOL_HALO:POOL_HALO + tm, :] = xb
        pos = tile_in_seq * tm + lax.broadcasted_iota(jnp.int32, (tm, 1), 0)
        ps = _pool_p(xb, xbext, pos, tm)
        scale_b = ps_ref[...]
        dq = dob * scale_b
        qs, dps = [], []
        for gi, win_len in enumerate(POOL_WINDOWS):
            cols = slice(gi * HEAD_DIM, (gi + 1) * HEAD_DIM)
            pw_b = pw_ref[gi].astype(BF16)
            p_b = ps[gi].astype(BF16)
            dq_b = dq[:, cols].astype(BF16)
            qs.append(_dot(p_b, pw_b))
            dpw_ref[gi] += _dot_tn(p_b, dq_b)
            dp = _dot_nt(dq_b, pw_b)
            dps.append(dp)
            rext[0:tm, cols] = dp * _inv_count(pos, win_len)
        sm_ref[2:3, :] += _colsum(dob * jnp.concatenate(qs, axis=1))
        dxbs = [acc - dp for acc, dp in zip(_window_sums(rext, tm, False), dps)]
        emit_dz(2 * BR_W, 3 * BR_W, jnp.concatenate(dxbs, axis=1))
        rext[tm:tm + POOL_HALO, :] = rext[0:POOL_HALO, :]

        bg = z_ref[:, 3 * BR_W:4 * BR_W].astype(F32)
        cg = z_ref[:, 4 * BR_W:5 * BR_W].astype(F32)
        hc = z_ref[:, 5 * BR_W:6 * BR_W].astype(F32)
        zz = cg * hc
        zprev = zpc_ref[POOL_HALO - CONV_HALO:POOL_HALO, :].astype(F32)
        zzext[0:CONV_HALO, :] = jnp.where(first_of_seq, 0.0, zprev[:, 0:BR_W] * zprev[:, BR_W:2 * BR_W])
        zzext[CONV_HALO:CONV_HALO + tm, :] = zz
        zm2 = zzext[pl.ds(CONV_HALO - 2, tm), :]
        zm1 = zzext[pl.ds(CONV_HALO - 1, tm), :]
        w0, w1, w2c = cw_ref[0:1, :], cw_ref[1:2, :], cw_ref[2:3, :]
        yconv = w0 * zm2 + w1 * zm1 + w2c * zz
        dyc = doc * bg
        sm_ref[3:4, :] += _colsum(dyc * zm2)
        sm_ref[4:5, :] += _colsum(dyc * zm1)
        sm_ref[5:6, :] += _colsum(dyc * zz)
        dyext[0:tm, :] = dyc
        dzz = w2c * dyc + w1 * dyext[pl.ds(1, tm), :] + w0 * dyext[pl.ds(2, tm), :]
        dyext[tm:tm + CONV_HALO, :] = dyext[0:CONV_HALO, :]
        emit_dz(3 * BR_W, 6 * BR_W, jnp.concatenate([doc * yconv, dzz * hc, dzz * cg], axis=1))

        dh = dh_parts[0]
        for part in dh_parts[1:]:
            dh = dh + part
        x_t = x_ref[...]
        rstd = lax.rsqrt(jnp.mean(x_t * x_t, axis=-1, keepdims=True) + EPS)
        dx, dshift, dscale, dg = _rms_mod_bwd(dh, x_t * rstd, rstd, gmix_ref[...], scale1)
        dmod_ref[0, 0:1, :] += dshift
        dmod_ref[0, 1:2, :] += dscale
        dg_ref[0:1, :] += dg
        dx_ref[...] = dxm_ref[...] + dx

    tok = lambda cols: pl.BlockSpec((tm, cols), lambda i: (tile_of(i), 0))
    modspec = pl.BlockSpec((1, 8, d), lambda i: (i // n_seq_tiles, 0, 0))
    in_specs = [
        tok(d), tok(d), tok(N_DEV * blk),
        pl.BlockSpec((POOL_HALO, BR_W), lambda i: (halo_row_block(i), 2)),
        pl.BlockSpec((POOL_HALO, 2 * BR_W), lambda i: (halo_row_block(i), 2)),
        tok(3 * d), tok(d), _mod_spec(mod, layer, n_seq_tiles),
        *[_layer_spec(a, layer) for a in (g_mix, ln_g, ln_b, w_s, bexp, pool_w, pool_scale, conv_w)],
        ANY, ANY, ANY, ANY, ANY, ANY,
    ]
    acc = lambda shape: pl.BlockSpec(shape, lambda i: (0,) * len(shape))
    out_shape = [
        jax.ShapeDtypeStruct((t_all, d), F32), jax.ShapeDtypeStruct((t_all, N_DEV * blk), BF16),
        jax.ShapeDtypeStruct((t_all, 3 * d), BF16), jax.ShapeDtypeStruct((t_all, d), BF16),
        jax.ShapeDtypeStruct(mod.shape[1:], F32), jax.ShapeDtypeStruct((8, d), F32), jax.ShapeDtypeStruct((8, BR_W), F32),
        jax.ShapeDtypeStruct((HEADS, CHUNK, CHUNK), F32), jax.ShapeDtypeStruct((CHUNK, BR_W), F32),
        jax.ShapeDtypeStruct((n_win, HEAD_DIM, HEAD_DIM), F32),
    ]
    out_specs = [tok(d), tok(N_DEV * blk), tok(3 * d), tok(d), modspec, acc((8, d)), acc((8, BR_W)),
                 acc((HEADS, CHUNK, CHUNK)), acc((CHUNK, BR_W)), acc((n_win, HEAD_DIM, HEAD_DIM))]
    scratch = [
        pltpu.VMEM((d, N_DEV * blk), BF16), pltpu.VMEM((BR_W, d), BF16), pltpu.VMEM((BR_W, d), BF16),
        pltpu.VMEM((BR_W, d), BF16), pltpu.VMEM((d, d), BF16),
        pltpu.VMEM((tm + POOL_HALO, BR_W), F32), pltpu.VMEM((tm + CONV_HALO, BR_W), F32),
        pltpu.VMEM((tm + POOL_HALO, BR_W), F32), pltpu.VMEM((tm + CONV_HALO, BR_W), F32),
        pltpu.VMEM((tm, BR_W), F32), pltpu.VMEM((tm, BR_W), F32),
        pltpu.SemaphoreType.DMA((1 + 4 * N_DEV,)),
    ]
    return pl.pallas_call(
        body, name=f"mixer_bwd_{layer}", grid=(t_all // tm,), in_specs=in_specs, out_specs=out_specs, out_shape=out_shape,
        scratch_shapes=scratch, compiler_params=_cparams(dimension_semantics=("arbitrary",)),
    )(dxm, x, z, z, z, ycat, mo, mod, g_mix, ln_g, ln_b, w_s, bexp, pool_w, pool_scale, conv_w, win, wpa, wpb, wpc, wo, dep)


def _wgrad(a, b, a_spec, b_spec, out_struct, out_spec, grid_kn, tk, tn, split, name, dep=None):
    deps = [] if dep is None else [dep]
    t_all = a.shape[-2]
    tt = min(WGRAD_TOKENS, t_all)
    n_t = t_all // tt

    def body(a_ref, b_ref, *rest):
        o_ref, acc = rest[len(deps):]
        t = pl.program_id(2)

        @pl.when(t == 0)
        def _():
            acc[...] = jnp.zeros(acc.shape, F32)

        acc[...] += _dot_tn(a_ref[...], b_ref[...])

        @pl.when(t == n_t - 1)
        def _():
            if split:
                for j in range(split):
                    w = tn // split
                    o_ref[j] = acc[:, j * w:(j + 1) * w].astype(o_ref.dtype)
            else:
                o_ref[...] = acc[...].astype(o_ref.dtype)

    return pl.pallas_call(
        body, name=name, grid=(*grid_kn, n_t), in_specs=[a_spec(tt), b_spec(tt)] + [ANY] * len(deps), out_specs=out_spec,
        out_shape=out_struct,
        scratch_shapes=[pltpu.VMEM((tk, tn), F32)],
        compiler_params=_cparams(dimension_semantics=("arbitrary", "arbitrary", "arbitrary")),
    )(a, b, *deps)


def _mixer_wgrads(h, dz, mrg, dmo, ocat, dycat, layer, send_first=None):
    d = h.shape[1]
    blk = dz.shape[1] // N_DEV
    g_win = _wgrad(
        h, dz, lambda tt: pl.BlockSpec((tt, d), lambda k, n, t: (t, 0)), lambda tt: pl.BlockSpec((tt, blk), lambda k, n, t: (t, n)),
        jax.ShapeDtypeStruct((N_DEV, d, blk), BF16), pl.BlockSpec((None, d, blk), lambda k, n, t: (n, 0, 0)),
        (1, N_DEV), d, blk, 0, f"wgrad_in_{layer}")
    dep = None if send_first is None else send_first(g_win)
    g_wo = _wgrad(
        mrg, dmo, lambda tt: pl.BlockSpec((tt, d), lambda k, n, t: (t, 0)), lambda tt: pl.BlockSpec((tt, d), lambda k, n, t: (t, 0)),
        jax.ShapeDtypeStruct((d, d), BF16), pl.BlockSpec((d, d), lambda k, n, t: (0, 0)), (1, 1), d, d, 0, f"wgrad_o_{layer}", dep)
    g_wp = []
    for n, nm in enumerate("abc"):
        g_wp.append(_wgrad(
            ocat, dycat, lambda tt, n=n: pl.BlockSpec((tt, BR_W), lambda k, nn, t: (t, n)),
            lambda tt, n=n: pl.BlockSpec((tt, d), lambda k, nn, t: (t, n)),
            jax.ShapeDtypeStruct((N_DEV, BR_W, d // N_DEV), BF16),
            pl.BlockSpec((N_DEV, BR_W, d // N_DEV), lambda k, nn, t: (0, 0, 0)), (1, 1), BR_W, d, N_DEV, f"wgrad_p{nm}_{layer}"))
    return [g_win, g_wo.reshape(N_DEV, d // N_DEV, d), *g_wp]


def _ffn_wgrads(h2, dab, hid, df, layer):
    d = h2.shape[1]
    fb = dab.shape[-1]
    n_hid = N_DEV // 2
    g_w13 = _wgrad(
        dab, h2, lambda tt: pl.BlockSpec((None, tt, fb), lambda k, n, t: (k, t, 0)),
        lambda tt: pl.BlockSpec((tt, d), lambda k, n, t: (t, 0)),
        jax.ShapeDtypeStruct((N_DEV, fb, d), BF16), pl.BlockSpec((None, fb, d), lambda k, n, t: (k, 0, 0)),
        (N_DEV, 1), fb, d, 0, f"wgrad_13_{layer}")
    g_w2 = _wgrad(
        hid, df, lambda tt: pl.BlockSpec((None, tt, fb), lambda k, n, t: (k, t, 0)),
        lambda tt: pl.BlockSpec((tt, d), lambda k, n, t: (t, 0)),
        jax.ShapeDtypeStruct((n_hid * fb, d), BF16), pl.BlockSpec((fb, d), lambda k, n, t: (k, 0)),
        (n_hid, 1), fb, d, 0, f"wgrad_2_{layer}")
    return [g_w13, g_w2.reshape(N_DEV, fb // 2, d)]


def _remote(src, dst, send_sem, recv_sem, dev):
    return pltpu.make_async_remote_copy(src_ref=src, dst_ref=dst, send_sem=send_sem, recv_sem=recv_sem, device_id=dev,
                                        device_id_type=MESH_ID)


HBM_SPEC = pl.BlockSpec(memory_space=pltpu.HBM)
SEM_SPEC = pl.BlockSpec(memory_space=pltpu.SEMAPHORE)
DATAFLOW = pltpu.SideEffectType.DATAFLOW_SIDE_EFFECTING
GATHER, SCATTER = "gather", "scatter"


def _xfer_sem(a, k):
    return a * (N_DEV - 1) + k - 1


def _gather_zones(weights, layer):
    me = _my_index()
    return [lax.dynamic_update_index_in_dim(lax.empty((N_DEV, *w.shape[1:]), BF16), w[layer].astype(BF16)[None], me, 0)
            for w in weights]


def _scatter_zones(partials):
    return [lax.empty(p.shape, p.dtype) for p in partials]


def _xfer_src(kind, src, land, a, me, idx):
    return land[a].at[me] if kind == GATHER else src[a].at[idx]


def _xfer_start(srcs, lands, kind, name, after):
    n, n_src = len(lands), len(srcs)

    def body(*refs):
        src, land = refs[:n_src], refs[n_src:n_src + n]
        send_sems, recv_sems = refs[n_src + n + 1], refs[n_src + n + 2]
        token = refs[-1]
        me = _my_index()
        for k in range(1, N_DEV):
            dev, idx = _peer(k)
            for a in range(n):
                q = _xfer_sem(a, k)
                _remote(_xfer_src(kind, src, land, a, me, idx), land[a].at[me], send_sems.at[q], recv_sems.at[q], dev).start()
        token[...] = jnp.zeros(token.shape, token.dtype)

    both = [*srcs, *lands]
    sems = pltpu.SemaphoreType.DMA((n * (N_DEV - 1),))
    out_shape = (sems, sems, *[pltpu.HBM(a.shape, a.dtype) for a in both], jax.ShapeDtypeStruct((8, 128), F32))
    outs = pl.pallas_call(
        body, name=name, in_specs=[HBM_SPEC] * len(both) + [ANY],
        out_specs=(SEM_SPEC, SEM_SPEC, *[HBM_SPEC] * len(both), VMEM_SPEC),
        out_shape=out_shape, input_output_aliases={i: 2 + i for i in range(len(both))},
        compiler_params=pltpu.CompilerParams(has_side_effects=DATAFLOW),
    )(*[pltpu.with_memory_space_constraint(a, pltpu.HBM) for a in both], after)
    return outs[0], outs[1], list(outs[2:2 + n_src]), list(outs[2 + n_src:2 + n_src + n]), outs[-1]


def _gather_two_level(zones, name, after):
    n = len(zones)

    def body(*refs):
        out = refs[n + 1:2 * n + 1]
        send_sems, recv_sems = refs[2 * n + 1], refs[2 * n + 2]
        x, y, c = lax.axis_index("x"), lax.axis_index("y"), lax.axis_index("c")
        index = lambda px, py, pc: 4 * px + 2 * py + pc
        me, sibling = (x, y, c), (x, y, 1 - c)
        chips = [(1 - x, y), (x, 1 - y), (1 - x, 1 - y)]

        def copy(a, k, block, to):
            rows = out[a].at[index(*block)]
            return _remote(rows, rows, send_sems.at[a * (N_DEV - 1) + k], recv_sems.at[a * (N_DEV - 1) + k], to)

        first = [copy(a, 0, me, sibling) for a in range(n)]
        first += [copy(a, 1 + j, me, (*chip, c)) for j, chip in enumerate(chips) for a in range(n)]
        for cp in first:
            cp.start()
        passed = []
        for j, chip in enumerate(chips):
            for a in range(n):
                copy(a, 1 + j, (*chip, c), me).wait_recv()
                passed.append(copy(a, 4 + j, (*chip, c), sibling))
                passed[-1].start()
        for a in range(n):
            copy(a, 0, sibling, me).wait_recv()
            for j, chip in enumerate(chips):
                copy(a, 4 + j, (*chip, 1 - c), me).wait_recv()
        for cp in first + passed:
            cp.wait_send()

    sems = pltpu.SemaphoreType.DMA((n * (N_DEV - 1),))
    return pl.pallas_call(
        body, name=name, in_specs=[ANY] * (n + 1), out_specs=[ANY] * n, out_shape=[jax.ShapeDtypeStruct(z.shape, z.dtype) for z in zones],
        input_output_aliases={i: i for i in range(n)}, scratch_shapes=[sems, sems],
        compiler_params=pltpu.CompilerParams(has_side_effects=True),
    )(*zones, after)


def _xfer_wait(send_sems, recv_sems, srcs, lands, rows, after, kind, name):
    n, n_src = len(lands), len(srcs)

    def body(*refs):
        src, land = refs[:n_src], refs[n_src:n_src + n]
        send_sems, recv_sems = refs[n_src + n], refs[n_src + n + 1]
        me = _my_index()
        for k in range(1, N_DEV):
            dev, idx = _peer(k)
            for a in range(n):
                q = _xfer_sem(rows[a], k)
                cp = _remote(_xfer_src(kind, src, land, a, me, idx), land[a].at[idx], send_sems.at[q], recv_sems.at[q], dev)
                cp.wait_send()
                cp.wait_recv()

    both = [*srcs, *lands]
    outs = pl.pallas_call(
        body, name=name, in_specs=[HBM_SPEC] * len(both) + [SEM_SPEC, SEM_SPEC, ANY], out_specs=[HBM_SPEC] * len(both),
        out_shape=[pltpu.HBM(a.shape, a.dtype) for a in both], input_output_aliases={i: i for i in range(len(both))},
        compiler_params=pltpu.CompilerParams(has_side_effects=DATAFLOW),
    )(*both, send_sems, recv_sems, after)
    return list(outs[:n_src]), list(outs[n_src:])


def _pre(c_pad, conv_pad, w_mod, b_mod_mine):
    n_layers, d, blk = w_mod.shape

    def body(c_ref, conv_ref, wmod_ref, bmod_ref, cact_ref, mod_ref, convall_ref, cact_mine, msh, send_sems, recv_sems):
        me = _my_index()
        c = c_ref[...]
        cact_mine[...] = c * _sigmoid(c)
        cact_ref[me] = cact_mine[...]
        convall_ref[me] = conv_ref[...]
        sends = []
        for k in range(1, N_DEV):
            dev, _ = _peer(k)
            sends.append(_remote(cact_mine, cact_ref.at[me], send_sems.at[0, k - 1], recv_sems.at[0, k - 1], dev))
            sends.append(_remote(conv_ref, convall_ref.at[me], send_sems.at[1, k - 1], recv_sems.at[1, k - 1], dev))
        for cp in sends:
            cp.start()
        for k in range(1, N_DEV):
            dev, idx = _peer(k)
            _remote(cact_mine, cact_ref.at[idx], send_sems.at[0, k - 1], recv_sems.at[0, k - 1], dev).wait_recv()
            _remote(conv_ref, convall_ref.at[idx], send_sems.at[1, k - 1], recv_sems.at[1, k - 1], dev).wait_recv()
        for cp in sends:
            cp.wait_send()
        cact_b = cact_ref[...].reshape(N_DEV * 8, d).astype(BF16)
        for l in range(n_layers):
            m = _dot(cact_b, wmod_ref[l].astype(BF16)) + bmod_ref[l]
            msh[l] = m.reshape(N_DEV, 8, blk)
        mod_ref[me] = msh[:, me]
        sends = []
        for k in range(1, N_DEV):
            dev, idx = _peer(k)
            sends.append(_remote(msh.at[:, idx], mod_ref.at[me], send_sems.at[2, k - 1], recv_sems.at[2, k - 1], dev))
        for cp in sends:
            cp.start()
        for k in range(1, N_DEV):
            dev, idx = _peer(k)
            _remote(msh.at[:, idx], mod_ref.at[idx], send_sems.at[2, k - 1], recv_sems.at[2, k - 1], dev).wait_recv()
        for cp in sends:
            cp.wait_send()

    out_shape = [jax.ShapeDtypeStruct((N_DEV, 8, d), F32), jax.ShapeDtypeStruct((N_DEV, n_layers, 8, blk), F32),
                 jax.ShapeDtypeStruct((N_DEV, *conv_pad.shape), F32)]
    return pl.pallas_call(
        body, name="pre", in_specs=[VMEM_SPEC] * 4, out_specs=[VMEM_SPEC] * 3, out_shape=out_shape,
        scratch_shapes=[pltpu.VMEM((8, d), F32), pltpu.VMEM((n_layers, N_DEV, 8, blk), F32),
                        pltpu.SemaphoreType.DMA((3, N_DEV - 1)), pltpu.SemaphoreType.DMA((3, N_DEV - 1))],
        compiler_params=_cparams(has_side_effects=True),
    )(c_pad, conv_pad, w_mod, b_mod_mine)


def _small_reduce(zone_buf, own_buf, zone_d, own_d, cact_all):
    _, seg, _ = zone_buf.shape
    _, n_layers, _, blk = zone_d.shape
    d = cact_all.shape[-1]

    def body(zb_ref, ob_ref, zd_ref, od_ref, cact_ref, red_ref, gw_ref, dcols):
        me = _my_index()
        total = jnp.where(me == 0, ob_ref[0], zb_ref[0])
        for p in range(1, N_DEV):
            total = total + jnp.where(me == p, ob_ref[p], zb_ref[p])
        red_ref[...] = total
        for p in range(N_DEV):
            dcols[p] = jnp.where(me == p, od_ref[p], zd_ref[p])
        cact_b = cact_ref[...].reshape(N_DEV * 8, d).astype(BF16)
        for l in range(n_layers):
            gw_ref[l] = _dot_tn(cact_b, dcols[:, l].reshape(N_DEV * 8, blk).astype(BF16))

    out_shape = [jax.ShapeDtypeStruct((seg, 128), F32), jax.ShapeDtypeStruct((n_layers, d, blk), F32)]
    return pl.pallas_call(
        body, name="small_reduce", in_specs=[VMEM_SPEC] * 5, out_specs=[VMEM_SPEC] * 2, out_shape=out_shape,
        scratch_shapes=[pltpu.VMEM(zone_d.shape, F32)], compiler_params=_cparams(),
    )(zone_buf, own_buf, zone_d, own_d, cact_all)


def _adamw_math(w, g, m, v):
    m = ADAM_B1 * m + (1.0 - ADAM_B1) * g
    v = ADAM_B2 * v + (1.0 - ADAM_B2) * (g * g)
    m_hat = m / (1.0 - ADAM_B1 ** ADAM_STEP)
    v_hat = v / (1.0 - ADAM_B2 ** ADAM_STEP)
    delta = -ADAM_LR * (m_hat / (jnp.sqrt(v_hat) + ADAM_EPS) + ADAM_WD * w)
    return delta, m, v


def _adamw(parts, w, m, v, name, first=0, earlier=None, own=None):
    n_layers = len(parts)
    n_parts, rows, cols = parts[0].shape
    tr = rows
    while tr * cols * 4 > (1 << 20) and tr % 32 == 0:
        tr //= 2
    n_r = rows // tr
    n_earlier = 0 if earlier is None else 4
    n_own = 0 if own is None else n_layers
    n_in = n_layers + n_own + 3

    def body(me_ref, *refs):
        p_refs, o_refs = refs[:n_layers], refs[n_layers:n_layers + n_own]
        w_ref, m_ref, v_ref = refs[n_layers + n_own:n_in]
        g_out, d_out, m_out, v_out = refs[n_in + n_earlier:]
        layer = pl.program_id(0)

        def part(q, p):
            value = p_refs[q][p]
            if n_own:
                value = jnp.where(me_ref[0] == p, o_refs[q][...], value)
            return value.astype(F32)

        for q in range(n_layers):

            @pl.when(layer == q)
            def _(q=q):
                g = part(q, 0)
                for p in range(1, n_parts):
                    g = g + part(q, p)
                delta, m_new, v_new = _adamw_math(w_ref[...], g, m_ref[...], v_ref[...])
                g_out[...] = g
                d_out[...] = delta
                m_out[...] = m_new
                v_out[...] = v_new

    def moving(q, l, r):
        return jnp.where(l == q, r, jnp.where(l < q, 0, n_r - 1))

    parts_spec = lambda q: pl.BlockSpec((n_parts, tr, cols), lambda l, r, me: (0, moving(q, l, r), 0))
    own_spec = lambda q: pl.BlockSpec((None, tr, cols), lambda l, r, me: (me[0], moving(q, l, r), 0))
    spec = pl.BlockSpec((None, tr, cols), lambda l, r, me: (first + l, r, 0))
    out = jax.ShapeDtypeStruct(w.shape, F32)
    grid_spec = pltpu.PrefetchScalarGridSpec(
        num_scalar_prefetch=1, grid=(n_layers, n_r),
        in_specs=[parts_spec(q) for q in range(n_layers)] + [own_spec(q) for q in range(n_own)] + [spec, spec, spec]
        + [ANY] * n_earlier,
        out_specs=[spec] * 4)
    return pl.pallas_call(
        body, name=name, grid_spec=grid_spec, out_shape=[out] * 4,
        input_output_aliases={1 + n_in + i: i for i in range(n_earlier)},
        compiler_params=_cparams(dimension_semantics=("arbitrary", "arbitrary")),
    )(_my_index().astype(jnp.int32).reshape(1), *parts, *(own or ()), w, m, v, *(earlier or ()))


def _adamw_small(gs, ws, ms, vs):
    n = len(ws)

    def body(*refs):
        g, w, m, v = (refs[i * n:(i + 1) * n] for i in range(4))
        d_out, m_out, v_out = (refs[i * n:(i + 1) * n] for i in range(4, 7))
        for i in range(n):
            delta, m_new, v_new = _adamw_math(w[i][...], g[i][...], m[i][...], v[i][...])
            d_out[i][...] = delta
            m_out[i][...] = m_new
            v_out[i][...] = v_new

    out = [jax.ShapeDtypeStruct(w.shape, F32) for w in ws]
    res = pl.pallas_call(body, name="adamw_small", in_specs=[VMEM_SPEC] * (4 * n), out_specs=[VMEM_SPEC] * (3 * n),
                         out_shape=out * 3, compiler_params=_cparams())(*gs, *ws, *ms, *vs)
    return res[:n], res[n:2 * n], res[2 * n:]


def _pack(arrays, rows_multiple):
    flat = jnp.concatenate([a.reshape(-1) for a in arrays])
    per = 128 * rows_multiple
    total = -(-flat.shape[0] // per) * per
    return jnp.pad(flat, (0, total - flat.shape[0])).reshape(total // 128, 128)


def _unpack(buf, like):
    flat = buf.reshape(-1)
    out, off = [], 0
    for a in like:
        out.append(flat[off:off + a.size].reshape(a.shape))
        off += a.size
    return out


def kernel(x, c, w_mod, b_mod, g_mix, w_in, gm_ln_g, gm_ln_b, gm_w_s, gm_b_s, w_pa, pool_w, pool_scale, w_pb, conv_w, w_pc, w_o, g_ffn, w_13, w_2, g_final, loss_target, m_w_mod, m_b_mod, m_g_mix, m_w_in, m_gm_ln_g, m_gm_ln_b, m_gm_w_s, m_gm_b_s, m_w_pa, m_pool_w, m_pool_scale, m_w_pb, m_conv_w, m_w_pc, m_w_o, m_g_ffn, m_w_13, m_w_2, m_g_final, v_w_mod, v_b_mod, v_g_mix, v_w_in, v_gm_ln_g, v_gm_ln_b, v_gm_w_s, v_gm_b_s, v_w_pa, v_pool_w, v_pool_scale, v_w_pb, v_conv_w, v_w_pc, v_w_o, v_g_ffn, v_w_13, v_w_2, v_g_final):
    nb, seq, d = x.shape
    n_layers = w_in.shape[0]
    t_all = nb * seq
    blk = w_in.shape[-1]
    me = _my_index()
    conv_shard = conv_w.shape[-1]

    c_pad = jnp.pad(c, ((0, 8 - nb), (0, 0)))
    conv_pad = jnp.pad(conv_w.reshape(n_layers * 3, conv_shard), ((0, 16 - n_layers * 3), (0, 128 - conv_shard)))
    b_mod_mine = lax.dynamic_slice_in_dim(b_mod, me * blk, blk, axis=1).reshape(n_layers, 1, blk)
    cact_all, mod_blocks, conv_all = _pre(c_pad, conv_pad, w_mod, b_mod_mine)
    mod = jnp.transpose(mod_blocks, (1, 2, 0, 3)).reshape(n_layers, 8, N_MOD, d)[:, :nb]
    mod = jnp.pad(mod, ((0, 0), (0, 0), (0, 8 - N_MOD), (0, 0)))
    conv_full = jnp.transpose(conv_all[:, :n_layers * 3, :conv_shard].reshape(N_DEV, n_layers, 3, conv_shard), (1, 2, 0, 3))
    conv_full = jnp.pad(conv_full.reshape(n_layers, 3, N_DEV * conv_shard), ((0, 0), (0, 5), (0, 0)))
    bexp = jnp.repeat(jnp.transpose(gm_b_s, (0, 2, 1)), HEAD_DIM, axis=2)

    mixer_w, ffn_w = [w_in, w_o, w_pa, w_pb, w_pc], [w_13, w_2]
    n_mix = len(mixer_w)

    def send_weights(l, ws, name, after):
        return _xfer_start([], _gather_zones(ws, l), GATHER, name, after)

    def send_grads(partials, name, after):
        return _xfer_start(partials, _scatter_zones(partials), SCATTER, name, after)

    def arrived(flight, after, kind, name):
        send_sems, recv_sems, srcs, zones, _ = flight
        return _xfer_wait(send_sems, recv_sems, srcs, zones, list(range(len(zones))), after, kind, name)

    rows = lambda a: a.reshape(n_layers, 1, -1)
    mix_small = (rows(g_mix), rows(gm_ln_g), rows(gm_ln_b), gm_w_s, bexp, pool_w, rows(pool_scale), conv_full)
    g_ffn_rows = rows(g_ffn)
    xs = x.reshape(t_all, d)
    saved, weights = [], []
    after = mod
    for l in range(n_layers):
        if l == 0:
            win_g, wo_g, wpa_g, wpb_g, wpc_g = _gather_two_level(_gather_zones(mixer_w, 0), "gather_mixer_0", cact_all)
        else:
            win_g, wo_g, wpa_g, wpb_g, wpc_g = arrived(flight, after, GATHER, f"gather_wait_mixer_{l}")[1]
        flight = send_weights(l, ffn_w, f"gather_start_ffn_{l}", win_g)
        wo_g = wo_g.reshape(d, d)
        h, z, ycat, ocat, mrg, mo, xmid = _mixer_fwd(xs, mod, *mix_small, win_g, wpa_g, wpb_g, wpc_g, wo_g, flight[4], l, seq)
        w13_g, w2_g = arrived(flight, xmid, GATHER, f"gather_wait_ffn_{l}")[1]
        dep = w13_g
        if l + 1 < n_layers:
            flight = send_weights(l + 1, mixer_w, f"gather_start_mixer_{l + 1}", w13_g)
            dep = flight[4]
        w2_g = w2_g.reshape(N_DEV * w_2.shape[1], d)
        h2, ab, hid, f, xo = _ffn_fwd(xmid, mod, g_ffn_rows, w13_g, w2_g, dep, l, seq)
        saved.append((xs, h, z, ycat, ocat, mrg, mo, xmid, h2, ab, hid, f))
        weights.append((win_g, wpa_g, wpb_g, wpc_g, wo_g, w13_g, w2_g))
        xs = after = xo

    dx, loss_blk, dgf_blk = _final_loss(xs, g_final.reshape(1, d), loss_target.reshape(t_all, d), seq)
    loss = lax.psum(loss_blk[0, 0], ("x", "y", "c"))

    ffn_flight = [None] * n_layers
    mix_flight = [None] * n_layers
    small_grads = [None] * n_layers
    dmods = [None] * n_layers
    dep = dx
    for l in reversed(range(n_layers)):
        x_in, h, z, ycat, ocat, mrg, mo, xmid, h2, ab, hid, f = saved[l]
        win_g, wpa_g, wpb_g, wpc_g, wo_g, w13_g, w2_g = weights[l]
        dxm, df, dab, dmod2, dg_ffn = _ffn_bwd(dx, xmid, ab, f, mod, g_ffn_rows, w13_g, w2_g, dep, l, seq)
        ffn_flight[l] = send_grads(_ffn_wgrads(h2, dab, hid, df, l), f"grads_start_ffn_{l}", dxm)
        dx, dz, dycat, dmo, dmod1, dg_mix, sm, dws, dssum, dpw = _mixer_bwd(
            dxm, x_in, z, ycat, mo, mod, *mix_small, win_g, wpa_g, wpb_g, wpc_g, wo_g, ffn_flight[l][4], l, seq)
        if l > 0:
            mix_flight[l] = send_grads(_mixer_wgrads(h, dz, mrg, dmo, ocat, dycat, l), f"grads_start_mixer_{l}", dx)
            dep = mix_flight[l][4]
        dmod = jnp.concatenate([dmod1[:, 0:3], dmod2[:, 0:3]], axis=1).reshape(nb, N_MOD * d)
        dmods[l] = dmod
        db_s = jnp.transpose(jnp.sum(dssum.reshape(CHUNK, HEADS, HEAD_DIM), axis=2))
        small_grads[l] = [jnp.sum(dmod, axis=0), dg_mix[0], sm[0], sm[1], dws, db_s, dpw, sm[2], sm[3:6], dg_ffn[0]]
    grad_x = dx.reshape(nb, seq, d)

    names = ["b_mod", "g_mix", "ln_g", "ln_b", "w_s", "b_s", "pool_w", "pool_scale", "conv_w", "g_ffn"]
    per_name = [jnp.stack([small_grads[l][n] for l in range(n_layers)]) for n in range(len(names))] + [dgf_blk[0]]
    buf = _pack(per_name, 8 * N_DEV)
    dmod_all = jnp.pad(jnp.stack(dmods), ((0, 0), (0, 8 - nb), (0, 0)))
    dmod_blocks = jnp.transpose(dmod_all.reshape(n_layers, 8, N_DEV, blk), (2, 0, 1, 3))
    seg = buf.shape[0] // N_DEV
    small_flight = send_grads([buf.reshape(N_DEV, seg, 128), dmod_blocks], "small_start", buf)
    win_flight = []

    def send_win(g_win):
        win_flight.append(send_grads([g_win], "grads_start_in_0", small_flight[4]))
        return win_flight[0][4]

    mixer_partials = _mixer_wgrads(h, dz, mrg, dmo, ocat, dycat, 0, send_win)
    (own_buf, own_d), (zone_buf, zone_d) = arrived(small_flight, mixer_partials[1], SCATTER, "small_wait")
    red_seg, grad_w_mod = _small_reduce(zone_buf, own_buf, zone_d, own_d, cact_all)
    red_zone = lax.dynamic_update_index_in_dim(lax.empty((N_DEV, seg, 128), F32), red_seg[None], me, 0)
    red_flight = _xfer_start([], [red_zone], GATHER, "small_gather_start", red_seg)

    mix_flight[0] = send_grads(mixer_partials[1:], "grads_start_mixer_0", red_flight[4])
    results = {}
    results["w_mod"] = _adamw([grad_w_mod[l][None] for l in range(n_layers)], w_mod, m_w_mod, v_w_mod, "adamw_w_mod")
    layers = list(range(n_layers))
    done = (results["w_mod"][1][0, 0, 0] + results["w_mod"][2][0, 0, 0]).reshape(1)
    ffn_recv = [arrived(ffn_flight[l], done, SCATTER, f"grads_wait_ffn_{l}") for l in reversed(layers)][::-1]
    mix_recv = [None] + [arrived(mix_flight[l], done, SCATTER, f"grads_wait_mixer_{l}") for l in reversed(layers[1:])][::-1]
    swap = lambda a: jnp.swapaxes(a, 1, 2)
    got = lambda recv, ls, a: dict(parts=[recv[l][1][a] for l in ls], own=[recv[l][0][a] for l in ls])
    results["w_13"] = [swap(r) for r in _adamw(w=swap(w_13), m=swap(m_w_13), v=swap(v_w_13), name="adamw_w_13", **got(ffn_recv, layers, 0))]
    results["w_2"] = _adamw(w=w_2, m=m_w_2, v=v_w_2, name="adamw_w_2", **got(ffn_recv, layers, 1))
    mix_m = [m_w_in, m_w_o, m_w_pa, m_w_pb, m_w_pc]
    mix_v = [v_w_in, v_w_o, v_w_pa, v_w_pb, v_w_pc]
    mix_names = ["w_in", "w_o", "w_pa", "w_pb", "w_pc"]
    early = [_adamw(w=mixer_w[a], m=mix_m[a], v=mix_v[a], name=f"adamw_{mix_names[a]}_later", first=1, **got(mix_recv, layers[1:], a))
             for a in range(n_mix)]

    red = arrived(red_flight, results["w_2"][1], GATHER, "small_gather_wait")[1][0].reshape(buf.shape)
    (g_b_mod, g_g_mix, g_ln_g, g_ln_b, g_w_s, g_b_s, g_pool_w, g_pool_scale, g_conv_full, g_g_ffn, g_g_final) = _unpack(red, per_name)
    g_conv = lax.dynamic_slice_in_dim(g_conv_full, me * conv_shard, conv_shard, axis=2)

    small_w =[b_mod, g_mix, gm_ln_g, gm_ln_b, gm_w_s, gm_b_s, pool_w, pool_scale, conv_w, g_ffn, g_final]
    small_m = [m_b_mod, m_g_mix, m_gm_ln_g, m_gm_ln_b, m_gm_w_s, m_gm_b_s, m_pool_w, m_pool_scale, m_conv_w, m_g_ffn, m_g_final]
    small_v = [v_b_mod, v_g_mix, v_gm_ln_g, v_gm_ln_b, v_gm_w_s, v_gm_b_s, v_pool_w, v_pool_scale, v_conv_w, v_g_ffn, v_g_final]
    small_g = [g_b_mod, g_g_mix, g_ln_g, g_ln_b, g_w_s, g_b_s, g_pool_w, g_pool_scale, g_conv, g_g_ffn, g_g_final]
    small_names = ["b_mod", "g_mix", "gm_ln_g", "gm_ln_b", "gm_w_s", "gm_b_s", "pool_w", "pool_scale", "conv_w", "g_ffn", "g_final"]
    sd, sm_new, sv_new = _adamw_small(small_g, small_w, small_m, small_v)
    for n, nm in enumerate(small_names):
        results[nm] = (small_g[n], sd[n], sm_new[n], sv_new[n])

    done = (results["w_13"][1][0, 0, 0] + sd[-1][0] + sum(e[1][1, 0, 0] for e in early)).reshape(1)
    win_own, win_zone = arrived(win_flight[0], done, SCATTER, "grads_wait_in_0")
    rest_own, rest_zone = arrived(mix_flight[0], win_zone[0], SCATTER, "grads_wait_mixer_0")
    mix_recv[0] = (win_own + rest_own, win_zone + rest_zone)
    for a in range(n_mix):
        results[mix_names[a]] = _adamw(w=mixer_w[a], m=mix_m[a], v=mix_v[a], name=f"adamw_{mix_names[a]}_first", earlier=early[a],
                                       **got(mix_recv, [0], a))

    order = ["w_mod", "b_mod", "g_mix", "w_in", "gm_ln_g", "gm_ln_b", "gm_w_s", "gm_b_s", "w_pa", "pool_w", "pool_scale",
             "w_pb", "conv_w", "w_pc", "w_o", "g_ffn", "w_13", "w_2", "g_final"]
    return (loss, grad_x, *[results[nm][0] for nm in order], *[results[nm][1] for nm in order],
            *[results[nm][2] for nm in order], *[results[nm][3] for nm in order])
```

```python
import functools

import jax
import jax.numpy as jnp
from jax import lax
from jax.experimental import pallas as pl
from jax.experimental.pallas import tpu as pltpu

F32 = jnp.float32
BF16 = jnp.bfloat16
MESH_ID = pl.DeviceIdType.MESH

N_DEV = 8
EPS = 1e-6
CHUNK = 128
HEADS = 4
HEAD_DIM = 128
BR_W = 512
POOL_WINDOWS = (2, 4, 8, 16)
POOL_HALO = 16
CONV_HALO = 8
N_MOD = 6
ADAM_LR = 0.001
ADAM_B1 = 0.9
ADAM_B2 = 0.999
ADAM_EPS = 1e-08
ADAM_WD = 0.01
ADAM_STEP = 10

TOKEN_TILE = 256
ELEMENTWISE_TILE = 1024
WGRAD_TOKENS = 2048
VMEM_LIMIT = 56 * 1024 * 1024
GELU_K = 0.7978845608028654
GELU_C = 0.044715

ANY = pl.BlockSpec(memory_space=pl.ANY)
VMEM_SPEC = pl.BlockSpec(memory_space=pltpu.VMEM)


def _cparams(**kw):
    return pltpu.CompilerParams(vmem_limit_bytes=VMEM_LIMIT, **kw)


def _dot(a, b):
    return jnp.dot(a, b, preferred_element_type=F32)


def _dot_nt(a, b):
    return lax.dot_general(a, b, (((1,), (1,)), ((), ())), preferred_element_type=F32)


def _dot_tn(a, b):
    return lax.dot_general(a, b, (((0,), (0,)), ((), ())), preferred_element_type=F32)


def _colsum(a):
    return jnp.sum(a, axis=0, keepdims=True)


def _sigmoid(x):
    return 0.5 * jnp.tanh(0.5 * x) + 0.5


def _gelu(x):
    x2 = x * x
    t = jnp.tanh(x * (GELU_K + (GELU_K * GELU_C) * x2))
    return (0.5 * x) * (1.0 + t), t, x2


def _gelu_grad(x, t, x2):
    one_t = 1.0 + t
    return 0.5 * one_t + (0.5 * x) * (one_t * (1.0 - t)) * (GELU_K + (3.0 * GELU_K * GELU_C) * x2)


def _tril_mask():
    r = lax.broadcasted_iota(jnp.int32, (CHUNK, CHUNK), 0)
    c = lax.broadcasted_iota(jnp.int32, (CHUNK, CHUNK), 1)
    return (r >= c).astype(F32)


def _my_index():
    return 4 * lax.axis_index("x") + 2 * lax.axis_index("y") + lax.axis_index("c")


def _peer(k):
    x, y, c = lax.axis_index("x"), lax.axis_index("y"), lax.axis_index("c")
    px = 1 - x if (k >> 2) & 1 else x
    py = 1 - y if (k >> 1) & 1 else y
    pc = 1 - c if k & 1 else c
    return (px, py, pc), 4 * px + 2 * py + pc


def _load_weights(step, pairs, sem):
    @pl.when(step == 0)
    def _():
        copies = [pltpu.make_async_copy(src, dst, sem.at[n]) for n, (src, dst) in enumerate(pairs)]
        for cp in copies:
            cp.start()
        for cp in copies:
            cp.wait()


def _wp_pairs(wp_hbm, wp_v):
    return [(wp_hbm.at[j], wp_v.at[:, pl.ds(HEAD_DIM * j, HEAD_DIM)]) for j in range(N_DEV)]


def _layer_spec(a, layer):
    return pl.BlockSpec((None, *a.shape[1:]), lambda i: (layer,) + (0,) * (a.ndim - 1))


def _mod_spec(mod, layer, n_seq_tiles):
    return pl.BlockSpec((None, 1, *mod.shape[2:]), lambda i: (layer, i // n_seq_tiles, 0, 0))


def _rms_mod(x, g, shift, scale):
    rstd = lax.rsqrt(jnp.mean(x * x, axis=-1, keepdims=True) + EPS)
    xn = x * rstd
    return xn, rstd, (xn * g) * (1.0 + scale) + shift


def _rms_mod_bwd(dh, xn, rstd, g, scale):
    dxn = dh * (1.0 + scale) * g
    dx = rstd * (dxn - xn * jnp.mean(dxn * xn, axis=-1, keepdims=True))
    return dx, _colsum(dh), _colsum(dh * (xn * g)), _colsum(dh * (1.0 + scale) * xn)


def _gmlp_s(vn_b, wmask_b, bexp, s_scr, tm):
    for ch in range(tm // CHUNK):
        rows = slice(ch * CHUNK, (ch + 1) * CHUNK)
        for hh in range(HEADS):
            cols = slice(hh * HEAD_DIM, (hh + 1) * HEAD_DIM)
            s_scr[rows, cols] = _dot(wmask_b[hh], vn_b[rows, cols]) + bexp[:, cols]
    return s_scr[...]


def _inv_count(pos, win):
    return 1.0 / jnp.minimum(pos + 1, win).astype(F32)


def _window_sums(ext, tm, trailing):
    n = tm + POOL_HALO
    sums = []
    for g, win in enumerate(POOL_WINDOWS):
        s = ext[:, g * HEAD_DIM:(g + 1) * HEAD_DIM]
        span = 1
        while span < win:
            s = s + pltpu.roll(s, span if trailing else n - span, 0)
            span *= 2
        sums.append(s[POOL_HALO:POOL_HALO + tm] if trailing else s[0:tm])
    return sums


def _pool_p(xb, xbext, pos, tm):
    sums = _window_sums(xbext, tm, True)
    return [sums[g] * _inv_count(pos, win) - xb[:, g * HEAD_DIM:(g + 1) * HEAD_DIM] for g, win in enumerate(POOL_WINDOWS)]


def _mixer_fwd(x, mod, g_mix, ln_g, ln_b, w_s, bexp, pool_w, pool_scale, conv_w, win, wpa, wpb, wpc, wo, dep, layer, seq):
    t_all, d = x.shape
    tm = min(TOKEN_TILE, seq)
    n_seq_tiles = seq // tm
    blk = win.shape[-1]

    def body(x_ref, mod_ref, gmix_ref, lng_ref, lnb_ref, ws_ref, bexp_ref, pw_ref, ps_ref, cw_ref,
             win_hbm, wpa_hbm, wpb_hbm, wpc_hbm, wo_hbm, dep_ref,
             h_ref, z_ref, ycat_ref, ocat_ref, mrg_ref, mo_ref, xmid_ref,
             win_v, wpa_v, wpb_v, wpc_v, wo_v, xbext, zcext, s_scr, sem):
        i = pl.program_id(0)
        pairs = [(win_hbm, win_v), (wo_hbm, wo_v)]
        pairs += _wp_pairs(wpa_hbm, wpa_v) + _wp_pairs(wpb_hbm, wpb_v) + _wp_pairs(wpc_hbm, wpc_v)
        _load_weights(i, pairs, sem)
        tile_in_seq = i % n_seq_tiles

        @pl.when(tile_in_seq == 0)
        def _():
            xbext[0:POOL_HALO, :] = jnp.zeros((POOL_HALO, BR_W), F32)
            zcext[0:CONV_HALO, :] = jnp.zeros((CONV_HALO, BR_W), F32)

        x_t = x_ref[...]
        shift1, scale1, gate1 = mod_ref[0, 0:1, :], mod_ref[0, 1:2, :], mod_ref[0, 2:3, :]
        _, _, h = _rms_mod(x_t, gmix_ref[...], shift1, scale1)
        hb = h.astype(BF16)
        h_ref[...] = hb
        def project(j):
            zj = _dot(hb, win_v[j])
            z_ref[:, j * blk:(j + 1) * blk] = zj.astype(BF16)
            return zj

        z0, z1 = project(0), project(1)
        u = z0[:, 0:BR_W]
        v = jnp.concatenate([z0[:, BR_W:blk], z1[:, 0:2 * BR_W - blk]], axis=1)
        xb = z1[:, 2 * BR_W - blk:blk]

        gu = _gelu(u)[0]
        gv = _gelu(v)[0]
        mu = jnp.mean(gv, axis=-1, keepdims=True)
        cen = gv - mu
        rs = lax.rsqrt(jnp.mean(cen * cen, axis=-1, keepdims=True) + EPS)
        vn = (cen * rs) * lng_ref[...] + lnb_ref[...]
        mask = _tril_mask()
        wmask_b = [(ws_ref[hh] * mask).astype(BF16) for hh in range(HEADS)]
        s = _gmlp_s(vn.astype(BF16), wmask_b, bexp_ref[...], s_scr, tm)
        oa = (gu * s).astype(BF16)
        ya = _dot(oa, wpa_v[...])

        xbext[POOL_HALO:POOL_HALO + tm, :] = xb
        pos = tile_in_seq * tm + lax.broadcasted_iota(jnp.int32, (tm, 1), 0)
        ps = _pool_p(xb, xbext, pos, tm)
        qs = [_dot(ps[g].astype(BF16), pw_ref[g].astype(BF16)) for g in range(len(POOL_WINDOWS))]
        ob = (jnp.concatenate(qs, axis=1) * ps_ref[...]).astype(BF16)
        yb = _dot(ob, wpb_v[...])
        xbext[0:POOL_HALO, :] = xbext[tm:tm + POOL_HALO, :]

        z2, z3 = project(2), project(3)
        bg = z2[:, 0:BR_W]
        cg = jnp.concatenate([z2[:, BR_W:blk], z3[:, 0:2 * BR_W - blk]], axis=1)
        hc = z3[:, 2 * BR_W - blk:blk]
        zz = cg * hc
        zcext[CONV_HALO:CONV_HALO + tm, :] = zz
        yconv = (cw_ref[0:1, :] * zcext[pl.ds(CONV_HALO - 2, tm), :] + cw_ref[1:2, :] * zcext[pl.ds(CONV_HALO - 1, tm), :]
                 + cw_ref[2:3, :] * zz)
        oc = (bg * yconv).astype(BF16)
        yc = _dot(oc, wpc_v[...])
        zcext[0:CONV_HALO, :] = zcext[tm:tm + CONV_HALO, :]

        ocat_ref[:, 0:BR_W] = oa
        ocat_ref[:, BR_W:2 * BR_W] = ob
        ocat_ref[:, 2 * BR_W:3 * BR_W] = oc
        ycat_ref[:, 0:d] = ya.astype(BF16)
        ycat_ref[:, d:2 * d] = yb.astype(BF16)
        ycat_ref[:, 2 * d:3 * d] = yc.astype(BF16)

        ys = (ya, yb, yc)
        zg = jnp.concatenate([project(j).astype(BF16) for j in range(4, N_DEV)], axis=1)
        mb = _sigmoid(zg[:, 0:d]) * ys[0].astype(BF16)
        for n in range(1, 3):
            mb = mb + _sigmoid(zg[:, n * d:(n + 1) * d]) * ys[n].astype(BF16)
        mrg_ref[...] = mb
        mo = _dot(mb, wo_v[...])
        mo_ref[...] = mo.astype(BF16)
        xmid_ref[...] = x_t + gate1 * mo

    tok = lambda cols: pl.BlockSpec((tm, cols), lambda i: (i, 0))
    in_specs = [
        tok(d),
        _mod_spec(mod, layer, n_seq_tiles),
        *[_layer_spec(a, layer) for a in (g_mix, ln_g, ln_b, w_s, bexp, pool_w, pool_scale, conv_w)],
        ANY, ANY, ANY, ANY, ANY, ANY,
    ]
    out_shape = [
        jax.ShapeDtypeStruct((t_all, d), BF16),
        jax.ShapeDtypeStruct((t_all, N_DEV * blk), BF16),
        jax.ShapeDtypeStruct((t_all, 3 * d), BF16),
        jax.ShapeDtypeStruct((t_all, 3 * BR_W), BF16),
        jax.ShapeDtypeStruct((t_all, d), BF16),
        jax.ShapeDtypeStruct((t_all, d), BF16),
        jax.ShapeDtypeStruct((t_all, d), F32),
    ]
    out_specs = [tok(d), tok(N_DEV * blk), tok(3 * d), tok(3 * BR_W), tok(d), tok(d), tok(d)]
    scratch = [
        pltpu.VMEM((N_DEV, d, blk), BF16), pltpu.VMEM((BR_W, d), BF16), pltpu.VMEM((BR_W, d), BF16),
        pltpu.VMEM((BR_W, d), BF16), pltpu.VMEM((d, d), BF16),
        pltpu.VMEM((tm + POOL_HALO, BR_W), F32), pltpu.VMEM((tm + CONV_HALO, BR_W), F32), pltpu.VMEM((tm, BR_W), F32),
        pltpu.SemaphoreType.DMA((2 + 3 * N_DEV,)),
    ]
    return pl.pallas_call(
        body, name=f"mixer_fwd_{layer}", grid=(t_all // tm,), in_specs=in_specs, out_specs=out_specs, out_shape=out_shape,
        scratch_shapes=scratch, compiler_params=_cparams(dimension_semantics=("arbitrary",)),
    )(x, mod, g_mix, ln_g, ln_b, w_s, bexp, pool_w, pool_scale, conv_w, win, wpa, wpb, wpc, wo, dep)


def _ffn_fwd(x, mod, g_ffn, w13, w2, dep, layer, seq):
    t_all, d = x.shape
    tm = min(TOKEN_TILE, seq)
    n_seq_tiles = seq // tm
    fb = w13.shape[-1]
    n_hid = N_DEV // 2

    def body(x_ref, mod_ref, g_ref, w13_hbm, w2_hbm, dep_ref, h_ref, ab_ref, hid_ref, f_ref, xo_ref, w13_v, w2_v, sem):
        i = pl.program_id(0)
        _load_weights(i, [(w13_hbm, w13_v), (w2_hbm, w2_v)], sem)
        x_t = x_ref[...]
        shift2, scale2, gate2 = mod_ref[0, 3:4, :], mod_ref[0, 4:5, :], mod_ref[0, 5:6, :]
        _, _, h = _rms_mod(x_t, g_ref[...], shift2, scale2)
        hb = h.astype(BF16)
        h_ref[...] = hb
        f = jnp.zeros((tm, d), F32)
        for k in range(n_hid):
            a = _dot(hb, w13_v[k])
            b = _dot(hb, w13_v[n_hid + k])
            a, b = a.astype(BF16), b.astype(BF16)
            ab_ref[k] = a
            ab_ref[n_hid + k] = b
            hid = (a * _sigmoid(a)) * b
            hid_ref[k] = hid
            f = f + _dot(hid, w2_v[k * fb:(k + 1) * fb, :])
        f_ref[...] = f.astype(BF16)
        xo_ref[...] = x_t + gate2 * f

    tok = lambda cols: pl.BlockSpec((tm, cols), lambda i: (i, 0))
    blk3 = lambda n: pl.BlockSpec((n, tm, fb), lambda i: (0, i, 0))
    in_specs = [tok(d), _mod_spec(mod, layer, n_seq_tiles), _layer_spec(g_ffn, layer), ANY, ANY, ANY]
    out_shape = [
        jax.ShapeDtypeStruct((t_all, d), BF16),
        jax.ShapeDtypeStruct((N_DEV, t_all, fb), BF16),
        jax.ShapeDtypeStruct((n_hid, t_all, fb), BF16),
        jax.ShapeDtypeStruct((t_all, d), BF16),
        jax.ShapeDtypeStruct((t_all, d), F32),
    ]
    out_specs = [tok(d), blk3(N_DEV), blk3(n_hid), tok(d), tok(d)]
    scratch = [pltpu.VMEM((N_DEV, d, fb), BF16), pltpu.VMEM((n_hid * fb, d), BF16), pltpu.SemaphoreType.DMA((2,))]
    return pl.pallas_call(
        body, name=f"ffn_fwd_{layer}", grid=(t_all // tm,), in_specs=in_specs, out_specs=out_specs, out_shape=out_shape,
        scratch_shapes=scratch, compiler_params=_cparams(dimension_semantics=("arbitrary",)),
    )(x, mod, g_ffn, w13, w2, dep)


def _final_loss(x, g_final, target, seq):
    t_all, d = x.shape
    tm = min(ELEMENTWISE_TILE, seq)

    def body(x_ref, g_ref, t_ref, dx_ref, loss_ref, dg_ref):
        i = pl.program_id(0)

        @pl.when(i == 0)
        def _():
            loss_ref[...] = jnp.zeros(loss_ref.shape, F32)
            dg_ref[...] = jnp.zeros(dg_ref.shape, F32)

        x_t = x_ref[...]
        g = g_ref[...]
        rstd = lax.rsqrt(jnp.mean(x_t * x_t, axis=-1, keepdims=True) + EPS)
        xn = x_t * rstd
        err = xn * g - t_ref[...]
        loss_ref[0:1, :] += _colsum(err * err) * (0.5 / d)
        dy = err * (1.0 / d)
        dg_ref[0:1, :] += _colsum(dy * xn)
        dxn = dy * g
        dx_ref[...] = rstd * (dxn - xn * jnp.mean(dxn * xn, axis=-1, keepdims=True))

        @pl.when(i == pl.num_programs(0) - 1)
        def _():
            loss_ref[...] = jnp.broadcast_to(jnp.sum(loss_ref[0:1, :], axis=1, keepdims=True), loss_ref.shape)

    tok = pl.BlockSpec((tm, d), lambda i: (i, 0))
    acc = pl.BlockSpec((8, d), lambda i: (0, 0))
    return pl.pallas_call(
        body, name="final_loss", grid=(t_all // tm,),
        in_specs=[tok, pl.BlockSpec((1, d), lambda i: (0, 0)), tok], out_specs=[tok, acc, acc],
        out_shape=[jax.ShapeDtypeStruct((t_all, d), F32), jax.ShapeDtypeStruct((8, d), F32), jax.ShapeDtypeStruct((8, d), F32)],
        compiler_params=_cparams(dimension_semantics=("arbitrary",)),
    )(x, g_final, target)


def _ffn_bwd(dxo, xmid, ab, f, mod, g_ffn, w13, w2, dep, layer, seq):
    t_all, d = xmid.shape
    tm = min(TOKEN_TILE, seq)
    n_seq_tiles = seq // tm
    fb = w13.shape[-1]
    n_hid = N_DEV // 2

    def body(dxo_ref, x_ref, ab_ref, f_ref, mod_ref, g_ref, w13_hbm, w2_hbm, dep_ref,
             dx_ref, df_ref, dab_ref, dmod_ref, dg_ref, w13_v, w2_v, sem):
        i = pl.program_id(0)
        _load_weights(i, [(w13_hbm, w13_v), (w2_hbm, w2_v)], sem)

        @pl.when(i == 0)
        def _():
            dg_ref[...] = jnp.zeros(dg_ref.shape, F32)

        @pl.when(i % n_seq_tiles == 0)
        def _():
            dmod_ref[...] = jnp.zeros(dmod_ref.shape, F32)

        scale2, gate2 = mod_ref[0, 4:5, :], mod_ref[0, 5:6, :]
        g = g_ref[...]
        x_t = x_ref[...]
        rstd = lax.rsqrt(jnp.mean(x_t * x_t, axis=-1, keepdims=True) + EPS)
        xn = x_t * rstd
        dxo_t = dxo_ref[...]
        dmod_ref[0, 2:3, :] += _colsum(dxo_t * f_ref[...].astype(F32))
        dfb = (dxo_t * gate2).astype(BF16)
        df_ref[...] = dfb
        dh = jnp.zeros((tm, d), F32)
        for k in range(n_hid):
            dhid = _dot_nt(dfb, w2_v[k * fb:(k + 1) * fb, :]).astype(BF16)
            a = ab_ref[k]
            b = ab_ref[n_hid + k]
            sg = _sigmoid(a)
            da = dhid * b * (sg * (1.0 + a * (1.0 - sg)))
            db = dhid * (a * sg)
            dab_ref[k] = da
            dab_ref[n_hid + k] = db
            dh = dh + _dot_nt(da, w13_v[k]) + _dot_nt(db, w13_v[n_hid + k])
        dx, dshift, dscale, dg = _rms_mod_bwd(dh, xn, rstd, g, scale2)
        dmod_ref[0, 0:1, :] += dshift
        dmod_ref[0, 1:2, :] += dscale
        dg_ref[0:1, :] += dg
        dx_ref[...] = dxo_t + dx

    tok = lambda cols: pl.BlockSpec((tm, cols), lambda i: (i, 0))
    blk3 = lambda n: pl.BlockSpec((n, tm, fb), lambda i: (0, i, 0))
    modspec = pl.BlockSpec((1, 8, d), lambda i: (i // n_seq_tiles, 0, 0))
    in_specs = [tok(d), tok(d), blk3(N_DEV), tok(d), _mod_spec(mod, layer, n_seq_tiles), _layer_spec(g_ffn, layer), ANY, ANY, ANY]
    out_shape = [
        jax.ShapeDtypeStruct((t_all, d), F32), jax.ShapeDtypeStruct((t_all, d), BF16),
        jax.ShapeDtypeStruct((N_DEV, t_all, fb), BF16), jax.ShapeDtypeStruct(mod.shape[1:], F32),
        jax.ShapeDtypeStruct((8, d), F32),
    ]
    out_specs = [tok(d), tok(d), blk3(N_DEV), modspec, pl.BlockSpec((8, d), lambda i: (0, 0))]
    scratch = [pltpu.VMEM((N_DEV, d, fb), BF16), pltpu.VMEM((n_hid * fb, d), BF16), pltpu.SemaphoreType.DMA((2,))]
    return pl.pallas_call(
        body, name=f"ffn_bwd_{layer}", grid=(t_all // tm,), in_specs=in_specs, out_specs=out_specs, out_shape=out_shape,
        scratch_shapes=scratch, compiler_params=_cparams(dimension_semantics=("arbitrary",)),
    )(dxo, xmid, ab, f, mod, g_ffn, w13, w2, dep)


def _mixer_bwd(dxm, x, z, ycat, mo, mod, g_mix, ln_g, ln_b, w_s, bexp, pool_w, pool_scale, conv_w,
               win, wpa, wpb, wpc, wo, dep, layer, seq):
    t_all, d = x.shape
    tm = min(TOKEN_TILE, seq)
    n_seq_tiles = seq // tm
    blk = win.shape[-1]
    n_win = len(POOL_WINDOWS)

    def tile_of(i):
        return (i // n_seq_tiles) * n_seq_tiles + (n_seq_tiles - 1 - i % n_seq_tiles)

    def halo_row_block(i):
        return jnp.maximum(tile_of(i) * (tm // POOL_HALO) - 1, 0)

    def body(dxm_ref, x_ref, z_ref, zpb_ref, zpc_ref, ycat_ref, mo_ref, mod_ref, gmix_ref, lng_ref, lnb_ref, ws_ref,
             bexp_ref, pw_ref, ps_ref, cw_ref, win_hbm, wpa_hbm, wpb_hbm, wpc_hbm, wo_hbm, dep_ref,
             dx_ref, dz_ref, dycat_ref, dmo_ref, dmod_ref, dg_ref, sm_ref, dws_ref, dssum_ref, dpw_ref,
             win_v, wpa_v, wpb_v, wpc_v, wo_v, xbext, zzext, rext, dyext, s_scr, dvn_scr, sem):
        i = pl.program_id(0)
        pairs = [(win_hbm.at[j], win_v.at[:, pl.ds(blk * j, blk)]) for j in range(N_DEV)] + [(wo_hbm, wo_v)]
        pairs += _wp_pairs(wpa_hbm, wpa_v) + _wp_pairs(wpb_hbm, wpb_v) + _wp_pairs(wpc_hbm, wpc_v)
        _load_weights(i, pairs, sem)
        tile_in_seq = n_seq_tiles - 1 - i % n_seq_tiles
        first_of_seq = tile_in_seq == 0

        @pl.when(i == 0)
        def _():
            for r in (dg_ref, sm_ref, dws_ref, dssum_ref, dpw_ref):
                r[...] = jnp.zeros(r.shape, F32)

        @pl.when(i % n_seq_tiles == 0)
        def _():
            dmod_ref[...] = jnp.zeros(dmod_ref.shape, F32)
            rext[tm:tm + POOL_HALO, :] = jnp.zeros((POOL_HALO, BR_W), F32)
            dyext[tm:tm + CONV_HALO, :] = jnp.zeros((CONV_HALO, BR_W), F32)

        shift1, scale1, gate1 = mod_ref[0, 0:1, :], mod_ref[0, 1:2, :], mod_ref[0, 2:3, :]
        dxm_t = dxm_ref[...]
        dmo = (dxm_t * gate1).astype(BF16)
        dmo_ref[...] = dmo
        dmerged = _dot_nt(dmo, wo_v[...])
        dmod_ref[0, 2:3, :] += _colsum(dxm_t * mo_ref[...].astype(F32))

        dh_parts = []

        def emit_dz(lo, hi, value):
            vb = value.astype(BF16)
            dz_ref[:, lo:hi] = vb
            dh_parts.append(_dot_nt(vb, win_v[:, lo:hi]))

        dys = []
        dmerged = dmerged.astype(BF16)
        for n in range(3):
            gt = _sigmoid(z_ref[:, 3 * d + n * d:3 * d + (n + 1) * d])
            dyn = dmerged * gt
            emit_dz(3 * d + n * d, 3 * d + (n + 1) * d, dyn * ycat_ref[:, n * d:(n + 1) * d] * (1.0 - gt))
            dycat_ref[:, n * d:(n + 1) * d] = dyn
            dys.append(dyn)

        doa = _dot_nt(dys[0], wpa_v[...])
        dob = _dot_nt(dys[1], wpb_v[...])
        doc = _dot_nt(dys[2], wpc_v[...])

        u = z_ref[:, 0:BR_W].astype(F32)
        v = z_ref[:, BR_W:2 * BR_W].astype(F32)
        gu, tu, u2 = _gelu(u)
        gv, tv, v2 = _gelu(v)
        mu = jnp.mean(gv, axis=-1, keepdims=True)
        cen = gv - mu
        rs = lax.rsqrt(jnp.mean(cen * cen, axis=-1, keepdims=True) + EPS)
        vhat = cen * rs
        lng = lng_ref[...]
        vn_b = (vhat * lng + lnb_ref[...]).astype(BF16)
        mask = _tril_mask()
        wmask = [ws_ref[hh] * mask for hh in range(HEADS)]
        s = _gmlp_s(vn_b, [w.astype(BF16) for w in wmask], bexp_ref[...], s_scr, tm)
        du = (doa * s) * _gelu_grad(u, tu, u2)
        ds = doa * gu
        ds_b = ds.astype(BF16)
        dssum = jnp.zeros((CHUNK, BR_W), F32)
        for ch in range(tm // CHUNK):
            rows = slice(ch * CHUNK, (ch + 1) * CHUNK)
            dssum = dssum + ds[rows, :]
            for hh in range(HEADS):
                cols = slice(hh * HEAD_DIM, (hh + 1) * HEAD_DIM)
                dvn_scr[rows, cols] = _dot_tn(wmask[hh].astype(BF16), ds_b[rows, cols])
                dws_ref[hh] += _dot_nt(ds_b[rows, cols], vn_b[rows, cols]) * mask
        dssum_ref[...] += dssum
        dvn = dvn_scr[...]
        sm_ref[0:1, :] += _colsum(dvn * vhat)
        sm_ref[1:2, :] += _colsum(dvn)
        dvhat = dvn * lng
        dgv = rs * (dvhat - jnp.mean(dvhat, axis=-1, keepdims=True) - vhat * jnp.mean(dvhat * vhat, axis=-1, keepdims=True))
        dv = dgv * _gelu_grad(v, tv, v2)
        emit_dz(0, 2 * BR_W, jnp.concatenate([du, dv], axis=1))

        xb = z_ref[:, 2 * BR_W:3 * BR_W].astype(F32)
        xbext[0:POOL_HALO, :] = jnp.where(first_of_seq, 0.0, zpb_ref[...].astype(F32))
        xbext[POOL_HALO:POOL_HALO + tm, :] = xb
        pos = tile_in_seq * tm + lax.broadcasted_iota(jnp.int32, (tm, 1), 0)
        ps = _pool_p(xb, xbext, pos, tm)
        scale_b = ps_ref[...]
        dq = dob * scale_b
        qs, dps = [], []
        for gi, win_len in enumerate(POOL_WINDOWS):
            cols = slice(gi * HEAD_DIM, (gi + 1) * HEAD_DIM)
            pw_b = pw_ref[gi].astype(BF16)
            p_b = ps[gi].astype(BF16)
            dq_b = dq[:, cols].astype(BF16)
            qs.append(_dot(p_b, pw_b))
            dpw_ref[gi] += _dot_tn(p_b, dq_b)
            dp = _dot_nt(dq_b, pw_b)
            dps.append(dp)
            rext[0:tm, cols] = dp * _inv_count(pos, win_len)
        sm_ref[2:3, :] += _colsum(dob * jnp.concatenate(qs, axis=1))
        dxbs = [acc - dp for acc, dp in zip(_window_sums(rext, tm, False), dps)]
        emit_dz(2 * BR_W, 3 * BR_W, jnp.concatenate(dxbs, axis=1))
        rext[tm:tm + POOL_HALO, :] = rext[0:POOL_HALO, :]

        bg = z_ref[:, 3 * BR_W:4 * BR_W].astype(F32)
        cg = z_ref[:, 4 * BR_W:5 * BR_W].astype(F32)
        hc = z_ref[:, 5 * BR_W:6 * BR_W].astype(F32)
        zz = cg * hc
        zprev = zpc_ref[POOL_HALO - CONV_HALO:POOL_HALO, :].astype(F32)
        zzext[0:CONV_HALO, :] = jnp.where(first_of_seq, 0.0, zprev[:, 0:BR_W] * zprev[:, BR_W:2 * BR_W])
        zzext[CONV_HALO:CONV_HALO + tm, :] = zz
        zm2 = zzext[pl.ds(CONV_HALO - 2, tm), :]
        zm1 = zzext[pl.ds(CONV_HALO - 1, tm), :]
        w0, w1, w2c = cw_ref[0:1, :], cw_ref[1:2, :], cw_ref[2:3, :]
        yconv = w0 * zm2 + w1 * zm1 + w2c * zz
        dyc = doc * bg
        sm_ref[3:4, :] += _colsum(dyc * zm2)
        sm_ref[4:5, :] += _colsum(dyc * zm1)
        sm_ref[5:6, :] += _colsum(dyc * zz)
        dyext[0:tm, :] = dyc
        dzz = w2c * dyc + w1 * dyext[pl.ds(1, tm), :] + w0 * dyext[pl.ds(2, tm), :]
        dyext[tm:tm + CONV_HALO, :] = dyext[0:CONV_HALO, :]
        emit_dz(3 * BR_W, 6 * BR_W, jnp.concatenate([doc * yconv, dzz * hc, dzz * cg], axis=1))

        dh = dh_parts[0]
        for part in dh_parts[1:]:
            dh = dh + part
        x_t = x_ref[...]
        rstd = lax.rsqrt(jnp.mean(x_t * x_t, axis=-1, keepdims=True) + EPS)
        dx, dshift, dscale, dg = _rms_mod_bwd(dh, x_t * rstd, rstd, gmix_ref[...], scale1)
        dmod_ref[0, 0:1, :] += dshift
        dmod_ref[0, 1:2, :] += dscale
        dg_ref[0:1, :] += dg
        dx_ref[...] = dxm_ref[...] + dx

    tok = lambda cols: pl.BlockSpec((tm, cols), lambda i: (tile_of(i), 0))
    modspec = pl.BlockSpec((1, 8, d), lambda i: (i // n_seq_tiles, 0, 0))
    in_specs = [
        tok(d), tok(d), tok(N_DEV * blk),
        pl.BlockSpec((POOL_HALO, BR_W), lambda i: (halo_row_block(i), 2)),
        pl.BlockSpec((POOL_HALO, 2 * BR_W), lambda i: (halo_row_block(i), 2)),
        tok(3 * d), tok(d), _mod_spec(mod, layer, n_seq_tiles),
        *[_layer_spec(a, layer) for a in (g_mix, ln_g, ln_b, w_s, bexp, pool_w, pool_scale, conv_w)],
        ANY, ANY, ANY, ANY, ANY, ANY,
    ]
    acc = lambda shape: pl.BlockSpec(shape, lambda i: (0,) * len(shape))
    out_shape = [
        jax.ShapeDtypeStruct((t_all, d), F32), jax.ShapeDtypeStruct((t_all, N_DEV * blk), BF16),
        jax.ShapeDtypeStruct((t_all, 3 * d), BF16), jax.ShapeDtypeStruct((t_all, d), BF16),
        jax.ShapeDtypeStruct(mod.shape[1:], F32), jax.ShapeDtypeStruct((8, d), F32), jax.ShapeDtypeStruct((8, BR_W), F32),
        jax.ShapeDtypeStruct((HEADS, CHUNK, CHUNK), F32), jax.ShapeDtypeStruct((CHUNK, BR_W), F32),
        jax.ShapeDtypeStruct((n_win, HEAD_DIM, HEAD_DIM), F32),
    ]
    out_specs = [tok(d), tok(N_DEV * blk), tok(3 * d), tok(d), modspec, acc((8, d)), acc((8, BR_W)),
                 acc((HEADS, CHUNK, CHUNK)), acc((CHUNK, BR_W)), acc((n_win, HEAD_DIM, HEAD_DIM))]
    scratch = [
        pltpu.VMEM((d, N_DEV * blk), BF16), pltpu.VMEM((BR_W, d), BF16), pltpu.VMEM((BR_W, d), BF16),
        pltpu.VMEM((BR_W, d), BF16), pltpu.VMEM((d, d), BF16),
        pltpu.VMEM((tm + POOL_HALO, BR_W), F32), pltpu.VMEM((tm + CONV_HALO, BR_W), F32),
        pltpu.VMEM((tm + POOL_HALO, BR_W), F32), pltpu.VMEM((tm + CONV_HALO, BR_W), F32),
        pltpu.VMEM((tm, BR_W), F32), pltpu.VMEM((tm, BR_W), F32),
        pltpu.SemaphoreType.DMA((1 + 4 * N_DEV,)),
    ]
    return pl.pallas_call(
        body, name=f"mixer_bwd_{layer}", grid=(t_all // tm,), in_specs=in_specs, out_specs=out_specs, out_shape=out_shape,
        scratch_shapes=scratch, compiler_params=_cparams(dimension_semantics=("arbitrary",)),
    )(dxm, x, z, z, z, ycat, mo, mod, g_mix, ln_g, ln_b, w_s, bexp, pool_w, pool_scale, conv_w, win, wpa, wpb, wpc, wo, dep)


def _wgrad(a, b, a_spec, b_spec, out_struct, out_spec, grid_kn, tk, tn, split, name):
    t_all = a.shape[-2]
    tt = min(WGRAD_TOKENS, t_all)
    n_t = t_all // tt

    def body(a_ref, b_ref, o_ref, acc):
        t = pl.program_id(2)

        @pl.when(t == 0)
        def _():
            acc[...] = jnp.zeros(acc.shape, F32)

        acc[...] += _dot_tn(a_ref[...], b_ref[...])

        @pl.when(t == n_t - 1)
        def _():
            if split:
                for j in range(split):
                    w = tn // split
                    o_ref[j] = acc[:, j * w:(j + 1) * w].astype(o_ref.dtype)
            else:
                o_ref[...] = acc[...].astype(o_ref.dtype)

    return pl.pallas_call(
        body, name=name, grid=(*grid_kn, n_t), in_specs=[a_spec(tt), b_spec(tt)], out_specs=out_spec, out_shape=out_struct,
        scratch_shapes=[pltpu.VMEM((tk, tn), F32)],
        compiler_params=_cparams(dimension_semantics=("arbitrary", "arbitrary", "arbitrary")),
    )(a, b)


def _mixer_wgrads(h, dz, mrg, dmo, ocat, dycat, layer):
    d = h.shape[1]
    blk = dz.shape[1] // N_DEV
    g_win = _wgrad(
        h, dz, lambda tt: pl.BlockSpec((tt, d), lambda k, n, t: (t, 0)), lambda tt: pl.BlockSpec((tt, blk), lambda k, n, t: (t, n)),
        jax.ShapeDtypeStruct((N_DEV, d, blk), BF16), pl.BlockSpec((None, d, blk), lambda k, n, t: (n, 0, 0)),
        (1, N_DEV), d, blk, 0, f"wgrad_in_{layer}")
    g_wo = _wgrad(
        mrg, dmo, lambda tt: pl.BlockSpec((tt, d), lambda k, n, t: (t, 0)), lambda tt: pl.BlockSpec((tt, d), lambda k, n, t: (t, 0)),
        jax.ShapeDtypeStruct((d, d), BF16), pl.BlockSpec((d, d), lambda k, n, t: (0, 0)), (1, 1), d, d, 0, f"wgrad_o_{layer}")
    g_wp = []
    for n, nm in enumerate("abc"):
        g_wp.append(_wgrad(
            ocat, dycat, lambda tt, n=n: pl.BlockSpec((tt, BR_W), lambda k, nn, t: (t, n)),
            lambda tt, n=n: pl.BlockSpec((tt, d), lambda k, nn, t: (t, n)),
            jax.ShapeDtypeStruct((N_DEV, BR_W, d // N_DEV), BF16),
            pl.BlockSpec((N_DEV, BR_W, d // N_DEV), lambda k, nn, t: (0, 0, 0)), (1, 1), BR_W, d, N_DEV, f"wgrad_p{nm}_{layer}"))
    return [g_win, g_wo.reshape(N_DEV, d // N_DEV, d), *g_wp]


def _ffn_wgrads(h2, dab, hid, df, layer):
    d = h2.shape[1]
    fb = dab.shape[-1]
    n_hid = N_DEV // 2
    g_w13 = _wgrad(
        dab, h2, lambda tt: pl.BlockSpec((None, tt, fb), lambda k, n, t: (k, t, 0)),
        lambda tt: pl.BlockSpec((tt, d), lambda k, n, t: (t, 0)),
        jax.ShapeDtypeStruct((N_DEV, fb, d), BF16), pl.BlockSpec((None, fb, d), lambda k, n, t: (k, 0, 0)),
        (N_DEV, 1), fb, d, 0, f"wgrad_13_{layer}")
    g_w2 = _wgrad(
        hid, df, lambda tt: pl.BlockSpec((None, tt, fb), lambda k, n, t: (k, t, 0)),
        lambda tt: pl.BlockSpec((tt, d), lambda k, n, t: (t, 0)),
        jax.ShapeDtypeStruct((n_hid * fb, d), BF16), pl.BlockSpec((fb, d), lambda k, n, t: (k, 0)),
        (n_hid, 1), fb, d, 0, f"wgrad_2_{layer}")
    return [g_w13, g_w2.reshape(N_DEV, fb // 2, d)]


def _remote(src, dst, send_sem, recv_sem, dev):
    return pltpu.make_async_remote_copy(src_ref=src, dst_ref=dst, send_sem=send_sem, recv_sem=recv_sem, device_id=dev,
                                        device_id_type=MESH_ID)


HBM_SPEC = pl.BlockSpec(memory_space=pltpu.HBM)
SEM_SPEC = pl.BlockSpec(memory_space=pltpu.SEMAPHORE)
DATAFLOW = pltpu.SideEffectType.DATAFLOW_SIDE_EFFECTING
GATHER, SCATTER = "gather", "scatter"


def _xfer_sem(a, k):
    return a * (N_DEV - 1) + k - 1


def _gather_zones(weights, layer):
    me = _my_index()
    return [lax.dynamic_update_index_in_dim(lax.empty((N_DEV, *w.shape[1:]), BF16), w[layer].astype(BF16)[None], me, 0)
            for w in weights]


def _scatter_zones(partials):
    return [lax.empty(p.shape, p.dtype) for p in partials]


def _xfer_src(kind, src, land, a, me, idx):
    return land[a].at[me] if kind == GATHER else src[a].at[idx]


def _xfer_start(srcs, lands, kind, name, after):
    n, n_src = len(lands), len(srcs)

    def body(*refs):
        src, land = refs[:n_src], refs[n_src:n_src + n]
        send_sems, recv_sems = refs[n_src + n + 1], refs[n_src + n + 2]
        token = refs[-1]
        me = _my_index()
        for k in range(1, N_DEV):
            dev, idx = _peer(k)
            for a in range(n):
                q = _xfer_sem(a, k)
                _remote(_xfer_src(kind, src, land, a, me, idx), land[a].at[me], send_sems.at[q], recv_sems.at[q], dev).start()
        token[...] = jnp.zeros(token.shape, token.dtype)

    both = [*srcs, *lands]
    sems = pltpu.SemaphoreType.DMA((n * (N_DEV - 1),))
    out_shape = (sems, sems, *[pltpu.HBM(a.shape, a.dtype) for a in both], jax.ShapeDtypeStruct((8, 128), F32))
    outs = pl.pallas_call(
        body, name=name, in_specs=[HBM_SPEC] * len(both) + [ANY],
        out_specs=(SEM_SPEC, SEM_SPEC, *[HBM_SPEC] * len(both), VMEM_SPEC),
        out_shape=out_shape, input_output_aliases={i: 2 + i for i in range(len(both))},
        compiler_params=pltpu.CompilerParams(has_side_effects=DATAFLOW),
    )(*[pltpu.with_memory_space_constraint(a, pltpu.HBM) for a in both], after)
    return outs[0], outs[1], list(outs[2:2 + n_src]), list(outs[2 + n_src:2 + n_src + n]), outs[-1]


def _gather_two_level(zones, name, after):
    n = len(zones)

    def body(*refs):
        out = refs[n + 1:2 * n + 1]
        send_sems, recv_sems = refs[2 * n + 1], refs[2 * n + 2]
        x, y, c = lax.axis_index("x"), lax.axis_index("y"), lax.axis_index("c")
        index = lambda px, py, pc: 4 * px + 2 * py + pc
        me, sibling = (x, y, c), (x, y, 1 - c)
        chips = [(1 - x, y), (x, 1 - y), (1 - x, 1 - y)]

        def copy(a, k, block, to):
            rows = out[a].at[index(*block)]
            return _remote(rows, rows, send_sems.at[a * (N_DEV - 1) + k], recv_sems.at[a * (N_DEV - 1) + k], to)

        first = [copy(a, 0, me, sibling) for a in range(n)]
        first += [copy(a, 1 + j, me, (*chip, c)) for j, chip in enumerate(chips) for a in range(n)]
        for cp in first:
            cp.start()
        passed = []
        for j, chip in enumerate(chips):
            for a in range(n):
                copy(a, 1 + j, (*chip, c), me).wait_recv()
                passed.append(copy(a, 4 + j, (*chip, c), sibling))
                passed[-1].start()
        for a in range(n):
            copy(a, 0, sibling, me).wait_recv()
            for j, chip in enumerate(chips):
                copy(a, 4 + j, (*chip, 1 - c), me).wait_recv()
        for cp in first + passed:
            cp.wait_send()

    sems = pltpu.SemaphoreType.DMA((n * (N_DEV - 1),))
    return pl.pallas_call(
        body, name=name, in_specs=[ANY] * (n + 1), out_specs=[ANY] * n, out_shape=[jax.ShapeDtypeStruct(z.shape, z.dtype) for z in zones],
        input_output_aliases={i: i for i in range(n)}, scratch_shapes=[sems, sems],
        compiler_params=pltpu.CompilerParams(has_side_effects=True),
    )(*zones, after)


def _xfer_wait(send_sems, recv_sems, srcs, lands, rows, after, kind, name):
    n, n_src = len(lands), len(srcs)

    def body(*refs):
        src, land = refs[:n_src], refs[n_src:n_src + n]
        send_sems, recv_sems = refs[n_src + n], refs[n_src + n + 1]
        me = _my_index()
        for k in range(1, N_DEV):
            dev, idx = _peer(k)
            for a in range(n):
                q = _xfer_sem(rows[a], k)
                cp = _remote(_xfer_src(kind, src, land, a, me, idx), land[a].at[idx], send_sems.at[q], recv_sems.at[q], dev)
                cp.wait_send()
                cp.wait_recv()

    both = [*srcs, *lands]
    outs = pl.pallas_call(
        body, name=name, in_specs=[HBM_SPEC] * len(both) + [SEM_SPEC, SEM_SPEC, ANY], out_specs=[HBM_SPEC] * len(both),
        out_shape=[pltpu.HBM(a.shape, a.dtype) for a in both], input_output_aliases={i: i for i in range(len(both))},
        compiler_params=pltpu.CompilerParams(has_side_effects=DATAFLOW),
    )(*both, send_sems, recv_sems, after)
    return list(outs[:n_src]), list(outs[n_src:])


def _pre(c_pad, conv_pad, w_mod, b_mod_mine):
    n_layers, d, blk = w_mod.shape

    def body(c_ref, conv_ref, wmod_ref, bmod_ref, cact_ref, mod_ref, convall_ref, cact_mine, msh, send_sems, recv_sems):
        me = _my_index()
        c = c_ref[...]
        cact_mine[...] = c * _sigmoid(c)
        cact_ref[me] = cact_mine[...]
        convall_ref[me] = conv_ref[...]
        sends = []
        for k in range(1, N_DEV):
            dev, _ = _peer(k)
            sends.append(_remote(cact_mine, cact_ref.at[me], send_sems.at[0, k - 1], recv_sems.at[0, k - 1], dev))
            sends.append(_remote(conv_ref, convall_ref.at[me], send_sems.at[1, k - 1], recv_sems.at[1, k - 1], dev))
        for cp in sends:
            cp.start()
        for k in range(1, N_DEV):
            dev, idx = _peer(k)
            _remote(cact_mine, cact_ref.at[idx], send_sems.at[0, k - 1], recv_sems.at[0, k - 1], dev).wait_recv()
            _remote(conv_ref, convall_ref.at[idx], send_sems.at[1, k - 1], recv_sems.at[1, k - 1], dev).wait_recv()
        for cp in sends:
            cp.wait_send()
        cact_b = cact_ref[...].reshape(N_DEV * 8, d).astype(BF16)
        for l in range(n_layers):
            m = _dot(cact_b, wmod_ref[l].astype(BF16)) + bmod_ref[l]
            msh[l] = m.reshape(N_DEV, 8, blk)
        mod_ref[me] = msh[:, me]
        sends = []
        for k in range(1, N_DEV):
            dev, idx = _peer(k)
            sends.append(_remote(msh.at[:, idx], mod_ref.at[me], send_sems.at[2, k - 1], recv_sems.at[2, k - 1], dev))
        for cp in sends:
            cp.start()
        for k in range(1, N_DEV):
            dev, idx = _peer(k)
            _remote(msh.at[:, idx], mod_ref.at[idx], send_sems.at[2, k - 1], recv_sems.at[2, k - 1], dev).wait_recv()
        for cp in sends:
            cp.wait_send()

    out_shape = [jax.ShapeDtypeStruct((N_DEV, 8, d), F32), jax.ShapeDtypeStruct((N_DEV, n_layers, 8, blk), F32),
                 jax.ShapeDtypeStruct((N_DEV, *conv_pad.shape), F32)]
    return pl.pallas_call(
        body, name="pre", in_specs=[VMEM_SPEC] * 4, out_specs=[VMEM_SPEC] * 3, out_shape=out_shape,
        scratch_shapes=[pltpu.VMEM((8, d), F32), pltpu.VMEM((n_layers, N_DEV, 8, blk), F32),
                        pltpu.SemaphoreType.DMA((3, N_DEV - 1)), pltpu.SemaphoreType.DMA((3, N_DEV - 1))],
        compiler_params=_cparams(has_side_effects=True),
    )(c_pad, conv_pad, w_mod, b_mod_mine)


def _small_reduce(zone_buf, own_buf, zone_d, own_d, cact_all):
    _, seg, _ = zone_buf.shape
    _, n_layers, _, blk = zone_d.shape
    d = cact_all.shape[-1]

    def body(zb_ref, ob_ref, zd_ref, od_ref, cact_ref, red_ref, gw_ref, dcols):
        me = _my_index()
        total = jnp.where(me == 0, ob_ref[0], zb_ref[0])
        for p in range(1, N_DEV):
            total = total + jnp.where(me == p, ob_ref[p], zb_ref[p])
        red_ref[...] = total
        for p in range(N_DEV):
            dcols[p] = jnp.where(me == p, od_ref[p], zd_ref[p])
        cact_b = cact_ref[...].reshape(N_DEV * 8, d).astype(BF16)
        for l in range(n_layers):
            gw_ref[l] = _dot_tn(cact_b, dcols[:, l].reshape(N_DEV * 8, blk).astype(BF16))

    out_shape = [jax.ShapeDtypeStruct((seg, 128), F32), jax.ShapeDtypeStruct((n_layers, d, blk), F32)]
    return pl.pallas_call(
        body, name="small_reduce", in_specs=[VMEM_SPEC] * 5, out_specs=[VMEM_SPEC] * 2, out_shape=out_shape,
        scratch_shapes=[pltpu.VMEM(zone_d.shape, F32)], compiler_params=_cparams(),
    )(zone_buf, own_buf, zone_d, own_d, cact_all)


def _adamw_math(w, g, m, v):
    m = ADAM_B1 * m + (1.0 - ADAM_B1) * g
    v = ADAM_B2 * v + (1.0 - ADAM_B2) * (g * g)
    m_hat = m / (1.0 - ADAM_B1 ** ADAM_STEP)
    v_hat = v / (1.0 - ADAM_B2 ** ADAM_STEP)
    delta = -ADAM_LR * (m_hat / (jnp.sqrt(v_hat) + ADAM_EPS) + ADAM_WD * w)
    return delta, m, v


def _adamw(parts, w, m, v, name, first=0, earlier=None, own=None):
    n_layers = len(parts)
    n_parts, rows, cols = parts[0].shape
    tr = rows
    while tr * cols * 4 > (1 << 20) and tr % 32 == 0:
        tr //= 2
    n_r = rows // tr
    n_earlier = 0 if earlier is None else 4
    n_own = 0 if own is None else n_layers
    n_in = n_layers + n_own + 3

    def body(me_ref, *refs):
        p_refs, o_refs = refs[:n_layers], refs[n_layers:n_layers + n_own]
        w_ref, m_ref, v_ref = refs[n_layers + n_own:n_in]
        g_out, d_out, m_out, v_out = refs[n_in + n_earlier:]
        layer = pl.program_id(0)

        def part(q, p):
            value = p_refs[q][p]
            if n_own:
                value = jnp.where(me_ref[0] == p, o_refs[q][...], value)
            return value.astype(F32)

        for q in range(n_layers):

            @pl.when(layer == q)
            def _(q=q):
                g = part(q, 0)
                for p in range(1, n_parts):
                    g = g + part(q, p)
                delta, m_new, v_new = _adamw_math(w_ref[...], g, m_ref[...], v_ref[...])
                g_out[...] = g
                d_out[...] = delta
                m_out[...] = m_new
                v_out[...] = v_new

    def moving(q, l, r):
        return jnp.where(l == q, r, jnp.where(l < q, 0, n_r - 1))

    parts_spec = lambda q: pl.BlockSpec((n_parts, tr, cols), lambda l, r, me: (0, moving(q, l, r), 0))
    own_spec = lambda q: pl.BlockSpec((None, tr, cols), lambda l, r, me: (me[0], moving(q, l, r), 0))
    spec = pl.BlockSpec((None, tr, cols), lambda l, r, me: (first + l, r, 0))
    out = jax.ShapeDtypeStruct(w.shape, F32)
    grid_spec = pltpu.PrefetchScalarGridSpec(
        num_scalar_prefetch=1, grid=(n_layers, n_r),
        in_specs=[parts_spec(q) for q in range(n_layers)] + [own_spec(q) for q in range(n_own)] + [spec, spec, spec]
        + [ANY] * n_earlier,
        out_specs=[spec] * 4)
    return pl.pallas_call(
        body, name=name, grid_spec=grid_spec, out_shape=[out] * 4,
        input_output_aliases={1 + n_in + i: i for i in range(n_earlier)},
        compiler_params=_cparams(dimension_semantics=("arbitrary", "arbitrary")),
    )(_my_index().astype(jnp.int32).reshape(1), *parts, *(own or ()), w, m, v, *(earlier or ()))


def _adamw_small(gs, ws, ms, vs):
    n = len(ws)

    def body(*refs):
        g, w, m, v = (refs[i * n:(i + 1) * n] for i in range(4))
        d_out, m_out, v_out = (refs[i * n:(i + 1) * n] for i in range(4, 7))
        for i in range(n):
            delta, m_new, v_new = _adamw_math(w[i][...], g[i][...], m[i][...], v[i][...])
            d_out[i][...] = delta
            m_out[i][...] = m_new
            v_out[i][...] = v_new

    out = [jax.ShapeDtypeStruct(w.shape, F32) for w in ws]
    res = pl.pallas_call(body, name="adamw_small", in_specs=[VMEM_SPEC] * (4 * n), out_specs=[VMEM_SPEC] * (3 * n),
                         out_shape=out * 3, compiler_params=_cparams())(*gs, *ws, *ms, *vs)
    return res[:n], res[n:2 * n], res[2 * n:]


def _pack(arrays, rows_multiple):
    flat = jnp.concatenate([a.reshape(-1) for a in arrays])
    per = 128 * rows_multiple
    total = -(-flat.shape[0] // per) * per
    return jnp.pad(flat, (0, total - flat.shape[0])).reshape(total // 128, 128)


def _unpack(buf, like):
    flat = buf.reshape(-1)
    out, off = [], 0
    for a in like:
        out.append(flat[off:off + a.size].reshape(a.shape))
        off += a.size
    return out


def kernel(x, c, w_mod, b_mod, g_mix, w_in, gm_ln_g, gm_ln_b, gm_w_s, gm_b_s, w_pa, pool_w, pool_scale, w_pb, conv_w, w_pc, w_o, g_ffn, w_13, w_2, g_final, loss_target, m_w_mod, m_b_mod, m_g_mix, m_w_in, m_gm_ln_g, m_gm_ln_b, m_gm_w_s, m_gm_b_s, m_w_pa, m_pool_w, m_pool_scale, m_w_pb, m_conv_w, m_w_pc, m_w_o, m_g_ffn, m_w_13, m_w_2, m_g_final, v_w_mod, v_b_mod, v_g_mix, v_w_in, v_gm_ln_g, v_gm_ln_b, v_gm_w_s, v_gm_b_s, v_w_pa, v_pool_w, v_pool_scale, v_w_pb, v_conv_w, v_w_pc, v_w_o, v_g_ffn, v_w_13, v_w_2, v_g_final):
    nb, seq, d = x.shape
    n_layers = w_in.shape[0]
    t_all = nb * seq
    blk = w_in.shape[-1]
    me = _my_index()
    conv_shard = conv_w.shape[-1]

    c_pad = jnp.pad(c, ((0, 8 - nb), (0, 0)))
    conv_pad = jnp.pad(conv_w.reshape(n_layers * 3, conv_shard), ((0, 16 - n_layers * 3), (0, 128 - conv_shard)))
    b_mod_mine = lax.dynamic_slice_in_dim(b_mod, me * blk, blk, axis=1).reshape(n_layers, 1, blk)
    cact_all, mod_blocks, conv_all = _pre(c_pad, conv_pad, w_mod, b_mod_mine)
    mod = jnp.transpose(mod_blocks, (1, 2, 0, 3)).reshape(n_layers, 8, N_MOD, d)[:, :nb]
    mod = jnp.pad(mod, ((0, 0), (0, 0), (0, 8 - N_MOD), (0, 0)))
    conv_full = jnp.transpose(conv_all[:, :n_layers * 3, :conv_shard].reshape(N_DEV, n_layers, 3, conv_shard), (1, 2, 0, 3))
    conv_full = jnp.pad(conv_full.reshape(n_layers, 3, N_DEV * conv_shard), ((0, 0), (0, 5), (0, 0)))
    bexp = jnp.repeat(jnp.transpose(gm_b_s, (0, 2, 1)), HEAD_DIM, axis=2)

    mixer_w, ffn_w = [w_in, w_o, w_pa, w_pb, w_pc], [w_13, w_2]
    n_mix = len(mixer_w)

    def send_weights(l, ws, name, after):
        return _xfer_start([], _gather_zones(ws, l), GATHER, name, after)

    def send_grads(partials, name, after):
        return _xfer_start(partials, _scatter_zones(partials), SCATTER, name, after)

    def arrived(flight, after, kind, name):
        send_sems, recv_sems, srcs, zones, _ = flight
        return _xfer_wait(send_sems, recv_sems, srcs, zones, list(range(len(zones))), after, kind, name)

    rows = lambda a: a.reshape(n_layers, 1, -1)
    mix_small = (rows(g_mix), rows(gm_ln_g), rows(gm_ln_b), gm_w_s, bexp, pool_w, rows(pool_scale), conv_full)
    g_ffn_rows = rows(g_ffn)
    xs = x.reshape(t_all, d)
    saved, weights = [], []
    after = mod
    for l in range(n_layers):
        if l == 0:
            win_g, wo_g, wpa_g, wpb_g, wpc_g = _gather_two_level(_gather_zones(mixer_w, 0), "gather_mixer_0", cact_all)
        else:
            win_g, wo_g, wpa_g, wpb_g, wpc_g = arrived(flight, after, GATHER, f"gather_wait_mixer_{l}")[1]
        flight = send_weights(l, ffn_w, f"gather_start_ffn_{l}", win_g)
        wo_g = wo_g.reshape(d, d)
        h, z, ycat, ocat, mrg, mo, xmid = _mixer_fwd(xs, mod, *mix_small, win_g, wpa_g, wpb_g, wpc_g, wo_g, flight[4], l, seq)
        w13_g, w2_g = arrived(flight, xmid, GATHER, f"gather_wait_ffn_{l}")[1]
        dep = w13_g
        if l + 1 < n_layers:
            flight = send_weights(l + 1, mixer_w, f"gather_start_mixer_{l + 1}", w13_g)
            dep = flight[4]
        w2_g = w2_g.reshape(N_DEV * w_2.shape[1], d)
        h2, ab, hid, f, xo = _ffn_fwd(xmid, mod, g_ffn_rows, w13_g, w2_g, dep, l, seq)
        saved.append((xs, h, z, ycat, ocat, mrg, mo, xmid, h2, ab, hid, f))
        weights.append((win_g, wpa_g, wpb_g, wpc_g, wo_g, w13_g, w2_g))
        xs = after = xo

    dx, loss_blk, dgf_blk = _final_loss(xs, g_final.reshape(1, d), loss_target.reshape(t_all, d), seq)
    loss = lax.psum(loss_blk[0, 0], ("x", "y", "c"))

    ffn_flight = [None] * n_layers
    mix_flight = [None] * n_layers
    small_grads = [None] * n_layers
    dmods = [None] * n_layers
    dep = dx
    for l in reversed(range(n_layers)):
        x_in, h, z, ycat, ocat, mrg, mo, xmid, h2, ab, hid, f = saved[l]
        win_g, wpa_g, wpb_g, wpc_g, wo_g, w13_g, w2_g = weights[l]
        dxm, df, dab, dmod2, dg_ffn = _ffn_bwd(dx, xmid, ab, f, mod, g_ffn_rows, w13_g, w2_g, dep, l, seq)
        ffn_flight[l] = send_grads(_ffn_wgrads(h2, dab, hid, df, l), f"grads_start_ffn_{l}", dxm)
        dx, dz, dycat, dmo, dmod1, dg_mix, sm, dws, dssum, dpw = _mixer_bwd(
            dxm, x_in, z, ycat, mo, mod, *mix_small, win_g, wpa_g, wpb_g, wpc_g, wo_g, ffn_flight[l][4], l, seq)
        if l > 0:
            mix_flight[l] = send_grads(_mixer_wgrads(h, dz, mrg, dmo, ocat, dycat, l), f"grads_start_mixer_{l}", dx)
            dep = mix_flight[l][4]
        dmod = jnp.concatenate([dmod1[:, 0:3], dmod2[:, 0:3]], axis=1).reshape(nb, N_MOD * d)
        dmods[l] = dmod
        db_s = jnp.transpose(jnp.sum(dssum.reshape(CHUNK, HEADS, HEAD_DIM), axis=2))
        small_grads[l] = [jnp.sum(dmod, axis=0), dg_mix[0], sm[0], sm[1], dws, db_s, dpw, sm[2], sm[3:6], dg_ffn[0]]
    grad_x = dx.reshape(nb, seq, d)

    names = ["b_mod", "g_mix", "ln_g", "ln_b", "w_s", "b_s", "pool_w", "pool_scale", "conv_w", "g_ffn"]
    per_name = [jnp.stack([small_grads[l][n] for l in range(n_layers)]) for n in range(len(names))] + [dgf_blk[0]]
    buf = _pack(per_name, 8 * N_DEV)
    dmod_all = jnp.pad(jnp.stack(dmods), ((0, 0), (0, 8 - nb), (0, 0)))
    dmod_blocks = jnp.transpose(dmod_all.reshape(n_layers, 8, N_DEV, blk), (2, 0, 1, 3))
    seg = buf.shape[0] // N_DEV
    small_flight = send_grads([buf.reshape(N_DEV, seg, 128), dmod_blocks], "small_start", buf)
    mixer_partials = _mixer_wgrads(h, dz, mrg, dmo, ocat, dycat, 0)
    (own_buf, own_d), (zone_buf, zone_d) = arrived(small_flight, mixer_partials[0], SCATTER, "small_wait")
    red_seg, grad_w_mod = _small_reduce(zone_buf, own_buf, zone_d, own_d, cact_all)
    red_zone = lax.dynamic_update_index_in_dim(lax.empty((N_DEV, seg, 128), F32), red_seg[None], me, 0)

    mix_flight[0] = send_grads(mixer_partials, "grads_start_mixer_0", red_seg)
    red_flight = _xfer_start([], [red_zone], GATHER, "small_gather_start", mix_flight[0][4])
    results = {}
    results["w_mod"] = _adamw([grad_w_mod[l][None] for l in range(n_layers)], w_mod, m_w_mod, v_w_mod, "adamw_w_mod")
    layers = list(range(n_layers))
    done = (results["w_mod"][1][0, 0, 0] + results["w_mod"][2][0, 0, 0]).reshape(1)
    ffn_recv = [arrived(ffn_flight[l], done, SCATTER, f"grads_wait_ffn_{l}") for l in reversed(layers)][::-1]
    mix_recv = [None] + [arrived(mix_flight[l], done, SCATTER, f"grads_wait_mixer_{l}") for l in reversed(layers[1:])][::-1]
    swap = lambda a: jnp.swapaxes(a, 1, 2)
    got = lambda recv, ls, a: dict(parts=[recv[l][1][a] for l in ls], own=[recv[l][0][a] for l in ls])
    results["w_13"] = [swap(r) for r in _adamw(w=swap(w_13), m=swap(m_w_13), v=swap(v_w_13), name="adamw_w_13", **got(ffn_recv, layers, 0))]
    results["w_2"] = _adamw(w=w_2, m=m_w_2, v=v_w_2, name="adamw_w_2", **got(ffn_recv, layers, 1))
    mix_m = [m_w_in, m_w_o, m_w_pa, m_w_pb, m_w_pc]
    mix_v = [v_w_in, v_w_o, v_w_pa, v_w_pb, v_w_pc]
    mix_names = ["w_in", "w_o", "w_pa", "w_pb", "w_pc"]
    early = [_adamw(w=mixer_w[a], m=mix_m[a], v=mix_v[a], name=f"adamw_{mix_names[a]}_later", first=1, **got(mix_recv, layers[1:], a))
             for a in range(n_mix)]
    done = (results["w_13"][1][0, 0, 0] + results["w_2"][1][0, 0, 0] + sum(e[1][1, 0, 0] for e in early)).reshape(1)
    mix_recv[0] = arrived(mix_flight[0], done, SCATTER, "grads_wait_mixer_0")
    for a in range(n_mix):
        results[mix_names[a]] = _adamw(w=mixer_w[a], m=mix_m[a], v=mix_v[a], name=f"adamw_{mix_names[a]}_first", earlier=early[a],
                                       **got(mix_recv, [0], a))

    red = arrived(red_flight, results["w_in"][1], GATHER, "small_gather_wait")[1][0].reshape(buf.shape)
    (g_b_mod, g_g_mix, g_ln_g, g_ln_b, g_w_s, g_b_s, g_pool_w, g_pool_scale, g_conv_full, g_g_ffn, g_g_final) = _unpack(red, per_name)
    g_conv = lax.dynamic_slice_in_dim(g_conv_full, me * conv_shard, conv_shard, axis=2)
    small_w = [b_mod, g_mix, gm_ln_g, gm_ln_b, gm_w_s, gm_b_s, pool_w, pool_scale, conv_w, g_ffn, g_final]
    small_m = [m_b_mod, m_g_mix, m_gm_ln_g, m_gm_ln_b, m_gm_w_s, m_gm_b_s, m_pool_w, m_pool_scale, m_conv_w, m_g_ffn, m_g_final]
    small_v = [v_b_mod, v_g_mix, v_gm_ln_g, v_gm_ln_b, v_gm_w_s, v_gm_b_s, v_pool_w, v_pool_scale, v_conv_w, v_g_ffn, v_g_final]
    small_g = [g_b_mod, g_g_mix, g_ln_g, g_ln_b, g_w_s, g_b_s, g_pool_w, g_pool_scale, g_conv, g_g_ffn, g_g_final]
    small_names = ["b_mod", "g_mix", "gm_ln_g", "gm_ln_b", "gm_w_s", "gm_b_s", "pool_w", "pool_scale", "conv_w", "g_ffn", "g_final"]
    sd, sm_new, sv_new = _adamw_small(small_g, small_w, small_m, small_v)
    for n, nm in enumerate(small_names):
        results[nm] = (small_g[n], sd[n], sm_new[n], sv_new[n])

    order = ["w_mod", "b_mod", "g_mix", "w_in", "gm_ln_g", "gm_ln_b", "gm_w_s", "gm_b_s", "w_pa", "pool_w", "pool_scale",
             "w_pb", "conv_w", "w_pc", "w_o", "g_ffn", "w_13", "w_2", "g_final"]
    return (loss, grad_x, *[results[nm][0] for nm in order], *[results[nm][1] for nm in order],
            *[results[nm][2] for nm in order], *[results[nm][3] for nm in order])
```
